```python
import jax
import jax.numpy as jnp
from jax import lax
import numpy as np


D_MODEL = 2048
BATCH = 4
SEQ = 2048
DEPTH = 1

PLE_DIM = 256
MIX_W = D_MODEL
SSD_W = MIX_W // 2
ATTN_W = MIX_W - SSD_W
SSD_HEADDIM = 64
SSD_HEADS = SSD_W // SSD_HEADDIM
SSD_GROUPS = 2
SSD_STATE = 128
SSD_CHUNK = 128
CONV_W = 5
XBC_W = SSD_W + 2 * SSD_GROUPS * SSD_STATE
DIFF_HEAD_DIM = 64
DIFF_HEADS = ATTN_W // (2 * DIFF_HEAD_DIM)
ROT_DIM = DIFF_HEAD_DIM // 4
ROPE_THETA = 500000.0
Q_BLOCK = 128
IN_W = SSD_W + XBC_W + 2 * SSD_HEADS + 3 * ATTN_W
MOE_GROUPS = 8
EXPERTS_PER_GROUP = 8
N_EXPERTS = MOE_GROUPS * EXPERTS_PER_GROUP
TOP_K = 2
EXPERT_FF = D_MODEL // 4
ROW_BLOCK = 128
EPS = 1e-6

kernel_name = 'hymba_ssd_diffattn_hmoe_ple_layer'


def rms_norm(x, g):
    xf = x.astype(jnp.float32)
    y = xf * lax.rsqrt(jnp.mean(xf * xf, axis=-1, keepdims=True) + EPS)
    return (y * g.astype(jnp.float32)).astype(x.dtype)


def ssd_scan(x, dt, a, bm, cm):
    b, s, h, p = x.shape
    g, n = bm.shape[2], bm.shape[3]
    r = h // g
    c = s // SSD_CHUNK
    l = SSD_CHUNK
    xd = (x.astype(jnp.float32) * dt[..., None]).reshape(b, c, l, g, r, p)
    da = (dt * a).reshape(b, c, l, g, r).transpose(0, 3, 4, 1, 2)
    bc = bm.astype(jnp.float32).reshape(b, c, l, g, n)
    cc = cm.astype(jnp.float32).reshape(b, c, l, g, n)
    a_cs = jnp.cumsum(da, axis=-1)
    seg = a_cs[..., :, None] - a_cs[..., None, :]
    lower = jnp.tril(jnp.ones((l, l), dtype=bool))
    decay_in = jnp.exp(jnp.where(lower, seg, -jnp.inf))
    cb = jnp.einsum('bclgn,bcsgn->bcgls', cc, bc)
    y_diag = jnp.einsum('bcgls,bgrcls,bcsgrp->bclgrp', cb, decay_in, xd)
    decay_to_end = jnp.exp(a_cs[..., -1:] - a_cs)
    chunk_states = jnp.einsum('bclgn,bgrcl,bclgrp->bcgrpn', bc, decay_to_end, xd)
    chunk_decay = jnp.exp(a_cs[..., -1])

    def step(state, inp):
        st, dec = inp
        return state * dec[..., None, None] + st, state

    _, states_in = lax.scan(step, jnp.zeros_like(chunk_states[:, 0]),
                            (jnp.moveaxis(chunk_states, 1, 0), jnp.moveaxis(chunk_decay, 3, 0)))
    states_in = jnp.moveaxis(states_in, 0, 1)
    y_off = jnp.einsum('bclgn,bcgrpn,bgrcl->bclgrp', cc, states_in, jnp.exp(a_cs))
    return (y_diag + y_off).reshape(b, s, h, p).astype(x.dtype)


def ssd_mixer(z, xbc, dt_f_raw, dt_b_raw, conv_w, conv_b, dt_bias_f, dt_bias_b,
              a_log_f, a_log_b, d_skip, norm_g):
    b, s, _ = z.shape
    pad = CONV_W // 2
    xbc = lax.conv_general_dilated(xbc, conv_w[:, None, :].astype(xbc.dtype), (1,), [(pad, pad)],
                                   dimension_numbers=('NWC', 'WIO', 'NWC'),
                                   feature_group_count=XBC_W)
    xbc = jax.nn.silu(xbc + conv_b)
    xs, bm, cm = jnp.split(xbc, [SSD_W, SSD_W + SSD_GROUPS * SSD_STATE], axis=-1)
    xs = xs.reshape(b, s, SSD_HEADS, SSD_HEADDIM)
    bm = bm.reshape(b, s, SSD_GROUPS, SSD_STATE)
    cm = cm.reshape(b, s, SSD_GROUPS, SSD_STATE)
    dt_f = jax.nn.softplus((dt_f_raw + dt_bias_f).astype(jnp.float32))
    dt_b = jax.nn.softplus((dt_b_raw + dt_bias_b).astype(jnp.float32))
    a_f = -jnp.exp(a_log_f.astype(jnp.float32))
    a_b = -jnp.exp(a_log_b.astype(jnp.float32))
    y_f = ssd_scan(xs, dt_f, a_f, bm, cm)
    y_b = jnp.flip(ssd_scan(jnp.flip(xs, 1), jnp.flip(dt_b, 1), a_b,
                            jnp.flip(bm, 1), jnp.flip(cm, 1)), 1)
    y = (y_f + y_b + xs * d_skip[:, None]).reshape(b, s, SSD_W)
    return rms_norm(y * jax.nn.silu(z), norm_g)


def rope_partial(t, cos, sin):
    half = ROT_DIM // 2
    c = cos[:, :, None, None, :].astype(t.dtype)
    sn = sin[:, :, None, None, :].astype(t.dtype)
    t1 = t[..., :half]
    t2 = t[..., half:ROT_DIM]
    return jnp.concatenate([t1 * c - t2 * sn, t2 * c + t1 * sn, t[..., ROT_DIM:]], axis=-1)


def diff_attention(q, k, v, cos, sin, lam, lam_init, subln_g):
    b, s = q.shape[0], q.shape[1]
    q = rope_partial(q.reshape(b, s, DIFF_HEADS, 2, DIFF_HEAD_DIM), cos, sin) * (DIFF_HEAD_DIM ** -0.5)
    k = rope_partial(k.reshape(b, s, DIFF_HEADS, 2, DIFF_HEAD_DIM), cos, sin)
    v = v.reshape(b, s, DIFF_HEADS, 2 * DIFF_HEAD_DIM)
    nq = s // Q_BLOCK
    q_blocks = jnp.moveaxis(q.reshape(b, nq, Q_BLOCK, DIFF_HEADS, 2, DIFF_HEAD_DIM), 1, 0)

    def attend(qb):
        scores = jnp.einsum('bqhcd,bkhcd->bhcqk', qb, k).astype(jnp.float32)
        probs = jax.nn.softmax(scores, axis=-1)
        w = probs[:, :, 0] - lam * probs[:, :, 1]
        return jnp.einsum('bhqk,bkhe->bqhe', w.astype(v.dtype), v)

    o = lax.map(attend, q_blocks)
    o = jnp.moveaxis(o, 0, 1).reshape(b, s, DIFF_HEADS, 2 * DIFF_HEAD_DIM)
    o = rms_norm(o, subln_g) * (1.0 - lam_init)
    return o.reshape(b, s, ATTN_W)


def hier_moe(h, w_rg, b_rg, w_re, b_re, w_gate, w_up, w_down):
    bsz, s, d = h.shape
    t = bsz * s
    hf = h.reshape(t, d)
    g_logits = (hf @ w_rg + b_rg).astype(jnp.float32)
    g_prob = jax.nn.softmax(g_logits, axis=-1)
    g_sel = jnp.argmax(g_logits, axis=-1)
    p_g = jnp.take_along_axis(g_prob, g_sel[:, None], axis=1)[:, 0]
    e_logits = (hf @ w_re + b_re).astype(jnp.float32).reshape(t, MOE_GROUPS, EXPERTS_PER_GROUP)
    e_logits = jnp.take_along_axis(e_logits, g_sel[:, None, None], axis=1)[:, 0]
    top_w, top_i = lax.top_k(jax.nn.softmax(e_logits, axis=-1), TOP_K)
    top_w = top_w / jnp.sum(top_w, axis=-1, keepdims=True)
    gate = p_g[:, None] * top_w
    expert_id = (g_sel[:, None] * EXPERTS_PER_GROUP + top_i).astype(jnp.int32)
    e_flat = expert_id.reshape(-1)
    tok_flat = jnp.repeat(jnp.arange(t, dtype=jnp.int32), TOP_K)
    w_flat = gate.reshape(-1)
    order = jnp.argsort(e_flat)
    e_sorted = e_flat[order]
    counts = jnp.bincount(e_flat, length=N_EXPERTS).astype(jnp.int32)
    padded = ((counts + ROW_BLOCK - 1) // ROW_BLOCK) * ROW_BLOCK
    ends_pad = jnp.cumsum(padded)
    starts_pad = ends_pad - padded
    starts = jnp.cumsum(counts) - counts
    dest = starts_pad[e_sorted] + (jnp.arange(t * TOP_K, dtype=jnp.int32) - starts[e_sorted])
    n_rows = ((t * TOP_K + N_EXPERTS * (ROW_BLOCK - 1)) + ROW_BLOCK - 1) // ROW_BLOCK * ROW_BLOCK
    n_blocks = n_rows // ROW_BLOCK
    row_tok = jnp.full((n_rows,), t, dtype=jnp.int32).at[dest].set(tok_flat[order])
    row_w = jnp.zeros((n_rows,), jnp.float32).at[dest].set(w_flat[order])
    block_expert = jnp.minimum(
        jnp.searchsorted(ends_pad, jnp.arange(n_blocks, dtype=jnp.int32) * ROW_BLOCK, side='right'),
        N_EXPERTS - 1)
    x_rows = jnp.concatenate([hf, jnp.zeros((1, d), hf.dtype)], axis=0)[row_tok]
    x_rows = x_rows.reshape(n_blocks, ROW_BLOCK, d)

    def expert_block(args):
        xb, e = args
        return (jax.nn.silu(xb @ w_gate[e]) * (xb @ w_up[e])) @ w_down[e]

    y_rows = lax.map(expert_block, (x_rows, block_expert)).reshape(n_rows, d)
    y = jax.ops.segment_sum(y_rows * row_w[:, None].astype(y_rows.dtype), row_tok,
                            num_segments=t + 1)[:t]
    return y.reshape(bsz, s, d)


def setup_inputs(seed: int = 0) -> dict:
    key = jax.random.key(seed)
    ks = list(jax.random.split(key, 40))
    f32 = jnp.float32

    def nrm(k, shape, scale):
        return jax.random.normal(k, shape, f32) * scale

    def gain(k, shape):
        return 1.0 + 0.01 * jax.random.normal(k, shape, f32)

    x = jax.random.normal(ks[0], (BATCH, SEQ, D_MODEL), f32)
    p = jax.random.normal(ks[1], (DEPTH, BATCH, SEQ, PLE_DIM), f32)
    offsets = jax.random.randint(ks[2], (BATCH, 1), 0, 1024, dtype=jnp.int32)
    positions = offsets + jnp.arange(SEQ, dtype=jnp.int32)[None, :]
    dt0 = jnp.exp(jax.random.uniform(ks[7], (DEPTH, SSD_HEADS), f32, np.log(1e-3), np.log(1e-1)))
    dt1 = jnp.exp(jax.random.uniform(ks[8], (DEPTH, SSD_HEADS), f32, np.log(1e-3), np.log(1e-1)))
    inv_softplus = lambda y: y + jnp.log(-jnp.expm1(-y))
    return {
        'x': x,
        'p': p,
        'positions': positions,
        'norm_mix_g': gain(ks[3], (DEPTH, D_MODEL)),
        'w_in': nrm(ks[4], (DEPTH, D_MODEL, IN_W), D_MODEL ** -0.5),
        'conv_w': nrm(ks[5], (DEPTH, CONV_W, XBC_W), CONV_W ** -0.5),
        'conv_b': nrm(ks[6], (DEPTH, XBC_W), 0.02),
        'dt_bias_f': inv_softplus(dt0),
        'dt_bias_b': inv_softplus(dt1),
        'a_log_f': jnp.log(jax.random.uniform(ks[9], (DEPTH, SSD_HEADS), f32, 1.0, 16.0)),
        'a_log_b': jnp.log(jax.random.uniform(ks[10], (DEPTH, SSD_HEADS), f32, 1.0, 16.0)),
        'd_skip': gain(ks[11], (DEPTH, SSD_HEADS)),
        'ssd_norm_g': gain(ks[12], (DEPTH, SSD_W)),
        'lam_q1': nrm(ks[13], (DEPTH, DIFF_HEAD_DIM), 0.1),
        'lam_k1': nrm(ks[14], (DEPTH, DIFF_HEAD_DIM), 0.1),
        'lam_q2': nrm(ks[15], (DEPTH, DIFF_HEAD_DIM), 0.1),
        'lam_k2': nrm(ks[16], (DEPTH, DIFF_HEAD_DIM), 0.1),
        'subln_g': gain(ks[17], (DEPTH, 2 * DIFF_HEAD_DIM)),
        'w_out': nrm(ks[18], (DEPTH, MIX_W, D_MODEL), MIX_W ** -0.5),
        'norm_ffn_g': gain(ks[19], (DEPTH, D_MODEL)),
        'w_route_group': nrm(ks[20], (DEPTH, D_MODEL, MOE_GROUPS), D_MODEL ** -0.5),
        'b_route_group': nrm(ks[21], (DEPTH, MOE_GROUPS), 0.01),
        'w_route_expert': nrm(ks[22], (DEPTH, D_MODEL, N_EXPERTS), D_MODEL ** -0.5),
        'b_route_expert': nrm(ks[23], (DEPTH, N_EXPERTS), 0.01),
        'w_exp_gate': nrm(ks[24], (DEPTH, N_EXPERTS, D_MODEL, EXPERT_FF), D_MODEL ** -0.5),
        'w_exp_up': nrm(ks[25], (DEPTH, N_EXPERTS, D_MODEL, EXPERT_FF), D_MODEL ** -0.5),
        'w_exp_down': nrm(ks[26], (DEPTH, N_EXPERTS, EXPERT_FF, D_MODEL), EXPERT_FF ** -0.5),
        'w_ple_proj': nrm(ks[27], (DEPTH, PLE_DIM, D_MODEL), PLE_DIM ** -0.5),
        'ple_norm_g': gain(ks[28], (DEPTH, D_MODEL)),
        'w_ple_gate': nrm(ks[29], (DEPTH, D_MODEL, D_MODEL), D_MODEL ** -0.5),
        'b_ple_gate': nrm(ks[30], (DEPTH, D_MODEL), 0.02),
        'final_norm_g': gain(ks[31], (D_MODEL,)),
    }


def reference(x, p, positions, norm_mix_g, w_in, conv_w, conv_b, dt_bias_f, dt_bias_b,
              a_log_f, a_log_b, d_skip, ssd_norm_g, lam_q1, lam_k1, lam_q2, lam_k2, subln_g,
              w_out, norm_ffn_g, w_route_group, b_route_group, w_route_expert, b_route_expert,
              w_exp_gate, w_exp_up, w_exp_down, w_ple_proj, ple_norm_g, w_ple_gate, b_ple_gate,
              final_norm_g):
    inv_freq = ROPE_THETA ** (-jnp.arange(0, ROT_DIM, 2, dtype=jnp.float32) / ROT_DIM)
    angles = positions.astype(jnp.float32)[..., None] * inv_freq
    cos, sin = jnp.cos(angles), jnp.sin(angles)
    sizes = [SSD_W, XBC_W, SSD_HEADS, SSD_HEADS, ATTN_W, ATTN_W, ATTN_W]
    split_at = np.cumsum(sizes)[:-1].tolist()
    h = x
    for i in range(DEPTH):
        lam_init = 0.8 - 0.6 * float(np.exp(-0.3 * i))
        n = rms_norm(h, norm_mix_g[i])
        proj = n @ w_in[i]
        z, xbc, dt_f, dt_b, q, k, v = jnp.split(proj, split_at, axis=-1)
        y_ssd = ssd_mixer(z, xbc, dt_f, dt_b, conv_w[i], conv_b[i], dt_bias_f[i], dt_bias_b[i],
                          a_log_f[i], a_log_b[i], d_skip[i], ssd_norm_g[i])
        lam = (jnp.exp(jnp.sum(lam_q1[i] * lam_k1[i]).astype(jnp.float32))
               - jnp.exp(jnp.sum(lam_q2[i] * lam_k2[i]).astype(jnp.float32)) + lam_init)
        y_att = diff_attention(q, k, v, cos, sin, lam, lam_init, subln_g[i])
        h = h + jnp.concatenate([y_ssd, y_att], axis=-1) @ w_out[i]
        h = h + hier_moe(rms_norm(h, norm_ffn_g[i]), w_route_group[i], b_route_group[i],
                         w_route_expert[i], b_route_expert[i], w_exp_gate[i], w_exp_up[i],
                         w_exp_down[i])
        ple = rms_norm(p[i] @ w_ple_proj[i], ple_norm_g[i])
        h = h + jax.nn.sigmoid(h @ w_ple_gate[i] + b_ple_gate[i]) * ple
    return rms_norm(h, final_norm_g)
```

```python
import functools

import jax
import jax.numpy as jnp
import numpy as np
from jax import lax
from jax.experimental import pallas as pl
from jax.experimental.pallas import tpu as pltpu

D_MODEL = 2048
PLE_DIM = 256
SSD_W = 1024
ATTN_W = 1024
SSD_HEADDIM = 64
SSD_HEADS = 16
SSD_GROUPS = 2
HEADS_PER_GROUP = SSD_HEADS // SSD_GROUPS
SSD_STATE = 128
CHUNK = 128
CONV_W = 5
XBC_W = SSD_W + 2 * SSD_GROUPS * SSD_STATE
DIFF_HEAD_DIM = 64
DIFF_HEADS = 8
ROT_DIM = 16
ROPE_THETA = 500000.0
MOE_GROUPS = 8
EXPERTS_PER_GROUP = 8
N_EXPERTS = 64
TOP_K = 2
EXPERT_FF = 512
ROW_BLOCK = 128
EPS = 1e-6
LAM_INIT = 0.2

LANES = 128
DT_PAD = LANES
MAIN_W = SSD_W + XBC_W + 3 * ATTN_W
COL_Z = 0
COL_Q = SSD_W
COL_K = COL_Q + ATTN_W
COL_XBC = COL_K + ATTN_W
COL_V = COL_XBC + XBC_W
VMEM_LIMIT = 56 * 1024 * 1024
NEG_BIG = -1e30

_f32 = jnp.float32
_bf16 = jnp.bfloat16


def _silu(v):
    return v * (1.0 / (1.0 + jnp.exp(-v)))


def _rms(v, g):
    return v * lax.rsqrt(jnp.mean(v * v, axis=-1, keepdims=True) + EPS) * g


IN_TM = 512
IN_TN = 512


def _inproj_kernel(x_ref, g_ref, w_ref, wdt_ref, rc_ref, rp_ref, rm_ref,
                   out_ref, dt_ref, n_scr):
    j = pl.program_id(1)

    @pl.when(j == 0)
    def _():
        n = _rms(x_ref[...], g_ref[...])
        n_scr[...] = n.astype(_bf16)
        dt_ref[...] = jnp.dot(n_scr[...], wdt_ref[...], preferred_element_type=_f32)

    acc = jnp.dot(n_scr[...], w_ref[...], preferred_element_type=_f32)
    q_lo, k_lo, k_hi = COL_Q // IN_TN, COL_K // IN_TN, (COL_K + ATTN_W) // IN_TN
    is_rot = jnp.logical_and(j >= q_lo, j < k_hi)

    @pl.when(is_rot)
    def _():
        scale = jnp.where(j < k_lo, DIFF_HEAD_DIM ** -0.5, 1.0).astype(_f32)
        rc, rp, rm = rc_ref[...], rp_ref[...], rm_ref[...]
        for hb in range(IN_TN // LANES):
            t = acc[:, hb * LANES:(hb + 1) * LANES]
            r = (t * rc + pltpu.roll(t, ROT_DIM // 2, 1) * rp
                 + pltpu.roll(t, LANES - ROT_DIM // 2, 1) * rm)
            out_ref[:, hb * LANES:(hb + 1) * LANES] = (r * scale).astype(_bf16)

    @pl.when(jnp.logical_not(is_rot))
    def _():
        out_ref[...] = acc.astype(_bf16)


def _inproj(x2, g, w_main, w_dt, rc, rp, rm):
    t = x2.shape[0]
    grid = (t // IN_TM, MAIN_W // IN_TN)
    return pl.pallas_call(
        _inproj_kernel,
        grid=grid,
        in_specs=[
            pl.BlockSpec((IN_TM, D_MODEL), lambda i, j: (i, 0)),
            pl.BlockSpec((1, D_MODEL), lambda i, j: (0, 0)),
            pl.BlockSpec((D_MODEL, IN_TN), lambda i, j: (0, j)),
            pl.BlockSpec((D_MODEL, DT_PAD), lambda i, j: (0, 0)),
            pl.BlockSpec((IN_TM, LANES), lambda i, j: (i, 0)),
            pl.BlockSpec((IN_TM, LANES), lambda i, j: (i, 0)),
            pl.BlockSpec((IN_TM, LANES), lambda i, j: (i, 0)),
        ],
        out_specs=[
            pl.BlockSpec((IN_TM, IN_TN), lambda i, j: (i, j)),
            pl.BlockSpec((IN_TM, DT_PAD), lambda i, j: (i, 0)),
        ],
        out_shape=[
            jax.ShapeDtypeStruct((t, MAIN_W), _bf16),
            jax.ShapeDtypeStruct((t, DT_PAD), _f32),
        ],
        scratch_shapes=[pltpu.VMEM((IN_TM, D_MODEL), _bf16)],
        compiler_params=pltpu.CompilerParams(
            dimension_semantics=("arbitrary", "arbitrary"),
            vmem_limit_bytes=VMEM_LIMIT),
        name="inproj",
    )(x2, g, w_main, w_dt, rc, rp, rm)


CONV_HALO = 16


def _ssd_kernel(xbc_ref, dt_ref, cw_ref, cb_ref, dtb_ref, a_ref, dsk_ref,
                y_ref, xpad, xact, dts, yacc, state, seq):
    n_chunks = seq // CHUNK

    zeros_halo = jnp.zeros((CONV_HALO, XBC_W), _bf16)
    xpad[0:CONV_HALO, :] = zeros_halo
    xpad[CONV_HALO + seq:CONV_HALO + seq + CONV_HALO, :] = zeros_halo
    xpad[CONV_HALO:CONV_HALO + seq, :] = xbc_ref[0]
    cw = cw_ref[...]
    cb = cb_ref[...]
    win = CHUNK + 2 * CONV_HALO

    def conv_body(c, carry):
        r0 = pl.multiple_of(c * CHUNK, CHUNK)
        blk = xpad[pl.ds(r0, win), :].astype(_f32)
        acc = blk * cw[CONV_W // 2:CONV_W // 2 + 1, :]
        for k in range(CONV_W):
            sh = CONV_W // 2 - k
            if sh == 0:
                continue
            acc = acc + pltpu.roll(blk, sh % win, 0) * cw[k:k + 1, :]
        v = acc[CONV_HALO:CONV_HALO + CHUNK, :] + cb
        xact[pl.ds(r0, CHUNK), :] = _silu(v).astype(_bf16)
        return carry

    lax.fori_loop(0, n_chunks, conv_body, 0)

    raw = dt_ref[0] + dtb_ref[...]
    dts[...] = jnp.maximum(raw, 0.0) + jnp.log1p(jnp.exp(-jnp.abs(raw)))

    a_row = a_ref[...]
    dsk = dsk_ref[...]
    row_i = lax.broadcasted_iota(jnp.int32, (CHUNK, CHUNK), 0)
    col_i = lax.broadcasted_iota(jnp.int32, (CHUNK, CHUNK), 1)

    def chunk_step(c, reverse):
        r0 = pl.multiple_of(c * CHUNK, CHUNK)
        dtc = dts[pl.ds(r0, CHUNK), :]
        cs = dtc * a_row
        k = 1
        while k < CHUNK:
            if reverse:
                cs = cs + jnp.where(row_i < CHUNK - k, pltpu.roll(cs, CHUNK - k, 0), 0.0)
            else:
                cs = cs + jnp.where(row_i >= k, pltpu.roll(cs, k, 0), 0.0)
            k *= 2
        cs_t = cs.T
        dt_t = dtc.T
        end_col = cs_t[:, 0:1] if reverse else cs_t[:, CHUNK - 1:CHUNK]
        w_t = jnp.exp(end_col - cs_t) * dt_t
        exp_cs = jnp.exp(cs)
        tri = (row_i <= col_i) if reverse else (row_i >= col_i)
        lane0 = SSD_HEADS if reverse else 0
        for g in range(SSD_GROUPS):
            b_g = xact[pl.ds(r0, CHUNK), SSD_W + g * SSD_STATE:SSD_W + (g + 1) * SSD_STATE]
            c_g = xact[pl.ds(r0, CHUNK),
                       SSD_W + (SSD_GROUPS + g) * SSD_STATE:SSD_W + (SSD_GROUPS + g + 1) * SSD_STATE]
            cbm = lax.dot_general(c_g, b_g, (((1,), (1,)), ((), ())),
                                  preferred_element_type=_f32)
            b_t = b_g.astype(_f32).T
            c_f = c_g.astype(_f32)
            for hh in range(HEADS_PER_GROUP):
                h = g * HEADS_PER_GROUP + hh
                ln = lane0 + h
                xs_h = xact[pl.ds(r0, CHUNK), h * SSD_HEADDIM:(h + 1) * SSD_HEADDIM]
                seg = cs[:, ln:ln + 1] - cs_t[ln:ln + 1, :]
                m_h = cbm * jnp.exp(jnp.where(tri, seg, NEG_BIG)) * dt_t[ln:ln + 1, :]
                st = state[h]
                y_h = jnp.dot(m_h.astype(_bf16), xs_h, preferred_element_type=_f32)
                y_h = y_h + jnp.dot((c_f * exp_cs[:, ln:ln + 1]).astype(_bf16),
                                    st.astype(_bf16), preferred_element_type=_f32)
                upd = jnp.dot((b_t * w_t[ln:ln + 1, :]).astype(_bf16), xs_h,
                              preferred_element_type=_f32)
                if reverse:
                    dec = exp_cs[0:1, ln:ln + 1]
                else:
                    dec = exp_cs[CHUNK - 1:CHUNK, ln:ln + 1]
                state[h] = st * dec + upd
                cols = slice(h * SSD_HEADDIM, (h + 1) * SSD_HEADDIM)
                if reverse:
                    y_ref[0, pl.ds(r0, CHUNK), cols] = (
                        yacc[pl.ds(r0, CHUNK), cols] + y_h).astype(_bf16)
                else:
                    yacc[pl.ds(r0, CHUNK), cols] = (
                        y_h + xs_h.astype(_f32) * dsk[:, cols])

    state[...] = jnp.zeros_like(state)

    def fwd_body(c, carry):
        chunk_step(c, False)
        return carry

    lax.fori_loop(0, n_chunks, fwd_body, 0)

    state[...] = jnp.zeros_like(state)

    def bwd_body(i, carry):
        chunk_step(n_chunks - 1 - i, True)
        return carry

    lax.fori_loop(0, n_chunks, bwd_body, 0)


def _ssd(proj3, dt3, conv_w, conv_b, dt_bias, a_row, d_skip_row):
    b, seq, _ = proj3.shape
    kern = functools.partial(_ssd_kernel, seq=seq)
    assert COL_XBC % XBC_W == 0
    return pl.pallas_call(
        kern,
        grid=(b,),
        in_specs=[
            pl.BlockSpec((1, seq, XBC_W), lambda i: (i, 0, COL_XBC // XBC_W)),
            pl.BlockSpec((1, seq, DT_PAD), lambda i: (i, 0, 0)),
            pl.BlockSpec((CONV_W, XBC_W), lambda i: (0, 0)),
            pl.BlockSpec((1, XBC_W), lambda i: (0, 0)),
            pl.BlockSpec((1, DT_PAD), lambda i: (0, 0)),
            pl.BlockSpec((1, DT_PAD), lambda i: (0, 0)),
            pl.BlockSpec((1, SSD_W), lambda i: (0, 0)),
        ],
        out_specs=pl.BlockSpec((1, seq, SSD_W), lambda i: (i, 0, 0)),
        out_shape=jax.ShapeDtypeStruct((b, seq, SSD_W), _bf16),
        scratch_shapes=[
            pltpu.VMEM((seq + 2 * CONV_HALO, XBC_W), _bf16),
            pltpu.VMEM((seq, XBC_W), _bf16),
            pltpu.VMEM((seq, DT_PAD), _f32),
            pltpu.VMEM((seq, SSD_W), _f32),
            pltpu.VMEM((SSD_HEADS, SSD_STATE, SSD_HEADDIM), _f32),
        ],
        compiler_params=pltpu.CompilerParams(
            dimension_semantics=("arbitrary",),
            vmem_limit_bytes=VMEM_LIMIT),
        name="ssd",
    )(proj3, dt3, conv_w, conv_b, dt_bias, a_row, d_skip_row)


ATT_QB = 256


def _attn_kernel(q_ref, k_ref, v_ref, lam_ref, g_ref, o_ref, *, seq):
    lv = lam_ref[...]
    lam = (jnp.exp(jnp.sum(lv[0:1] * lv[1:2], axis=-1, keepdims=True))
           - jnp.exp(jnp.sum(lv[2:3] * lv[3:4], axis=-1, keepdims=True)) + LAM_INIT)
    k = k_ref[0]
    v = v_ref[0]
    g = g_ref[...]
    first = lax.broadcasted_iota(jnp.int32, (1, 2 * DIFF_HEAD_DIM), 1) < DIFF_HEAD_DIM
    nt = (((1,), (1,)), ((), ()))

    def body(i, carry):
        r0 = pl.multiple_of(i * ATT_QB, ATT_QB)
        q = q_ref[0, pl.ds(r0, ATT_QB), :]
        zero = jnp.zeros_like(q)
        s1 = lax.dot_general(jnp.where(first, q, zero), k, nt, preferred_element_type=_f32)
        s2 = lax.dot_general(jnp.where(first, zero, q), k, nt, preferred_element_type=_f32)
        e1 = jnp.exp(s1 - jnp.max(s1, axis=-1, keepdims=True))
        e2 = jnp.exp(s2 - jnp.max(s2, axis=-1, keepdims=True))
        r1 = 1.0 / jnp.sum(e1, axis=-1, keepdims=True)
        r2 = lam / jnp.sum(e2, axis=-1, keepdims=True)
        w = (e1 * r1 - e2 * r2).astype(_bf16)
        o = jnp.dot(w, v, preferred_element_type=_f32)
        o = _rms(o, g) * (1.0 - LAM_INIT)
        o_ref[0, pl.ds(r0, ATT_QB), :] = o.astype(_bf16)
        return carry

    lax.fori_loop(0, seq // ATT_QB, body, 0)


def _attn(proj3, lam_vecs, subln_g):
    b, seq, _ = proj3.shape
    hw = 2 * DIFF_HEAD_DIM
    kern = functools.partial(_attn_kernel, seq=seq)
    return pl.pallas_call(
        kern,
        grid=(b, DIFF_HEADS),
        in_specs=[
            pl.BlockSpec((1, seq, hw), lambda i, h: (i, 0, COL_Q // hw + h)),
            pl.BlockSpec((1, seq, hw), lambda i, h: (i, 0, COL_K // hw + h)),
            pl.BlockSpec((1, seq, hw), lambda i, h: (i, 0, COL_V // hw + h)),
            pl.BlockSpec((4, DIFF_HEAD_DIM), lambda i, h: (0, 0)),
            pl.BlockSpec((1, hw), lambda i, h: (0, 0)),
        ],
        out_specs=pl.BlockSpec((1, seq, hw), lambda i, h: (i, 0, h)),
        out_shape=jax.ShapeDtypeStruct((b, seq, ATTN_W), _bf16),
        compiler_params=pltpu.CompilerParams(
            dimension_semantics=("arbitrary", "arbitrary"),
            vmem_limit_bytes=VMEM_LIMIT),
        name="diffattn",
    )(proj3, proj3, proj3, lam_vecs, subln_g)


OUT_TM = 256
ROUTE_ROWS = 8


def _outproj_kernel(y_ref, z_ref, att_ref, x_ref, w_ref, gs_ref, gf_ref, wr_ref, br_ref,
                    h_ref, eid_ref, gate_ref):
    y = y_ref[...].astype(_f32)
    z = z_ref[...].astype(_f32)
    s = _rms(y * _silu(z), gs_ref[...]).astype(_bf16)
    acc = jnp.dot(s, w_ref[0:SSD_W, :], preferred_element_type=_f32)
    acc = acc + jnp.dot(att_ref[...], w_ref[SSD_W:SSD_W + ATTN_W, :], preferred_element_type=_f32)
    h1 = x_ref[...] + acc
    h_ref[...] = h1
    hn = _rms(h1, gf_ref[...])
    logits = lax.dot_general(wr_ref[...], hn, (((1,), (1,)), ((), ())),
                             precision=lax.Precision.HIGHEST,
                             preferred_element_type=_f32) + br_ref[...]
    tm = logits.shape[1]
    iota = lax.broadcasted_iota(jnp.int32, (MOE_GROUPS, tm), 0)

    def first_argmax(val, vmax):
        return jnp.min(jnp.where(val == vmax, iota, MOE_GROUPS), axis=0, keepdims=True)

    gl = logits[0:MOE_GROUPS]
    gmax = jnp.max(gl, axis=0, keepdims=True)
    p_g = 1.0 / jnp.sum(jnp.exp(gl - gmax), axis=0, keepdims=True)
    g_sel = first_argmax(gl, gmax)
    el = jnp.zeros((EXPERTS_PER_GROUP, tm), _f32)
    for gi in range(MOE_GROUPS):
        lo = MOE_GROUPS + gi * EXPERTS_PER_GROUP
        el = jnp.where(g_sel == gi, logits[lo:lo + EXPERTS_PER_GROUP], el)
    ee = jnp.exp(el - jnp.max(el, axis=0, keepdims=True))
    pe = ee / jnp.sum(ee, axis=0, keepdims=True)
    p1 = jnp.max(pe, axis=0, keepdims=True)
    i1 = first_argmax(pe, p1)
    pe2 = jnp.where(iota == i1, -1.0, pe)
    p2 = jnp.max(pe2, axis=0, keepdims=True)
    i2 = first_argmax(pe2, p2)
    den = p1 + p2
    base = g_sel * EXPERTS_PER_GROUP
    eid_ref[...] = jnp.where(iota == 0, base + i1, jnp.where(iota == 1, base + i2, 0))
    gate_ref[...] = jnp.where(iota == 0, p_g * (p1 / den),
                              jnp.where(iota == 1, p_g * (p2 / den), 0.0))


def _outproj(y2, proj2, att2, x2, w_out, g_ssd, g_ffn, wr_t, br_col):
    t = x2.shape[0]
    row = lambda i: (i, 0)
    fix = lambda i: (0, 0)
    return pl.pallas_call(
        _outproj_kernel,
        grid=(t // OUT_TM,),
        in_specs=[
            pl.BlockSpec((OUT_TM, SSD_W), row),
            pl.BlockSpec((OUT_TM, SSD_W), row),
            pl.BlockSpec((OUT_TM, ATTN_W), row),
            pl.BlockSpec((OUT_TM, D_MODEL), row),
            pl.BlockSpec((SSD_W + ATTN_W, D_MODEL), fix),
            pl.BlockSpec((1, SSD_W), fix),
            pl.BlockSpec((1, D_MODEL), fix),
            pl.BlockSpec((LANES, D_MODEL), fix),
            pl.BlockSpec((LANES, 1), fix),
        ],
        out_specs=[
            pl.BlockSpec((OUT_TM, D_MODEL), row),
            pl.BlockSpec((ROUTE_ROWS, OUT_TM), lambda i: (0, i)),
            pl.BlockSpec((ROUTE_ROWS, OUT_TM), lambda i: (0, i)),
        ],
        out_shape=[
            jax.ShapeDtypeStruct((t, D_MODEL), _f32),
            jax.ShapeDtypeStruct((ROUTE_ROWS, t), jnp.int32),
            jax.ShapeDtypeStruct((ROUTE_ROWS, t), _f32),
        ],
        compiler_params=pltpu.CompilerParams(
            dimension_semantics=("arbitrary",),
            vmem_limit_bytes=VMEM_LIMIT),
        name="outproj_router",
    )(y2, proj2, att2, x2, w_out, g_ssd, g_ffn, wr_t, br_col)


def _moe_kernel(be_ref, src_ref, nused_ref,
                h_hbm, g_ref, wg_ref, wu_ref, wd_ref, rw_ref,
                out_ref, xbuf, sem, wg_s, wu_s, wd_s):
    i = pl.program_id(0)
    nused = nused_ref[0]
    slot = i % 2

    def issue(blk, sl):
        base = blk * ROW_BLOCK

        def body(r, carry):
            tok = src_ref[base + r]
            pltpu.make_async_copy(h_hbm.at[pl.ds(tok, 1), :],
                                  xbuf.at[sl, pl.ds(r, 1), :], sem.at[sl]).start()
            return carry

        lax.fori_loop(0, ROW_BLOCK, body, 0)

    @pl.when(i == 0)
    def _():
        issue(0, 0)

    @pl.when(i + 1 < nused)
    def _():
        issue(i + 1, 1 - slot)

    @pl.when(i < nused)
    def _():
        pltpu.make_async_copy(h_hbm.at[pl.ds(0, ROW_BLOCK), :], xbuf.at[slot], sem.at[slot]).wait()
        new_expert = jnp.logical_or(i == 0, be_ref[i] != be_ref[jnp.maximum(i - 1, 0)])

        @pl.when(new_expert)
        def _():
            wg_s[...] = wg_ref[0].astype(_bf16)
            wu_s[...] = wu_ref[0].astype(_bf16)
            wd_s[...] = wd_ref[0].astype(_bf16)

        x = _rms(xbuf[slot], g_ref[...]).astype(_bf16)
        gg = jnp.dot(x, wg_s[...], preferred_element_type=_f32)
        uu = jnp.dot(x, wu_s[...], preferred_element_type=_f32)
        a = (_silu(gg) * uu).astype(_bf16)
        y = jnp.dot(a, wd_s[...], preferred_element_type=_f32)
        out_ref[...] = y * rw_ref[...]

    @pl.when(i >= nused)
    def _():
        out_ref[...] = jnp.zeros_like(out_ref)


def _moe(block_expert, row_src, n_used, h1, g_ffn, w_gate, w_up, w_down, row_w):
    n_rows = row_src.shape[0]
    n_blocks = n_rows // ROW_BLOCK
    wmap = lambda i, be, rs, nu: (be[i], 0, 0)
    fix = lambda i, be, rs, nu: (0, 0)
    omap = lambda i, be, rs, nu: (i, 0)
    grid_spec = pltpu.PrefetchScalarGridSpec(
        num_scalar_prefetch=3,
        grid=(n_blocks,),
        in_specs=[
            pl.BlockSpec(memory_space=pl.ANY),
            pl.BlockSpec((1, D_MODEL), fix),
            pl.BlockSpec((1, D_MODEL, EXPERT_FF), wmap),
            pl.BlockSpec((1, D_MODEL, EXPERT_FF), wmap),
            pl.BlockSpec((1, EXPERT_FF, D_MODEL), wmap),
            pl.BlockSpec((ROW_BLOCK, 1), omap),
        ],
        out_specs=pl.BlockSpec((ROW_BLOCK, D_MODEL), omap),
        scratch_shapes=[
            pltpu.VMEM((2, ROW_BLOCK, D_MODEL), _f32),
            pltpu.SemaphoreType.DMA((2,)),
            pltpu.VMEM((D_MODEL, EXPERT_FF), _bf16),
            pltpu.VMEM((D_MODEL, EXPERT_FF), _bf16),
            pltpu.VMEM((EXPERT_FF, D_MODEL), _bf16),
        ],
    )
    return pl.pallas_call(
        _moe_kernel,
        grid_spec=grid_spec,
        out_shape=jax.ShapeDtypeStruct((n_rows, D_MODEL), _f32),
        compiler_params=pltpu.CompilerParams(
            dimension_semantics=("arbitrary",),
            vmem_limit_bytes=VMEM_LIMIT),
        name="moe_experts",
    )(block_expert, row_src, n_used, h1, g_ffn, w_gate, w_up, w_down, row_w)


TAIL_TM = 256


def _tail_kernel(pos_ref, h_ref, y_hbm, p_ref, wpp_ref, gp_ref, wpg_ref, bpg_ref, gfin_ref,
                 out_ref, ybuf, sem, *, n_tok):
    i = pl.program_id(0)
    n_steps = pl.num_programs(0)
    slot = i % 2

    def issue(step, sl):
        base = step * TAIL_TM

        def body(r, carry):
            for kk in range(TOP_K):
                row = pos_ref[kk * n_tok + base + r]
                pltpu.make_async_copy(y_hbm.at[pl.ds(row, 1), :],
                                      ybuf.at[sl, kk, pl.ds(r, 1), :], sem.at[sl]).start()
            return carry

        lax.fori_loop(0, TAIL_TM, body, 0)

    @pl.when(i == 0)
    def _():
        issue(0, 0)

    @pl.when(i + 1 < n_steps)
    def _():
        issue(i + 1, 1 - slot)

    for kk in range(TOP_K):
        pltpu.make_async_copy(y_hbm.at[pl.ds(0, TAIL_TM), :], ybuf.at[slot, kk], sem.at[slot]).wait()

    h2 = h_ref[...] + ybuf[slot, 0] + ybuf[slot, 1]
    ple = _rms(jnp.dot(p_ref[...].astype(_bf16), wpp_ref[...], preferred_element_type=_f32),
               gp_ref[...])
    lg = jnp.dot(h2.astype(_bf16), wpg_ref[...], preferred_element_type=_f32) + bpg_ref[...]
    h3 = h2 + (1.0 / (1.0 + jnp.exp(-lg))) * ple
    out_ref[...] = _rms(h3, gfin_ref[...])


def _tail(pos, h1, y_rows, p2, w_pp, g_ple, w_pg, b_pg, g_fin):
    t = h1.shape[0]
    row = lambda i, ps: (i, 0)
    fix = lambda i, ps: (0, 0)
    grid_spec = pltpu.PrefetchScalarGridSpec(
        num_scalar_prefetch=1,
        grid=(t // TAIL_TM,),
        in_specs=[
            pl.BlockSpec((TAIL_TM, D_MODEL), row),
            pl.BlockSpec(memory_space=pl.ANY),
            pl.BlockSpec((TAIL_TM, PLE_DIM), row),
            pl.BlockSpec((PLE_DIM, D_MODEL), fix),
            pl.BlockSpec((1, D_MODEL), fix),
            pl.BlockSpec((D_MODEL, D_MODEL), fix),
            pl.BlockSpec((1, D_MODEL), fix),
            pl.BlockSpec((1, D_MODEL), fix),
        ],
        out_specs=pl.BlockSpec((TAIL_TM, D_MODEL), row),
        scratch_shapes=[
            pltpu.VMEM((2, TOP_K, TAIL_TM, D_MODEL), _f32),
            pltpu.SemaphoreType.DMA((2,)),
        ],
    )
    return pl.pallas_call(
        functools.partial(_tail_kernel, n_tok=t),
        grid_spec=grid_spec,
        out_shape=jax.ShapeDtypeStruct((t, D_MODEL), _f32),
        compiler_params=pltpu.CompilerParams(
            dimension_semantics=("arbitrary",),
            vmem_limit_bytes=VMEM_LIMIT),
        name="tail",
    )(pos, h1, y_rows, p2, w_pp, g_ple, w_pg, b_pg, g_fin)


def _rope_tables(positions):
    half = ROT_DIM // 2
    inv_freq = ROPE_THETA ** (-jnp.arange(0, ROT_DIM, 2, dtype=_f32) / ROT_DIM)
    ang = positions.astype(_f32).reshape(-1, 1) * inv_freq
    cos, sin = jnp.cos(ang), jnp.sin(ang)
    t = ang.shape[0]
    rest = DIFF_HEAD_DIM - ROT_DIM
    zero8 = jnp.zeros((t, half), _f32)
    comp = lambda a, b, fill: jnp.concatenate([a, b, jnp.full((t, rest), fill, _f32)], axis=1)
    twice = lambda c: jnp.concatenate([c, c], axis=1)
    rc = twice(comp(cos, cos, 1.0))
    rp = twice(comp(zero8, sin, 0.0))
    rm = twice(comp(-sin, zero8, 0.0))
    return rc, rp, rm


def _dispatch(eid, gate, n_tok):
    n_assign = n_tok * TOP_K
    e_flat = jnp.stack([eid[0], eid[1]], axis=1).reshape(-1)
    w_flat = jnp.stack([gate[0], gate[1]], axis=1).reshape(-1)
    order = jnp.argsort(e_flat).astype(jnp.int32)
    counts = jnp.bincount(e_flat, length=N_EXPERTS).astype(jnp.int32)
    padded = ((counts + ROW_BLOCK - 1) // ROW_BLOCK) * ROW_BLOCK
    ends_pad = jnp.cumsum(padded)
    starts_pad = ends_pad - padded
    starts = jnp.cumsum(counts) - counts
    n_rows = ((n_assign + N_EXPERTS * (ROW_BLOCK - 1)) + ROW_BLOCK - 1) // ROW_BLOCK * ROW_BLOCK
    n_blocks = n_rows // ROW_BLOCK
    n_used = (ends_pad[-1] // ROW_BLOCK).astype(jnp.int32)
    blk = jnp.arange(n_blocks, dtype=jnp.int32)
    block_expert = jnp.minimum(
        jnp.searchsorted(ends_pad, blk * ROW_BLOCK, side='right'), N_EXPERTS - 1).astype(jnp.int32)
    rows = jnp.arange(n_rows, dtype=jnp.int32)
    e_r = block_expert[rows // ROW_BLOCK]
    idx = rows - starts_pad[e_r]
    valid = idx < counts[e_r]
    src = order[jnp.clip(starts[e_r] + idx, 0, n_assign - 1)]
    row_src = jnp.where(valid, src // TOP_K, 0).astype(jnp.int32)
    row_w = jnp.where(valid, w_flat[src], 0.0).astype(_f32)
    block_expert = jnp.where(blk < n_used, block_expert, block_expert[n_used - 1])
    inv = jnp.zeros((n_assign,), jnp.int32).at[order].set(
        jnp.arange(n_assign, dtype=jnp.int32), unique_indices=True)
    pos_flat = starts_pad[e_flat] + inv - starts[e_flat]
    pos = pos_flat.reshape(n_tok, TOP_K).T.reshape(-1).astype(jnp.int32)
    return block_expert, row_src, n_used.reshape(1), row_w.reshape(n_rows, 1), pos


def kernel(x, p, positions, norm_mix_g, w_in, conv_w, conv_b, dt_bias_f, dt_bias_b, a_log_f, a_log_b, d_skip, ssd_norm_g, lam_q1, lam_k1, lam_q2, lam_k2, subln_g, w_out, norm_ffn_g, w_route_group, b_route_group, w_route_expert, b_route_expert, w_exp_gate, w_exp_up, w_exp_down, w_ple_proj, ple_norm_g, w_ple_gate, b_ple_gate, final_norm_g):
    b, seq, d = x.shape
    t = b * seq
    x2 = x.reshape(t, d)
    row = lambda v: v.reshape(1, -1).astype(_f32)

    wi = w_in[0]
    o_xbc, o_dt, o_q = SSD_W, SSD_W + XBC_W, SSD_W + XBC_W + 2 * SSD_HEADS
    w_main = jnp.concatenate(
        [wi[:, :o_xbc], wi[:, o_q:o_q + 2 * ATTN_W], wi[:, o_xbc:o_dt], wi[:, o_q + 2 * ATTN_W:]],
        axis=1).astype(_bf16)
    w_dt = jnp.pad(wi[:, o_dt:o_q], ((0, 0), (0, DT_PAD - 2 * SSD_HEADS))).astype(_bf16)
    pad_dt = lambda v: jnp.pad(v, (0, DT_PAD - 2 * SSD_HEADS)).reshape(1, DT_PAD)
    dt_bias = pad_dt(jnp.concatenate([dt_bias_f[0], dt_bias_b[0]]))
    a_row = pad_dt(jnp.concatenate([-jnp.exp(a_log_f[0]), -jnp.exp(a_log_b[0])]))
    d_skip_row = jnp.repeat(d_skip[0], SSD_HEADDIM).reshape(1, SSD_W)
    lam_vecs = jnp.stack([lam_q1[0], lam_k1[0], lam_q2[0], lam_k2[0]])
    n_route = MOE_GROUPS + N_EXPERTS
    wr_t = jnp.pad(jnp.concatenate([w_route_group[0], w_route_expert[0]], axis=1).T,
                   ((0, LANES - n_route), (0, 0)))
    br_col = jnp.pad(jnp.concatenate([b_route_group[0], b_route_expert[0]]),
                     (0, LANES - n_route)).reshape(LANES, 1)
    rc, rp, rm = _rope_tables(positions)

    proj, dt = _inproj(x2, row(norm_mix_g[0]), w_main, w_dt, rc, rp, rm)
    proj3 = proj.reshape(b, seq, MAIN_W)
    y_ssd = _ssd(proj3, dt.reshape(b, seq, DT_PAD), conv_w[0], row(conv_b[0]),
                 dt_bias, a_row, d_skip_row)
    att = _attn(proj3, lam_vecs, row(subln_g[0]))
    h1, eid, gate = _outproj(y_ssd.reshape(t, SSD_W), proj, att.reshape(t, ATTN_W), x2,
                             w_out[0].astype(_bf16), row(ssd_norm_g[0]), row(norm_ffn_g[0]),
                             wr_t, br_col)
    block_expert, row_src, n_used, row_w, pos = _dispatch(eid, gate, t)
    y_rows = _moe(block_expert, row_src, n_used, h1, row(norm_ffn_g[0]),
                  w_exp_gate[0], w_exp_up[0], w_exp_down[0], row_w)
    out = _tail(pos, h1, y_rows, p[0].reshape(t, PLE_DIM), w_ple_proj[0].astype(_bf16),
                row(ple_norm_g[0]), w_ple_gate[0].astype(_bf16), row(b_ple_gate[0]),
                row(final_norm_g))
    return out.reshape(b, seq, d)
```

```python
import functools

import jax
import jax.numpy as jnp
import numpy as np
from jax import lax
from jax.experimental import pallas as pl
from jax.experimental.pallas import tpu as pltpu

D_MODEL = 2048
PLE_DIM = 256
SSD_W = 1024
ATTN_W = 1024
SSD_HEADDIM = 64
SSD_HEADS = 16
SSD_GROUPS = 2
HEADS_PER_GROUP = SSD_HEADS // SSD_GROUPS
SSD_STATE = 128
CHUNK = 128
CONV_W = 5
XBC_W = SSD_W + 2 * SSD_GROUPS * SSD_STATE
DIFF_HEAD_DIM = 64
DIFF_HEADS = 8
ROT_DIM = 16
ROPE_THETA = 500000.0
MOE_GROUPS = 8
EXPERTS_PER_GROUP = 8
N_EXPERTS = 64
TOP_K = 2
EXPERT_FF = 512
ROW_BLOCK = 128
EPS = 1e-6
LAM_INIT = 0.2

LANES = 128
DT_PAD = LANES
MAIN_W = SSD_W + XBC_W + 3 * ATTN_W
COL_Z = 0
COL_Q = SSD_W
COL_K = COL_Q + ATTN_W
COL_XBC = COL_K + ATTN_W
COL_V = COL_XBC + XBC_W
VMEM_LIMIT = 56 * 1024 * 1024
NEG_BIG = -1e30

_f32 = jnp.float32
_bf16 = jnp.bfloat16


def _silu(v):
    return v * (1.0 / (1.0 + jnp.exp(-v)))


def _rms(v, g):
    return v * lax.rsqrt(jnp.mean(v * v, axis=-1, keepdims=True) + EPS) * g


IN_TM = 512
IN_TN = 512


def _inproj_kernel(x_ref, g_ref, w_ref, wdt_ref, rc_ref, rp_ref, rm_ref,
                   out_ref, dt_ref, n_scr):
    j = pl.program_id(1)

    @pl.when(j == 0)
    def _():
        n = _rms(x_ref[...], g_ref[...])
        n_scr[...] = n.astype(_bf16)
        dt_ref[...] = jnp.dot(n_scr[...], wdt_ref[...], preferred_element_type=_f32)

    acc = jnp.dot(n_scr[...], w_ref[...], preferred_element_type=_f32)
    q_lo, k_lo, k_hi = COL_Q // IN_TN, COL_K // IN_TN, (COL_K + ATTN_W) // IN_TN
    is_rot = jnp.logical_and(j >= q_lo, j < k_hi)

    @pl.when(is_rot)
    def _():
        scale = jnp.where(j < k_lo, DIFF_HEAD_DIM ** -0.5, 1.0).astype(_f32)
        rc, rp, rm = rc_ref[...], rp_ref[...], rm_ref[...]
        for hb in range(IN_TN // LANES):
            t = acc[:, hb * LANES:(hb + 1) * LANES]
            r = (t * rc + pltpu.roll(t, ROT_DIM // 2, 1) * rp
                 + pltpu.roll(t, LANES - ROT_DIM // 2, 1) * rm)
            out_ref[:, hb * LANES:(hb + 1) * LANES] = (r * scale).astype(_bf16)

    @pl.when(jnp.logical_not(is_rot))
    def _():
        out_ref[...] = acc.astype(_bf16)


def _inproj(x2, g, w_main, w_dt, rc, rp, rm):
    t = x2.shape[0]
    grid = (t // IN_TM, MAIN_W // IN_TN)
    return pl.pallas_call(
        _inproj_kernel,
        grid=grid,
        in_specs=[
            pl.BlockSpec((IN_TM, D_MODEL), lambda i, j: (i, 0)),
            pl.BlockSpec((1, D_MODEL), lambda i, j: (0, 0)),
            pl.BlockSpec((D_MODEL, IN_TN), lambda i, j: (0, j)),
            pl.BlockSpec((D_MODEL, DT_PAD), lambda i, j: (0, 0)),
            pl.BlockSpec((IN_TM, LANES), lambda i, j: (i, 0)),
            pl.BlockSpec((IN_TM, LANES), lambda i, j: (i, 0)),
            pl.BlockSpec((IN_TM, LANES), lambda i, j: (i, 0)),
        ],
        out_specs=[
            pl.BlockSpec((IN_TM, IN_TN), lambda i, j: (i, j)),
            pl.BlockSpec((IN_TM, DT_PAD), lambda i, j: (i, 0)),
        ],
        out_shape=[
            jax.ShapeDtypeStruct((t, MAIN_W), _bf16),
            jax.ShapeDtypeStruct((t, DT_PAD), _f32),
        ],
        scratch_shapes=[pltpu.VMEM((IN_TM, D_MODEL), _bf16)],
        compiler_params=pltpu.CompilerParams(
            dimension_semantics=("arbitrary", "arbitrary"),
            vmem_limit_bytes=VMEM_LIMIT),
        name="inproj",
    )(x2, g, w_main, w_dt, rc, rp, rm)


CONV_HALO = 16


def _ssd_kernel(xbc_ref, dt_ref, cw_ref, cb_ref, dtb_ref, a_ref, dsk_ref,
                y_ref, xpad, xact, dts, yacc, state, seq):
    n_chunks = seq // CHUNK

    zeros_halo = jnp.zeros((CONV_HALO, XBC_W), _bf16)
    xpad[0:CONV_HALO, :] = zeros_halo
    xpad[CONV_HALO + seq:CONV_HALO + seq + CONV_HALO, :] = zeros_halo
    xpad[CONV_HALO:CONV_HALO + seq, :] = xbc_ref[0]
    cw = cw_ref[...]
    cb = cb_ref[...]
    win = CHUNK + 2 * CONV_HALO

    def conv_body(c, carry):
        r0 = pl.multiple_of(c * CHUNK, CHUNK)
        blk = xpad[pl.ds(r0, win), :].astype(_f32)
        acc = blk * cw[CONV_W // 2:CONV_W // 2 + 1, :]
        for k in range(CONV_W):
            sh = CONV_W // 2 - k
            if sh == 0:
                continue
            acc = acc + pltpu.roll(blk, sh % win, 0) * cw[k:k + 1, :]
        v = acc[CONV_HALO:CONV_HALO + CHUNK, :] + cb
        xact[pl.ds(r0, CHUNK), :] = _silu(v).astype(_bf16)
        return carry

    lax.fori_loop(0, n_chunks, conv_body, 0)

    raw = dt_ref[0] + dtb_ref[...]
    dts[...] = jnp.maximum(raw, 0.0) + jnp.log1p(jnp.exp(-jnp.abs(raw)))

    a_row = a_ref[...]
    dsk = dsk_ref[...]
    row_i = lax.broadcasted_iota(jnp.int32, (CHUNK, CHUNK), 0)
    col_i = lax.broadcasted_iota(jnp.int32, (CHUNK, CHUNK), 1)

    def chunk_step(c, reverse):
        r0 = pl.multiple_of(c * CHUNK, CHUNK)
        dtc = dts[pl.ds(r0, CHUNK), :]
        cs = dtc * a_row
        k = 1
        while k < CHUNK:
            if reverse:
                cs = cs + jnp.where(row_i < CHUNK - k, pltpu.roll(cs, CHUNK - k, 0), 0.0)
            else:
                cs = cs + jnp.where(row_i >= k, pltpu.roll(cs, k, 0), 0.0)
            k *= 2
        cs_t = cs.T
        dt_t = dtc.T
        end_col = cs_t[:, 0:1] if reverse else cs_t[:, CHUNK - 1:CHUNK]
        w_t = jnp.exp(end_col - cs_t) * dt_t
        exp_cs = jnp.exp(cs)
        tri = (row_i <= col_i) if reverse else (row_i >= col_i)
        lane0 = SSD_HEADS if reverse else 0
        for g in range(SSD_GROUPS):
            b_g = xact[pl.ds(r0, CHUNK), SSD_W + g * SSD_STATE:SSD_W + (g + 1) * SSD_STATE]
            c_g = xact[pl.ds(r0, CHUNK),
                       SSD_W + (SSD_GROUPS + g) * SSD_STATE:SSD_W + (SSD_GROUPS + g + 1) * SSD_STATE]
            cbm = lax.dot_general(c_g, b_g, (((1,), (1,)), ((), ())),
                                  preferred_element_type=_f32)
            b_t = b_g.astype(_f32).T
            c_f = c_g.astype(_f32)
            for hh in range(HEADS_PER_GROUP):
                h = g * HEADS_PER_GROUP + hh
                ln = lane0 + h
                xs_h = xact[pl.ds(r0, CHUNK), h * SSD_HEADDIM:(h + 1) * SSD_HEADDIM]
                seg = cs[:, ln:ln + 1] - cs_t[ln:ln + 1, :]
                m_h = cbm * jnp.exp(jnp.where(tri, seg, NEG_BIG)) * dt_t[ln:ln + 1, :]
                st = state[h]
                y_h = jnp.dot(m_h.astype(_bf16), xs_h, preferred_element_type=_f32)
                y_h = y_h + jnp.dot((c_f * exp_cs[:, ln:ln + 1]).astype(_bf16),
                                    st.astype(_bf16), preferred_element_type=_f32)
                upd = jnp.dot((b_t * w_t[ln:ln + 1, :]).astype(_bf16), xs_h,
                              preferred_element_type=_f32)
                if reverse:
                    dec = exp_cs[0:1, ln:ln + 1]
                else:
                    dec = exp_cs[CHUNK - 1:CHUNK, ln:ln + 1]
                state[h] = st * dec + upd
                cols = slice(h * SSD_HEADDIM, (h + 1) * SSD_HEADDIM)
                if reverse:
                    y_ref[0, pl.ds(r0, CHUNK), cols] = (
                        yacc[pl.ds(r0, CHUNK), cols] + y_h).astype(_bf16)
                else:
                    yacc[pl.ds(r0, CHUNK), cols] = (
                        y_h + xs_h.astype(_f32) * dsk[:, cols])

    state[...] = jnp.zeros_like(state)

    def fwd_body(c, carry):
        chunk_step(c, False)
        return carry

    lax.fori_loop(0, n_chunks, fwd_body, 0)

    state[...] = jnp.zeros_like(state)

    def bwd_body(i, carry):
        chunk_step(n_chunks - 1 - i, True)
        return carry

    lax.fori_loop(0, n_chunks, bwd_body, 0)


def _ssd(proj3, dt3, conv_w, conv_b, dt_bias, a_row, d_skip_row):
    b, seq, _ = proj3.shape
    kern = functools.partial(_ssd_kernel, seq=seq)
    assert COL_XBC % XBC_W == 0
    return pl.pallas_call(
        kern,
        grid=(b,),
        in_specs=[
            pl.BlockSpec((1, seq, XBC_W), lambda i: (i, 0, COL_XBC // XBC_W)),
            pl.BlockSpec((1, seq, DT_PAD), lambda i: (i, 0, 0)),
            pl.BlockSpec((CONV_W, XBC_W), lambda i: (0, 0)),
            pl.BlockSpec((1, XBC_W), lambda i: (0, 0)),
            pl.BlockSpec((1, DT_PAD), lambda i: (0, 0)),
            pl.BlockSpec((1, DT_PAD), lambda i: (0, 0)),
            pl.BlockSpec((1, SSD_W), lambda i: (0, 0)),
        ],
        out_specs=pl.BlockSpec((1, seq, SSD_W), lambda i: (i, 0, 0)),
        out_shape=jax.ShapeDtypeStruct((b, seq, SSD_W), _bf16),
        scratch_shapes=[
            pltpu.VMEM((seq + 2 * CONV_HALO, XBC_W), _bf16),
            pltpu.VMEM((seq, XBC_W), _bf16),
            pltpu.VMEM((seq, DT_PAD), _f32),
            pltpu.VMEM((seq, SSD_W), _f32),
            pltpu.VMEM((SSD_HEADS, SSD_STATE, SSD_HEADDIM), _f32),
        ],
        compiler_params=pltpu.CompilerParams(
            dimension_semantics=("arbitrary",),
            vmem_limit_bytes=VMEM_LIMIT),
        name="ssd",
    )(proj3, dt3, conv_w, conv_b, dt_bias, a_row, d_skip_row)


ATT_QB = 256


def _attn_kernel(q_ref, k_ref, v_ref, lam_ref, g_ref, o_ref, *, seq):
    lv = lam_ref[...]
    lam = (jnp.exp(jnp.sum(lv[0:1] * lv[1:2], axis=-1, keepdims=True))
           - jnp.exp(jnp.sum(lv[2:3] * lv[3:4], axis=-1, keepdims=True)) + LAM_INIT)
    k = k_ref[0]
    v = v_ref[0]
    g = g_ref[...]
    first = lax.broadcasted_iota(jnp.int32, (1, 2 * DIFF_HEAD_DIM), 1) < DIFF_HEAD_DIM
    nt = (((1,), (1,)), ((), ()))

    def body(i, carry):
        r0 = pl.multiple_of(i * ATT_QB, ATT_QB)
        q = q_ref[0, pl.ds(r0, ATT_QB), :]
        zero = jnp.zeros_like(q)
        s1 = lax.dot_general(jnp.where(first, q, zero), k, nt, preferred_element_type=_f32)
        s2 = lax.dot_general(jnp.where(first, zero, q), k, nt, preferred_element_type=_f32)
        e1 = jnp.exp(s1 - jnp.max(s1, axis=-1, keepdims=True))
        e2 = jnp.exp(s2 - jnp.max(s2, axis=-1, keepdims=True))
        r1 = 1.0 / jnp.sum(e1, axis=-1, keepdims=True)
        r2 = lam / jnp.sum(e2, axis=-1, keepdims=True)
        w = (e1 * r1 - e2 * r2).astype(_bf16)
        o = jnp.dot(w, v, preferred_element_type=_f32)
        o = _rms(o, g) * (1.0 - LAM_INIT)
        o_ref[0, pl.ds(r0, ATT_QB), :] = o.astype(_bf16)
        return carry

    lax.fori_loop(0, seq // ATT_QB, body, 0)


def _attn(proj3, lam_vecs, subln_g):
    b, seq, _ = proj3.shape
    hw = 2 * DIFF_HEAD_DIM
    kern = functools.partial(_attn_kernel, seq=seq)
    return pl.pallas_call(
        kern,
        grid=(b, DIFF_HEADS),
        in_specs=[
            pl.BlockSpec((1, seq, hw), lambda i, h: (i, 0, COL_Q // hw + h)),
            pl.BlockSpec((1, seq, hw), lambda i, h: (i, 0, COL_K // hw + h)),
            pl.BlockSpec((1, seq, hw), lambda i, h: (i, 0, COL_V // hw + h)),
            pl.BlockSpec((4, DIFF_HEAD_DIM), lambda i, h: (0, 0)),
            pl.BlockSpec((1, hw), lambda i, h: (0, 0)),
        ],
        out_specs=pl.BlockSpec((1, seq, hw), lambda i, h: (i, 0, h)),
        out_shape=jax.ShapeDtypeStruct((b, seq, ATTN_W), _bf16),
        compiler_params=pltpu.CompilerParams(
            dimension_semantics=("arbitrary", "arbitrary"),
            vmem_limit_bytes=VMEM_LIMIT),
        name="diffattn",
    )(proj3, proj3, proj3, lam_vecs, subln_g)


OUT_TM = 256
ROUTE_ROWS = 8


def _outproj_kernel(y_ref, z_ref, att_ref, x_ref, w_ref, gs_ref, gf_ref, wr_ref, br_ref,
                    h_ref, route_ref, cnt_ref, gate_ref, cnt_scr):
    @pl.when(pl.program_id(0) == 0)
    def _():
        cnt_scr[...] = jnp.zeros_like(cnt_scr)

    y = y_ref[...].astype(_f32)
    z = z_ref[...].astype(_f32)
    s = _rms(y * _silu(z), gs_ref[...]).astype(_bf16)
    acc = jnp.dot(s, w_ref[0:SSD_W, :], preferred_element_type=_f32)
    acc = acc + jnp.dot(att_ref[...], w_ref[SSD_W:SSD_W + ATTN_W, :], preferred_element_type=_f32)
    h1 = x_ref[...] + acc
    h_ref[...] = h1
    hn = _rms(h1, gf_ref[...])
    logits = lax.dot_general(wr_ref[...], hn, (((1,), (1,)), ((), ())),
                             precision=lax.Precision.HIGHEST,
                             preferred_element_type=_f32) + br_ref[...]
    tm = logits.shape[1]
    iota = lax.broadcasted_iota(jnp.int32, (MOE_GROUPS, tm), 0)

    def first_argmax(val, vmax):
        return jnp.min(jnp.where(val == vmax, iota, MOE_GROUPS), axis=0, keepdims=True)

    gl = logits[0:MOE_GROUPS]
    gmax = jnp.max(gl, axis=0, keepdims=True)
    p_g = 1.0 / jnp.sum(jnp.exp(gl - gmax), axis=0, keepdims=True)
    g_sel = first_argmax(gl, gmax)
    el = jnp.zeros((EXPERTS_PER_GROUP, tm), _f32)
    for gi in range(MOE_GROUPS):
        lo = MOE_GROUPS + gi * EXPERTS_PER_GROUP
        el = jnp.where(g_sel == gi, logits[lo:lo + EXPERTS_PER_GROUP], el)
    ee = jnp.exp(el - jnp.max(el, axis=0, keepdims=True))
    pe = ee / jnp.sum(ee, axis=0, keepdims=True)
    p1 = jnp.max(pe, axis=0, keepdims=True)
    i1 = first_argmax(pe, p1)
    pe2 = jnp.where(iota == i1, -1.0, pe)
    p2 = jnp.max(pe2, axis=0, keepdims=True)
    i2 = first_argmax(pe2, p2)
    den = p1 + p2
    e1 = g_sel * EXPERTS_PER_GROUP + i1
    e2 = g_sel * EXPERTS_PER_GROUP + i2
    iota_e = lax.broadcasted_iota(jnp.int32, (N_EXPERTS, tm), 0)
    oh1 = (iota_e == e1).astype(_f32)
    oh2 = (iota_e == e2).astype(_f32)
    both = oh1 + oh2
    earlier = (lax.broadcasted_iota(jnp.int32, (tm, tm), 0)
               < lax.broadcasted_iota(jnp.int32, (tm, tm), 1)).astype(_bf16)
    before = cnt_scr[:, 0:1] + jnp.dot(both.astype(_bf16), earlier,
                                       preferred_element_type=_f32)
    r1 = jnp.sum(oh1 * before, axis=0, keepdims=True).astype(jnp.int32)
    r2 = jnp.sum(oh2 * before, axis=0, keepdims=True).astype(jnp.int32)
    cnt_scr[...] = cnt_scr[...] + jnp.sum(both, axis=1, keepdims=True)
    cnt_ref[...] = cnt_scr[...].astype(jnp.int32)
    route_ref[...] = jnp.where(iota == 0, e1, jnp.where(iota == 1, e2,
                               jnp.where(iota == 2, r1, jnp.where(iota == 3, r2, 0))))
    g8 = jnp.where(iota == 0, p_g * (p1 / den), jnp.where(iota == 1, p_g * (p2 / den), 0.0))
    gate_ref[...] = jnp.concatenate(
        [g8, jnp.zeros((LANES - ROUTE_ROWS, tm), _f32)], axis=0).T


def _outproj(y2, proj2, att2, x2, w_out, g_ssd, g_ffn, wr_t, br_col):
    t = x2.shape[0]
    row = lambda i: (i, 0)
    fix = lambda i: (0, 0)
    return pl.pallas_call(
        _outproj_kernel,
        grid=(t // OUT_TM,),
        in_specs=[
            pl.BlockSpec((OUT_TM, SSD_W), row),
            pl.BlockSpec((OUT_TM, SSD_W), row),
            pl.BlockSpec((OUT_TM, ATTN_W), row),
            pl.BlockSpec((OUT_TM, D_MODEL), row),
            pl.BlockSpec((SSD_W + ATTN_W, D_MODEL), fix),
            pl.BlockSpec((1, SSD_W), fix),
            pl.BlockSpec((1, D_MODEL), fix),
            pl.BlockSpec((LANES, D_MODEL), fix),
            pl.BlockSpec((LANES, 1), fix),
        ],
        out_specs=[
            pl.BlockSpec((OUT_TM, D_MODEL), row),
            pl.BlockSpec((ROUTE_ROWS, OUT_TM), lambda i: (0, i)),
            pl.BlockSpec((N_EXPERTS, LANES), fix),
            pl.BlockSpec((OUT_TM, LANES), row),
        ],
        out_shape=[
            jax.ShapeDtypeStruct((t, D_MODEL), _f32),
            jax.ShapeDtypeStruct((ROUTE_ROWS, t), jnp.int32),
            jax.ShapeDtypeStruct((N_EXPERTS, LANES), jnp.int32),
            jax.ShapeDtypeStruct((t, LANES), _f32),
        ],
        scratch_shapes=[pltpu.VMEM((N_EXPERTS, LANES), _f32)],
        compiler_params=pltpu.CompilerParams(
            dimension_semantics=("arbitrary",),
            vmem_limit_bytes=VMEM_LIMIT),
        name="outproj_router",
    )(y2, proj2, att2, x2, w_out, g_ssd, g_ffn, wr_t, br_col)


def _n_rows(n_tok):
    n_assign = n_tok * TOP_K
    return (n_assign + N_EXPERTS * (ROW_BLOCK - 1) + ROW_BLOCK - 1) // ROW_BLOCK * ROW_BLOCK


def _prefix_sum(v, axis):
    n = v.shape[axis]
    idx = lax.broadcasted_iota(jnp.int32, v.shape, axis)
    k = 1
    while k < n:
        v = v + jnp.where(idx >= k, pltpu.roll(v, k, axis), 0)
        k *= 2
    return v


def _plan_kernel(route_ref, cnt_ref, pos_ref, blk_ref, meta_ref, *, n_blocks_pad):
    n_tok = route_ref.shape[1]
    cnt_col = jnp.concatenate(
        [cnt_ref[...], jnp.zeros((LANES - N_EXPERTS, LANES), jnp.int32)], axis=0)
    pad_up = lambda c: (c + (ROW_BLOCK - 1)) & (-ROW_BLOCK)
    ends_col = _prefix_sum(pad_up(cnt_col), 0)
    starts_col = (ends_col - pad_up(cnt_col)).astype(_f32)
    cnt_row = cnt_col.astype(_f32).T.astype(jnp.int32)
    ends_row = _prefix_sum(pad_up(cnt_row), 1)
    starts_row = ends_row - pad_up(cnt_row)
    n_used = ends_row[0:1, LANES - 1:LANES] >> (ROW_BLOCK.bit_length() - 1)

    ch = 1024
    iota_e = lax.broadcasted_iota(jnp.int32, (LANES, ch), 0)
    for c0 in range(0, n_tok, ch):
        rt = route_ref[:, c0:c0 + ch]
        s1 = jnp.sum(jnp.where(iota_e == rt[0:1], starts_col[:, 0:1], 0.0), axis=0, keepdims=True)
        s2 = jnp.sum(jnp.where(iota_e == rt[1:2], starts_col[:, 0:1], 0.0), axis=0, keepdims=True)
        p1 = s1.astype(jnp.int32) + rt[2:3]
        p2 = s2.astype(jnp.int32) + rt[3:4]
        sub = lax.broadcasted_iota(jnp.int32, (ROUTE_ROWS, ch), 0)
        pos_ref[:, c0:c0 + ch] = jnp.where(sub == 0, p1, jnp.where(sub == 1, p2, 0))

    blk_start = lax.broadcasted_iota(jnp.int32, (LANES, n_blocks_pad), 1) * ROW_BLOCK
    e_idx = lax.broadcasted_iota(jnp.int32, (LANES, n_blocks_pad), 0)
    real = e_idx < N_EXPERTS
    be = jnp.sum(jnp.where(jnp.logical_and(real, ends_col[:, 0:1] <= blk_start), 1.0, 0.0),
                 axis=0, keepdims=True).astype(jnp.int32)
    be = jnp.minimum(be, N_EXPERTS - 1)
    last = jnp.max(jnp.where(jnp.logical_and(real, cnt_col[:, 0:1] > 0), e_idx.astype(_f32), 0.0),
                   axis=0, keepdims=True).astype(jnp.int32)
    blk_i = lax.broadcasted_iota(jnp.int32, (1, n_blocks_pad), 1)
    be = jnp.where(blk_i < n_used, be, last)
    sub = lax.broadcasted_iota(jnp.int32, (ROUTE_ROWS, n_blocks_pad), 0)
    blk_ref[...] = jnp.where(sub == 0, be, jnp.where(sub == 1, n_used, 0))
    sub = lax.broadcasted_iota(jnp.int32, (ROUTE_ROWS, LANES), 0)
    meta_ref[...] = jnp.where(sub == 0, cnt_row[0:1], jnp.where(sub == 1, starts_row[0:1], 0))


def _plan(route, cnt):
    n_tok = route.shape[1]
    n_blocks_pad = -(-(_n_rows(n_tok) // ROW_BLOCK) // LANES) * LANES
    return pl.pallas_call(
        functools.partial(_plan_kernel, n_blocks_pad=n_blocks_pad),
        out_shape=[
            jax.ShapeDtypeStruct((ROUTE_ROWS, n_tok), jnp.int32),
            jax.ShapeDtypeStruct((ROUTE_ROWS, n_blocks_pad), jnp.int32),
            jax.ShapeDtypeStruct((ROUTE_ROWS, LANES), jnp.int32),
        ],
        compiler_params=pltpu.CompilerParams(vmem_limit_bytes=VMEM_LIMIT),
        name="route_plan",
    )(route, cnt)


DISP_TM = 256


def _dispatch_kernel(pos_ref, cnt_ref, start_ref, nused_ref, h_ref, x_hbm, zrow, sem, zsem,
                     *, n_tok, n_blocks):
    i = pl.program_id(0)
    base = i * DISP_TM

    def body(r, carry):
        for kk in range(TOP_K):
            row = pos_ref[kk * n_tok + base + r]
            pltpu.make_async_copy(h_ref.at[pl.ds(r, 1), :], x_hbm.at[pl.ds(row, 1), :], sem).start()
        return carry

    lax.fori_loop(0, DISP_TM, body, 0)

    @pl.when(i == pl.num_programs(0) - 1)
    def _():
        zrow[...] = jnp.zeros_like(zrow)

        def per_expert(e, n_fill):
            cnt = cnt_ref[e]
            start = start_ref[e]
            end = (cnt + (ROW_BLOCK - 1)) & (-ROW_BLOCK)

            def fill(r, carry):
                pltpu.make_async_copy(zrow.at[pl.ds(0, 1), :],
                                      x_hbm.at[pl.ds(start + r, 1), :], zsem).start()
                return carry

            lax.fori_loop(cnt, end, fill, 0)
            return n_fill + (end - cnt)

        n_fill = lax.fori_loop(0, N_EXPERTS, per_expert, 0)

        def fill_block(blk, carry):
            pltpu.make_async_copy(zrow, x_hbm.at[pl.ds(blk * ROW_BLOCK, ROW_BLOCK), :], zsem).start()
            return carry

        lax.fori_loop(nused_ref[0], n_blocks, fill_block, 0)

        def wait_row(r, carry):
            pltpu.make_async_copy(zrow.at[pl.ds(0, 1), :], x_hbm.at[pl.ds(0, 1), :], zsem).wait()
            return carry

        lax.fori_loop(0, n_fill, wait_row, 0)

        def wait_block(blk, carry):
            pltpu.make_async_copy(zrow, x_hbm.at[pl.ds(0, ROW_BLOCK), :], zsem).wait()
            return carry

        lax.fori_loop(nused_ref[0], n_blocks, wait_block, 0)

    for _ in range(TOP_K):
        pltpu.make_async_copy(h_ref, x_hbm.at[pl.ds(0, DISP_TM), :], sem).wait()


def _dispatch_rows(pos, counts, starts, n_used, h1):
    t = h1.shape[0]
    n_rows = _n_rows(t)
    grid_spec = pltpu.PrefetchScalarGridSpec(
        num_scalar_prefetch=4,
        grid=(t // DISP_TM,),
        in_specs=[pl.BlockSpec((DISP_TM, D_MODEL), lambda i, *_: (i, 0))],
        out_specs=pl.BlockSpec(memory_space=pl.ANY),
        scratch_shapes=[
            pltpu.VMEM((ROW_BLOCK, D_MODEL), _f32),
            pltpu.SemaphoreType.DMA(()),
            pltpu.SemaphoreType.DMA(()),
        ],
    )
    return pl.pallas_call(
        functools.partial(_dispatch_kernel, n_tok=t, n_blocks=n_rows // ROW_BLOCK),
        grid_spec=grid_spec,
        out_shape=jax.ShapeDtypeStruct((n_rows, D_MODEL), _f32),
        compiler_params=pltpu.CompilerParams(
            dimension_semantics=("arbitrary",),
            vmem_limit_bytes=VMEM_LIMIT),
        name="dispatch_rows",
    )(pos, counts, starts, n_used, h1)


def _moe_kernel(be_ref, nused_ref, x_ref, g_ref, wg_ref, wu_ref, wd_ref,
                out_ref, wg_s, wu_s, wd_s):
    i = pl.program_id(0)
    nused = nused_ref[0]

    @pl.when(i < nused)
    def _():
        new_expert = jnp.logical_or(i == 0, be_ref[i] != be_ref[jnp.maximum(i - 1, 0)])

        @pl.when(new_expert)
        def _():
            wg_s[...] = wg_ref[0].astype(_bf16)
            wu_s[...] = wu_ref[0].astype(_bf16)
            wd_s[...] = wd_ref[0].astype(_bf16)

        x = _rms(x_ref[...], g_ref[...]).astype(_bf16)
        gg = jnp.dot(x, wg_s[...], preferred_element_type=_f32)
        uu = jnp.dot(x, wu_s[...], preferred_element_type=_f32)
        a = (_silu(gg) * uu).astype(_bf16)
        out_ref[...] = jnp.dot(a, wd_s[...], preferred_element_type=_f32)

    @pl.when(i >= nused)
    def _():
        out_ref[...] = jnp.zeros_like(out_ref)


def _moe(block_expert, n_used, x_rows, g_ffn, w_gate, w_up, w_down):
    n_rows = x_rows.shape[0]
    n_blocks = n_rows // ROW_BLOCK
    wmap = lambda i, be, nu: (be[i], 0, 0)
    fix = lambda i, be, nu: (0, 0)
    omap = lambda i, be, nu: (i, 0)
    grid_spec = pltpu.PrefetchScalarGridSpec(
        num_scalar_prefetch=2,
        grid=(n_blocks,),
        in_specs=[
            pl.BlockSpec((ROW_BLOCK, D_MODEL), omap),
            pl.BlockSpec((1, D_MODEL), fix),
            pl.BlockSpec((1, D_MODEL, EXPERT_FF), wmap),
            pl.BlockSpec((1, D_MODEL, EXPERT_FF), wmap),
            pl.BlockSpec((1, EXPERT_FF, D_MODEL), wmap),
        ],
        out_specs=pl.BlockSpec((ROW_BLOCK, D_MODEL), omap),
        scratch_shapes=[
            pltpu.VMEM((D_MODEL, EXPERT_FF), _bf16),
            pltpu.VMEM((D_MODEL, EXPERT_FF), _bf16),
            pltpu.VMEM((EXPERT_FF, D_MODEL), _bf16),
        ],
    )
    return pl.pallas_call(
        _moe_kernel,
        grid_spec=grid_spec,
        out_shape=jax.ShapeDtypeStruct((n_rows, D_MODEL), _f32),
        compiler_params=pltpu.CompilerParams(
            dimension_semantics=("arbitrary",),
            vmem_limit_bytes=VMEM_LIMIT),
        name="moe_experts",
    )(block_expert, n_used, x_rows, g_ffn, w_gate, w_up, w_down)


TAIL_TM = 256


def _tail_kernel(pos_ref, h_ref, gate_ref, y_hbm, p_ref, wpp_ref, gp_ref, wpg_ref, bpg_ref,
                 gfin_ref, out_ref, ybuf, sem, *, n_tok):
    i = pl.program_id(0)
    n_steps = pl.num_programs(0)
    slot = i % 2

    def issue(step, sl):
        base = step * TAIL_TM

        def body(r, carry):
            for kk in range(TOP_K):
                row = pos_ref[kk * n_tok + base + r]
                pltpu.make_async_copy(y_hbm.at[pl.ds(row, 1), :],
                                      ybuf.at[sl, kk, pl.ds(r, 1), :], sem.at[sl]).start()
            return carry

        lax.fori_loop(0, TAIL_TM, body, 0)

    @pl.when(i == 0)
    def _():
        issue(0, 0)

    @pl.when(i + 1 < n_steps)
    def _():
        issue(i + 1, 1 - slot)

    for kk in range(TOP_K):
        pltpu.make_async_copy(y_hbm.at[pl.ds(0, TAIL_TM), :], ybuf.at[slot, kk], sem.at[slot]).wait()

    gates = gate_ref[...]
    h2 = h_ref[...] + gates[:, 0:1] * ybuf[slot, 0] + gates[:, 1:2] * ybuf[slot, 1]
    ple = _rms(jnp.dot(p_ref[...].astype(_bf16), wpp_ref[...], preferred_element_type=_f32),
               gp_ref[...])
    lg = jnp.dot(h2.astype(_bf16), wpg_ref[...], preferred_element_type=_f32) + bpg_ref[...]
    h3 = h2 + (1.0 / (1.0 + jnp.exp(-lg))) * ple
    out_ref[...] = _rms(h3, gfin_ref[...])


def _tail(pos, h1, gates, y_rows, p2, w_pp, g_ple, w_pg, b_pg, g_fin):
    t = h1.shape[0]
    row = lambda i, ps: (i, 0)
    fix = lambda i, ps: (0, 0)
    grid_spec = pltpu.PrefetchScalarGridSpec(
        num_scalar_prefetch=1,
        grid=(t // TAIL_TM,),
        in_specs=[
            pl.BlockSpec((TAIL_TM, D_MODEL), row),
            pl.BlockSpec((TAIL_TM, LANES), row),
            pl.BlockSpec(memory_space=pl.ANY),
            pl.BlockSpec((TAIL_TM, PLE_DIM), row),
            pl.BlockSpec((PLE_DIM, D_MODEL), fix),
            pl.BlockSpec((1, D_MODEL), fix),
            pl.BlockSpec((D_MODEL, D_MODEL), fix),
            pl.BlockSpec((1, D_MODEL), fix),
            pl.BlockSpec((1, D_MODEL), fix),
        ],
        out_specs=pl.BlockSpec((TAIL_TM, D_MODEL), row),
        scratch_shapes=[
            pltpu.VMEM((2, TOP_K, TAIL_TM, D_MODEL), _f32),
            pltpu.SemaphoreType.DMA((2,)),
        ],
    )
    return pl.pallas_call(
        functools.partial(_tail_kernel, n_tok=t),
        grid_spec=grid_spec,
        out_shape=jax.ShapeDtypeStruct((t, D_MODEL), _f32),
        compiler_params=pltpu.CompilerParams(
            dimension_semantics=("arbitrary",),
            vmem_limit_bytes=VMEM_LIMIT),
        name="tail",
    )(pos, h1, gates, y_rows, p2, w_pp, g_ple, w_pg, b_pg, g_fin)


def _rope_tables(positions):
    half = ROT_DIM // 2
    inv_freq = ROPE_THETA ** (-jnp.arange(0, ROT_DIM, 2, dtype=_f32) / ROT_DIM)
    ang = positions.astype(_f32).reshape(-1, 1) * inv_freq
    cos, sin = jnp.cos(ang), jnp.sin(ang)
    t = ang.shape[0]
    rest = DIFF_HEAD_DIM - ROT_DIM
    zero8 = jnp.zeros((t, half), _f32)
    comp = lambda a, b, fill: jnp.concatenate([a, b, jnp.full((t, rest), fill, _f32)], axis=1)
    twice = lambda c: jnp.concatenate([c, c], axis=1)
    rc = twice(comp(cos, cos, 1.0))
    rp = twice(comp(zero8, sin, 0.0))
    rm = twice(comp(-sin, zero8, 0.0))
    return rc, rp, rm


def kernel(x, p, positions, norm_mix_g, w_in, conv_w, conv_b, dt_bias_f, dt_bias_b, a_log_f, a_log_b, d_skip, ssd_norm_g, lam_q1, lam_k1, lam_q2, lam_k2, subln_g, w_out, norm_ffn_g, w_route_group, b_route_group, w_route_expert, b_route_expert, w_exp_gate, w_exp_up, w_exp_down, w_ple_proj, ple_norm_g, w_ple_gate, b_ple_gate, final_norm_g):
    b, seq, d = x.shape
    t = b * seq
    x2 = x.reshape(t, d)
    row = lambda v: v.reshape(1, -1).astype(_f32)

    wi = w_in[0]
    o_xbc, o_dt, o_q = SSD_W, SSD_W + XBC_W, SSD_W + XBC_W + 2 * SSD_HEADS
    w_main = jnp.concatenate(
        [wi[:, :o_xbc], wi[:, o_q:o_q + 2 * ATTN_W], wi[:, o_xbc:o_dt], wi[:, o_q + 2 * ATTN_W:]],
        axis=1).astype(_bf16)
    w_dt = jnp.pad(wi[:, o_dt:o_q], ((0, 0), (0, DT_PAD - 2 * SSD_HEADS))).astype(_bf16)
    pad_dt = lambda v: jnp.pad(v, (0, DT_PAD - 2 * SSD_HEADS)).reshape(1, DT_PAD)
    dt_bias = pad_dt(jnp.concatenate([dt_bias_f[0], dt_bias_b[0]]))
    a_row = pad_dt(jnp.concatenate([-jnp.exp(a_log_f[0]), -jnp.exp(a_log_b[0])]))
    d_skip_row = jnp.repeat(d_skip[0], SSD_HEADDIM).reshape(1, SSD_W)
    lam_vecs = jnp.stack([lam_q1[0], lam_k1[0], lam_q2[0], lam_k2[0]])
    n_route = MOE_GROUPS + N_EXPERTS
    wr_t = jnp.pad(jnp.concatenate([w_route_group[0], w_route_expert[0]], axis=1).T,
                   ((0, LANES - n_route), (0, 0)))
    br_col = jnp.pad(jnp.concatenate([b_route_group[0], b_route_expert[0]]),
                     (0, LANES - n_route)).reshape(LANES, 1)
    rc, rp, rm = _rope_tables(positions)

    proj, dt = _inproj(x2, row(norm_mix_g[0]), w_main, w_dt, rc, rp, rm)
    proj3 = proj.reshape(b, seq, MAIN_W)
    y_ssd = _ssd(proj3, dt.reshape(b, seq, DT_PAD), conv_w[0], row(conv_b[0]),
                 dt_bias, a_row, d_skip_row)
    att = _attn(proj3, lam_vecs, row(subln_g[0]))
    h1, route, cnt, gates = _outproj(
        y_ssd.reshape(t, SSD_W), proj, att.reshape(t, ATTN_W), x2, w_out[0].astype(_bf16),
        row(ssd_norm_g[0]), row(norm_ffn_g[0]), wr_t, br_col)
    pos8, blk8, meta8 = _plan(route, cnt)
    pos = pos8[:TOP_K].reshape(-1)
    block_expert = blk8[0, :_n_rows(t) // ROW_BLOCK]
    n_used = blk8[1, :1]
    x_rows = _dispatch_rows(pos, meta8[0, :N_EXPERTS], meta8[1, :N_EXPERTS], n_used, h1)
    y_rows = _moe(block_expert, n_used, x_rows, row(norm_ffn_g[0]),
                  w_exp_gate[0], w_exp_up[0], w_exp_down[0])
    out = _tail(pos, h1, gates, y_rows, p[0].reshape(t, PLE_DIM), w_ple_proj[0].astype(_bf16),
                row(ple_norm_g[0]), w_ple_gate[0].astype(_bf16), row(b_ple_gate[0]),
                row(final_norm_g))
    return out.reshape(b, seq, d)
```

```python
import functools

import jax
import jax.numpy as jnp
import numpy as np
from jax import lax
from jax.experimental import pallas as pl
from jax.experimental.pallas import tpu as pltpu

D_MODEL = 2048
PLE_DIM = 256
SSD_W = 1024
ATTN_W = 1024
SSD_HEADDIM = 64
SSD_HEADS = 16
SSD_GROUPS = 2
HEADS_PER_GROUP = SSD_HEADS // SSD_GROUPS
SSD_STATE = 128
CHUNK = 128
CONV_W = 5
XBC_W = SSD_W + 2 * SSD_GROUPS * SSD_STATE
DIFF_HEAD_DIM = 64
DIFF_HEADS = 8
ROT_DIM = 16
ROPE_THETA = 500000.0
MOE_GROUPS = 8
EXPERTS_PER_GROUP = 8
N_EXPERTS = 64
TOP_K = 2
EXPERT_FF = 512
ROW_BLOCK = 128
EPS = 1e-6
LAM_INIT = 0.2

LANES = 128
DT_PAD = LANES
MAIN_W = SSD_W + XBC_W + 3 * ATTN_W
COL_Z = 0
COL_Q = SSD_W
COL_K = COL_Q + ATTN_W
COL_XBC = COL_K + ATTN_W
COL_V = COL_XBC + XBC_W
VMEM_LIMIT = 56 * 1024 * 1024
NEG_BIG = -1e30
Q_SCALE = float(DIFF_HEAD_DIM ** -0.5 * np.log2(np.e))

_f32 = jnp.float32
_bf16 = jnp.bfloat16


def _silu(v):
    return v * (1.0 / (1.0 + jnp.exp(-v)))


def _rms(v, g):
    return v * lax.rsqrt(jnp.mean(v * v, axis=-1, keepdims=True) + EPS) * g


_HI16 = np.uint32(0xFFFF0000)


def _pack_bf16_pair(lo, hi):
    lo_w = lax.bitcast_convert_type(lo.astype(_bf16).astype(_f32), jnp.uint32) >> 16
    hi_w = lax.bitcast_convert_type(hi.astype(_bf16).astype(_f32), jnp.uint32) & _HI16
    return lo_w | hi_w


def _unpack_bf16_pair(w):
    lo = lax.bitcast_convert_type(w << 16, _f32)
    hi = lax.bitcast_convert_type(w & _HI16, _f32)
    return lo, hi


IN_TM = 512
IN_TN = 512


def _inproj_kernel(x_ref, g_ref, w_ref, wdt_ref, rc_ref, rp_ref, rm_ref,
                   out_ref, dt_ref, n_scr):
    j = pl.program_id(1)

    @pl.when(j == 0)
    def _():
        n = _rms(x_ref[...], g_ref[...])
        n_scr[...] = n.astype(_bf16)
        dt_ref[...] = jnp.dot(n_scr[...], wdt_ref[...], preferred_element_type=_f32)

    acc = jnp.dot(n_scr[...], w_ref[...], preferred_element_type=_f32)
    q_lo, k_lo, k_hi = COL_Q // IN_TN, COL_K // IN_TN, (COL_K + ATTN_W) // IN_TN
    is_rot = jnp.logical_and(j >= q_lo, j < k_hi)

    @pl.when(is_rot)
    def _():
        scale = jnp.where(j < k_lo, Q_SCALE, 1.0).astype(_f32)
        rc, rp, rm = rc_ref[...], rp_ref[...], rm_ref[...]
        for hb in range(IN_TN // LANES):
            t = acc[:, hb * LANES:(hb + 1) * LANES]
            r = (t * rc + pltpu.roll(t, ROT_DIM // 2, 1) * rp
                 + pltpu.roll(t, LANES - ROT_DIM // 2, 1) * rm)
            out_ref[:, hb * LANES:(hb + 1) * LANES] = (r * scale).astype(_bf16)

    @pl.when(jnp.logical_not(is_rot))
    def _():
        out_ref[...] = acc.astype(_bf16)


def _inproj(x2, g, w_main, w_dt, rc, rp, rm):
    t = x2.shape[0]
    grid = (t // IN_TM, MAIN_W // IN_TN)
    return pl.pallas_call(
        _inproj_kernel,
        grid=grid,
        in_specs=[
            pl.BlockSpec((IN_TM, D_MODEL), lambda i, j: (i, 0)),
            pl.BlockSpec((1, D_MODEL), lambda i, j: (0, 0)),
            pl.BlockSpec((D_MODEL, IN_TN), lambda i, j: (0, j)),
            pl.BlockSpec((D_MODEL, DT_PAD), lambda i, j: (0, 0)),
            pl.BlockSpec((IN_TM, LANES), lambda i, j: (i, 0)),
            pl.BlockSpec((IN_TM, LANES), lambda i, j: (i, 0)),
            pl.BlockSpec((IN_TM, LANES), lambda i, j: (i, 0)),
        ],
        out_specs=[
            pl.BlockSpec((IN_TM, IN_TN), lambda i, j: (i, j)),
            pl.BlockSpec((IN_TM, DT_PAD), lambda i, j: (i, 0)),
        ],
        out_shape=[
            jax.ShapeDtypeStruct((t, MAIN_W), _bf16),
            jax.ShapeDtypeStruct((t, DT_PAD), _f32),
        ],
        scratch_shapes=[pltpu.VMEM((IN_TM, D_MODEL), _bf16)],
        compiler_params=pltpu.CompilerParams(
            dimension_semantics=("arbitrary", "arbitrary"),
            vmem_limit_bytes=VMEM_LIMIT),
        name="inproj",
    )(x2, g, w_main, w_dt, rc, rp, rm)


CONV_HALO = 16


def _ssd_kernel(xbc_ref, dt_ref, cw_ref, cb_ref, dtb_ref, a_ref, dsk_ref,
                y_ref, xpad, xact, dts, yacc, state, seq):
    n_chunks = seq // CHUNK

    zeros_halo = jnp.zeros((CONV_HALO, XBC_W), _bf16)
    xpad[0:CONV_HALO, :] = zeros_halo
    xpad[CONV_HALO + seq:CONV_HALO + seq + CONV_HALO, :] = zeros_halo
    xpad[CONV_HALO:CONV_HALO + seq, :] = xbc_ref[0]
    cw = cw_ref[...]
    cb = cb_ref[...]
    win = CHUNK + 2 * CONV_HALO

    def conv_body(c, carry):
        r0 = pl.multiple_of(c * CHUNK, CHUNK)
        blk = xpad[pl.ds(r0, win), :].astype(_f32)
        acc = blk * cw[CONV_W // 2:CONV_W // 2 + 1, :]
        for k in range(CONV_W):
            sh = CONV_W // 2 - k
            if sh == 0:
                continue
            acc = acc + pltpu.roll(blk, sh % win, 0) * cw[k:k + 1, :]
        v = acc[CONV_HALO:CONV_HALO + CHUNK, :] + cb
        xact[pl.ds(r0, CHUNK), :] = _silu(v).astype(_bf16)
        return carry

    lax.fori_loop(0, n_chunks, conv_body, 0)

    raw = dt_ref[0] + dtb_ref[...]
    dts[...] = jnp.maximum(raw, 0.0) + jnp.log1p(jnp.exp(-jnp.abs(raw)))

    a_row = a_ref[...]
    dsk = dsk_ref[...]
    row_i = lax.broadcasted_iota(jnp.int32, (CHUNK, CHUNK), 0)
    col_i = lax.broadcasted_iota(jnp.int32, (CHUNK, CHUNK), 1)

    def chunk_step(c, reverse):
        r0 = pl.multiple_of(c * CHUNK, CHUNK)
        dtc = dts[pl.ds(r0, CHUNK), :]
        cs = dtc * a_row
        k = 1
        while k < CHUNK:
            if reverse:
                cs = cs + jnp.where(row_i < CHUNK - k, pltpu.roll(cs, CHUNK - k, 0), 0.0)
            else:
                cs = cs + jnp.where(row_i >= k, pltpu.roll(cs, k, 0), 0.0)
            k *= 2
        cs_t = cs.T
        dt_t = dtc.T
        end_col = cs_t[:, 0:1] if reverse else cs_t[:, CHUNK - 1:CHUNK]
        w_t = jnp.exp(end_col - cs_t) * dt_t
        exp_cs = jnp.exp(cs)
        tri = (row_i <= col_i) if reverse else (row_i >= col_i)
        lane0 = SSD_HEADS if reverse else 0
        for g in range(SSD_GROUPS):
            b_g = xact[pl.ds(r0, CHUNK), SSD_W + g * SSD_STATE:SSD_W + (g + 1) * SSD_STATE]
            c_g = xact[pl.ds(r0, CHUNK),
                       SSD_W + (SSD_GROUPS + g) * SSD_STATE:SSD_W + (SSD_GROUPS + g + 1) * SSD_STATE]
            cbm = lax.dot_general(c_g, b_g, (((1,), (1,)), ((), ())),
                                  preferred_element_type=_f32)
            b_t = b_g.astype(_f32).T
            c_f = c_g.astype(_f32)
            for hh in range(HEADS_PER_GROUP):
                h = g * HEADS_PER_GROUP + hh
                ln = lane0 + h
                xs_h = xact[pl.ds(r0, CHUNK), h * SSD_HEADDIM:(h + 1) * SSD_HEADDIM]
                seg = cs[:, ln:ln + 1] - cs_t[ln:ln + 1, :]
                m_h = cbm * jnp.exp(jnp.where(tri, seg, NEG_BIG)) * dt_t[ln:ln + 1, :]
                st = state[h]
                y_h = jnp.dot(m_h.astype(_bf16), xs_h, preferred_element_type=_f32)
                y_h = y_h + jnp.dot((c_f * exp_cs[:, ln:ln + 1]).astype(_bf16),
                                    st.astype(_bf16), preferred_element_type=_f32)
                upd = jnp.dot((b_t * w_t[ln:ln + 1, :]).astype(_bf16), xs_h,
                              preferred_element_type=_f32)
                if reverse:
                    dec = exp_cs[0:1, ln:ln + 1]
                else:
                    dec = exp_cs[CHUNK - 1:CHUNK, ln:ln + 1]
                state[h] = st * dec + upd
                cols = slice(h * SSD_HEADDIM, (h + 1) * SSD_HEADDIM)
                if reverse:
                    y_ref[0, pl.ds(r0, CHUNK), cols] = (
                        yacc[pl.ds(r0, CHUNK), cols] + y_h).astype(_bf16)
                else:
                    yacc[pl.ds(r0, CHUNK), cols] = (
                        y_h + xs_h.astype(_f32) * dsk[:, cols])

    state[...] = jnp.zeros_like(state)

    def fwd_body(c, carry):
        chunk_step(c, False)
        return carry

    lax.fori_loop(0, n_chunks, fwd_body, 0)

    state[...] = jnp.zeros_like(state)

    def bwd_body(i, carry):
        chunk_step(n_chunks - 1 - i, True)
        return carry

    lax.fori_loop(0, n_chunks, bwd_body, 0)


def _ssd(proj3, dt3, conv_w, conv_b, dt_bias, a_row, d_skip_row):
    b, seq, _ = proj3.shape
    kern = functools.partial(_ssd_kernel, seq=seq)
    assert COL_XBC % XBC_W == 0
    return pl.pallas_call(
        kern,
        grid=(b,),
        in_specs=[
            pl.BlockSpec((1, seq, XBC_W), lambda i: (i, 0, COL_XBC // XBC_W)),
            pl.BlockSpec((1, seq, DT_PAD), lambda i: (i, 0, 0)),
            pl.BlockSpec((CONV_W, XBC_W), lambda i: (0, 0)),
            pl.BlockSpec((1, XBC_W), lambda i: (0, 0)),
            pl.BlockSpec((1, DT_PAD), lambda i: (0, 0)),
            pl.BlockSpec((1, DT_PAD), lambda i: (0, 0)),
            pl.BlockSpec((1, SSD_W), lambda i: (0, 0)),
        ],
        out_specs=pl.BlockSpec((1, seq, SSD_W), lambda i: (i, 0, 0)),
        out_shape=jax.ShapeDtypeStruct((b, seq, SSD_W), _bf16),
        scratch_shapes=[
            pltpu.VMEM((seq + 2 * CONV_HALO, XBC_W), _bf16),
            pltpu.VMEM((seq, XBC_W), _bf16),
            pltpu.VMEM((seq, DT_PAD), _f32),
            pltpu.VMEM((seq, SSD_W), _f32),
            pltpu.VMEM((SSD_HEADS, SSD_STATE, SSD_HEADDIM), _f32),
        ],
        compiler_params=pltpu.CompilerParams(
            dimension_semantics=("arbitrary",),
            vmem_limit_bytes=VMEM_LIMIT),
        name="ssd",
    )(proj3, dt3, conv_w, conv_b, dt_bias, a_row, d_skip_row)


ATT_QB = 256
ATT_UNROLL = 2


def _attn_kernel(q_ref, k_ref, v_ref, lam_ref, g_ref, o_ref, v1_scr, *, seq):
    hw = 2 * DIFF_HEAD_DIM
    lv = lam_ref[...]
    lam = (jnp.exp(jnp.sum(lv[0:1] * lv[1:2], axis=-1, keepdims=True))
           - jnp.exp(jnp.sum(lv[2:3] * lv[3:4], axis=-1, keepdims=True)) + LAM_INIT)
    k = k_ref[0]
    g = g_ref[...]
    v1_scr[:, 0:hw] = v_ref[0]
    v1_scr[:, hw:2 * hw] = (lax.broadcasted_iota(jnp.int32, (seq, hw), 1) == 0).astype(_bf16)
    first = lax.broadcasted_iota(jnp.int32, (1, hw), 1) < DIFF_HEAD_DIM
    nt = (((1,), (1,)), ((), ()))

    def chain(r0):
        q = q_ref[0, pl.ds(r0, ATT_QB), :]
        zero = jnp.zeros_like(q)
        qz = jnp.concatenate([jnp.where(first, q, zero), jnp.where(first, zero, q)], axis=0)
        s = lax.dot_general(qz, k, nt, preferred_element_type=_f32)
        e = jnp.exp2(s - jnp.max(s, axis=-1, keepdims=True)).astype(_bf16)
        ov = jnp.dot(e, v1_scr[...], preferred_element_type=_f32)
        o1, l1 = ov[0:ATT_QB, 0:hw], ov[0:ATT_QB, hw:hw + 1]
        o2, l2 = ov[ATT_QB:, 0:hw], ov[ATT_QB:, hw:hw + 1]
        o = o1 / l1 - (lam / l2) * o2
        o = _rms(o, g) * (1.0 - LAM_INIT)
        o_ref[0, pl.ds(r0, ATT_QB), :] = o.astype(_bf16)

    step = ATT_QB * ATT_UNROLL

    def body(i, carry):
        for u in range(ATT_UNROLL):
            chain(pl.multiple_of(i * step + u * ATT_QB, ATT_QB))
        return carry

    lax.fori_loop(0, seq // step, body, 0)


def _attn(proj3, lam_vecs, subln_g):
    b, seq, _ = proj3.shape
    hw = 2 * DIFF_HEAD_DIM
    kern = functools.partial(_attn_kernel, seq=seq)
    return pl.pallas_call(
        kern,
        grid=(b, DIFF_HEADS),
        in_specs=[
            pl.BlockSpec((1, seq, hw), lambda i, h: (i, 0, COL_Q // hw + h)),
            pl.BlockSpec((1, seq, hw), lambda i, h: (i, 0, COL_K // hw + h)),
            pl.BlockSpec((1, seq, hw), lambda i, h: (i, 0, COL_V // hw + h)),
            pl.BlockSpec((4, DIFF_HEAD_DIM), lambda i, h: (0, 0)),
            pl.BlockSpec((1, hw), lambda i, h: (0, 0)),
        ],
        out_specs=pl.BlockSpec((1, seq, hw), lambda i, h: (i, 0, h)),
        out_shape=jax.ShapeDtypeStruct((b, seq, ATTN_W), _bf16),
        scratch_shapes=[pltpu.VMEM((seq, 2 * hw), _bf16)],
        compiler_params=pltpu.CompilerParams(
            dimension_semantics=("arbitrary", "arbitrary"),
            vmem_limit_bytes=VMEM_LIMIT),
        name="diffattn",
    )(proj3, proj3, proj3, lam_vecs, subln_g)


OUT_TM = 256
ROUTE_ROWS = 8


def _outproj_kernel(y_ref, z_ref, att_ref, x_ref, w_ref, gs_ref, gf_ref, wr_ref, br_ref,
                    h_ref, hn_ref, route_ref, cnt_ref, gate_ref, cnt_scr):
    @pl.when(pl.program_id(0) == 0)
    def _():
        cnt_scr[...] = jnp.zeros_like(cnt_scr)

    y = y_ref[...].astype(_f32)
    z = z_ref[...].astype(_f32)
    s = _rms(y * _silu(z), gs_ref[...]).astype(_bf16)
    acc = jnp.dot(s, w_ref[0:SSD_W, :], preferred_element_type=_f32)
    acc = acc + jnp.dot(att_ref[...], w_ref[SSD_W:SSD_W + ATTN_W, :], preferred_element_type=_f32)
    h1 = x_ref[...] + acc
    h_ref[...] = h1
    hn = _rms(h1, gf_ref[...])
    half = D_MODEL // 2
    hn_ref[...] = _pack_bf16_pair(hn[:, 0:half], hn[:, half:D_MODEL])
    hn_hi = hn.astype(_bf16)
    hn_lo = (hn - hn_hi.astype(_f32)).astype(_bf16)
    a = jnp.dot(hn_hi, wr_ref[...], preferred_element_type=_f32)
    bb = jnp.dot(hn_lo, wr_ref[:, 0:LANES], preferred_element_type=_f32)
    logits = (a[:, 0:LANES] + a[:, LANES:2 * LANES] + bb + br_ref[...]).T
    tm = logits.shape[1]
    iota = lax.broadcasted_iota(jnp.int32, (MOE_GROUPS, tm), 0)

    def first_argmax(val, vmax):
        return jnp.min(jnp.where(val == vmax, iota, MOE_GROUPS), axis=0, keepdims=True)

    gl = logits[0:MOE_GROUPS]
    gmax = jnp.max(gl, axis=0, keepdims=True)
    p_g = 1.0 / jnp.sum(jnp.exp(gl - gmax), axis=0, keepdims=True)
    g_sel = first_argmax(gl, gmax)
    el = jnp.zeros((EXPERTS_PER_GROUP, tm), _f32)
    for gi in range(MOE_GROUPS):
        lo = MOE_GROUPS + gi * EXPERTS_PER_GROUP
        el = jnp.where(g_sel == gi, logits[lo:lo + EXPERTS_PER_GROUP], el)
    ee = jnp.exp(el - jnp.max(el, axis=0, keepdims=True))
    pe = ee / jnp.sum(ee, axis=0, keepdims=True)
    p1 = jnp.max(pe, axis=0, keepdims=True)
    i1 = first_argmax(pe, p1)
    pe2 = jnp.where(iota == i1, -1.0, pe)
    p2 = jnp.max(pe2, axis=0, keepdims=True)
    i2 = first_argmax(pe2, p2)
    den = p1 + p2
    e1 = g_sel * EXPERTS_PER_GROUP + i1
    e2 = g_sel * EXPERTS_PER_GROUP + i2
    iota_e = lax.broadcasted_iota(jnp.int32, (N_EXPERTS, tm), 0)
    oh1 = (iota_e == e1).astype(_f32)
    oh2 = (iota_e == e2).astype(_f32)
    both = oh1 + oh2
    earlier = (lax.broadcasted_iota(jnp.int32, (tm, tm), 0)
               < lax.broadcasted_iota(jnp.int32, (tm, tm), 1)).astype(_bf16)
    before = cnt_scr[:, 0:1] + jnp.dot(both.astype(_bf16), earlier,
                                       preferred_element_type=_f32)
    r1 = jnp.sum(oh1 * before, axis=0, keepdims=True).astype(jnp.int32)
    r2 = jnp.sum(oh2 * before, axis=0, keepdims=True).astype(jnp.int32)
    cnt_scr[...] = cnt_scr[...] + jnp.sum(both, axis=1, keepdims=True)
    cnt_ref[...] = cnt_scr[...].astype(jnp.int32)
    route_ref[...] = jnp.where(iota == 0, e1, jnp.where(iota == 1, e2,
                               jnp.where(iota == 2, r1, jnp.where(iota == 3, r2, 0))))
    g8 = jnp.where(iota == 0, p_g * (p1 / den), jnp.where(iota == 1, p_g * (p2 / den), 0.0))
    gate_ref[...] = jnp.concatenate(
        [g8, jnp.zeros((LANES - ROUTE_ROWS, tm), _f32)], axis=0).T


def _outproj(y2, proj2, att2, x2, w_out, g_ssd, g_ffn, wr_t, br_col):
    t = x2.shape[0]
    row = lambda i: (i, 0)
    fix = lambda i: (0, 0)
    return pl.pallas_call(
        _outproj_kernel,
        grid=(t // OUT_TM,),
        in_specs=[
            pl.BlockSpec((OUT_TM, SSD_W), row),
            pl.BlockSpec((OUT_TM, SSD_W), row),
            pl.BlockSpec((OUT_TM, ATTN_W), row),
            pl.BlockSpec((OUT_TM, D_MODEL), row),
            pl.BlockSpec((SSD_W + ATTN_W, D_MODEL), fix),
            pl.BlockSpec((1, SSD_W), fix),
            pl.BlockSpec((1, D_MODEL), fix),
            pl.BlockSpec((D_MODEL, 2 * LANES), fix),
            pl.BlockSpec((1, LANES), fix),
        ],
        out_specs=[
            pl.BlockSpec((OUT_TM, D_MODEL), row),
            pl.BlockSpec((OUT_TM, D_MODEL // 2), row),
            pl.BlockSpec((ROUTE_ROWS, OUT_TM), lambda i: (0, i)),
            pl.BlockSpec((N_EXPERTS, LANES), fix),
            pl.BlockSpec((OUT_TM, LANES), row),
        ],
        out_shape=[
            jax.ShapeDtypeStruct((t, D_MODEL), _f32),
            jax.ShapeDtypeStruct((t, D_MODEL // 2), jnp.uint32),
            jax.ShapeDtypeStruct((ROUTE_ROWS, t), jnp.int32),
            jax.ShapeDtypeStruct((N_EXPERTS, LANES), jnp.int32),
            jax.ShapeDtypeStruct((t, LANES), _f32),
        ],
        scratch_shapes=[pltpu.VMEM((N_EXPERTS, LANES), _f32)],
        compiler_params=pltpu.CompilerParams(
            dimension_semantics=("arbitrary",),
            vmem_limit_bytes=VMEM_LIMIT),
        name="outproj_router",
    )(y2, proj2, att2, x2, w_out, g_ssd, g_ffn, wr_t, br_col)


def _n_rows(n_tok):
    n_assign = n_tok * TOP_K
    return (n_assign + N_EXPERTS * (ROW_BLOCK - 1) + ROW_BLOCK - 1) // ROW_BLOCK * ROW_BLOCK


def _prefix_sum(v, axis):
    n = v.shape[axis]
    idx = lax.broadcasted_iota(jnp.int32, v.shape, axis)
    k = 1
    while k < n:
        v = v + jnp.where(idx >= k, pltpu.roll(v, k, axis), 0)
        k *= 2
    return v


def _plan_kernel(route_ref, cnt_ref, pos_ref, blk_ref, meta_ref, *, n_blocks_pad):
    n_tok = route_ref.shape[1]
    cnt_col = jnp.concatenate(
        [cnt_ref[...], jnp.zeros((LANES - N_EXPERTS, LANES), jnp.int32)], axis=0)
    pad_up = lambda c: (c + (ROW_BLOCK - 1)) & (-ROW_BLOCK)
    ends_col = _prefix_sum(pad_up(cnt_col), 0)
    starts_col = (ends_col - pad_up(cnt_col)).astype(_f32)
    cnt_row = cnt_col.astype(_f32).T.astype(jnp.int32)
    ends_row = _prefix_sum(pad_up(cnt_row), 1)
    starts_row = ends_row - pad_up(cnt_row)
    n_used = ends_row[0:1, LANES - 1:LANES] >> (ROW_BLOCK.bit_length() - 1)

    ch = 1024
    iota_e = lax.broadcasted_iota(jnp.int32, (LANES, ch), 0)
    for c0 in range(0, n_tok, ch):
        rt = route_ref[:, c0:c0 + ch]
        s1 = jnp.sum(jnp.where(iota_e == rt[0:1], starts_col[:, 0:1], 0.0), axis=0, keepdims=True)
        s2 = jnp.sum(jnp.where(iota_e == rt[1:2], starts_col[:, 0:1], 0.0), axis=0, keepdims=True)
        p1 = s1.astype(jnp.int32) + rt[2:3]
        p2 = s2.astype(jnp.int32) + rt[3:4]
        sub = lax.broadcasted_iota(jnp.int32, (ROUTE_ROWS, ch), 0)
        pos_ref[:, c0:c0 + ch] = jnp.where(sub == 0, p1, jnp.where(sub == 1, p2, 0))

    blk_start = lax.broadcasted_iota(jnp.int32, (LANES, n_blocks_pad), 1) * ROW_BLOCK
    e_idx = lax.broadcasted_iota(jnp.int32, (LANES, n_blocks_pad), 0)
    real = e_idx < N_EXPERTS
    be = jnp.sum(jnp.where(jnp.logical_and(real, ends_col[:, 0:1] <= blk_start), 1.0, 0.0),
                 axis=0, keepdims=True).astype(jnp.int32)
    be = jnp.minimum(be, N_EXPERTS - 1)
    last = jnp.max(jnp.where(jnp.logical_and(real, cnt_col[:, 0:1] > 0), e_idx.astype(_f32), 0.0),
                   axis=0, keepdims=True).astype(jnp.int32)
    blk_i = lax.broadcasted_iota(jnp.int32, (1, n_blocks_pad), 1)
    be = jnp.where(blk_i < n_used, be, last)
    sub = lax.broadcasted_iota(jnp.int32, (ROUTE_ROWS, n_blocks_pad), 0)
    blk_ref[...] = jnp.where(sub == 0, be, jnp.where(sub == 1, n_used, 0))
    sub = lax.broadcasted_iota(jnp.int32, (ROUTE_ROWS, LANES), 0)
    meta_ref[...] = jnp.where(sub == 0, cnt_row[0:1], jnp.where(sub == 1, starts_row[0:1], 0))


def _plan(route, cnt):
    n_tok = route.shape[1]
    n_blocks_pad = -(-(_n_rows(n_tok) // ROW_BLOCK) // LANES) * LANES
    return pl.pallas_call(
        functools.partial(_plan_kernel, n_blocks_pad=n_blocks_pad),
        out_shape=[
            jax.ShapeDtypeStruct((ROUTE_ROWS, n_tok), jnp.int32),
            jax.ShapeDtypeStruct((ROUTE_ROWS, n_blocks_pad), jnp.int32),
            jax.ShapeDtypeStruct((ROUTE_ROWS, LANES), jnp.int32),
        ],
        compiler_params=pltpu.CompilerParams(vmem_limit_bytes=VMEM_LIMIT),
        name="route_plan",
    )(route, cnt)


DISP_TM = 256


def _dispatch_kernel(pos_ref, cnt_ref, start_ref, nused_ref, h_ref, x_hbm, zrow, sem, zsem,
                     *, n_tok, n_blocks):
    i = pl.program_id(0)
    base = i * DISP_TM

    def body(r, carry):
        for kk in range(TOP_K):
            row = pos_ref[kk * n_tok + base + r]
            pltpu.make_async_copy(h_ref.at[pl.ds(r, 1), :], x_hbm.at[pl.ds(row, 1), :], sem).start()
        return carry

    lax.fori_loop(0, DISP_TM, body, 0)

    @pl.when(i == pl.num_programs(0) - 1)
    def _():
        zrow[...] = jnp.zeros_like(zrow)

        def per_expert(e, n_fill):
            cnt = cnt_ref[e]
            start = start_ref[e]
            end = (cnt + (ROW_BLOCK - 1)) & (-ROW_BLOCK)

            def fill(r, carry):
                pltpu.make_async_copy(zrow.at[pl.ds(0, 1), :],
                                      x_hbm.at[pl.ds(start + r, 1), :], zsem).start()
                return carry

            lax.fori_loop(cnt, end, fill, 0)
            return n_fill + (end - cnt)

        n_fill = lax.fori_loop(0, N_EXPERTS, per_expert, 0)

        def fill_block(blk, carry):
            pltpu.make_async_copy(zrow, x_hbm.at[pl.ds(blk * ROW_BLOCK, ROW_BLOCK), :], zsem).start()
            return carry

        lax.fori_loop(nused_ref[0], n_blocks, fill_block, 0)

        def wait_row(r, carry):
            pltpu.make_async_copy(zrow.at[pl.ds(0, 1), :], x_hbm.at[pl.ds(0, 1), :], zsem).wait()
            return carry

        lax.fori_loop(0, n_fill, wait_row, 0)

        def wait_block(blk, carry):
            pltpu.make_async_copy(zrow, x_hbm.at[pl.ds(0, ROW_BLOCK), :], zsem).wait()
            return carry

        lax.fori_loop(nused_ref[0], n_blocks, wait_block, 0)

    for _ in range(TOP_K):
        pltpu.make_async_copy(h_ref, x_hbm.at[pl.ds(0, DISP_TM), :], sem).wait()


def _dispatch_rows(pos, counts, starts, n_used, hn_packed):
    t, width = hn_packed.shape
    n_rows = _n_rows(t)
    grid_spec = pltpu.PrefetchScalarGridSpec(
        num_scalar_prefetch=4,
        grid=(t // DISP_TM,),
        in_specs=[pl.BlockSpec((DISP_TM, width), lambda i, *_: (i, 0))],
        out_specs=pl.BlockSpec(memory_space=pl.ANY),
        scratch_shapes=[
            pltpu.VMEM((ROW_BLOCK, width), hn_packed.dtype),
            pltpu.SemaphoreType.DMA(()),
            pltpu.SemaphoreType.DMA(()),
        ],
    )
    return pl.pallas_call(
        functools.partial(_dispatch_kernel, n_tok=t, n_blocks=n_rows // ROW_BLOCK),
        grid_spec=grid_spec,
        out_shape=jax.ShapeDtypeStruct((n_rows, width), hn_packed.dtype),
        compiler_params=pltpu.CompilerParams(
            dimension_semantics=("arbitrary",),
            vmem_limit_bytes=VMEM_LIMIT),
        name="dispatch_rows",
    )(pos, counts, starts, n_used, hn_packed)


def _moe_kernel(be_ref, nused_ref, x_ref, wg_ref, wu_ref, wd_ref,
                out_ref, wg_s, wu_s, wd_s):
    i = pl.program_id(0)
    nused = nused_ref[0]
    half = D_MODEL // 2

    @pl.when(i < nused)
    def _():
        new_expert = jnp.logical_or(i == 0, be_ref[i] != be_ref[jnp.maximum(i - 1, 0)])

        @pl.when(new_expert)
        def _():
            wg_s[...] = wg_ref[0].astype(_bf16)
            wu_s[...] = wu_ref[0].astype(_bf16)
            wd_s[...] = wd_ref[0].astype(_bf16)

        x_lo, x_hi = _unpack_bf16_pair(x_ref[...])
        x_lo, x_hi = x_lo.astype(_bf16), x_hi.astype(_bf16)
        gg = (jnp.dot(x_lo, wg_s[0:half, :], preferred_element_type=_f32)
              + jnp.dot(x_hi, wg_s[half:D_MODEL, :], preferred_element_type=_f32))
        uu = (jnp.dot(x_lo, wu_s[0:half, :], preferred_element_type=_f32)
              + jnp.dot(x_hi, wu_s[half:D_MODEL, :], preferred_element_type=_f32))
        a = (_silu(gg) * uu).astype(_bf16)
        y = jnp.dot(a, wd_s[...], preferred_element_type=_f32)
        out_ref[...] = _pack_bf16_pair(y[:, 0:half], y[:, half:D_MODEL])

    @pl.when(i >= nused)
    def _():
        out_ref[...] = jnp.zeros_like(out_ref)


def _moe(block_expert, n_used, x_rows, w_gate, w_up, w_down):
    n_rows, width = x_rows.shape
    n_blocks = n_rows // ROW_BLOCK
    wmap = lambda i, be, nu: (be[i], 0, 0)
    omap = lambda i, be, nu: (i, 0)
    grid_spec = pltpu.PrefetchScalarGridSpec(
        num_scalar_prefetch=2,
        grid=(n_blocks,),
        in_specs=[
            pl.BlockSpec((ROW_BLOCK, width), omap),
            pl.BlockSpec((1, D_MODEL, EXPERT_FF), wmap),
            pl.BlockSpec((1, D_MODEL, EXPERT_FF), wmap),
            pl.BlockSpec((1, EXPERT_FF, D_MODEL), wmap),
        ],
        out_specs=pl.BlockSpec((ROW_BLOCK, width), omap),
        scratch_shapes=[
            pltpu.VMEM((D_MODEL, EXPERT_FF), _bf16),
            pltpu.VMEM((D_MODEL, EXPERT_FF), _bf16),
            pltpu.VMEM((EXPERT_FF, D_MODEL), _bf16),
        ],
    )
    return pl.pallas_call(
        _moe_kernel,
        grid_spec=grid_spec,
        out_shape=jax.ShapeDtypeStruct((n_rows, width), x_rows.dtype),
        compiler_params=pltpu.CompilerParams(
            dimension_semantics=("arbitrary",),
            vmem_limit_bytes=VMEM_LIMIT),
        name="moe_experts",
    )(block_expert, n_used, x_rows, w_gate, w_up, w_down)


TAIL_TM = 256


def _tail_kernel(pos_ref, h_ref, gate_ref, y_hbm, p_ref, wpp_ref, gp_ref, wpg_ref, bpg_ref,
                 gfin_ref, out_ref, ybuf, sem, *, n_tok):
    i = pl.program_id(0)
    n_steps = pl.num_programs(0)
    slot = i % 2

    def issue(step, sl):
        base = step * TAIL_TM

        def body(r, carry):
            for kk in range(TOP_K):
                row = pos_ref[kk * n_tok + base + r]
                pltpu.make_async_copy(y_hbm.at[pl.ds(row, 1), :],
                                      ybuf.at[sl, kk, pl.ds(r, 1), :], sem.at[sl]).start()
            return carry

        lax.fori_loop(0, TAIL_TM, body, 0)

    @pl.when(i == 0)
    def _():
        issue(0, 0)

    @pl.when(i + 1 < n_steps)
    def _():
        issue(i + 1, 1 - slot)

    for kk in range(TOP_K):
        pltpu.make_async_copy(y_hbm.at[pl.ds(0, TAIL_TM), :], ybuf.at[slot, kk], sem.at[slot]).wait()

    gates = gate_ref[...]
    g1, g2 = gates[:, 0:1], gates[:, 1:2]
    y1_lo, y1_hi = _unpack_bf16_pair(ybuf[slot, 0])
    y2_lo, y2_hi = _unpack_bf16_pair(ybuf[slot, 1])
    moe = jnp.concatenate([g1 * y1_lo + g2 * y2_lo, g1 * y1_hi + g2 * y2_hi], axis=1)
    h2 = h_ref[...] + moe
    ple = _rms(jnp.dot(p_ref[...].astype(_bf16), wpp_ref[...], preferred_element_type=_f32),
               gp_ref[...])
    lg = jnp.dot(h2.astype(_bf16), wpg_ref[...], preferred_element_type=_f32) + bpg_ref[...]
    h3 = h2 + (1.0 / (1.0 + jnp.exp(-lg))) * ple
    out_ref[...] = _rms(h3, gfin_ref[...])


def _tail(pos, h1, gates, y_rows, p2, w_pp, g_ple, w_pg, b_pg, g_fin):
    t = h1.shape[0]
    row = lambda i, ps: (i, 0)
    fix = lambda i, ps: (0, 0)
    grid_spec = pltpu.PrefetchScalarGridSpec(
        num_scalar_prefetch=1,
        grid=(t // TAIL_TM,),
        in_specs=[
            pl.BlockSpec((TAIL_TM, D_MODEL), row),
            pl.BlockSpec((TAIL_TM, LANES), row),
            pl.BlockSpec(memory_space=pl.ANY),
            pl.BlockSpec((TAIL_TM, PLE_DIM), row),
            pl.BlockSpec((PLE_DIM, D_MODEL), fix),
            pl.BlockSpec((1, D_MODEL), fix),
            pl.BlockSpec((D_MODEL, D_MODEL), fix),
            pl.BlockSpec((1, D_MODEL), fix),
            pl.BlockSpec((1, D_MODEL), fix),
        ],
        out_specs=pl.BlockSpec((TAIL_TM, D_MODEL), row),
        scratch_shapes=[
            pltpu.VMEM((2, TOP_K, TAIL_TM, y_rows.shape[1]), y_rows.dtype),
            pltpu.SemaphoreType.DMA((2,)),
        ],
    )
    return pl.pallas_call(
        functools.partial(_tail_kernel, n_tok=t),
        grid_spec=grid_spec,
        out_shape=jax.ShapeDtypeStruct((t, D_MODEL), _f32),
        compiler_params=pltpu.CompilerParams(
            dimension_semantics=("arbitrary",),
            vmem_limit_bytes=VMEM_LIMIT),
        name="tail",
    )(pos, h1, gates, y_rows, p2, w_pp, g_ple, w_pg, b_pg, g_fin)


def _rope_tables(positions):
    half = ROT_DIM // 2
    inv_freq = ROPE_THETA ** (-jnp.arange(0, ROT_DIM, 2, dtype=_f32) / ROT_DIM)
    ang = positions.astype(_f32).reshape(-1, 1) * inv_freq
    cos, sin = jnp.cos(ang), jnp.sin(ang)
    t = ang.shape[0]
    rest = DIFF_HEAD_DIM - ROT_DIM
    zero8 = jnp.zeros((t, half), _f32)
    comp = lambda a, b, fill: jnp.concatenate([a, b, jnp.full((t, rest), fill, _f32)], axis=1)
    twice = lambda c: jnp.concatenate([c, c], axis=1)
    rc = twice(comp(cos, cos, 1.0))
    rp = twice(comp(zero8, sin, 0.0))
    rm = twice(comp(-sin, zero8, 0.0))
    return rc, rp, rm


def kernel(x, p, positions, norm_mix_g, w_in, conv_w, conv_b, dt_bias_f, dt_bias_b, a_log_f, a_log_b, d_skip, ssd_norm_g, lam_q1, lam_k1, lam_q2, lam_k2, subln_g, w_out, norm_ffn_g, w_route_group, b_route_group, w_route_expert, b_route_expert, w_exp_gate, w_exp_up, w_exp_down, w_ple_proj, ple_norm_g, w_ple_gate, b_ple_gate, final_norm_g):
    b, seq, d = x.shape
    t = b * seq
    x2 = x.reshape(t, d)
    row = lambda v: v.reshape(1, -1).astype(_f32)

    wi = w_in[0]
    o_xbc, o_dt, o_q = SSD_W, SSD_W + XBC_W, SSD_W + XBC_W + 2 * SSD_HEADS
    w_main = jnp.concatenate(
        [wi[:, :o_xbc], wi[:, o_q:o_q + 2 * ATTN_W], wi[:, o_xbc:o_dt], wi[:, o_q + 2 * ATTN_W:]],
        axis=1).astype(_bf16)
    w_dt = jnp.pad(wi[:, o_dt:o_q], ((0, 0), (0, DT_PAD - 2 * SSD_HEADS))).astype(_bf16)
    pad_dt = lambda v: jnp.pad(v, (0, DT_PAD - 2 * SSD_HEADS)).reshape(1, DT_PAD)
    dt_bias = pad_dt(jnp.concatenate([dt_bias_f[0], dt_bias_b[0]]))
    a_row = pad_dt(jnp.concatenate([-jnp.exp(a_log_f[0]), -jnp.exp(a_log_b[0])]))
    d_skip_row = jnp.repeat(d_skip[0], SSD_HEADDIM).reshape(1, SSD_W)
    lam_vecs = jnp.stack([lam_q1[0], lam_k1[0], lam_q2[0], lam_k2[0]])
    n_route = MOE_GROUPS + N_EXPERTS
    w_route = jnp.pad(jnp.concatenate([w_route_group[0], w_route_expert[0]], axis=1),
                      ((0, 0), (0, LANES - n_route)))
    w_route_hi = w_route.astype(_bf16)
    w_route_lo = (w_route - w_route_hi.astype(_f32)).astype(_bf16)
    w_route2 = jnp.concatenate([w_route_hi, w_route_lo], axis=1)
    b_route = jnp.pad(jnp.concatenate([b_route_group[0], b_route_expert[0]]),
                      (0, LANES - n_route)).reshape(1, LANES)
    rc, rp, rm = _rope_tables(positions)

    proj, dt = _inproj(x2, row(norm_mix_g[0]), w_main, w_dt, rc, rp, rm)
    proj3 = proj.reshape(b, seq, MAIN_W)
    y_ssd = _ssd(proj3, dt.reshape(b, seq, DT_PAD), conv_w[0], row(conv_b[0]),
                 dt_bias, a_row, d_skip_row)
    att = _attn(proj3, lam_vecs, row(subln_g[0]))
    h1, hn_packed, route, cnt, gates = _outproj(
        y_ssd.reshape(t, SSD_W), proj, att.reshape(t, ATTN_W), x2, w_out[0].astype(_bf16),
        row(ssd_norm_g[0]), row(norm_ffn_g[0]), w_route2, b_route)
    pos8, blk8, meta8 = _plan(route, cnt)
    pos = pos8[:TOP_K].reshape(-1)
    block_expert = blk8[0, :_n_rows(t) // ROW_BLOCK]
    n_used = blk8[1, :1]
    x_rows = _dispatch_rows(pos, meta8[0, :N_EXPERTS], meta8[1, :N_EXPERTS], n_used, hn_packed)
    y_rows = _moe(block_expert, n_used, x_rows, w_exp_gate[0], w_exp_up[0], w_exp_down[0])
    out = _tail(pos, h1, gates, y_rows, p[0].reshape(t, PLE_DIM), w_ple_proj[0].astype(_bf16),
                row(ple_norm_g[0]), w_ple_gate[0].astype(_bf16), row(b_ple_gate[0]),
                row(final_norm_g))
    return out.reshape(b, seq, d)
```

```python
import functools

import jax
import jax.numpy as jnp
import numpy as np
from jax import lax
from jax.experimental import pallas as pl
from jax.experimental.pallas import tpu as pltpu

D_MODEL = 2048
PLE_DIM = 256
SSD_W = 1024
ATTN_W = 1024
SSD_HEADDIM = 64
SSD_HEADS = 16
SSD_GROUPS = 2
HEADS_PER_GROUP = SSD_HEADS // SSD_GROUPS
SSD_STATE = 128
CHUNK = 128
CONV_W = 5
XBC_W = SSD_W + 2 * SSD_GROUPS * SSD_STATE
DIFF_HEAD_DIM = 64
DIFF_HEADS = 8
ROT_DIM = 16
ROPE_THETA = 500000.0
MOE_GROUPS = 8
EXPERTS_PER_GROUP = 8
N_EXPERTS = 64
TOP_K = 2
EXPERT_FF = 512
ROW_BLOCK = 128
EPS = 1e-6
LAM_INIT = 0.2

LANES = 128
DT_PAD = LANES
MAIN_W = SSD_W + XBC_W + 3 * ATTN_W
COL_Z = 0
COL_Q = SSD_W
COL_K = COL_Q + ATTN_W
COL_XBC = COL_K + ATTN_W
COL_V = COL_XBC + XBC_W
VMEM_LIMIT = 56 * 1024 * 1024
NEG_BIG = -1e30
Q_SCALE = float(DIFF_HEAD_DIM ** -0.5 * np.log2(np.e))

_f32 = jnp.float32
_bf16 = jnp.bfloat16


def _silu(v):
    return v * (1.0 / (1.0 + jnp.exp(-v)))


def _rms(v, g):
    return v * lax.rsqrt(jnp.mean(v * v, axis=-1, keepdims=True) + EPS) * g


_HI16 = np.uint32(0xFFFF0000)


def _pack_bf16_pair(lo, hi):
    lo_w = lax.bitcast_convert_type(lo.astype(_bf16).astype(_f32), jnp.uint32) >> 16
    hi_w = lax.bitcast_convert_type(hi.astype(_bf16).astype(_f32), jnp.uint32) & _HI16
    return lo_w | hi_w


def _unpack_bf16_pair(w):
    lo = lax.bitcast_convert_type(w << 16, _f32)
    hi = lax.bitcast_convert_type(w & _HI16, _f32)
    return lo, hi


ROW_TILE = 8
assert D_MODEL // 2 == ROW_TILE * LANES


def _store_row_tiles(ref, words, n):
    for c in range(ROW_TILE):
        ref[pl.ds(c, n, stride=ROW_TILE), :] = words[:, c * LANES:(c + 1) * LANES]


def _load_row_tiles(ref, n):
    return jnp.concatenate(
        [ref[pl.ds(c, n, stride=ROW_TILE), :] for c in range(ROW_TILE)], axis=1)


IN_TM = 512
IN_TN = 512


def _inproj_kernel(x_ref, g_ref, w_ref, wdt_ref, rc_ref, rp_ref, rm_ref,
                   out_ref, dt_ref, n_scr):
    j = pl.program_id(1)

    @pl.when(j == 0)
    def _():
        n = _rms(x_ref[...], g_ref[...])
        n_scr[...] = n.astype(_bf16)
        dt_ref[...] = jnp.dot(n_scr[...], wdt_ref[...], preferred_element_type=_f32)

    acc = jnp.dot(n_scr[...], w_ref[...], preferred_element_type=_f32)
    q_lo, k_lo, k_hi = COL_Q // IN_TN, COL_K // IN_TN, (COL_K + ATTN_W) // IN_TN
    is_rot = jnp.logical_and(j >= q_lo, j < k_hi)

    @pl.when(is_rot)
    def _():
        scale = jnp.where(j < k_lo, Q_SCALE, 1.0).astype(_f32)
        rc, rp, rm = rc_ref[...], rp_ref[...], rm_ref[...]
        for hb in range(IN_TN // LANES):
            t = acc[:, hb * LANES:(hb + 1) * LANES]
            r = (t * rc + pltpu.roll(t, ROT_DIM // 2, 1) * rp
                 + pltpu.roll(t, LANES - ROT_DIM // 2, 1) * rm)
            out_ref[:, hb * LANES:(hb + 1) * LANES] = (r * scale).astype(_bf16)

    @pl.when(jnp.logical_not(is_rot))
    def _():
        out_ref[...] = acc.astype(_bf16)


def _inproj(x2, g, w_main, w_dt, rc, rp, rm):
    t = x2.shape[0]
    grid = (t // IN_TM, MAIN_W // IN_TN)
    return pl.pallas_call(
        _inproj_kernel,
        grid=grid,
        in_specs=[
            pl.BlockSpec((IN_TM, D_MODEL), lambda i, j: (i, 0)),
            pl.BlockSpec((1, D_MODEL), lambda i, j: (0, 0)),
            pl.BlockSpec((D_MODEL, IN_TN), lambda i, j: (0, j)),
            pl.BlockSpec((D_MODEL, DT_PAD), lambda i, j: (0, 0)),
            pl.BlockSpec((IN_TM, LANES), lambda i, j: (i, 0)),
            pl.BlockSpec((IN_TM, LANES), lambda i, j: (i, 0)),
            pl.BlockSpec((IN_TM, LANES), lambda i, j: (i, 0)),
        ],
        out_specs=[
            pl.BlockSpec((IN_TM, IN_TN), lambda i, j: (i, j)),
            pl.BlockSpec((IN_TM, DT_PAD), lambda i, j: (i, 0)),
        ],
        out_shape=[
            jax.ShapeDtypeStruct((t, MAIN_W), _bf16),
            jax.ShapeDtypeStruct((t, DT_PAD), _f32),
        ],
        scratch_shapes=[pltpu.VMEM((IN_TM, D_MODEL), _bf16)],
        compiler_params=pltpu.CompilerParams(
            dimension_semantics=("arbitrary", "arbitrary"),
            vmem_limit_bytes=VMEM_LIMIT),
        name="inproj",
    )(x2, g, w_main, w_dt, rc, rp, rm)


CONV_HALO = 16


def _ssd_kernel(xbc_ref, dt_ref, cw_ref, cb_ref, dtb_ref, a_ref, dsk_ref,
                y_ref, xpad, xact, dts, yacc, state, seq):
    n_chunks = seq // CHUNK

    zeros_halo = jnp.zeros((CONV_HALO, XBC_W), _bf16)
    xpad[0:CONV_HALO, :] = zeros_halo
    xpad[CONV_HALO + seq:CONV_HALO + seq + CONV_HALO, :] = zeros_halo
    xpad[CONV_HALO:CONV_HALO + seq, :] = xbc_ref[0]
    cw = cw_ref[...]
    cb = cb_ref[...]
    win = CHUNK + 2 * CONV_HALO

    def conv_body(c, carry):
        r0 = pl.multiple_of(c * CHUNK, CHUNK)
        blk = xpad[pl.ds(r0, win), :].astype(_f32)
        acc = blk * cw[CONV_W // 2:CONV_W // 2 + 1, :]
        for k in range(CONV_W):
            sh = CONV_W // 2 - k
            if sh == 0:
                continue
            acc = acc + pltpu.roll(blk, sh % win, 0) * cw[k:k + 1, :]
        v = acc[CONV_HALO:CONV_HALO + CHUNK, :] + cb
        xact[pl.ds(r0, CHUNK), :] = _silu(v).astype(_bf16)
        return carry

    lax.fori_loop(0, n_chunks, conv_body, 0)

    raw = dt_ref[0] + dtb_ref[...]
    dts[...] = jnp.maximum(raw, 0.0) + jnp.log1p(jnp.exp(-jnp.abs(raw)))

    a_row = a_ref[...]
    dsk = dsk_ref[...]
    row_i = lax.broadcasted_iota(jnp.int32, (CHUNK, CHUNK), 0)
    col_i = lax.broadcasted_iota(jnp.int32, (CHUNK, CHUNK), 1)

    def chunk_step(c, reverse):
        r0 = pl.multiple_of(c * CHUNK, CHUNK)
        dtc = dts[pl.ds(r0, CHUNK), :]
        cs = dtc * a_row
        k = 1
        while k < CHUNK:
            if reverse:
                cs = cs + jnp.where(row_i < CHUNK - k, pltpu.roll(cs, CHUNK - k, 0), 0.0)
            else:
                cs = cs + jnp.where(row_i >= k, pltpu.roll(cs, k, 0), 0.0)
            k *= 2
        cs_t = cs.T
        dt_t = dtc.T
        end_col = cs_t[:, 0:1] if reverse else cs_t[:, CHUNK - 1:CHUNK]
        w_t = jnp.exp(end_col - cs_t) * dt_t
        exp_cs = jnp.exp(cs)
        tri = (row_i <= col_i) if reverse else (row_i >= col_i)
        lane0 = SSD_HEADS if reverse else 0
        for g in range(SSD_GROUPS):
            b_g = xact[pl.ds(r0, CHUNK), SSD_W + g * SSD_STATE:SSD_W + (g + 1) * SSD_STATE]
            c_g = xact[pl.ds(r0, CHUNK),
                       SSD_W + (SSD_GROUPS + g) * SSD_STATE:SSD_W + (SSD_GROUPS + g + 1) * SSD_STATE]
            cbm = lax.dot_general(c_g, b_g, (((1,), (1,)), ((), ())),
                                  preferred_element_type=_f32)
            b_t = b_g.astype(_f32).T
            c_f = c_g.astype(_f32)
            for hh in range(HEADS_PER_GROUP):
                h = g * HEADS_PER_GROUP + hh
                ln = lane0 + h
                xs_h = xact[pl.ds(r0, CHUNK), h * SSD_HEADDIM:(h + 1) * SSD_HEADDIM]
                seg = cs[:, ln:ln + 1] - cs_t[ln:ln + 1, :]
                m_h = cbm * jnp.exp(jnp.where(tri, seg, NEG_BIG)) * dt_t[ln:ln + 1, :]
                st = state[h]
                y_h = jnp.dot(m_h.astype(_bf16), xs_h, preferred_element_type=_f32)
                y_h = y_h + jnp.dot((c_f * exp_cs[:, ln:ln + 1]).astype(_bf16),
                                    st.astype(_bf16), preferred_element_type=_f32)
                upd = jnp.dot((b_t * w_t[ln:ln + 1, :]).astype(_bf16), xs_h,
                              preferred_element_type=_f32)
                if reverse:
                    dec = exp_cs[0:1, ln:ln + 1]
                else:
                    dec = exp_cs[CHUNK - 1:CHUNK, ln:ln + 1]
                state[h] = st * dec + upd
                cols = slice(h * SSD_HEADDIM, (h + 1) * SSD_HEADDIM)
                if reverse:
                    y_ref[0, pl.ds(r0, CHUNK), cols] = (
                        yacc[pl.ds(r0, CHUNK), cols] + y_h).astype(_bf16)
                else:
                    yacc[pl.ds(r0, CHUNK), cols] = (
                        y_h + xs_h.astype(_f32) * dsk[:, cols])

    state[...] = jnp.zeros_like(state)

    def fwd_body(c, carry):
        chunk_step(c, False)
        return carry

    lax.fori_loop(0, n_chunks, fwd_body, 0)

    state[...] = jnp.zeros_like(state)

    def bwd_body(i, carry):
        chunk_step(n_chunks - 1 - i, True)
        return carry

    lax.fori_loop(0, n_chunks, bwd_body, 0)


def _ssd(proj3, dt3, conv_w, conv_b, dt_bias, a_row, d_skip_row):
    b, seq, _ = proj3.shape
    kern = functools.partial(_ssd_kernel, seq=seq)
    assert COL_XBC % XBC_W == 0
    return pl.pallas_call(
        kern,
        grid=(b,),
        in_specs=[
            pl.BlockSpec((1, seq, XBC_W), lambda i: (i, 0, COL_XBC // XBC_W)),
            pl.BlockSpec((1, seq, DT_PAD), lambda i: (i, 0, 0)),
            pl.BlockSpec((CONV_W, XBC_W), lambda i: (0, 0)),
            pl.BlockSpec((1, XBC_W), lambda i: (0, 0)),
            pl.BlockSpec((1, DT_PAD), lambda i: (0, 0)),
            pl.BlockSpec((1, DT_PAD), lambda i: (0, 0)),
            pl.BlockSpec((1, SSD_W), lambda i: (0, 0)),
        ],
        out_specs=pl.BlockSpec((1, seq, SSD_W), lambda i: (i, 0, 0)),
        out_shape=jax.ShapeDtypeStruct((b, seq, SSD_W), _bf16),
        scratch_shapes=[
            pltpu.VMEM((seq + 2 * CONV_HALO, XBC_W), _bf16),
            pltpu.VMEM((seq, XBC_W), _bf16),
            pltpu.VMEM((seq, DT_PAD), _f32),
            pltpu.VMEM((seq, SSD_W), _f32),
            pltpu.VMEM((SSD_HEADS, SSD_STATE, SSD_HEADDIM), _f32),
        ],
        compiler_params=pltpu.CompilerParams(
            dimension_semantics=("arbitrary",),
            vmem_limit_bytes=VMEM_LIMIT),
        name="ssd",
    )(proj3, dt3, conv_w, conv_b, dt_bias, a_row, d_skip_row)


ATT_QB = 256
ATT_UNROLL = 2


def _attn_kernel(q_ref, k_ref, v_ref, lam_ref, g_ref, o_ref, v1_scr, *, seq):
    hw = 2 * DIFF_HEAD_DIM
    lv = lam_ref[...]
    lam = (jnp.exp(jnp.sum(lv[0:1] * lv[1:2], axis=-1, keepdims=True))
           - jnp.exp(jnp.sum(lv[2:3] * lv[3:4], axis=-1, keepdims=True)) + LAM_INIT)
    k = k_ref[0]
    g = g_ref[...]
    v1_scr[:, 0:hw] = v_ref[0]
    v1_scr[:, hw:2 * hw] = (lax.broadcasted_iota(jnp.int32, (seq, hw), 1) == 0).astype(_bf16)
    first = lax.broadcasted_iota(jnp.int32, (1, hw), 1) < DIFF_HEAD_DIM
    nt = (((1,), (1,)), ((), ()))

    def chain(r0):
        q = q_ref[0, pl.ds(r0, ATT_QB), :]
        zero = jnp.zeros_like(q)
        qz = jnp.concatenate([jnp.where(first, q, zero), jnp.where(first, zero, q)], axis=0)
        s = lax.dot_general(qz, k, nt, preferred_element_type=_f32)
        e = jnp.exp2(s - jnp.max(s, axis=-1, keepdims=True)).astype(_bf16)
        ov = jnp.dot(e, v1_scr[...], preferred_element_type=_f32)
        o1, l1 = ov[0:ATT_QB, 0:hw], ov[0:ATT_QB, hw:hw + 1]
        o2, l2 = ov[ATT_QB:, 0:hw], ov[ATT_QB:, hw:hw + 1]
        o = o1 / l1 - (lam / l2) * o2
        o = _rms(o, g) * (1.0 - LAM_INIT)
        o_ref[0, pl.ds(r0, ATT_QB), :] = o.astype(_bf16)

    step = ATT_QB * ATT_UNROLL

    def body(i, carry):
        for u in range(ATT_UNROLL):
            chain(pl.multiple_of(i * step + u * ATT_QB, ATT_QB))
        return carry

    lax.fori_loop(0, seq // step, body, 0)


def _attn(proj3, lam_vecs, subln_g):
    b, seq, _ = proj3.shape
    hw = 2 * DIFF_HEAD_DIM
    kern = functools.partial(_attn_kernel, seq=seq)
    return pl.pallas_call(
        kern,
        grid=(b, DIFF_HEADS),
        in_specs=[
            pl.BlockSpec((1, seq, hw), lambda i, h: (i, 0, COL_Q // hw + h)),
            pl.BlockSpec((1, seq, hw), lambda i, h: (i, 0, COL_K // hw + h)),
            pl.BlockSpec((1, seq, hw), lambda i, h: (i, 0, COL_V // hw + h)),
            pl.BlockSpec((4, DIFF_HEAD_DIM), lambda i, h: (0, 0)),
            pl.BlockSpec((1, hw), lambda i, h: (0, 0)),
        ],
        out_specs=pl.BlockSpec((1, seq, hw), lambda i, h: (i, 0, h)),
        out_shape=jax.ShapeDtypeStruct((b, seq, ATTN_W), _bf16),
        scratch_shapes=[pltpu.VMEM((seq, 2 * hw), _bf16)],
        compiler_params=pltpu.CompilerParams(
            dimension_semantics=("arbitrary", "arbitrary"),
            vmem_limit_bytes=VMEM_LIMIT),
        name="diffattn",
    )(proj3, proj3, proj3, lam_vecs, subln_g)


OUT_TM = 256
ROUTE_ROWS = 8


def _outproj_kernel(y_ref, z_ref, att_ref, x_ref, w_ref, gs_ref, gf_ref, wr_ref, br_ref,
                    h_ref, hn_ref, route_ref, cnt_ref, gate_ref, cnt_scr):
    @pl.when(pl.program_id(0) == 0)
    def _():
        cnt_scr[...] = jnp.zeros_like(cnt_scr)

    y = y_ref[...].astype(_f32)
    z = z_ref[...].astype(_f32)
    s = _rms(y * _silu(z), gs_ref[...]).astype(_bf16)
    acc = jnp.dot(s, w_ref[0:SSD_W, :], preferred_element_type=_f32)
    acc = acc + jnp.dot(att_ref[...], w_ref[SSD_W:SSD_W + ATTN_W, :], preferred_element_type=_f32)
    h1 = x_ref[...] + acc
    h_ref[...] = h1
    hn = _rms(h1, gf_ref[...])
    half = D_MODEL // 2
    _store_row_tiles(hn_ref, _pack_bf16_pair(hn[:, 0:half], hn[:, half:D_MODEL]), hn.shape[0])
    hn_hi = hn.astype(_bf16)
    hn_lo = (hn - hn_hi.astype(_f32)).astype(_bf16)
    a = jnp.dot(hn_hi, wr_ref[...], preferred_element_type=_f32)
    bb = jnp.dot(hn_lo, wr_ref[:, 0:LANES], preferred_element_type=_f32)
    logits = (a[:, 0:LANES] + a[:, LANES:2 * LANES] + bb + br_ref[...]).T
    tm = logits.shape[1]
    iota = lax.broadcasted_iota(jnp.int32, (MOE_GROUPS, tm), 0)

    def first_argmax(val, vmax):
        return jnp.min(jnp.where(val == vmax, iota, MOE_GROUPS), axis=0, keepdims=True)

    gl = logits[0:MOE_GROUPS]
    gmax = jnp.max(gl, axis=0, keepdims=True)
    p_g = 1.0 / jnp.sum(jnp.exp(gl - gmax), axis=0, keepdims=True)
    g_sel = first_argmax(gl, gmax)
    el = jnp.zeros((EXPERTS_PER_GROUP, tm), _f32)
    for gi in range(MOE_GROUPS):
        lo = MOE_GROUPS + gi * EXPERTS_PER_GROUP
        el = jnp.where(g_sel == gi, logits[lo:lo + EXPERTS_PER_GROUP], el)
    ee = jnp.exp(el - jnp.max(el, axis=0, keepdims=True))
    pe = ee / jnp.sum(ee, axis=0, keepdims=True)
    p1 = jnp.max(pe, axis=0, keepdims=True)
    i1 = first_argmax(pe, p1)
    pe2 = jnp.where(iota == i1, -1.0, pe)
    p2 = jnp.max(pe2, axis=0, keepdims=True)
    i2 = first_argmax(pe2, p2)
    den = p1 + p2
    e1 = g_sel * EXPERTS_PER_GROUP + i1
    e2 = g_sel * EXPERTS_PER_GROUP + i2
    iota_e = lax.broadcasted_iota(jnp.int32, (N_EXPERTS, tm), 0)
    oh1 = (iota_e == e1).astype(_f32)
    oh2 = (iota_e == e2).astype(_f32)
    both = oh1 + oh2
    earlier = (lax.broadcasted_iota(jnp.int32, (tm, tm), 0)
               < lax.broadcasted_iota(jnp.int32, (tm, tm), 1)).astype(_bf16)
    before = cnt_scr[:, 0:1] + jnp.dot(both.astype(_bf16), earlier,
                                       preferred_element_type=_f32)
    r1 = jnp.sum(oh1 * before, axis=0, keepdims=True).astype(jnp.int32)
    r2 = jnp.sum(oh2 * before, axis=0, keepdims=True).astype(jnp.int32)
    cnt_scr[...] = cnt_scr[...] + jnp.sum(both, axis=1, keepdims=True)
    cnt_ref[...] = cnt_scr[...].astype(jnp.int32)
    route_ref[...] = jnp.where(iota == 0, e1, jnp.where(iota == 1, e2,
                               jnp.where(iota == 2, r1, jnp.where(iota == 3, r2, 0))))
    g8 = jnp.where(iota == 0, p_g * (p1 / den), jnp.where(iota == 1, p_g * (p2 / den), 0.0))
    gate_ref[...] = jnp.concatenate(
        [g8, jnp.zeros((LANES - ROUTE_ROWS, tm), _f32)], axis=0).T


def _outproj(y2, proj2, att2, x2, w_out, g_ssd, g_ffn, wr_t, br_col):
    t = x2.shape[0]
    row = lambda i: (i, 0)
    fix = lambda i: (0, 0)
    return pl.pallas_call(
        _outproj_kernel,
        grid=(t // OUT_TM,),
        in_specs=[
            pl.BlockSpec((OUT_TM, SSD_W), row),
            pl.BlockSpec((OUT_TM, SSD_W), row),
            pl.BlockSpec((OUT_TM, ATTN_W), row),
            pl.BlockSpec((OUT_TM, D_MODEL), row),
            pl.BlockSpec((SSD_W + ATTN_W, D_MODEL), fix),
            pl.BlockSpec((1, SSD_W), fix),
            pl.BlockSpec((1, D_MODEL), fix),
            pl.BlockSpec((D_MODEL, 2 * LANES), fix),
            pl.BlockSpec((1, LANES), fix),
        ],
        out_specs=[
            pl.BlockSpec((OUT_TM, D_MODEL), row),
            pl.BlockSpec((OUT_TM * ROW_TILE, LANES), row),
            pl.BlockSpec((ROUTE_ROWS, OUT_TM), lambda i: (0, i)),
            pl.BlockSpec((N_EXPERTS, LANES), fix),
            pl.BlockSpec((OUT_TM, LANES), row),
        ],
        out_shape=[
            jax.ShapeDtypeStruct((t, D_MODEL), _f32),
            jax.ShapeDtypeStruct((t * ROW_TILE, LANES), jnp.uint32),
            jax.ShapeDtypeStruct((ROUTE_ROWS, t), jnp.int32),
            jax.ShapeDtypeStruct((N_EXPERTS, LANES), jnp.int32),
            jax.ShapeDtypeStruct((t, LANES), _f32),
        ],
        scratch_shapes=[pltpu.VMEM((N_EXPERTS, LANES), _f32)],
        compiler_params=pltpu.CompilerParams(
            dimension_semantics=("arbitrary",),
            vmem_limit_bytes=VMEM_LIMIT),
        name="outproj_router",
    )(y2, proj2, att2, x2, w_out, g_ssd, g_ffn, wr_t, br_col)


def _n_rows(n_tok):
    n_assign = n_tok * TOP_K
    return (n_assign + N_EXPERTS * (ROW_BLOCK - 1) + ROW_BLOCK - 1) // ROW_BLOCK * ROW_BLOCK


def _prefix_sum(v, axis):
    n = v.shape[axis]
    idx = lax.broadcasted_iota(jnp.int32, v.shape, axis)
    k = 1
    while k < n:
        v = v + jnp.where(idx >= k, pltpu.roll(v, k, axis), 0)
        k *= 2
    return v


def _plan_kernel(route_ref, cnt_ref, pos_ref, blk_ref, meta_ref, *, n_blocks_pad):
    n_tok = route_ref.shape[1]
    cnt_col = jnp.concatenate(
        [cnt_ref[...], jnp.zeros((LANES - N_EXPERTS, LANES), jnp.int32)], axis=0)
    pad_up = lambda c: (c + (ROW_BLOCK - 1)) & (-ROW_BLOCK)
    ends_col = _prefix_sum(pad_up(cnt_col), 0)
    starts_col = (ends_col - pad_up(cnt_col)).astype(_f32)
    cnt_row = cnt_col.astype(_f32).T.astype(jnp.int32)
    ends_row = _prefix_sum(pad_up(cnt_row), 1)
    starts_row = ends_row - pad_up(cnt_row)
    n_used = ends_row[0:1, LANES - 1:LANES] >> (ROW_BLOCK.bit_length() - 1)

    ch = 1024
    iota_e = lax.broadcasted_iota(jnp.int32, (LANES, ch), 0)
    for c0 in range(0, n_tok, ch):
        rt = route_ref[:, c0:c0 + ch]
        s1 = jnp.sum(jnp.where(iota_e == rt[0:1], starts_col[:, 0:1], 0.0), axis=0, keepdims=True)
        s2 = jnp.sum(jnp.where(iota_e == rt[1:2], starts_col[:, 0:1], 0.0), axis=0, keepdims=True)
        p1 = s1.astype(jnp.int32) + rt[2:3]
        p2 = s2.astype(jnp.int32) + rt[3:4]
        sub = lax.broadcasted_iota(jnp.int32, (ROUTE_ROWS, ch), 0)
        pos_ref[:, c0:c0 + ch] = jnp.where(sub == 0, p1, jnp.where(sub == 1, p2, 0))

    blk_start = lax.broadcasted_iota(jnp.int32, (LANES, n_blocks_pad), 1) * ROW_BLOCK
    e_idx = lax.broadcasted_iota(jnp.int32, (LANES, n_blocks_pad), 0)
    real = e_idx < N_EXPERTS
    be = jnp.sum(jnp.where(jnp.logical_and(real, ends_col[:, 0:1] <= blk_start), 1.0, 0.0),
                 axis=0, keepdims=True).astype(jnp.int32)
    be = jnp.minimum(be, N_EXPERTS - 1)
    last = jnp.max(jnp.where(jnp.logical_and(real, cnt_col[:, 0:1] > 0), e_idx.astype(_f32), 0.0),
                   axis=0, keepdims=True).astype(jnp.int32)
    blk_i = lax.broadcasted_iota(jnp.int32, (1, n_blocks_pad), 1)
    be = jnp.where(blk_i < n_used, be, last)
    sub = lax.broadcasted_iota(jnp.int32, (ROUTE_ROWS, n_blocks_pad), 0)
    blk_ref[...] = jnp.where(sub == 0, be, jnp.where(sub == 1, n_used, 0))
    sub = lax.broadcasted_iota(jnp.int32, (ROUTE_ROWS, LANES), 0)
    meta_ref[...] = jnp.where(sub == 0, cnt_row[0:1], jnp.where(sub == 1, starts_row[0:1], 0))


def _plan(route, cnt):
    n_tok = route.shape[1]
    n_blocks_pad = -(-(_n_rows(n_tok) // ROW_BLOCK) // LANES) * LANES
    return pl.pallas_call(
        functools.partial(_plan_kernel, n_blocks_pad=n_blocks_pad),
        out_shape=[
            jax.ShapeDtypeStruct((ROUTE_ROWS, n_tok), jnp.int32),
            jax.ShapeDtypeStruct((ROUTE_ROWS, n_blocks_pad), jnp.int32),
            jax.ShapeDtypeStruct((ROUTE_ROWS, LANES), jnp.int32),
        ],
        compiler_params=pltpu.CompilerParams(vmem_limit_bytes=VMEM_LIMIT),
        name="route_plan",
    )(route, cnt)


DISP_TM = 256


def _dispatch_kernel(pos_ref, cnt_ref, start_ref, nused_ref, h_ref, x_hbm, stage, zrow, sem, zsem,
                     *, n_tok, n_blocks):
    i = pl.program_id(0)
    n_steps = pl.num_programs(0)
    slot = i % 2
    base = i * DISP_TM
    tile = lambda r: pl.ds(pl.multiple_of(r * ROW_TILE, ROW_TILE), ROW_TILE)

    def wait_slot(sl):
        for _ in range(TOP_K):
            pltpu.make_async_copy(stage.at[sl], x_hbm.at[pl.ds(0, DISP_TM * ROW_TILE), :],
                                  sem.at[sl]).wait()

    @pl.when(i >= 2)
    def _():
        wait_slot(slot)

    stage[slot] = h_ref[...]

    def body(r, carry):
        for kk in range(TOP_K):
            row = pos_ref[kk * n_tok + base + r]
            pltpu.make_async_copy(stage.at[slot, tile(r), :], x_hbm.at[tile(row), :],
                                  sem.at[slot]).start()
        return carry

    lax.fori_loop(0, DISP_TM, body, 0, unroll=8)

    @pl.when(i == n_steps - 1)
    def _():
        zrow[...] = jnp.zeros_like(zrow)

        def per_expert(e, n_fill):
            cnt = cnt_ref[e]
            start = start_ref[e]
            end = (cnt + (ROW_BLOCK - 1)) & (-ROW_BLOCK)

            def fill(r, carry):
                pltpu.make_async_copy(zrow.at[tile(0), :], x_hbm.at[tile(start + r), :], zsem).start()
                return carry

            lax.fori_loop(cnt, end, fill, 0)
            return n_fill + (end - cnt)

        n_fill = lax.fori_loop(0, N_EXPERTS, per_expert, 0)
        blk_rows = ROW_BLOCK * ROW_TILE

        def fill_block(blk, carry):
            dst = pl.ds(pl.multiple_of(blk * blk_rows, blk_rows), blk_rows)
            pltpu.make_async_copy(zrow, x_hbm.at[dst, :], zsem).start()
            return carry

        lax.fori_loop(nused_ref[0], n_blocks, fill_block, 0)

        def wait_row(r, carry):
            pltpu.make_async_copy(zrow.at[tile(0), :], x_hbm.at[tile(0), :], zsem).wait()
            return carry

        lax.fori_loop(0, n_fill, wait_row, 0)

        def wait_block(blk, carry):
            pltpu.make_async_copy(zrow, x_hbm.at[pl.ds(0, blk_rows), :], zsem).wait()
            return carry

        lax.fori_loop(nused_ref[0], n_blocks, wait_block, 0)

        @pl.when(n_steps >= 2)
        def _():
            wait_slot(1 - slot)

        wait_slot(slot)


def _dispatch_rows(pos, counts, starts, n_used, hn_packed):
    t = hn_packed.shape[0] // ROW_TILE
    n_rows = _n_rows(t)
    blk = (DISP_TM * ROW_TILE, LANES)
    grid_spec = pltpu.PrefetchScalarGridSpec(
        num_scalar_prefetch=4,
        grid=(t // DISP_TM,),
        in_specs=[pl.BlockSpec(blk, lambda i, *_: (i, 0))],
        out_specs=pl.BlockSpec(memory_space=pl.ANY),
        scratch_shapes=[
            pltpu.VMEM((2,) + blk, hn_packed.dtype),
            pltpu.VMEM((ROW_BLOCK * ROW_TILE, LANES), hn_packed.dtype),
            pltpu.SemaphoreType.DMA((2,)),
            pltpu.SemaphoreType.DMA(()),
        ],
    )
    return pl.pallas_call(
        functools.partial(_dispatch_kernel, n_tok=t, n_blocks=n_rows // ROW_BLOCK),
        grid_spec=grid_spec,
        out_shape=jax.ShapeDtypeStruct((n_rows * ROW_TILE, LANES), hn_packed.dtype),
        compiler_params=pltpu.CompilerParams(
            dimension_semantics=("arbitrary",),
            vmem_limit_bytes=VMEM_LIMIT),
        name="dispatch_rows",
    )(pos, counts, starts, n_used, hn_packed)


def _moe_kernel(be_ref, nused_ref, x_ref, wg_hbm, wu_hbm, wd_hbm,
                out_ref, wg_f, wu_f, wd_f, wg_s, wu_s, wd_s, wsem, nexp, *, n_blocks):
    i = pl.program_id(0)
    nused = nused_ref[0]
    half = D_MODEL // 2

    def weight_copies(e, sl):
        return (pltpu.make_async_copy(wg_hbm.at[e], wg_f.at[sl], wsem.at[sl]),
                pltpu.make_async_copy(wu_hbm.at[e], wu_f.at[sl], wsem.at[sl]),
                pltpu.make_async_copy(wd_hbm.at[e], wd_f.at[sl], wsem.at[sl]))

    @pl.when(i == 0)
    def _():
        nexp[0] = 0
        for cp in weight_copies(be_ref[0], 0):
            cp.start()

    @pl.when(i < nused)
    def _():
        e = be_ref[i]
        new_expert = jnp.logical_or(i == 0, e != be_ref[jnp.maximum(i - 1, 0)])

        @pl.when(new_expert)
        def _():
            sl = nexp[0] % 2
            nxt = lax.while_loop(
                lambda b: jnp.logical_and(b < nused, be_ref[jnp.minimum(b, n_blocks - 1)] == e),
                lambda b: b + 1, i + 1)

            @pl.when(nxt < nused)
            def _():
                for cp in weight_copies(be_ref[jnp.minimum(nxt, n_blocks - 1)], 1 - sl):
                    cp.start()

            for cp in weight_copies(e, sl):
                cp.wait()
            wg_s[...] = wg_f[sl].astype(_bf16)
            wu_s[...] = wu_f[sl].astype(_bf16)
            wd_s[...] = wd_f[sl].astype(_bf16)
            nexp[0] = nexp[0] + 1

        x_lo, x_hi = _unpack_bf16_pair(_load_row_tiles(x_ref, ROW_BLOCK))
        x_lo, x_hi = x_lo.astype(_bf16), x_hi.astype(_bf16)
        gg = (jnp.dot(x_lo, wg_s[0:half, :], preferred_element_type=_f32)
              + jnp.dot(x_hi, wg_s[half:D_MODEL, :], preferred_element_type=_f32))
        uu = (jnp.dot(x_lo, wu_s[0:half, :], preferred_element_type=_f32)
              + jnp.dot(x_hi, wu_s[half:D_MODEL, :], preferred_element_type=_f32))
        a = (_silu(gg) * uu).astype(_bf16)
        y = jnp.dot(a, wd_s[...], preferred_element_type=_f32)
        _store_row_tiles(out_ref, _pack_bf16_pair(y[:, 0:half], y[:, half:D_MODEL]), ROW_BLOCK)

    @pl.when(i >= nused)
    def _():
        out_ref[...] = jnp.zeros_like(out_ref)


def _moe(block_expert, n_used, x_rows, w_gate, w_up, w_down):
    n_blocks = x_rows.shape[0] // (ROW_BLOCK * ROW_TILE)
    blk = (ROW_BLOCK * ROW_TILE, LANES)
    omap = lambda i, be, nu: (i, 0)
    grid_spec = pltpu.PrefetchScalarGridSpec(
        num_scalar_prefetch=2,
        grid=(n_blocks,),
        in_specs=[
            pl.BlockSpec(blk, omap),
            pl.BlockSpec(memory_space=pl.ANY),
            pl.BlockSpec(memory_space=pl.ANY),
            pl.BlockSpec(memory_space=pl.ANY),
        ],
        out_specs=pl.BlockSpec(blk, omap),
        scratch_shapes=[
            pltpu.VMEM((2, D_MODEL, EXPERT_FF), _f32),
            pltpu.VMEM((2, D_MODEL, EXPERT_FF), _f32),
            pltpu.VMEM((2, EXPERT_FF, D_MODEL), _f32),
            pltpu.VMEM((D_MODEL, EXPERT_FF), _bf16),
            pltpu.VMEM((D_MODEL, EXPERT_FF), _bf16),
            pltpu.VMEM((EXPERT_FF, D_MODEL), _bf16),
            pltpu.SemaphoreType.DMA((2,)),
            pltpu.SMEM((1,), jnp.int32),
        ],
    )
    return pl.pallas_call(
        functools.partial(_moe_kernel, n_blocks=n_blocks),
        grid_spec=grid_spec,
        out_shape=jax.ShapeDtypeStruct(x_rows.shape, x_rows.dtype),
        compiler_params=pltpu.CompilerParams(
            dimension_semantics=("arbitrary",),
            vmem_limit_bytes=VMEM_LIMIT),
        name="moe_experts",
    )(block_expert, n_used, x_rows, w_gate, w_up, w_down)


TAIL_TM = 256


def _tail_kernel(pos_ref, h_ref, gate_ref, y_hbm, p_ref, wpp_ref, gp_ref, wpg_ref, bpg_ref,
                 gfin_ref, out_ref, ybuf, sem, *, n_tok):
    i = pl.program_id(0)
    n_steps = pl.num_programs(0)
    slot = i % 2
    tile = lambda r: pl.ds(pl.multiple_of(r * ROW_TILE, ROW_TILE), ROW_TILE)

    def start_row(base, r, sl):
        for kk in range(TOP_K):
            row = pos_ref[kk * n_tok + base + r]
            pltpu.make_async_copy(y_hbm.at[tile(row), :], ybuf.at[sl, kk, tile(r), :],
                                  sem.at[sl]).start()

    def wait_slot(sl):
        for kk in range(TOP_K):
            pltpu.make_async_copy(y_hbm.at[pl.ds(0, TAIL_TM * ROW_TILE), :], ybuf.at[sl, kk],
                                  sem.at[sl]).wait()

    @pl.when(i == 0)
    def _():
        def body(r, carry):
            start_row(0, r, 0)
            return carry

        lax.fori_loop(0, TAIL_TM, body, 0, unroll=8)

    wait_slot(slot)

    nxt_base = jnp.minimum(i + 1, n_steps - 1) * TAIL_TM
    for r in range(TAIL_TM):
        start_row(nxt_base, r, 1 - slot)

    gates = gate_ref[...]
    g1, g2 = gates[:, 0:1], gates[:, 1:2]
    y1_lo, y1_hi = _unpack_bf16_pair(_load_row_tiles(ybuf.at[slot, 0], TAIL_TM))
    y2_lo, y2_hi = _unpack_bf16_pair(_load_row_tiles(ybuf.at[slot, 1], TAIL_TM))
    moe = jnp.concatenate([g1 * y1_lo + g2 * y2_lo, g1 * y1_hi + g2 * y2_hi], axis=1)
    h2 = h_ref[...] + moe
    ple = _rms(jnp.dot(p_ref[...].astype(_bf16), wpp_ref[...], preferred_element_type=_f32),
               gp_ref[...])
    lg = jnp.dot(h2.astype(_bf16), wpg_ref[...], preferred_element_type=_f32) + bpg_ref[...]
    h3 = h2 + (1.0 / (1.0 + jnp.exp(-lg))) * ple
    out_ref[...] = _rms(h3, gfin_ref[...])

    @pl.when(i == n_steps - 1)
    def _():
        wait_slot(1 - slot)


def _tail(pos, h1, gates, y_rows, p2, w_pp, g_ple, w_pg, b_pg, g_fin):
    t = h1.shape[0]
    row = lambda i, ps: (i, 0)
    fix = lambda i, ps: (0, 0)
    grid_spec = pltpu.PrefetchScalarGridSpec(
        num_scalar_prefetch=1,
        grid=(t // TAIL_TM,),
        in_specs=[
            pl.BlockSpec((TAIL_TM, D_MODEL), row),
            pl.BlockSpec((TAIL_TM, LANES), row),
            pl.BlockSpec(memory_space=pl.ANY),
            pl.BlockSpec((TAIL_TM, PLE_DIM), row),
            pl.BlockSpec((PLE_DIM, D_MODEL), fix),
            pl.BlockSpec((1, D_MODEL), fix),
            pl.BlockSpec((D_MODEL, D_MODEL), fix),
            pl.BlockSpec((1, D_MODEL), fix),
            pl.BlockSpec((1, D_MODEL), fix),
        ],
        out_specs=pl.BlockSpec((TAIL_TM, D_MODEL), row),
        scratch_shapes=[
            pltpu.VMEM((2, TOP_K, TAIL_TM * ROW_TILE, LANES), y_rows.dtype),
            pltpu.SemaphoreType.DMA((2,)),
        ],
    )
    return pl.pallas_call(
        functools.partial(_tail_kernel, n_tok=t),
        grid_spec=grid_spec,
        out_shape=jax.ShapeDtypeStruct((t, D_MODEL), _f32),
        compiler_params=pltpu.CompilerParams(
            dimension_semantics=("arbitrary",),
            vmem_limit_bytes=VMEM_LIMIT),
        name="tail",
    )(pos, h1, gates, y_rows, p2, w_pp, g_ple, w_pg, b_pg, g_fin)


def _rope_tables(positions):
    half = ROT_DIM // 2
    inv_freq = ROPE_THETA ** (-jnp.arange(0, ROT_DIM, 2, dtype=_f32) / ROT_DIM)
    ang = positions.astype(_f32).reshape(-1, 1) * inv_freq
    cos, sin = jnp.cos(ang), jnp.sin(ang)
    t = ang.shape[0]
    rest = DIFF_HEAD_DIM - ROT_DIM
    zero8 = jnp.zeros((t, half), _f32)
    comp = lambda a, b, fill: jnp.concatenate([a, b, jnp.full((t, rest), fill, _f32)], axis=1)
    twice = lambda c: jnp.concatenate([c, c], axis=1)
    rc = twice(comp(cos, cos, 1.0))
    rp = twice(comp(zero8, sin, 0.0))
    rm = twice(comp(-sin, zero8, 0.0))
    return rc, rp, rm


def kernel(x, p, positions, norm_mix_g, w_in, conv_w, conv_b, dt_bias_f, dt_bias_b, a_log_f, a_log_b, d_skip, ssd_norm_g, lam_q1, lam_k1, lam_q2, lam_k2, subln_g, w_out, norm_ffn_g, w_route_group, b_route_group, w_route_expert, b_route_expert, w_exp_gate, w_exp_up, w_exp_down, w_ple_proj, ple_norm_g, w_ple_gate, b_ple_gate, final_norm_g):
    b, seq, d = x.shape
    t = b * seq
    x2 = x.reshape(t, d)
    row = lambda v: v.reshape(1, -1).astype(_f32)

    wi = w_in[0]
    o_xbc, o_dt, o_q = SSD_W, SSD_W + XBC_W, SSD_W + XBC_W + 2 * SSD_HEADS
    w_main = jnp.concatenate(
        [wi[:, :o_xbc], wi[:, o_q:o_q + 2 * ATTN_W], wi[:, o_xbc:o_dt], wi[:, o_q + 2 * ATTN_W:]],
        axis=1).astype(_bf16)
    w_dt = jnp.pad(wi[:, o_dt:o_q], ((0, 0), (0, DT_PAD - 2 * SSD_HEADS))).astype(_bf16)
    pad_dt = lambda v: jnp.pad(v, (0, DT_PAD - 2 * SSD_HEADS)).reshape(1, DT_PAD)
    dt_bias = pad_dt(jnp.concatenate([dt_bias_f[0], dt_bias_b[0]]))
    a_row = pad_dt(jnp.concatenate([-jnp.exp(a_log_f[0]), -jnp.exp(a_log_b[0])]))
    d_skip_row = jnp.repeat(d_skip[0], SSD_HEADDIM).reshape(1, SSD_W)
    lam_vecs = jnp.stack([lam_q1[0], lam_k1[0], lam_q2[0], lam_k2[0]])
    n_route = MOE_GROUPS + N_EXPERTS
    w_route = jnp.pad(jnp.concatenate([w_route_group[0], w_route_expert[0]], axis=1),
                      ((0, 0), (0, LANES - n_route)))
    w_route_hi = w_route.astype(_bf16)
    w_route_lo = (w_route - w_route_hi.astype(_f32)).astype(_bf16)
    w_route2 = jnp.concatenate([w_route_hi, w_route_lo], axis=1)
    b_route = jnp.pad(jnp.concatenate([b_route_group[0], b_route_expert[0]]),
                      (0, LANES - n_route)).reshape(1, LANES)
    rc, rp, rm = _rope_tables(positions)

    proj, dt = _inproj(x2, row(norm_mix_g[0]), w_main, w_dt, rc, rp, rm)
    proj3 = proj.reshape(b, seq, MAIN_W)
    y_ssd = _ssd(proj3, dt.reshape(b, seq, DT_PAD), conv_w[0], row(conv_b[0]),
                 dt_bias, a_row, d_skip_row)
    att = _attn(proj3, lam_vecs, row(subln_g[0]))
    h1, hn_packed, route, cnt, gates = _outproj(
        y_ssd.reshape(t, SSD_W), proj, att.reshape(t, ATTN_W), x2, w_out[0].astype(_bf16),
        row(ssd_norm_g[0]), row(norm_ffn_g[0]), w_route2, b_route)
    pos8, blk8, meta8 = _plan(route, cnt)
    pos = pos8[:TOP_K].reshape(-1)
    block_expert = blk8[0, :_n_rows(t) // ROW_BLOCK]
    n_used = blk8[1, :1]
    x_rows = _dispatch_rows(pos, meta8[0, :N_EXPERTS], meta8[1, :N_EXPERTS], n_used, hn_packed)
    y_rows = _moe(block_expert, n_used, x_rows, w_exp_gate[0], w_exp_up[0], w_exp_down[0])
    out = _tail(pos, h1, gates, y_rows, p[0].reshape(t, PLE_DIM), w_ple_proj[0].astype(_bf16),
                row(ple_norm_g[0]), w_ple_gate[0].astype(_bf16), row(b_ple_gate[0]),
                row(final_norm_g))
    return out.reshape(b, seq, d)
```

```python
import functools

import jax
import jax.numpy as jnp
import numpy as np
from jax import lax
from jax.experimental import pallas as pl
from jax.experimental.pallas import tpu as pltpu

D_MODEL = 2048
PLE_DIM = 256
SSD_W = 1024
ATTN_W = 1024
SSD_HEADDIM = 64
SSD_HEADS = 16
SSD_GROUPS = 2
HEADS_PER_GROUP = SSD_HEADS // SSD_GROUPS
SSD_STATE = 128
CHUNK = 128
CONV_W = 5
XBC_W = SSD_W + 2 * SSD_GROUPS * SSD_STATE
DIFF_HEAD_DIM = 64
DIFF_HEADS = 8
ROT_DIM = 16
ROPE_THETA = 500000.0
MOE_GROUPS = 8
EXPERTS_PER_GROUP = 8
N_EXPERTS = 64
TOP_K = 2
EXPERT_FF = 512
ROW_BLOCK = 128
EPS = 1e-6
LAM_INIT = 0.2

LANES = 128
DT_PAD = LANES
MAIN_W = SSD_W + XBC_W + 3 * ATTN_W
COL_Z = 0
COL_Q = SSD_W
COL_K = COL_Q + ATTN_W
COL_XBC = COL_K + ATTN_W
COL_V = COL_XBC + XBC_W
VMEM_LIMIT = 56 * 1024 * 1024
NEG_BIG = -1e30
Q_SCALE = float(DIFF_HEAD_DIM ** -0.5 * np.log2(np.e))

_f32 = jnp.float32
_bf16 = jnp.bfloat16


def _silu(v):
    return v * (1.0 / (1.0 + jnp.exp(-v)))


def _rms(v, g):
    return v * lax.rsqrt(jnp.mean(v * v, axis=-1, keepdims=True) + EPS) * g


_HI16 = np.uint32(0xFFFF0000)


def _pack_bf16_pair(lo, hi):
    lo_w = lax.bitcast_convert_type(lo.astype(_bf16).astype(_f32), jnp.uint32) >> 16
    hi_w = lax.bitcast_convert_type(hi.astype(_bf16).astype(_f32), jnp.uint32) & _HI16
    return lo_w | hi_w


def _unpack_bf16_pair(w):
    lo = lax.bitcast_convert_type(w << 16, _f32)
    hi = lax.bitcast_convert_type(w & _HI16, _f32)
    return lo, hi


ROW_TILE = 8
assert D_MODEL // 2 == ROW_TILE * LANES


def _store_row_tiles(ref, words, n):
    for c in range(ROW_TILE):
        ref[pl.ds(c, n, stride=ROW_TILE), :] = words[:, c * LANES:(c + 1) * LANES]


def _load_row_tiles(ref, n):
    return jnp.concatenate(
        [ref[pl.ds(c, n, stride=ROW_TILE), :] for c in range(ROW_TILE)], axis=1)


IN_TM = 1024
IN_TN = 512


def _inproj_kernel(x_ref, g_ref, w_ref, wdt_ref, rc_ref, rp_ref, rm_ref,
                   out_ref, dt_ref, n_scr):
    j = pl.program_id(1)

    @pl.when(j == 0)
    def _():
        n = _rms(x_ref[...], g_ref[...])
        n_scr[...] = n.astype(_bf16)
        dt_ref[...] = jnp.dot(n_scr[...], wdt_ref[...], preferred_element_type=_f32)

    acc = jnp.dot(n_scr[...], w_ref[...], preferred_element_type=_f32)
    q_lo, k_lo, k_hi = COL_Q // IN_TN, COL_K // IN_TN, (COL_K + ATTN_W) // IN_TN
    is_rot = jnp.logical_and(j >= q_lo, j < k_hi)

    @pl.when(is_rot)
    def _():
        scale = jnp.where(j < k_lo, Q_SCALE, 1.0).astype(_f32)
        rc, rp, rm = rc_ref[...], rp_ref[...], rm_ref[...]
        for hb in range(IN_TN // LANES):
            t = acc[:, hb * LANES:(hb + 1) * LANES]
            r = (t * rc + pltpu.roll(t, ROT_DIM // 2, 1) * rp
                 + pltpu.roll(t, LANES - ROT_DIM // 2, 1) * rm)
            out_ref[:, hb * LANES:(hb + 1) * LANES] = (r * scale).astype(_bf16)

    @pl.when(jnp.logical_not(is_rot))
    def _():
        out_ref[...] = acc.astype(_bf16)


def _inproj(x2, g, w_main, w_dt, rc, rp, rm):
    t = x2.shape[0]
    grid = (t // IN_TM, MAIN_W // IN_TN)
    return pl.pallas_call(
        _inproj_kernel,
        grid=grid,
        in_specs=[
            pl.BlockSpec((IN_TM, D_MODEL), lambda i, j: (i, 0)),
            pl.BlockSpec((1, D_MODEL), lambda i, j: (0, 0)),
            pl.BlockSpec((D_MODEL, IN_TN), lambda i, j: (0, j)),
            pl.BlockSpec((D_MODEL, DT_PAD), lambda i, j: (0, 0)),
            pl.BlockSpec((IN_TM, LANES), lambda i, j: (i, 0)),
            pl.BlockSpec((IN_TM, LANES), lambda i, j: (i, 0)),
            pl.BlockSpec((IN_TM, LANES), lambda i, j: (i, 0)),
        ],
        out_specs=[
            pl.BlockSpec((IN_TM, IN_TN), lambda i, j: (i, j)),
            pl.BlockSpec((IN_TM, DT_PAD), lambda i, j: (i, 0)),
        ],
        out_shape=[
            jax.ShapeDtypeStruct((t, MAIN_W), _bf16),
            jax.ShapeDtypeStruct((t, DT_PAD), _f32),
        ],
        scratch_shapes=[pltpu.VMEM((IN_TM, D_MODEL), _bf16)],
        compiler_params=pltpu.CompilerParams(
            dimension_semantics=("arbitrary", "arbitrary"),
            vmem_limit_bytes=VMEM_LIMIT),
        name="inproj",
    )(x2, g, w_main, w_dt, rc, rp, rm)


CONV_HALO = 16


def _ssd_kernel(xbc_ref, dt_ref, cw_ref, cb_ref, dtb_ref, a_ref, dsk_ref,
                y_ref, xpad, xact, dts, yacc, state, seq):
    n_chunks = seq // CHUNK

    zeros_halo = jnp.zeros((CONV_HALO, XBC_W), _bf16)
    xpad[0:CONV_HALO, :] = zeros_halo
    xpad[CONV_HALO + seq:CONV_HALO + seq + CONV_HALO, :] = zeros_halo
    xpad[CONV_HALO:CONV_HALO + seq, :] = xbc_ref[0]
    cw = cw_ref[...]
    cb = cb_ref[...]
    win = CHUNK + 2 * CONV_HALO

    def conv_body(c, carry):
        r0 = pl.multiple_of(c * CHUNK, CHUNK)
        blk = xpad[pl.ds(r0, win), :].astype(_f32)
        acc = blk * cw[CONV_W // 2:CONV_W // 2 + 1, :]
        for k in range(CONV_W):
            sh = CONV_W // 2 - k
            if sh == 0:
                continue
            acc = acc + pltpu.roll(blk, sh % win, 0) * cw[k:k + 1, :]
        v = acc[CONV_HALO:CONV_HALO + CHUNK, :] + cb
        xact[pl.ds(r0, CHUNK), :] = _silu(v).astype(_bf16)
        return carry

    lax.fori_loop(0, n_chunks, conv_body, 0)

    raw = dt_ref[0] + dtb_ref[...]
    dts[...] = jnp.maximum(raw, 0.0) + jnp.log1p(jnp.exp(-jnp.abs(raw)))

    a_row = a_ref[...]
    dsk = dsk_ref[...]
    row_i = lax.broadcasted_iota(jnp.int32, (CHUNK, CHUNK), 0)
    col_i = lax.broadcasted_iota(jnp.int32, (CHUNK, CHUNK), 1)

    def chunk_step(c, reverse):
        r0 = pl.multiple_of(c * CHUNK, CHUNK)
        dtc = dts[pl.ds(r0, CHUNK), :]
        cs = dtc * a_row
        k = 1
        while k < CHUNK:
            if reverse:
                cs = cs + jnp.where(row_i < CHUNK - k, pltpu.roll(cs, CHUNK - k, 0), 0.0)
            else:
                cs = cs + jnp.where(row_i >= k, pltpu.roll(cs, k, 0), 0.0)
            k *= 2
        cs_t = cs.T
        dt_t = dtc.T
        end_col = cs_t[:, 0:1] if reverse else cs_t[:, CHUNK - 1:CHUNK]
        w_t = jnp.exp(end_col - cs_t) * dt_t
        exp_cs = jnp.exp(cs)
        tri = (row_i <= col_i) if reverse else (row_i >= col_i)
        lane0 = SSD_HEADS if reverse else 0
        for g in range(SSD_GROUPS):
            b_g = xact[pl.ds(r0, CHUNK), SSD_W + g * SSD_STATE:SSD_W + (g + 1) * SSD_STATE]
            c_g = xact[pl.ds(r0, CHUNK),
                       SSD_W + (SSD_GROUPS + g) * SSD_STATE:SSD_W + (SSD_GROUPS + g + 1) * SSD_STATE]
            cbm = lax.dot_general(c_g, b_g, (((1,), (1,)), ((), ())),
                                  preferred_element_type=_f32)
            b_t = b_g.astype(_f32).T
            c_f = c_g.astype(_f32)
            for hh in range(HEADS_PER_GROUP):
                h = g * HEADS_PER_GROUP + hh
                ln = lane0 + h
                xs_h = xact[pl.ds(r0, CHUNK), h * SSD_HEADDIM:(h + 1) * SSD_HEADDIM]
                seg = cs[:, ln:ln + 1] - cs_t[ln:ln + 1, :]
                m_h = cbm * jnp.exp(jnp.where(tri, seg, NEG_BIG)) * dt_t[ln:ln + 1, :]
                st = state[h]
                y_h = jnp.dot(m_h.astype(_bf16), xs_h, preferred_element_type=_f32)
                y_h = y_h + jnp.dot((c_f * exp_cs[:, ln:ln + 1]).astype(_bf16),
                                    st.astype(_bf16), preferred_element_type=_f32)
                upd = jnp.dot((b_t * w_t[ln:ln + 1, :]).astype(_bf16), xs_h,
                              preferred_element_type=_f32)
                if reverse:
                    dec = exp_cs[0:1, ln:ln + 1]
                else:
                    dec = exp_cs[CHUNK - 1:CHUNK, ln:ln + 1]
                state[h] = st * dec + upd
                cols = slice(h * SSD_HEADDIM, (h + 1) * SSD_HEADDIM)
                if reverse:
                    y_ref[0, pl.ds(r0, CHUNK), cols] = (
                        yacc[pl.ds(r0, CHUNK), cols] + y_h).astype(_bf16)
                else:
                    yacc[pl.ds(r0, CHUNK), cols] = (
                        y_h + xs_h.astype(_f32) * dsk[:, cols])

    state[...] = jnp.zeros_like(state)

    def fwd_body(c, carry):
        chunk_step(c, False)
        return carry

    lax.fori_loop(0, n_chunks, fwd_body, 0)

    state[...] = jnp.zeros_like(state)

    def bwd_body(i, carry):
        chunk_step(n_chunks - 1 - i, True)
        return carry

    lax.fori_loop(0, n_chunks, bwd_body, 0)


def _ssd(proj3, dt3, conv_w, conv_b, dt_bias, a_row, d_skip_row):
    b, seq, _ = proj3.shape
    kern = functools.partial(_ssd_kernel, seq=seq)
    assert COL_XBC % XBC_W == 0
    return pl.pallas_call(
        kern,
        grid=(b,),
        in_specs=[
            pl.BlockSpec((1, seq, XBC_W), lambda i: (i, 0, COL_XBC // XBC_W)),
            pl.BlockSpec((1, seq, DT_PAD), lambda i: (i, 0, 0)),
            pl.BlockSpec((CONV_W, XBC_W), lambda i: (0, 0)),
            pl.BlockSpec((1, XBC_W), lambda i: (0, 0)),
            pl.BlockSpec((1, DT_PAD), lambda i: (0, 0)),
            pl.BlockSpec((1, DT_PAD), lambda i: (0, 0)),
            pl.BlockSpec((1, SSD_W), lambda i: (0, 0)),
        ],
        out_specs=pl.BlockSpec((1, seq, SSD_W), lambda i: (i, 0, 0)),
        out_shape=jax.ShapeDtypeStruct((b, seq, SSD_W), _bf16),
        scratch_shapes=[
            pltpu.VMEM((seq + 2 * CONV_HALO, XBC_W), _bf16),
            pltpu.VMEM((seq, XBC_W), _bf16),
            pltpu.VMEM((seq, DT_PAD), _f32),
            pltpu.VMEM((seq, SSD_W), _f32),
            pltpu.VMEM((SSD_HEADS, SSD_STATE, SSD_HEADDIM), _f32),
        ],
        compiler_params=pltpu.CompilerParams(
            dimension_semantics=("arbitrary",),
            vmem_limit_bytes=VMEM_LIMIT),
        name="ssd",
    )(proj3, dt3, conv_w, conv_b, dt_bias, a_row, d_skip_row)


ATT_QB = 256
ATT_UNROLL = 8


def _attn_kernel(q_ref, k_ref, v_ref, lam_ref, g_ref, o_ref, v1_scr, *, seq):
    hw = 2 * DIFF_HEAD_DIM
    lv = lam_ref[...]
    lam = (jnp.exp(jnp.sum(lv[0:1] * lv[1:2], axis=-1, keepdims=True))
           - jnp.exp(jnp.sum(lv[2:3] * lv[3:4], axis=-1, keepdims=True)) + LAM_INIT)
    k = k_ref[0]
    g = g_ref[...]
    v1_scr[:, 0:hw] = v_ref[0]
    v1_scr[:, hw:2 * hw] = (lax.broadcasted_iota(jnp.int32, (seq, hw), 1) == 0).astype(_bf16)
    first = lax.broadcasted_iota(jnp.int32, (1, hw), 1) < DIFF_HEAD_DIM
    nt = (((1,), (1,)), ((), ()))

    def chain(r0):
        q = q_ref[0, pl.ds(r0, ATT_QB), :]
        zero = jnp.zeros_like(q)
        ovs = []
        for qc in (jnp.where(first, q, zero), jnp.where(first, zero, q)):
            s = lax.dot_general(qc, k, nt, preferred_element_type=_f32)
            e = jnp.exp2(s - jnp.max(s, axis=-1, keepdims=True)).astype(_bf16)
            ovs.append(jnp.dot(e, v1_scr[...], preferred_element_type=_f32))
        o1, l1 = ovs[0][:, 0:hw], ovs[0][:, hw:hw + 1]
        o2, l2 = ovs[1][:, 0:hw], ovs[1][:, hw:hw + 1]
        o = o1 / l1 - (lam / l2) * o2
        o = _rms(o, g) * (1.0 - LAM_INIT)
        o_ref[0, pl.ds(r0, ATT_QB), :] = o.astype(_bf16)

    step = ATT_QB * ATT_UNROLL

    def body(i, carry):
        for u in range(ATT_UNROLL):
            chain(pl.multiple_of(i * step + u * ATT_QB, ATT_QB))
        return carry

    lax.fori_loop(0, seq // step, body, 0)


def _attn(proj3, lam_vecs, subln_g):
    b, seq, _ = proj3.shape
    hw = 2 * DIFF_HEAD_DIM
    kern = functools.partial(_attn_kernel, seq=seq)
    return pl.pallas_call(
        kern,
        grid=(b, DIFF_HEADS),
        in_specs=[
            pl.BlockSpec((1, seq, hw), lambda i, h: (i, 0, COL_Q // hw + h)),
            pl.BlockSpec((1, seq, hw), lambda i, h: (i, 0, COL_K // hw + h)),
            pl.BlockSpec((1, seq, hw), lambda i, h: (i, 0, COL_V // hw + h)),
            pl.BlockSpec((4, DIFF_HEAD_DIM), lambda i, h: (0, 0)),
            pl.BlockSpec((1, hw), lambda i, h: (0, 0)),
        ],
        out_specs=pl.BlockSpec((1, seq, hw), lambda i, h: (i, 0, h)),
        out_shape=jax.ShapeDtypeStruct((b, seq, ATTN_W), _bf16),
        scratch_shapes=[pltpu.VMEM((seq, 2 * hw), _bf16)],
        compiler_params=pltpu.CompilerParams(
            dimension_semantics=("arbitrary", "arbitrary"),
            vmem_limit_bytes=VMEM_LIMIT),
        name="diffattn",
    )(proj3, proj3, proj3, lam_vecs, subln_g)


OUT_TM = 256
ROUTE_ROWS = 8


def _outproj_kernel(y_ref, z_ref, att_ref, x_ref, w_ref, gs_ref, gf_ref, wr_ref, br_ref,
                    h_ref, hn_ref, route_ref, cnt_ref, gate_ref, cnt_scr):
    @pl.when(pl.program_id(0) == 0)
    def _():
        cnt_scr[...] = jnp.zeros_like(cnt_scr)

    y = y_ref[...].astype(_f32)
    z = z_ref[...].astype(_f32)
    s = _rms(y * _silu(z), gs_ref[...]).astype(_bf16)
    acc = jnp.dot(s, w_ref[0:SSD_W, :], preferred_element_type=_f32)
    acc = acc + jnp.dot(att_ref[...], w_ref[SSD_W:SSD_W + ATTN_W, :], preferred_element_type=_f32)
    h1 = x_ref[...] + acc
    h_ref[...] = h1
    hn = _rms(h1, gf_ref[...])
    half = D_MODEL // 2
    _store_row_tiles(hn_ref, _pack_bf16_pair(hn[:, 0:half], hn[:, half:D_MODEL]), hn.shape[0])
    hn_hi = hn.astype(_bf16)
    hn_lo = (hn - hn_hi.astype(_f32)).astype(_bf16)
    a = jnp.dot(hn_hi, wr_ref[...], preferred_element_type=_f32)
    bb = jnp.dot(hn_lo, wr_ref[:, 0:LANES], preferred_element_type=_f32)
    logits = (a[:, 0:LANES] + a[:, LANES:2 * LANES] + bb + br_ref[...]).T
    tm = logits.shape[1]
    iota = lax.broadcasted_iota(jnp.int32, (MOE_GROUPS, tm), 0)

    def first_argmax(val, vmax):
        return jnp.min(jnp.where(val == vmax, iota, MOE_GROUPS), axis=0, keepdims=True)

    gl = logits[0:MOE_GROUPS]
    gmax = jnp.max(gl, axis=0, keepdims=True)
    p_g = 1.0 / jnp.sum(jnp.exp(gl - gmax), axis=0, keepdims=True)
    g_sel = first_argmax(gl, gmax)
    el = jnp.zeros((EXPERTS_PER_GROUP, tm), _f32)
    for gi in range(MOE_GROUPS):
        lo = MOE_GROUPS + gi * EXPERTS_PER_GROUP
        el = jnp.where(g_sel == gi, logits[lo:lo + EXPERTS_PER_GROUP], el)
    ee = jnp.exp(el - jnp.max(el, axis=0, keepdims=True))
    pe = ee / jnp.sum(ee, axis=0, keepdims=True)
    p1 = jnp.max(pe, axis=0, keepdims=True)
    i1 = first_argmax(pe, p1)
    pe2 = jnp.where(iota == i1, -1.0, pe)
    p2 = jnp.max(pe2, axis=0, keepdims=True)
    i2 = first_argmax(pe2, p2)
    den = p1 + p2
    e1 = g_sel * EXPERTS_PER_GROUP + i1
    e2 = g_sel * EXPERTS_PER_GROUP + i2
    iota_e = lax.broadcasted_iota(jnp.int32, (N_EXPERTS, tm), 0)
    oh1 = (iota_e == e1).astype(_f32)
    oh2 = (iota_e == e2).astype(_f32)
    both = oh1 + oh2
    earlier = (lax.broadcasted_iota(jnp.int32, (tm, tm), 0)
               < lax.broadcasted_iota(jnp.int32, (tm, tm), 1)).astype(_bf16)
    before = cnt_scr[:, 0:1] + jnp.dot(both.astype(_bf16), earlier,
                                       preferred_element_type=_f32)
    r1 = jnp.sum(oh1 * before, axis=0, keepdims=True).astype(jnp.int32)
    r2 = jnp.sum(oh2 * before, axis=0, keepdims=True).astype(jnp.int32)
    cnt_scr[...] = cnt_scr[...] + jnp.sum(both, axis=1, keepdims=True)
    cnt_ref[...] = cnt_scr[...].astype(jnp.int32)
    route_ref[...] = jnp.where(iota == 0, e1, jnp.where(iota == 1, e2,
                               jnp.where(iota == 2, r1, jnp.where(iota == 3, r2, 0))))
    g8 = jnp.where(iota == 0, p_g * (p1 / den), jnp.where(iota == 1, p_g * (p2 / den), 0.0))
    gate_ref[...] = jnp.concatenate(
        [g8, jnp.zeros((LANES - ROUTE_ROWS, tm), _f32)], axis=0).T


def _outproj(y2, proj2, att2, x2, w_out, g_ssd, g_ffn, wr_t, br_col):
    t = x2.shape[0]
    row = lambda i: (i, 0)
    fix = lambda i: (0, 0)
    return pl.pallas_call(
        _outproj_kernel,
        grid=(t // OUT_TM,),
        in_specs=[
            pl.BlockSpec((OUT_TM, SSD_W), row),
            pl.BlockSpec((OUT_TM, SSD_W), row),
            pl.BlockSpec((OUT_TM, ATTN_W), row),
            pl.BlockSpec((OUT_TM, D_MODEL), row),
            pl.BlockSpec((SSD_W + ATTN_W, D_MODEL), fix),
            pl.BlockSpec((1, SSD_W), fix),
            pl.BlockSpec((1, D_MODEL), fix),
            pl.BlockSpec((D_MODEL, 2 * LANES), fix),
            pl.BlockSpec((1, LANES), fix),
        ],
        out_specs=[
            pl.BlockSpec((OUT_TM, D_MODEL), row),
            pl.BlockSpec((OUT_TM * ROW_TILE, LANES), row),
            pl.BlockSpec((ROUTE_ROWS, OUT_TM), lambda i: (0, i)),
            pl.BlockSpec((N_EXPERTS, LANES), fix),
            pl.BlockSpec((OUT_TM, LANES), row),
        ],
        out_shape=[
            jax.ShapeDtypeStruct((t, D_MODEL), _f32),
            jax.ShapeDtypeStruct((t * ROW_TILE, LANES), jnp.uint32),
            jax.ShapeDtypeStruct((ROUTE_ROWS, t), jnp.int32),
            jax.ShapeDtypeStruct((N_EXPERTS, LANES), jnp.int32),
            jax.ShapeDtypeStruct((t, LANES), _f32),
        ],
        scratch_shapes=[pltpu.VMEM((N_EXPERTS, LANES), _f32)],
        compiler_params=pltpu.CompilerParams(
            dimension_semantics=("arbitrary",),
            vmem_limit_bytes=VMEM_LIMIT),
        name="outproj_router",
    )(y2, proj2, att2, x2, w_out, g_ssd, g_ffn, wr_t, br_col)


def _n_rows(n_tok):
    n_assign = n_tok * TOP_K
    return (n_assign + N_EXPERTS * (ROW_BLOCK - 1) + ROW_BLOCK - 1) // ROW_BLOCK * ROW_BLOCK


def _prefix_sum(v, axis):
    n = v.shape[axis]
    idx = lax.broadcasted_iota(jnp.int32, v.shape, axis)
    k = 1
    while k < n:
        v = v + jnp.where(idx >= k, pltpu.roll(v, k, axis), 0)
        k *= 2
    return v


def _plan_kernel(route_ref, cnt_ref, pos_ref, blk_ref, meta_ref, *, n_blocks_pad):
    n_tok = route_ref.shape[1]
    cnt_col = jnp.concatenate(
        [cnt_ref[...], jnp.zeros((LANES - N_EXPERTS, LANES), jnp.int32)], axis=0)
    pad_up = lambda c: (c + (ROW_BLOCK - 1)) & (-ROW_BLOCK)
    ends_col = _prefix_sum(pad_up(cnt_col), 0)
    starts_col = (ends_col - pad_up(cnt_col)).astype(_f32)
    cnt_row = cnt_col.astype(_f32).T.astype(jnp.int32)
    ends_row = _prefix_sum(pad_up(cnt_row), 1)
    starts_row = ends_row - pad_up(cnt_row)
    n_used = ends_row[0:1, LANES - 1:LANES] >> (ROW_BLOCK.bit_length() - 1)

    ch = 1024
    iota_e = lax.broadcasted_iota(jnp.int32, (LANES, ch), 0)
    for c0 in range(0, n_tok, ch):
        rt = route_ref[:, c0:c0 + ch]
        s1 = jnp.sum(jnp.where(iota_e == rt[0:1], starts_col[:, 0:1], 0.0), axis=0, keepdims=True)
        s2 = jnp.sum(jnp.where(iota_e == rt[1:2], starts_col[:, 0:1], 0.0), axis=0, keepdims=True)
        p1 = s1.astype(jnp.int32) + rt[2:3]
        p2 = s2.astype(jnp.int32) + rt[3:4]
        sub = lax.broadcasted_iota(jnp.int32, (ROUTE_ROWS, ch), 0)
        pos_ref[:, c0:c0 + ch] = jnp.where(sub == 0, p1, jnp.where(sub == 1, p2, 0))

    blk_start = lax.broadcasted_iota(jnp.int32, (LANES, n_blocks_pad), 1) * ROW_BLOCK
    e_idx = lax.broadcasted_iota(jnp.int32, (LANES, n_blocks_pad), 0)
    real = e_idx < N_EXPERTS
    be = jnp.sum(jnp.where(jnp.logical_and(real, ends_col[:, 0:1] <= blk_start), 1.0, 0.0),
                 axis=0, keepdims=True).astype(jnp.int32)
    be = jnp.minimum(be, N_EXPERTS - 1)
    last = jnp.max(jnp.where(jnp.logical_and(real, cnt_col[:, 0:1] > 0), e_idx.astype(_f32), 0.0),
                   axis=0, keepdims=True).astype(jnp.int32)
    blk_i = lax.broadcasted_iota(jnp.int32, (1, n_blocks_pad), 1)
    be = jnp.where(blk_i < n_used, be, last)
    sub = lax.broadcasted_iota(jnp.int32, (ROUTE_ROWS, n_blocks_pad), 0)
    blk_ref[...] = jnp.where(sub == 0, be, jnp.where(sub == 1, n_used, 0))
    sub = lax.broadcasted_iota(jnp.int32, (ROUTE_ROWS, LANES), 0)
    meta_ref[...] = jnp.where(sub == 0, cnt_row[0:1], jnp.where(sub == 1, starts_row[0:1], 0))


def _plan(route, cnt):
    n_tok = route.shape[1]
    n_blocks_pad = -(-(_n_rows(n_tok) // ROW_BLOCK) // LANES) * LANES
    return pl.pallas_call(
        functools.partial(_plan_kernel, n_blocks_pad=n_blocks_pad),
        out_shape=[
            jax.ShapeDtypeStruct((ROUTE_ROWS, n_tok), jnp.int32),
            jax.ShapeDtypeStruct((ROUTE_ROWS, n_blocks_pad), jnp.int32),
            jax.ShapeDtypeStruct((ROUTE_ROWS, LANES), jnp.int32),
        ],
        compiler_params=pltpu.CompilerParams(vmem_limit_bytes=VMEM_LIMIT),
        name="route_plan",
    )(route, cnt)


DISP_TM = 256


def _dispatch_kernel(pos_ref, cnt_ref, start_ref, nused_ref, h_ref, x_hbm, stage, zrow, sem, zsem,
                     *, n_tok, n_blocks):
    i = pl.program_id(0)
    n_steps = pl.num_programs(0)
    slot = i % 2
    base = i * DISP_TM
    tile = lambda r: pl.ds(pl.multiple_of(r * ROW_TILE, ROW_TILE), ROW_TILE)

    def wait_slot(sl):
        for _ in range(TOP_K):
            pltpu.make_async_copy(stage.at[sl], x_hbm.at[pl.ds(0, DISP_TM * ROW_TILE), :],
                                  sem.at[sl]).wait()

    @pl.when(i >= 2)
    def _():
        wait_slot(slot)

    stage[slot] = h_ref[...]

    def body(r, carry):
        for kk in range(TOP_K):
            row = pos_ref[kk * n_tok + base + r]
            pltpu.make_async_copy(stage.at[slot, tile(r), :], x_hbm.at[tile(row), :],
                                  sem.at[slot]).start()
        return carry

    lax.fori_loop(0, DISP_TM, body, 0, unroll=8)

    @pl.when(i == n_steps - 1)
    def _():
        zrow[...] = jnp.zeros_like(zrow)

        def per_expert(e, n_fill):
            cnt = cnt_ref[e]
            start = start_ref[e]
            end = (cnt + (ROW_BLOCK - 1)) & (-ROW_BLOCK)

            def fill(r, carry):
                pltpu.make_async_copy(zrow.at[tile(0), :], x_hbm.at[tile(start + r), :], zsem).start()
                return carry

            lax.fori_loop(cnt, end, fill, 0)
            return n_fill + (end - cnt)

        n_fill = lax.fori_loop(0, N_EXPERTS, per_expert, 0)
        blk_rows = ROW_BLOCK * ROW_TILE

        def fill_block(blk, carry):
            dst = pl.ds(pl.multiple_of(blk * blk_rows, blk_rows), blk_rows)
            pltpu.make_async_copy(zrow, x_hbm.at[dst, :], zsem).start()
            return carry

        lax.fori_loop(nused_ref[0], n_blocks, fill_block, 0)

        def wait_row(r, carry):
            pltpu.make_async_copy(zrow.at[tile(0), :], x_hbm.at[tile(0), :], zsem).wait()
            return carry

        lax.fori_loop(0, n_fill, wait_row, 0)

        def wait_block(blk, carry):
            pltpu.make_async_copy(zrow, x_hbm.at[pl.ds(0, blk_rows), :], zsem).wait()
            return carry

        lax.fori_loop(nused_ref[0], n_blocks, wait_block, 0)

        @pl.when(n_steps >= 2)
        def _():
            wait_slot(1 - slot)

        wait_slot(slot)


def _dispatch_rows(pos, counts, starts, n_used, hn_packed):
    t = hn_packed.shape[0] // ROW_TILE
    n_rows = _n_rows(t)
    blk = (DISP_TM * ROW_TILE, LANES)
    grid_spec = pltpu.PrefetchScalarGridSpec(
        num_scalar_prefetch=4,
        grid=(t // DISP_TM,),
        in_specs=[pl.BlockSpec(blk, lambda i, *_: (i, 0))],
        out_specs=pl.BlockSpec(memory_space=pl.ANY),
        scratch_shapes=[
            pltpu.VMEM((2,) + blk, hn_packed.dtype),
            pltpu.VMEM((ROW_BLOCK * ROW_TILE, LANES), hn_packed.dtype),
            pltpu.SemaphoreType.DMA((2,)),
            pltpu.SemaphoreType.DMA(()),
        ],
    )
    return pl.pallas_call(
        functools.partial(_dispatch_kernel, n_tok=t, n_blocks=n_rows // ROW_BLOCK),
        grid_spec=grid_spec,
        out_shape=jax.ShapeDtypeStruct((n_rows * ROW_TILE, LANES), hn_packed.dtype),
        compiler_params=pltpu.CompilerParams(
            dimension_semantics=("arbitrary",),
            vmem_limit_bytes=VMEM_LIMIT),
        name="dispatch_rows",
    )(pos, counts, starts, n_used, hn_packed)


MOE_W_SPLIT = 4


def _moe_kernel(be_ref, nused_ref, x_ref, wg_hbm, wu_hbm, wd_hbm,
                out_ref, wg_f, wu_f, wd_f, wg_s, wu_s, wd_s, wsem, nexp, *, n_blocks):
    i = pl.program_id(0)
    nused = nused_ref[0]
    half = D_MODEL // 2

    def weight_copies(e, sl):
        cps = []
        for src, dst in ((wg_hbm, wg_f), (wu_hbm, wu_f), (wd_hbm, wd_f)):
            rows = src.shape[1] // MOE_W_SPLIT
            for c in range(MOE_W_SPLIT):
                rs = pl.ds(c * rows, rows)
                cps.append(pltpu.make_async_copy(src.at[e, rs, :], dst.at[sl, rs, :], wsem.at[sl]))
        return cps

    @pl.when(i == 0)
    def _():
        nexp[0] = 0
        for cp in weight_copies(be_ref[0], 0):
            cp.start()

    @pl.when(i < nused)
    def _():
        e = be_ref[i]
        new_expert = jnp.logical_or(i == 0, e != be_ref[jnp.maximum(i - 1, 0)])

        @pl.when(new_expert)
        def _():
            sl = nexp[0] % 2
            nxt = lax.while_loop(
                lambda b: jnp.logical_and(b < nused, be_ref[jnp.minimum(b, n_blocks - 1)] == e),
                lambda b: b + 1, i + 1)

            @pl.when(nxt < nused)
            def _():
                for cp in weight_copies(be_ref[jnp.minimum(nxt, n_blocks - 1)], 1 - sl):
                    cp.start()

            for cp in weight_copies(e, sl):
                cp.wait()
            wg_s[...] = wg_f[sl].astype(_bf16)
            wu_s[...] = wu_f[sl].astype(_bf16)
            wd_s[...] = wd_f[sl].astype(_bf16)
            nexp[0] = nexp[0] + 1

        x_lo, x_hi = _unpack_bf16_pair(_load_row_tiles(x_ref, ROW_BLOCK))
        x_lo, x_hi = x_lo.astype(_bf16), x_hi.astype(_bf16)
        gg = (jnp.dot(x_lo, wg_s[0:half, :], preferred_element_type=_f32)
              + jnp.dot(x_hi, wg_s[half:D_MODEL, :], preferred_element_type=_f32))
        uu = (jnp.dot(x_lo, wu_s[0:half, :], preferred_element_type=_f32)
              + jnp.dot(x_hi, wu_s[half:D_MODEL, :], preferred_element_type=_f32))
        a = (_silu(gg) * uu).astype(_bf16)
        y = jnp.dot(a, wd_s[...], preferred_element_type=_f32)
        _store_row_tiles(out_ref, _pack_bf16_pair(y[:, 0:half], y[:, half:D_MODEL]), ROW_BLOCK)

    @pl.when(i >= nused)
    def _():
        out_ref[...] = jnp.zeros_like(out_ref)


def _moe(block_expert, n_used, x_rows, w_gate, w_up, w_down):
    n_blocks = x_rows.shape[0] // (ROW_BLOCK * ROW_TILE)
    blk = (ROW_BLOCK * ROW_TILE, LANES)
    omap = lambda i, be, nu: (i, 0)
    grid_spec = pltpu.PrefetchScalarGridSpec(
        num_scalar_prefetch=2,
        grid=(n_blocks,),
        in_specs=[
            pl.BlockSpec(blk, omap),
            pl.BlockSpec(memory_space=pl.ANY),
            pl.BlockSpec(memory_space=pl.ANY),
            pl.BlockSpec(memory_space=pl.ANY),
        ],
        out_specs=pl.BlockSpec(blk, omap),
        scratch_shapes=[
            pltpu.VMEM((2, D_MODEL, EXPERT_FF), _f32),
            pltpu.VMEM((2, D_MODEL, EXPERT_FF), _f32),
            pltpu.VMEM((2, EXPERT_FF, D_MODEL), _f32),
            pltpu.VMEM((D_MODEL, EXPERT_FF), _bf16),
            pltpu.VMEM((D_MODEL, EXPERT_FF), _bf16),
            pltpu.VMEM((EXPERT_FF, D_MODEL), _bf16),
            pltpu.SemaphoreType.DMA((2,)),
            pltpu.SMEM((1,), jnp.int32),
        ],
    )
    return pl.pallas_call(
        functools.partial(_moe_kernel, n_blocks=n_blocks),
        grid_spec=grid_spec,
        out_shape=jax.ShapeDtypeStruct(x_rows.shape, x_rows.dtype),
        compiler_params=pltpu.CompilerParams(
            dimension_semantics=("arbitrary",),
            vmem_limit_bytes=VMEM_LIMIT),
        name="moe_experts",
    )(block_expert, n_used, x_rows, w_gate, w_up, w_down)


TAIL_TM = 256


def _tail_kernel(pos_ref, h_ref, gate_ref, y_hbm, p_ref, wpp_ref, gp_ref, wpg_ref, bpg_ref,
                 gfin_ref, out_ref, ybuf, sem, *, n_tok):
    i = pl.program_id(0)
    n_steps = pl.num_programs(0)
    slot = i % 2
    tile = lambda r: pl.ds(pl.multiple_of(r * ROW_TILE, ROW_TILE), ROW_TILE)

    def start_row(base, r, sl):
        for kk in range(TOP_K):
            row = pos_ref[kk * n_tok + base + r]
            pltpu.make_async_copy(y_hbm.at[tile(row), :], ybuf.at[sl, kk, tile(r), :],
                                  sem.at[sl]).start()

    def wait_slot(sl):
        for kk in range(TOP_K):
            pltpu.make_async_copy(y_hbm.at[pl.ds(0, TAIL_TM * ROW_TILE), :], ybuf.at[sl, kk],
                                  sem.at[sl]).wait()

    @pl.when(i == 0)
    def _():
        def body(r, carry):
            start_row(0, r, 0)
            return carry

        lax.fori_loop(0, TAIL_TM, body, 0, unroll=8)

    wait_slot(slot)

    nxt_base = jnp.minimum(i + 1, n_steps - 1) * TAIL_TM
    for r in range(TAIL_TM):
        start_row(nxt_base, r, 1 - slot)

    gates = gate_ref[...]
    g1, g2 = gates[:, 0:1], gates[:, 1:2]
    y1_lo, y1_hi = _unpack_bf16_pair(_load_row_tiles(ybuf.at[slot, 0], TAIL_TM))
    y2_lo, y2_hi = _unpack_bf16_pair(_load_row_tiles(ybuf.at[slot, 1], TAIL_TM))
    moe = jnp.concatenate([g1 * y1_lo + g2 * y2_lo, g1 * y1_hi + g2 * y2_hi], axis=1)
    h2 = h_ref[...] + moe
    ple = _rms(jnp.dot(p_ref[...].astype(_bf16), wpp_ref[...], preferred_element_type=_f32),
               gp_ref[...])
    lg = jnp.dot(h2.astype(_bf16), wpg_ref[...], preferred_element_type=_f32) + bpg_ref[...]
    h3 = h2 + (1.0 / (1.0 + jnp.exp(-lg))) * ple
    out_ref[...] = _rms(h3, gfin_ref[...])

    @pl.when(i == n_steps - 1)
    def _():
        wait_slot(1 - slot)


def _tail(pos, h1, gates, y_rows, p2, w_pp, g_ple, w_pg, b_pg, g_fin):
    t = h1.shape[0]
    row = lambda i, ps: (i, 0)
    fix = lambda i, ps: (0, 0)
    grid_spec = pltpu.PrefetchScalarGridSpec(
        num_scalar_prefetch=1,
        grid=(t // TAIL_TM,),
        in_specs=[
            pl.BlockSpec((TAIL_TM, D_MODEL), row),
            pl.BlockSpec((TAIL_TM, LANES), row),
            pl.BlockSpec(memory_space=pl.ANY),
            pl.BlockSpec((TAIL_TM, PLE_DIM), row),
            pl.BlockSpec((PLE_DIM, D_MODEL), fix),
            pl.BlockSpec((1, D_MODEL), fix),
            pl.BlockSpec((D_MODEL, D_MODEL), fix),
            pl.BlockSpec((1, D_MODEL), fix),
            pl.BlockSpec((1, D_MODEL), fix),
        ],
        out_specs=pl.BlockSpec((TAIL_TM, D_MODEL), row),
        scratch_shapes=[
            pltpu.VMEM((2, TOP_K, TAIL_TM * ROW_TILE, LANES), y_rows.dtype),
            pltpu.SemaphoreType.DMA((2,)),
        ],
    )
    return pl.pallas_call(
        functools.partial(_tail_kernel, n_tok=t),
        grid_spec=grid_spec,
        out_shape=jax.ShapeDtypeStruct((t, D_MODEL), _f32),
        compiler_params=pltpu.CompilerParams(
            dimension_semantics=("arbitrary",),
            vmem_limit_bytes=VMEM_LIMIT),
        name="tail",
    )(pos, h1, gates, y_rows, p2, w_pp, g_ple, w_pg, b_pg, g_fin)


def _rope_tables(positions):
    half = ROT_DIM // 2
    inv_freq = ROPE_THETA ** (-jnp.arange(0, ROT_DIM, 2, dtype=_f32) / ROT_DIM)
    ang = positions.astype(_f32).reshape(-1, 1) * inv_freq
    cos, sin = jnp.cos(ang), jnp.sin(ang)
    t = ang.shape[0]
    rest = DIFF_HEAD_DIM - ROT_DIM
    zero8 = jnp.zeros((t, half), _f32)
    comp = lambda a, b, fill: jnp.concatenate([a, b, jnp.full((t, rest), fill, _f32)], axis=1)
    twice = lambda c: jnp.concatenate([c, c], axis=1)
    rc = twice(comp(cos, cos, 1.0))
    rp = twice(comp(zero8, sin, 0.0))
    rm = twice(comp(-sin, zero8, 0.0))
    return rc, rp, rm


def kernel(x, p, positions, norm_mix_g, w_in, conv_w, conv_b, dt_bias_f, dt_bias_b, a_log_f, a_log_b, d_skip, ssd_norm_g, lam_q1, lam_k1, lam_q2, lam_k2, subln_g, w_out, norm_ffn_g, w_route_group, b_route_group, w_route_expert, b_route_expert, w_exp_gate, w_exp_up, w_exp_down, w_ple_proj, ple_norm_g, w_ple_gate, b_ple_gate, final_norm_g):
    b, seq, d = x.shape
    t = b * seq
    x2 = x.reshape(t, d)
    row = lambda v: v.reshape(1, -1).astype(_f32)

    wi = w_in[0]
    o_xbc, o_dt, o_q = SSD_W, SSD_W + XBC_W, SSD_W + XBC_W + 2 * SSD_HEADS
    w_main = jnp.concatenate(
        [wi[:, :o_xbc], wi[:, o_q:o_q + 2 * ATTN_W], wi[:, o_xbc:o_dt], wi[:, o_q + 2 * ATTN_W:]],
        axis=1).astype(_bf16)
    w_dt = jnp.pad(wi[:, o_dt:o_q], ((0, 0), (0, DT_PAD - 2 * SSD_HEADS))).astype(_bf16)
    pad_dt = lambda v: jnp.pad(v, (0, DT_PAD - 2 * SSD_HEADS)).reshape(1, DT_PAD)
    dt_bias = pad_dt(jnp.concatenate([dt_bias_f[0], dt_bias_b[0]]))
    a_row = pad_dt(jnp.concatenate([-jnp.exp(a_log_f[0]), -jnp.exp(a_log_b[0])]))
    d_skip_row = jnp.repeat(d_skip[0], SSD_HEADDIM).reshape(1, SSD_W)
    lam_vecs = jnp.stack([lam_q1[0], lam_k1[0], lam_q2[0], lam_k2[0]])
    n_route = MOE_GROUPS + N_EXPERTS
    w_route = jnp.pad(jnp.concatenate([w_route_group[0], w_route_expert[0]], axis=1),
                      ((0, 0), (0, LANES - n_route)))
    w_route_hi = w_route.astype(_bf16)
    w_route_lo = (w_route - w_route_hi.astype(_f32)).astype(_bf16)
    w_route2 = jnp.concatenate([w_route_hi, w_route_lo], axis=1)
    b_route = jnp.pad(jnp.concatenate([b_route_group[0], b_route_expert[0]]),
                      (0, LANES - n_route)).reshape(1, LANES)
    rc, rp, rm = _rope_tables(positions)

    proj, dt = _inproj(x2, row(norm_mix_g[0]), w_main, w_dt, rc, rp, rm)
    proj3 = proj.reshape(b, seq, MAIN_W)
    y_ssd = _ssd(proj3, dt.reshape(b, seq, DT_PAD), conv_w[0], row(conv_b[0]),
                 dt_bias, a_row, d_skip_row)
    att = _attn(proj3, lam_vecs, row(subln_g[0]))
    h1, hn_packed, route, cnt, gates = _outproj(
        y_ssd.reshape(t, SSD_W), proj, att.reshape(t, ATTN_W), x2, w_out[0].astype(_bf16),
        row(ssd_norm_g[0]), row(norm_ffn_g[0]), w_route2, b_route)
    pos8, blk8, meta8 = _plan(route, cnt)
    pos = pos8[:TOP_K].reshape(-1)
    block_expert = blk8[0, :_n_rows(t) // ROW_BLOCK]
    n_used = blk8[1, :1]
    x_rows = _dispatch_rows(pos, meta8[0, :N_EXPERTS], meta8[1, :N_EXPERTS], n_used, hn_packed)
    y_rows = _moe(block_expert, n_used, x_rows, w_exp_gate[0], w_exp_up[0], w_exp_down[0])
    out = _tail(pos, h1, gates, y_rows, p[0].reshape(t, PLE_DIM), w_ple_proj[0].astype(_bf16),
                row(ple_norm_g[0]), w_ple_gate[0].astype(_bf16), row(b_ple_gate[0]),
                row(final_norm_g))
    return out.reshape(b, seq, d)
```

```python
import functools

import jax
import jax.numpy as jnp
import numpy as np
from jax import lax
from jax.experimental import pallas as pl
from jax.experimental.pallas import tpu as pltpu

D_MODEL = 2048
PLE_DIM = 256
SSD_W = 1024
ATTN_W = 1024
SSD_HEADDIM = 64
SSD_HEADS = 16
SSD_GROUPS = 2
HEADS_PER_GROUP = SSD_HEADS // SSD_GROUPS
SSD_STATE = 128
CHUNK = 128
CONV_W = 5
XBC_W = SSD_W + 2 * SSD_GROUPS * SSD_STATE
DIFF_HEAD_DIM = 64
DIFF_HEADS = 8
ROT_DIM = 16
ROPE_THETA = 500000.0
MOE_GROUPS = 8
EXPERTS_PER_GROUP = 8
N_EXPERTS = 64
TOP_K = 2
EXPERT_FF = 512
ROW_BLOCK = 128
EPS = 1e-6
LAM_INIT = 0.2

LANES = 128
DT_PAD = LANES
MAIN_W = SSD_W + XBC_W + 3 * ATTN_W
COL_Z = 0
COL_Q = SSD_W
COL_K = COL_Q + ATTN_W
COL_XBC = COL_K + ATTN_W
COL_V = COL_XBC + XBC_W
VMEM_LIMIT = 56 * 1024 * 1024
NEG_BIG = -1e30
Q_SCALE = float(DIFF_HEAD_DIM ** -0.5 * np.log2(np.e))

_f32 = jnp.float32
_bf16 = jnp.bfloat16


def _silu(v):
    return v * (1.0 / (1.0 + jnp.exp(-v)))


def _rms(v, g):
    return v * lax.rsqrt(jnp.mean(v * v, axis=-1, keepdims=True) + EPS) * g


_HI16 = np.uint32(0xFFFF0000)


def _pack_bf16_pair(lo, hi):
    lo_w = lax.bitcast_convert_type(lo.astype(_bf16).astype(_f32), jnp.uint32) >> 16
    hi_w = lax.bitcast_convert_type(hi.astype(_bf16).astype(_f32), jnp.uint32) & _HI16
    return lo_w | hi_w


def _unpack_bf16_pair(w):
    lo = lax.bitcast_convert_type(w << 16, _f32)
    hi = lax.bitcast_convert_type(w & _HI16, _f32)
    return lo, hi


ROW_TILE = 8
assert D_MODEL // 2 == ROW_TILE * LANES


def _store_row_tiles(ref, words, n):
    for c in range(ROW_TILE):
        ref[pl.ds(c, n, stride=ROW_TILE), :] = words[:, c * LANES:(c + 1) * LANES]


def _load_row_tiles(ref, n):
    return jnp.concatenate(
        [ref[pl.ds(c, n, stride=ROW_TILE), :] for c in range(ROW_TILE)], axis=1)


IN_TM = 1024
IN_TN = 512


def _inproj_kernel(x_ref, g_ref, pos_ref, invf_ref, wt_ref, wdt_ref,
                   out_ref, dt_ref, n_scr, rc_scr, rp_scr, rm_scr):
    j = pl.program_id(1)
    nt = (((1,), (1,)), ((), ()))
    half = ROT_DIM // 2

    @pl.when(j == 0)
    def _():
        n = _rms(x_ref[...], g_ref[...])
        n_scr[...] = n.astype(_bf16)
        lane = lax.broadcasted_iota(jnp.int32, (1, LANES), 1)
        dt = lax.dot_general(n_scr[...], wdt_ref[...].astype(_bf16), nt, preferred_element_type=_f32)
        dt_ref[...] = jnp.where(lane < 2 * SSD_HEADS, dt, 0.0)
        ang = pos_ref[...] * invf_ref[...]
        cos, sin = jnp.cos(ang), jnp.sin(ang)
        l64 = lane & (DIFF_HEAD_DIM - 1)
        rc_scr[...] = jnp.where(l64 < ROT_DIM, cos, 1.0)
        rp_scr[...] = jnp.where(jnp.logical_and(l64 >= half, l64 < ROT_DIM), sin, 0.0)
        rm_scr[...] = jnp.where(l64 < half, -sin, 0.0)

    acc = lax.dot_general(n_scr[...], wt_ref[...].astype(_bf16), nt, preferred_element_type=_f32)
    q_lo, k_lo, k_hi = COL_Q // IN_TN, COL_K // IN_TN, (COL_K + ATTN_W) // IN_TN
    is_rot = jnp.logical_and(j >= q_lo, j < k_hi)

    @pl.when(is_rot)
    def _():
        scale = jnp.where(j < k_lo, Q_SCALE, 1.0).astype(_f32)
        rc, rp, rm = rc_scr[...], rp_scr[...], rm_scr[...]
        for hb in range(IN_TN // LANES):
            t = acc[:, hb * LANES:(hb + 1) * LANES]
            r = (t * rc + pltpu.roll(t, ROT_DIM // 2, 1) * rp
                 + pltpu.roll(t, LANES - ROT_DIM // 2, 1) * rm)
            out_ref[:, hb * LANES:(hb + 1) * LANES] = (r * scale).astype(_bf16)

    @pl.when(jnp.logical_not(is_rot))
    def _():
        out_ref[...] = acc.astype(_bf16)


SRC_XBC = SSD_W
SRC_DT = SRC_XBC + XBC_W
SRC_Q = SRC_DT + 2 * SSD_HEADS
SRC_V = SRC_Q + 2 * ATTN_W
SRC_ALIGN = 32
assert all(s % SRC_ALIGN == 0 for s in (SRC_XBC, SRC_DT, SRC_Q, SRC_V, IN_TN))


def _src_row(j):
    jq, jx, jv = COL_Q // IN_TN, COL_XBC // IN_TN, COL_V // IN_TN
    row = jnp.where(j < jq, j * IN_TN,
                    jnp.where(j < jx, SRC_Q + (j - jq) * IN_TN,
                              jnp.where(j < jv, SRC_XBC + (j - jx) * IN_TN,
                                        SRC_V + (j - jv) * IN_TN)))
    return pl.multiple_of(row, SRC_ALIGN)


def _inproj(x2, g, pos_col, invf_row, w_in_t):
    t = x2.shape[0]
    grid = (t // IN_TM, MAIN_W // IN_TN)
    return pl.pallas_call(
        _inproj_kernel,
        grid=grid,
        in_specs=[
            pl.BlockSpec((IN_TM, D_MODEL), lambda i, j: (i, 0)),
            pl.BlockSpec((1, D_MODEL), lambda i, j: (0, 0)),
            pl.BlockSpec((IN_TM, 1), lambda i, j: (i, 0)),
            pl.BlockSpec((1, LANES), lambda i, j: (0, 0)),
            pl.BlockSpec((pl.Element(IN_TN), pl.Element(D_MODEL)), lambda i, j: (_src_row(j), 0)),
            pl.BlockSpec((pl.Element(DT_PAD), pl.Element(D_MODEL)), lambda i, j: (SRC_DT, 0)),
        ],
        out_specs=[
            pl.BlockSpec((IN_TM, IN_TN), lambda i, j: (i, j)),
            pl.BlockSpec((IN_TM, DT_PAD), lambda i, j: (i, 0)),
        ],
        out_shape=[
            jax.ShapeDtypeStruct((t, MAIN_W), _bf16),
            jax.ShapeDtypeStruct((t, DT_PAD), _f32),
        ],
        scratch_shapes=[pltpu.VMEM((IN_TM, D_MODEL), _bf16)]
        + [pltpu.VMEM((IN_TM, LANES), _f32)] * 3,
        compiler_params=pltpu.CompilerParams(
            dimension_semantics=("arbitrary", "arbitrary"),
            vmem_limit_bytes=VMEM_LIMIT),
        name="inproj",
    )(x2, g, pos_col, invf_row, w_in_t, w_in_t)


CONV_HALO = 16


def _ssd_kernel(xbc_ref, dt_ref, cw_ref, cb_ref, dtb_ref, a_ref, dsk_ref,
                y_ref, xpad, xact, dts, yacc, state_f, state_b, seq):
    n_chunks = seq // CHUNK

    zeros_halo = jnp.zeros((CONV_HALO, XBC_W), _bf16)
    xpad[0:CONV_HALO, :] = zeros_halo
    xpad[CONV_HALO + seq:CONV_HALO + seq + CONV_HALO, :] = zeros_halo
    xpad[CONV_HALO:CONV_HALO + seq, :] = xbc_ref[0]
    cw = cw_ref[...]
    cb = cb_ref[...]
    win = CHUNK + 2 * CONV_HALO

    def conv_body(c, carry):
        r0 = pl.multiple_of(c * CHUNK, CHUNK)
        blk = xpad[pl.ds(r0, win), :].astype(_f32)
        acc = blk * cw[CONV_W // 2:CONV_W // 2 + 1, :]
        for k in range(CONV_W):
            sh = CONV_W // 2 - k
            if sh == 0:
                continue
            acc = acc + pltpu.roll(blk, sh % win, 0) * cw[k:k + 1, :]
        v = acc[CONV_HALO:CONV_HALO + CHUNK, :] + cb
        xact[pl.ds(r0, CHUNK), :] = _silu(v).astype(_bf16)
        return carry

    lax.fori_loop(0, n_chunks, conv_body, 0)

    raw = dt_ref[0] + dtb_ref[...]
    dts[...] = jnp.maximum(raw, 0.0) + jnp.log1p(jnp.exp(-jnp.abs(raw)))

    a_row = a_ref[...]
    dsk = dsk_ref[...]
    row_i = lax.broadcasted_iota(jnp.int32, (CHUNK, CHUNK), 0)
    col_i = lax.broadcasted_iota(jnp.int32, (CHUNK, CHUNK), 1)

    def chunk_step(c, reverse, state):
        r0 = pl.multiple_of(c * CHUNK, CHUNK)
        dtc = dts[pl.ds(r0, CHUNK), :]
        cs = dtc * a_row
        k = 1
        while k < CHUNK:
            if reverse:
                cs = cs + jnp.where(row_i < CHUNK - k, pltpu.roll(cs, CHUNK - k, 0), 0.0)
            else:
                cs = cs + jnp.where(row_i >= k, pltpu.roll(cs, k, 0), 0.0)
            k *= 2
        cs_t = cs.T
        dt_t = dtc.T
        end_col = cs_t[:, 0:1] if reverse else cs_t[:, CHUNK - 1:CHUNK]
        w_t = jnp.exp(end_col - cs_t) * dt_t
        exp_cs = jnp.exp(cs)
        tri = (row_i <= col_i) if reverse else (row_i >= col_i)
        lane0 = SSD_HEADS if reverse else 0
        for g in range(SSD_GROUPS):
            b_g = xact[pl.ds(r0, CHUNK), SSD_W + g * SSD_STATE:SSD_W + (g + 1) * SSD_STATE]
            c_g = xact[pl.ds(r0, CHUNK),
                       SSD_W + (SSD_GROUPS + g) * SSD_STATE:SSD_W + (SSD_GROUPS + g + 1) * SSD_STATE]
            cbm = lax.dot_general(c_g, b_g, (((1,), (1,)), ((), ())),
                                  preferred_element_type=_f32)
            b_t = b_g.astype(_f32).T
            c_f = c_g.astype(_f32)
            for hh in range(HEADS_PER_GROUP):
                h = g * HEADS_PER_GROUP + hh
                ln = lane0 + h
                xs_h = xact[pl.ds(r0, CHUNK), h * SSD_HEADDIM:(h + 1) * SSD_HEADDIM]
                seg = cs[:, ln:ln + 1] - cs_t[ln:ln + 1, :]
                m_h = cbm * jnp.exp(jnp.where(tri, seg, NEG_BIG)) * dt_t[ln:ln + 1, :]
                st = state[h]
                y_h = jnp.dot(m_h.astype(_bf16), xs_h, preferred_element_type=_f32)
                y_h = y_h + jnp.dot((c_f * exp_cs[:, ln:ln + 1]).astype(_bf16),
                                    st.astype(_bf16), preferred_element_type=_f32)
                upd = jnp.dot((b_t * w_t[ln:ln + 1, :]).astype(_bf16), xs_h,
                              preferred_element_type=_f32)
                if reverse:
                    dec = exp_cs[0:1, ln:ln + 1]
                else:
                    dec = exp_cs[CHUNK - 1:CHUNK, ln:ln + 1]
                state[h] = st * dec + upd
                cols = slice(h * SSD_HEADDIM, (h + 1) * SSD_HEADDIM)
                if reverse:
                    y_ref[0, pl.ds(r0, CHUNK), cols] = (
                        yacc[pl.ds(r0, CHUNK), cols] + y_h).astype(_bf16)
                else:
                    yacc[pl.ds(r0, CHUNK), cols] = (
                        y_h + xs_h.astype(_f32) * dsk[:, cols])

    state_f[...] = jnp.zeros_like(state_f)
    state_b[...] = jnp.zeros_like(state_b)

    def fwd_body(c, carry):
        chunk_step(c, False, state_f)
        return carry

    lax.fori_loop(0, n_chunks, fwd_body, 0)

    def bwd_body(i, carry):
        chunk_step(n_chunks - 1 - i, True, state_b)
        return carry

    lax.fori_loop(0, n_chunks, bwd_body, 0)


def _ssd(proj3, dt3, conv_w, conv_b, dt_bias, a_row, d_skip_row):
    b, seq, _ = proj3.shape
    kern = functools.partial(_ssd_kernel, seq=seq)
    assert COL_XBC % XBC_W == 0
    return pl.pallas_call(
        kern,
        grid=(b,),
        in_specs=[
            pl.BlockSpec((1, seq, XBC_W), lambda i: (i, 0, COL_XBC // XBC_W)),
            pl.BlockSpec((1, seq, DT_PAD), lambda i: (i, 0, 0)),
            pl.BlockSpec((CONV_W, XBC_W), lambda i: (0, 0)),
            pl.BlockSpec((1, XBC_W), lambda i: (0, 0)),
            pl.BlockSpec((1, DT_PAD), lambda i: (0, 0)),
            pl.BlockSpec((1, DT_PAD), lambda i: (0, 0)),
            pl.BlockSpec((1, SSD_W), lambda i: (0, 0)),
        ],
        out_specs=pl.BlockSpec((1, seq, SSD_W), lambda i: (i, 0, 0)),
        out_shape=jax.ShapeDtypeStruct((b, seq, SSD_W), _bf16),
        scratch_shapes=[
            pltpu.VMEM((seq + 2 * CONV_HALO, XBC_W), _bf16),
            pltpu.VMEM((seq, XBC_W), _bf16),
            pltpu.VMEM((seq, DT_PAD), _f32),
            pltpu.VMEM((seq, SSD_W), _f32),
            pltpu.VMEM((SSD_HEADS, SSD_STATE, SSD_HEADDIM), _f32),
            pltpu.VMEM((SSD_HEADS, SSD_STATE, SSD_HEADDIM), _f32),
        ],
        compiler_params=pltpu.CompilerParams(
            dimension_semantics=("arbitrary",),
            vmem_limit_bytes=VMEM_LIMIT),
        name="ssd",
    )(proj3, dt3, conv_w, conv_b, dt_bias, a_row, d_skip_row)


ATT_QB = 256
ATT_UNROLL = 8


def _attn_kernel(q_ref, k_ref, v_ref, lam_ref, g_ref, o_ref, v1_scr, *, seq):
    hw = 2 * DIFF_HEAD_DIM
    lv = lam_ref[...]
    lam = (jnp.exp(jnp.sum(lv[0:1] * lv[1:2], axis=-1, keepdims=True))
           - jnp.exp(jnp.sum(lv[2:3] * lv[3:4], axis=-1, keepdims=True)) + LAM_INIT)
    k = k_ref[0]
    g = g_ref[...]
    v1_scr[:, 0:hw] = v_ref[0]
    v1_scr[:, hw:2 * hw] = (lax.broadcasted_iota(jnp.int32, (seq, hw), 1) == 0).astype(_bf16)
    first = lax.broadcasted_iota(jnp.int32, (1, hw), 1) < DIFF_HEAD_DIM
    nt = (((1,), (1,)), ((), ()))

    def chain(r0):
        q = q_ref[0, pl.ds(r0, ATT_QB), :]
        zero = jnp.zeros_like(q)
        ovs = []
        for qc in (jnp.where(first, q, zero), jnp.where(first, zero, q)):
            s = lax.dot_general(qc, k, nt, preferred_element_type=_f32)
            e = jnp.exp2(s - jnp.max(s, axis=-1, keepdims=True)).astype(_bf16)
            ovs.append(jnp.dot(e, v1_scr[...], preferred_element_type=_f32))
        o1, l1 = ovs[0][:, 0:hw], ovs[0][:, hw:hw + 1]
        o2, l2 = ovs[1][:, 0:hw], ovs[1][:, hw:hw + 1]
        o = o1 / l1 - (lam / l2) * o2
        o = _rms(o, g) * (1.0 - LAM_INIT)
        o_ref[0, pl.ds(r0, ATT_QB), :] = o.astype(_bf16)

    step = ATT_QB * ATT_UNROLL

    def body(i, carry):
        for u in range(ATT_UNROLL):
            chain(pl.multiple_of(i * step + u * ATT_QB, ATT_QB))
        return carry

    lax.fori_loop(0, seq // step, body, 0)


def _attn(proj3, lam_vecs, subln_g):
    b, seq, _ = proj3.shape
    hw = 2 * DIFF_HEAD_DIM
    kern = functools.partial(_attn_kernel, seq=seq)
    return pl.pallas_call(
        kern,
        grid=(b, DIFF_HEADS),
        in_specs=[
            pl.BlockSpec((1, seq, hw), lambda i, h: (i, 0, COL_Q // hw + h)),
            pl.BlockSpec((1, seq, hw), lambda i, h: (i, 0, COL_K // hw + h)),
            pl.BlockSpec((1, seq, hw), lambda i, h: (i, 0, COL_V // hw + h)),
            pl.BlockSpec((4, DIFF_HEAD_DIM), lambda i, h: (0, 0)),
            pl.BlockSpec((1, hw), lambda i, h: (0, 0)),
        ],
        out_specs=pl.BlockSpec((1, seq, hw), lambda i, h: (i, 0, h)),
        out_shape=jax.ShapeDtypeStruct((b, seq, ATTN_W), _bf16),
        scratch_shapes=[pltpu.VMEM((seq, 2 * hw), _bf16)],
        compiler_params=pltpu.CompilerParams(
            dimension_semantics=("arbitrary", "arbitrary"),
            vmem_limit_bytes=VMEM_LIMIT),
        name="diffattn",
    )(proj3, proj3, proj3, lam_vecs, subln_g)


OUT_TM = 256
ROUTE_ROWS = 8


def _outproj_kernel(y_ref, z_ref, att_ref, x_ref, w_ref, gs_ref, gf_ref, wr_ref, br_ref,
                    h_ref, hn_ref, route_ref, cnt_ref, gate_ref, cnt_scr):
    @pl.when(pl.program_id(0) == 0)
    def _():
        cnt_scr[...] = jnp.zeros_like(cnt_scr)

    y = y_ref[...].astype(_f32)
    z = z_ref[...].astype(_f32)
    s = _rms(y * _silu(z), gs_ref[...]).astype(_bf16)
    acc = jnp.dot(s, w_ref[0:SSD_W, :], preferred_element_type=_f32)
    acc = acc + jnp.dot(att_ref[...], w_ref[SSD_W:SSD_W + ATTN_W, :], preferred_element_type=_f32)
    h1 = x_ref[...] + acc
    h_ref[...] = h1
    hn = _rms(h1, gf_ref[...])
    half = D_MODEL // 2
    _store_row_tiles(hn_ref, _pack_bf16_pair(hn[:, 0:half], hn[:, half:D_MODEL]), hn.shape[0])
    hn_hi = hn.astype(_bf16)
    hn_lo = (hn - hn_hi.astype(_f32)).astype(_bf16)
    a = jnp.dot(hn_hi, wr_ref[...], preferred_element_type=_f32)
    bb = jnp.dot(hn_lo, wr_ref[:, 0:LANES], preferred_element_type=_f32)
    logits = (a[:, 0:LANES] + a[:, LANES:2 * LANES] + bb + br_ref[...]).T
    tm = logits.shape[1]
    iota = lax.broadcasted_iota(jnp.int32, (MOE_GROUPS, tm), 0)

    def first_argmax(val, vmax):
        return jnp.min(jnp.where(val == vmax, iota, MOE_GROUPS), axis=0, keepdims=True)

    gl = logits[0:MOE_GROUPS]
    gmax = jnp.max(gl, axis=0, keepdims=True)
    p_g = 1.0 / jnp.sum(jnp.exp(gl - gmax), axis=0, keepdims=True)
    g_sel = first_argmax(gl, gmax)
    el = jnp.zeros((EXPERTS_PER_GROUP, tm), _f32)
    for gi in range(MOE_GROUPS):
        lo = MOE_GROUPS + gi * EXPERTS_PER_GROUP
        el = jnp.where(g_sel == gi, logits[lo:lo + EXPERTS_PER_GROUP], el)
    ee = jnp.exp(el - jnp.max(el, axis=0, keepdims=True))
    pe = ee / jnp.sum(ee, axis=0, keepdims=True)
    p1 = jnp.max(pe, axis=0, keepdims=True)
    i1 = first_argmax(pe, p1)
    pe2 = jnp.where(iota == i1, -1.0, pe)
    p2 = jnp.max(pe2, axis=0, keepdims=True)
    i2 = first_argmax(pe2, p2)
    den = p1 + p2
    e1 = g_sel * EXPERTS_PER_GROUP + i1
    e2 = g_sel * EXPERTS_PER_GROUP + i2
    iota_e = lax.broadcasted_iota(jnp.int32, (N_EXPERTS, tm), 0)
    oh1 = (iota_e == e1).astype(_f32)
    oh2 = (iota_e == e2).astype(_f32)
    both = oh1 + oh2
    earlier = (lax.broadcasted_iota(jnp.int32, (tm, tm), 0)
               < lax.broadcasted_iota(jnp.int32, (tm, tm), 1)).astype(_bf16)
    before = cnt_scr[:, 0:1] + jnp.dot(both.astype(_bf16), earlier,
                                       preferred_element_type=_f32)
    r1 = jnp.sum(oh1 * before, axis=0, keepdims=True).astype(jnp.int32)
    r2 = jnp.sum(oh2 * before, axis=0, keepdims=True).astype(jnp.int32)
    cnt_scr[...] = cnt_scr[...] + jnp.sum(both, axis=1, keepdims=True)
    cnt_ref[...] = cnt_scr[...].astype(jnp.int32)
    route_ref[...] = jnp.where(iota == 0, e1, jnp.where(iota == 1, e2,
                               jnp.where(iota == 2, r1, jnp.where(iota == 3, r2, 0))))
    g8 = jnp.where(iota == 0, p_g * (p1 / den), jnp.where(iota == 1, p_g * (p2 / den), 0.0))
    gate_ref[...] = jnp.concatenate(
        [g8, jnp.zeros((LANES - ROUTE_ROWS, tm), _f32)], axis=0).T


def _outproj(y2, proj2, att2, x2, w_out, g_ssd, g_ffn, wr_t, br_col):
    t = x2.shape[0]
    row = lambda i: (i, 0)
    fix = lambda i: (0, 0)
    return pl.pallas_call(
        _outproj_kernel,
        grid=(t // OUT_TM,),
        in_specs=[
            pl.BlockSpec((OUT_TM, SSD_W), row),
            pl.BlockSpec((OUT_TM, SSD_W), row),
            pl.BlockSpec((OUT_TM, ATTN_W), row),
            pl.BlockSpec((OUT_TM, D_MODEL), row),
            pl.BlockSpec((SSD_W + ATTN_W, D_MODEL), fix),
            pl.BlockSpec((1, SSD_W), fix),
            pl.BlockSpec((1, D_MODEL), fix),
            pl.BlockSpec((D_MODEL, 2 * LANES), fix),
            pl.BlockSpec((1, LANES), fix),
        ],
        out_specs=[
            pl.BlockSpec((OUT_TM, D_MODEL), row),
            pl.BlockSpec((OUT_TM * ROW_TILE, LANES), row),
            pl.BlockSpec((ROUTE_ROWS, OUT_TM), lambda i: (0, i)),
            pl.BlockSpec((N_EXPERTS, LANES), fix),
            pl.BlockSpec((OUT_TM, LANES), row),
        ],
        out_shape=[
            jax.ShapeDtypeStruct((t, D_MODEL), _f32),
            jax.ShapeDtypeStruct((t * ROW_TILE, LANES), jnp.uint32),
            jax.ShapeDtypeStruct((ROUTE_ROWS, t), jnp.int32),
            jax.ShapeDtypeStruct((N_EXPERTS, LANES), jnp.int32),
            jax.ShapeDtypeStruct((t, LANES), _f32),
        ],
        scratch_shapes=[pltpu.VMEM((N_EXPERTS, LANES), _f32)],
        compiler_params=pltpu.CompilerParams(
            dimension_semantics=("arbitrary",),
            vmem_limit_bytes=VMEM_LIMIT),
        name="outproj_router",
    )(y2, proj2, att2, x2, w_out, g_ssd, g_ffn, wr_t, br_col)


def _n_rows(n_tok):
    n_assign = n_tok * TOP_K
    return (n_assign + N_EXPERTS * (ROW_BLOCK - 1) + ROW_BLOCK - 1) // ROW_BLOCK * ROW_BLOCK


def _prefix_sum(v, axis):
    n = v.shape[axis]
    idx = lax.broadcasted_iota(jnp.int32, v.shape, axis)
    k = 1
    while k < n:
        v = v + jnp.where(idx >= k, pltpu.roll(v, k, axis), 0)
        k *= 2
    return v


def _plan_kernel(route_ref, cnt_ref, pos_ref, blk_ref, meta_ref, *, n_blocks_pad):
    n_tok = route_ref.shape[1]
    cnt_col = jnp.concatenate(
        [cnt_ref[...], jnp.zeros((LANES - N_EXPERTS, LANES), jnp.int32)], axis=0)
    pad_up = lambda c: (c + (ROW_BLOCK - 1)) & (-ROW_BLOCK)
    ends_col = _prefix_sum(pad_up(cnt_col), 0)
    starts_col = (ends_col - pad_up(cnt_col)).astype(_f32)
    cnt_row = cnt_col.astype(_f32).T.astype(jnp.int32)
    ends_row = _prefix_sum(pad_up(cnt_row), 1)
    starts_row = ends_row - pad_up(cnt_row)
    n_used = ends_row[0:1, LANES - 1:LANES] >> (ROW_BLOCK.bit_length() - 1)

    ch = 1024
    iota_e = lax.broadcasted_iota(jnp.int32, (LANES, ch), 0)
    for c0 in range(0, n_tok, ch):
        rt = route_ref[:, c0:c0 + ch]
        s1 = jnp.sum(jnp.where(iota_e == rt[0:1], starts_col[:, 0:1], 0.0), axis=0, keepdims=True)
        s2 = jnp.sum(jnp.where(iota_e == rt[1:2], starts_col[:, 0:1], 0.0), axis=0, keepdims=True)
        p1 = s1.astype(jnp.int32) + rt[2:3]
        p2 = s2.astype(jnp.int32) + rt[3:4]
        sub = lax.broadcasted_iota(jnp.int32, (ROUTE_ROWS, ch), 0)
        pos_ref[:, c0:c0 + ch] = jnp.where(sub == 0, p1, jnp.where(sub == 1, p2, 0))

    blk_start = lax.broadcasted_iota(jnp.int32, (LANES, n_blocks_pad), 1) * ROW_BLOCK
    e_idx = lax.broadcasted_iota(jnp.int32, (LANES, n_blocks_pad), 0)
    real = e_idx < N_EXPERTS
    be = jnp.sum(jnp.where(jnp.logical_and(real, ends_col[:, 0:1] <= blk_start), 1.0, 0.0),
                 axis=0, keepdims=True).astype(jnp.int32)
    be = jnp.minimum(be, N_EXPERTS - 1)
    last = jnp.max(jnp.where(jnp.logical_and(real, cnt_col[:, 0:1] > 0), e_idx.astype(_f32), 0.0),
                   axis=0, keepdims=True).astype(jnp.int32)
    blk_i = lax.broadcasted_iota(jnp.int32, (1, n_blocks_pad), 1)
    be = jnp.where(blk_i < n_used, be, last)
    sub = lax.broadcasted_iota(jnp.int32, (ROUTE_ROWS, n_blocks_pad), 0)
    blk_ref[...] = jnp.where(sub == 0, be, jnp.where(sub == 1, n_used, 0))
    sub = lax.broadcasted_iota(jnp.int32, (ROUTE_ROWS, LANES), 0)
    meta_ref[...] = jnp.where(sub == 0, cnt_row[0:1], jnp.where(sub == 1, starts_row[0:1], 0))


def _plan(route, cnt):
    n_tok = route.shape[1]
    n_blocks_pad = -(-(_n_rows(n_tok) // ROW_BLOCK) // LANES) * LANES
    return pl.pallas_call(
        functools.partial(_plan_kernel, n_blocks_pad=n_blocks_pad),
        out_shape=[
            jax.ShapeDtypeStruct((ROUTE_ROWS, n_tok), jnp.int32),
            jax.ShapeDtypeStruct((ROUTE_ROWS, n_blocks_pad), jnp.int32),
            jax.ShapeDtypeStruct((ROUTE_ROWS, LANES), jnp.int32),
        ],
        compiler_params=pltpu.CompilerParams(vmem_limit_bytes=VMEM_LIMIT),
        name="route_plan",
    )(route, cnt)


DISP_TM = 256


def _dispatch_kernel(pos_ref, cnt_ref, start_ref, nused_ref, h_ref, x_hbm, stage, zrow, sem, zsem,
                     *, n_tok, n_blocks):
    i = pl.program_id(0)
    n_steps = pl.num_programs(0)
    slot = i % 2
    base = i * DISP_TM
    tile = lambda r: pl.ds(pl.multiple_of(r * ROW_TILE, ROW_TILE), ROW_TILE)

    def wait_slot(sl):
        for _ in range(TOP_K):
            pltpu.make_async_copy(stage.at[sl], x_hbm.at[pl.ds(0, DISP_TM * ROW_TILE), :],
                                  sem.at[sl]).wait()

    @pl.when(i >= 2)
    def _():
        wait_slot(slot)

    stage[slot] = h_ref[...]

    def body(r, carry):
        for kk in range(TOP_K):
            row = pos_ref[kk * n_tok + base + r]
            pltpu.make_async_copy(stage.at[slot, tile(r), :], x_hbm.at[tile(row), :],
                                  sem.at[slot]).start(priority=kk % 2)
        return carry

    lax.fori_loop(0, DISP_TM, body, 0, unroll=8)

    @pl.when(i == n_steps - 1)
    def _():
        zrow[...] = jnp.zeros_like(zrow)

        def per_expert(e, n_fill):
            cnt = cnt_ref[e]
            start = start_ref[e]
            end = (cnt + (ROW_BLOCK - 1)) & (-ROW_BLOCK)

            def fill(r, carry):
                pltpu.make_async_copy(zrow.at[tile(0), :], x_hbm.at[tile(start + r), :], zsem).start()
                return carry

            lax.fori_loop(cnt, end, fill, 0)
            return n_fill + (end - cnt)

        n_fill = lax.fori_loop(0, N_EXPERTS, per_expert, 0)
        blk_rows = ROW_BLOCK * ROW_TILE

        def fill_block(blk, carry):
            dst = pl.ds(pl.multiple_of(blk * blk_rows, blk_rows), blk_rows)
            pltpu.make_async_copy(zrow, x_hbm.at[dst, :], zsem).start()
            return carry

        lax.fori_loop(nused_ref[0], n_blocks, fill_block, 0)

        def wait_row(r, carry):
            pltpu.make_async_copy(zrow.at[tile(0), :], x_hbm.at[tile(0), :], zsem).wait()
            return carry

        lax.fori_loop(0, n_fill, wait_row, 0)

        def wait_block(blk, carry):
            pltpu.make_async_copy(zrow, x_hbm.at[pl.ds(0, blk_rows), :], zsem).wait()
            return carry

        lax.fori_loop(nused_ref[0], n_blocks, wait_block, 0)

        @pl.when(n_steps >= 2)
        def _():
            wait_slot(1 - slot)

        wait_slot(slot)


def _dispatch_rows(pos, counts, starts, n_used, hn_packed):
    t = hn_packed.shape[0] // ROW_TILE
    n_rows = _n_rows(t)
    blk = (DISP_TM * ROW_TILE, LANES)
    grid_spec = pltpu.PrefetchScalarGridSpec(
        num_scalar_prefetch=4,
        grid=(t // DISP_TM,),
        in_specs=[pl.BlockSpec(blk, lambda i, *_: (i, 0))],
        out_specs=pl.BlockSpec(memory_space=pl.ANY),
        scratch_shapes=[
            pltpu.VMEM((2,) + blk, hn_packed.dtype),
            pltpu.VMEM((ROW_BLOCK * ROW_TILE, LANES), hn_packed.dtype),
            pltpu.SemaphoreType.DMA((2,)),
            pltpu.SemaphoreType.DMA(()),
        ],
    )
    return pl.pallas_call(
        functools.partial(_dispatch_kernel, n_tok=t, n_blocks=n_rows // ROW_BLOCK),
        grid_spec=grid_spec,
        out_shape=jax.ShapeDtypeStruct((n_rows * ROW_TILE, LANES), hn_packed.dtype),
        compiler_params=pltpu.CompilerParams(
            dimension_semantics=("arbitrary",),
            vmem_limit_bytes=VMEM_LIMIT),
        name="dispatch_rows",
    )(pos, counts, starts, n_used, hn_packed)


MOE_W_SPLIT = 4


def _moe_kernel(be_ref, nused_ref, x_ref, wg_hbm, wu_hbm, wd_hbm,
                out_ref, wg_f, wu_f, wd_f, wg_s, wu_s, wd_s, wsem, nexp, *, n_blocks):
    i = pl.program_id(0)
    nused = nused_ref[0]
    half = D_MODEL // 2

    def weight_copies(e, sl):
        cps = []
        for src, dst in ((wg_hbm, wg_f), (wu_hbm, wu_f), (wd_hbm, wd_f)):
            rows = src.shape[1] // MOE_W_SPLIT
            for c in range(MOE_W_SPLIT):
                rs = pl.ds(c * rows, rows)
                cps.append(pltpu.make_async_copy(src.at[e, rs, :], dst.at[sl, rs, :], wsem.at[sl]))
        return cps

    def start_all(cps):
        for n, cp in enumerate(cps):
            cp.start(priority=n % 2)

    @pl.when(i == 0)
    def _():
        nexp[0] = 0
        start_all(weight_copies(be_ref[0], 0))

    @pl.when(i < nused)
    def _():
        e = be_ref[i]
        new_expert = jnp.logical_or(i == 0, e != be_ref[jnp.maximum(i - 1, 0)])

        @pl.when(new_expert)
        def _():
            sl = nexp[0] % 2
            nxt = lax.while_loop(
                lambda b: jnp.logical_and(b < nused, be_ref[jnp.minimum(b, n_blocks - 1)] == e),
                lambda b: b + 1, i + 1)

            @pl.when(nxt < nused)
            def _():
                start_all(weight_copies(be_ref[jnp.minimum(nxt, n_blocks - 1)], 1 - sl))

            for cp in weight_copies(e, sl):
                cp.wait()
            wg_s[...] = wg_f[sl].astype(_bf16)
            wu_s[...] = wu_f[sl].astype(_bf16)
            wd_s[...] = wd_f[sl].astype(_bf16)
            nexp[0] = nexp[0] + 1

        x_lo, x_hi = _unpack_bf16_pair(_load_row_tiles(x_ref, ROW_BLOCK))
        x_lo, x_hi = x_lo.astype(_bf16), x_hi.astype(_bf16)
        gg = (jnp.dot(x_lo, wg_s[0:half, :], preferred_element_type=_f32)
              + jnp.dot(x_hi, wg_s[half:D_MODEL, :], preferred_element_type=_f32))
        uu = (jnp.dot(x_lo, wu_s[0:half, :], preferred_element_type=_f32)
              + jnp.dot(x_hi, wu_s[half:D_MODEL, :], preferred_element_type=_f32))
        a = (_silu(gg) * uu).astype(_bf16)
        y = jnp.dot(a, wd_s[...], preferred_element_type=_f32)
        _store_row_tiles(out_ref, _pack_bf16_pair(y[:, 0:half], y[:, half:D_MODEL]), ROW_BLOCK)

    @pl.when(i >= nused)
    def _():
        out_ref[...] = jnp.zeros_like(out_ref)


def _moe(block_expert, n_used, x_rows, w_gate, w_up, w_down):
    n_blocks = x_rows.shape[0] // (ROW_BLOCK * ROW_TILE)
    blk = (ROW_BLOCK * ROW_TILE, LANES)
    omap = lambda i, be, nu: (i, 0)
    grid_spec = pltpu.PrefetchScalarGridSpec(
        num_scalar_prefetch=2,
        grid=(n_blocks,),
        in_specs=[
            pl.BlockSpec(blk, omap),
            pl.BlockSpec(memory_space=pl.ANY),
            pl.BlockSpec(memory_space=pl.ANY),
            pl.BlockSpec(memory_space=pl.ANY),
        ],
        out_specs=pl.BlockSpec(blk, omap),
        scratch_shapes=[
            pltpu.VMEM((2, D_MODEL, EXPERT_FF), _f32),
            pltpu.VMEM((2, D_MODEL, EXPERT_FF), _f32),
            pltpu.VMEM((2, EXPERT_FF, D_MODEL), _f32),
            pltpu.VMEM((D_MODEL, EXPERT_FF), _bf16),
            pltpu.VMEM((D_MODEL, EXPERT_FF), _bf16),
            pltpu.VMEM((EXPERT_FF, D_MODEL), _bf16),
            pltpu.SemaphoreType.DMA((2,)),
            pltpu.SMEM((1,), jnp.int32),
        ],
    )
    return pl.pallas_call(
        functools.partial(_moe_kernel, n_blocks=n_blocks),
        grid_spec=grid_spec,
        out_shape=jax.ShapeDtypeStruct(x_rows.shape, x_rows.dtype),
        compiler_params=pltpu.CompilerParams(
            dimension_semantics=("arbitrary",),
            vmem_limit_bytes=VMEM_LIMIT),
        name="moe_experts",
    )(block_expert, n_used, x_rows, w_gate, w_up, w_down)


TAIL_TM = 256


def _tail_kernel(pos_ref, h_ref, gate_ref, y_hbm, p_ref, wpp_ref, gp_ref, wpg_ref, bpg_ref,
                 gfin_ref, out_ref, ybuf, sem, *, n_tok):
    i = pl.program_id(0)
    n_steps = pl.num_programs(0)
    slot = i % 2
    tile = lambda r: pl.ds(pl.multiple_of(r * ROW_TILE, ROW_TILE), ROW_TILE)

    def start_row(base, r, sl):
        for kk in range(TOP_K):
            row = pos_ref[kk * n_tok + base + r]
            pltpu.make_async_copy(y_hbm.at[tile(row), :], ybuf.at[sl, kk, tile(r), :],
                                  sem.at[sl]).start(priority=kk % 2)

    def wait_slot(sl):
        for kk in range(TOP_K):
            pltpu.make_async_copy(y_hbm.at[pl.ds(0, TAIL_TM * ROW_TILE), :], ybuf.at[sl, kk],
                                  sem.at[sl]).wait()

    @pl.when(i == 0)
    def _():
        def body(r, carry):
            start_row(0, r, 0)
            return carry

        lax.fori_loop(0, TAIL_TM, body, 0, unroll=8)

    wait_slot(slot)

    nxt_base = jnp.minimum(i + 1, n_steps - 1) * TAIL_TM
    for r in range(TAIL_TM):
        start_row(nxt_base, r, 1 - slot)

    gates = gate_ref[...]
    g1, g2 = gates[:, 0:1], gates[:, 1:2]
    y1_lo, y1_hi = _unpack_bf16_pair(_load_row_tiles(ybuf.at[slot, 0], TAIL_TM))
    y2_lo, y2_hi = _unpack_bf16_pair(_load_row_tiles(ybuf.at[slot, 1], TAIL_TM))
    moe = jnp.concatenate([g1 * y1_lo + g2 * y2_lo, g1 * y1_hi + g2 * y2_hi], axis=1)
    h2 = h_ref[...] + moe
    ple = _rms(jnp.dot(p_ref[...].astype(_bf16), wpp_ref[...], preferred_element_type=_f32),
               gp_ref[...])
    lg = jnp.dot(h2.astype(_bf16), wpg_ref[...], preferred_element_type=_f32) + bpg_ref[...]
    h3 = h2 + (1.0 / (1.0 + jnp.exp(-lg))) * ple
    out_ref[...] = _rms(h3, gfin_ref[...])

    @pl.when(i == n_steps - 1)
    def _():
        wait_slot(1 - slot)


def _tail(pos, h1, gates, y_rows, p2, w_pp, g_ple, w_pg, b_pg, g_fin):
    t = h1.shape[0]
    row = lambda i, ps: (i, 0)
    fix = lambda i, ps: (0, 0)
    grid_spec = pltpu.PrefetchScalarGridSpec(
        num_scalar_prefetch=1,
        grid=(t // TAIL_TM,),
        in_specs=[
            pl.BlockSpec((TAIL_TM, D_MODEL), row),
            pl.BlockSpec((TAIL_TM, LANES), row),
            pl.BlockSpec(memory_space=pl.ANY),
            pl.BlockSpec((TAIL_TM, PLE_DIM), row),
            pl.BlockSpec((PLE_DIM, D_MODEL), fix),
            pl.BlockSpec((1, D_MODEL), fix),
            pl.BlockSpec((D_MODEL, D_MODEL), fix),
            pl.BlockSpec((1, D_MODEL), fix),
            pl.BlockSpec((1, D_MODEL), fix),
        ],
        out_specs=pl.BlockSpec((TAIL_TM, D_MODEL), row),
        scratch_shapes=[
            pltpu.VMEM((2, TOP_K, TAIL_TM * ROW_TILE, LANES), y_rows.dtype),
            pltpu.SemaphoreType.DMA((2,)),
        ],
    )
    return pl.pallas_call(
        functools.partial(_tail_kernel, n_tok=t),
        grid_spec=grid_spec,
        out_shape=jax.ShapeDtypeStruct((t, D_MODEL), _f32),
        compiler_params=pltpu.CompilerParams(
            dimension_semantics=("arbitrary",),
            vmem_limit_bytes=VMEM_LIMIT),
        name="tail",
    )(pos, h1, gates, y_rows, p2, w_pp, g_ple, w_pg, b_pg, g_fin)


def _inv_freq_row():
    inv_freq = ROPE_THETA ** (-jnp.arange(0, ROT_DIM, 2, dtype=_f32) / ROT_DIM)
    comp = jnp.concatenate([inv_freq, inv_freq, jnp.zeros((DIFF_HEAD_DIM - ROT_DIM,), _f32)])
    return jnp.concatenate([comp, comp]).reshape(1, LANES)


def kernel(x, p, positions, norm_mix_g, w_in, conv_w, conv_b, dt_bias_f, dt_bias_b, a_log_f, a_log_b, d_skip, ssd_norm_g, lam_q1, lam_k1, lam_q2, lam_k2, subln_g, w_out, norm_ffn_g, w_route_group, b_route_group, w_route_expert, b_route_expert, w_exp_gate, w_exp_up, w_exp_down, w_ple_proj, ple_norm_g, w_ple_gate, b_ple_gate, final_norm_g):
    b, seq, d = x.shape
    t = b * seq
    x2 = x.reshape(t, d)
    row = lambda v: v.reshape(1, -1).astype(_f32)

    w_in_t = jnp.swapaxes(w_in[0], 0, 1)
    pos_col = positions.astype(_f32).reshape(t, 1)
    pad_dt = lambda v: jnp.pad(v, (0, DT_PAD - 2 * SSD_HEADS)).reshape(1, DT_PAD)
    dt_bias = pad_dt(jnp.concatenate([dt_bias_f[0], dt_bias_b[0]]))
    a_row = pad_dt(jnp.concatenate([-jnp.exp(a_log_f[0]), -jnp.exp(a_log_b[0])]))
    d_skip_row = jnp.repeat(d_skip[0], SSD_HEADDIM).reshape(1, SSD_W)
    lam_vecs = jnp.stack([lam_q1[0], lam_k1[0], lam_q2[0], lam_k2[0]])
    n_route = MOE_GROUPS + N_EXPERTS
    w_route = jnp.pad(jnp.concatenate([w_route_group[0], w_route_expert[0]], axis=1),
                      ((0, 0), (0, LANES - n_route)))
    w_route_hi = w_route.astype(_bf16)
    w_route_lo = (w_route - w_route_hi.astype(_f32)).astype(_bf16)
    w_route2 = jnp.concatenate([w_route_hi, w_route_lo], axis=1)
    b_route = jnp.pad(jnp.concatenate([b_route_group[0], b_route_expert[0]]),
                      (0, LANES - n_route)).reshape(1, LANES)

    proj, dt = _inproj(x2, row(norm_mix_g[0]), pos_col, _inv_freq_row(), w_in_t)
    proj3 = proj.reshape(b, seq, MAIN_W)
    y_ssd = _ssd(proj3, dt.reshape(b, seq, DT_PAD), conv_w[0], row(conv_b[0]),
                 dt_bias, a_row, d_skip_row)
    att = _attn(proj3, lam_vecs, row(subln_g[0]))
    h1, hn_packed, route, cnt, gates = _outproj(
        y_ssd.reshape(t, SSD_W), proj, att.reshape(t, ATTN_W), x2, w_out[0].astype(_bf16),
        row(ssd_norm_g[0]), row(norm_ffn_g[0]), w_route2, b_route)
    pos8, blk8, meta8 = _plan(route, cnt)
    pos = pos8[:TOP_K].reshape(-1)
    block_expert = blk8[0, :_n_rows(t) // ROW_BLOCK]
    n_used = blk8[1, :1]
    x_rows = _dispatch_rows(pos, meta8[0, :N_EXPERTS], meta8[1, :N_EXPERTS], n_used, hn_packed)
    y_rows = _moe(block_expert, n_used, x_rows, w_exp_gate[0], w_exp_up[0], w_exp_down[0])
    out = _tail(pos, h1, gates, y_rows, p[0].reshape(t, PLE_DIM), w_ple_proj[0].astype(_bf16),
                row(ple_norm_g[0]), w_ple_gate[0].astype(_bf16), row(b_ple_gate[0]),
                row(final_norm_g))
    return out.reshape(b, seq, d)
```

```python
import functools

import jax
import jax.numpy as jnp
import numpy as np
from jax import lax
from jax.experimental import pallas as pl
from jax.experimental.pallas import tpu as pltpu

D_MODEL = 2048
PLE_DIM = 256
SSD_W = 1024
ATTN_W = 1024
SSD_HEADDIM = 64
SSD_HEADS = 16
SSD_GROUPS = 2
HEADS_PER_GROUP = SSD_HEADS // SSD_GROUPS
SSD_STATE = 128
CHUNK = 128
CONV_W = 5
XBC_W = SSD_W + 2 * SSD_GROUPS * SSD_STATE
DIFF_HEAD_DIM = 64
DIFF_HEADS = 8
ROT_DIM = 16
ROPE_THETA = 500000.0
MOE_GROUPS = 8
EXPERTS_PER_GROUP = 8
N_EXPERTS = 64
TOP_K = 2
EXPERT_FF = 512
ROW_BLOCK = 128
EPS = 1e-6
LAM_INIT = 0.2

LANES = 128
DT_PAD = LANES
MAIN_W = SSD_W + XBC_W + 3 * ATTN_W
COL_Z = 0
COL_Q = SSD_W
COL_K = COL_Q + ATTN_W
COL_XBC = COL_K + ATTN_W
COL_V = COL_XBC + XBC_W
VMEM_LIMIT = 56 * 1024 * 1024
NEG_BIG = -1e30
Q_SCALE = float(DIFF_HEAD_DIM ** -0.5 * np.log2(np.e))

_f32 = jnp.float32
_bf16 = jnp.bfloat16


def _silu(v):
    return v * (1.0 / (1.0 + jnp.exp(-v)))


def _rms(v, g):
    return v * lax.rsqrt(jnp.mean(v * v, axis=-1, keepdims=True) + EPS) * g


_HI16 = np.uint32(0xFFFF0000)


def _pack_bf16_pair(lo, hi):
    lo_w = lax.bitcast_convert_type(lo.astype(_bf16).astype(_f32), jnp.uint32) >> 16
    hi_w = lax.bitcast_convert_type(hi.astype(_bf16).astype(_f32), jnp.uint32) & _HI16
    return lo_w | hi_w


def _unpack_bf16_pair(w):
    lo = lax.bitcast_convert_type(w << 16, _f32)
    hi = lax.bitcast_convert_type(w & _HI16, _f32)
    return lo, hi


ROW_TILE = 8
assert D_MODEL // 2 == ROW_TILE * LANES


def _store_row_tiles(ref, words, n):
    for c in range(ROW_TILE):
        ref[pl.ds(c, n, stride=ROW_TILE), :] = words[:, c * LANES:(c + 1) * LANES]


def _load_row_tiles(ref, n):
    return jnp.concatenate(
        [ref[pl.ds(c, n, stride=ROW_TILE), :] for c in range(ROW_TILE)], axis=1)


IN_TM = 1024
IN_TN = 512


def _inproj_kernel(x_ref, g_ref, pos_ref, invf_ref, wt_ref, wdt_ref,
                   out_ref, dt_ref, n_scr, rc_scr, rp_scr, rm_scr):
    j = pl.program_id(1)
    nt = (((1,), (1,)), ((), ()))
    half = ROT_DIM // 2

    @pl.when(j == 0)
    def _():
        n = _rms(x_ref[...], g_ref[...])
        n_scr[...] = n.astype(_bf16)
        lane = lax.broadcasted_iota(jnp.int32, (1, LANES), 1)
        dt = lax.dot_general(n_scr[...], wdt_ref[...].astype(_bf16), nt, preferred_element_type=_f32)
        dt_ref[...] = jnp.where(lane < 2 * SSD_HEADS, dt, 0.0)
        ang = pos_ref[...] * invf_ref[...]
        cos, sin = jnp.cos(ang), jnp.sin(ang)
        l64 = lane & (DIFF_HEAD_DIM - 1)
        rc_scr[...] = jnp.where(l64 < ROT_DIM, cos, 1.0)
        rp_scr[...] = jnp.where(jnp.logical_and(l64 >= half, l64 < ROT_DIM), sin, 0.0)
        rm_scr[...] = jnp.where(l64 < half, -sin, 0.0)

    acc = lax.dot_general(n_scr[...], wt_ref[...].astype(_bf16), nt, preferred_element_type=_f32)
    q_lo, k_lo, k_hi = COL_Q // IN_TN, COL_K // IN_TN, (COL_K + ATTN_W) // IN_TN
    is_rot = jnp.logical_and(j >= q_lo, j < k_hi)

    @pl.when(is_rot)
    def _():
        scale = jnp.where(j < k_lo, Q_SCALE, 1.0).astype(_f32)
        rc, rp, rm = rc_scr[...], rp_scr[...], rm_scr[...]
        for hb in range(IN_TN // LANES):
            t = acc[:, hb * LANES:(hb + 1) * LANES]
            r = (t * rc + pltpu.roll(t, ROT_DIM // 2, 1) * rp
                 + pltpu.roll(t, LANES - ROT_DIM // 2, 1) * rm)
            out_ref[:, hb * LANES:(hb + 1) * LANES] = (r * scale).astype(_bf16)

    @pl.when(jnp.logical_not(is_rot))
    def _():
        out_ref[...] = acc.astype(_bf16)


SRC_XBC = SSD_W
SRC_DT = SRC_XBC + XBC_W
SRC_Q = SRC_DT + 2 * SSD_HEADS
SRC_V = SRC_Q + 2 * ATTN_W
SRC_ALIGN = 32
assert all(s % SRC_ALIGN == 0 for s in (SRC_XBC, SRC_DT, SRC_Q, SRC_V, IN_TN))


def _src_row(j):
    jq, jx, jv = COL_Q // IN_TN, COL_XBC // IN_TN, COL_V // IN_TN
    row = jnp.where(j < jq, j * IN_TN,
                    jnp.where(j < jx, SRC_Q + (j - jq) * IN_TN,
                              jnp.where(j < jv, SRC_XBC + (j - jx) * IN_TN,
                                        SRC_V + (j - jv) * IN_TN)))
    return pl.multiple_of(row, SRC_ALIGN)


def _inproj(x2, g, pos_col, invf_row, w_in_t):
    t = x2.shape[0]
    grid = (t // IN_TM, MAIN_W // IN_TN)
    return pl.pallas_call(
        _inproj_kernel,
        grid=grid,
        in_specs=[
            pl.BlockSpec((IN_TM, D_MODEL), lambda i, j: (i, 0)),
            pl.BlockSpec((1, D_MODEL), lambda i, j: (0, 0)),
            pl.BlockSpec((IN_TM, 1), lambda i, j: (i, 0)),
            pl.BlockSpec((1, LANES), lambda i, j: (0, 0)),
            pl.BlockSpec((pl.Element(IN_TN), pl.Element(D_MODEL)), lambda i, j: (_src_row(j), 0)),
            pl.BlockSpec((pl.Element(DT_PAD), pl.Element(D_MODEL)), lambda i, j: (SRC_DT, 0)),
        ],
        out_specs=[
            pl.BlockSpec((IN_TM, IN_TN), lambda i, j: (i, j)),
            pl.BlockSpec((IN_TM, DT_PAD), lambda i, j: (i, 0)),
        ],
        out_shape=[
            jax.ShapeDtypeStruct((t, MAIN_W), _bf16),
            jax.ShapeDtypeStruct((t, DT_PAD), _f32),
        ],
        scratch_shapes=[pltpu.VMEM((IN_TM, D_MODEL), _bf16)]
        + [pltpu.VMEM((IN_TM, LANES), _f32)] * 3,
        compiler_params=pltpu.CompilerParams(
            dimension_semantics=("arbitrary", "arbitrary"),
            vmem_limit_bytes=VMEM_LIMIT),
        name="inproj",
    )(x2, g, pos_col, invf_row, w_in_t, w_in_t)


CONV_HALO = 16


def _ssd_kernel(xbc_ref, dt_ref, cw_ref, cb_ref, dtb_ref, a_ref, dsk_ref,
                y_ref, xpad, xact, dts, yacc, state_f, state_b, seq):
    n_chunks = seq // CHUNK

    zeros_halo = jnp.zeros((CONV_HALO, XBC_W), _bf16)
    xpad[0:CONV_HALO, :] = zeros_halo
    xpad[CONV_HALO + seq:CONV_HALO + seq + CONV_HALO, :] = zeros_halo
    xpad[CONV_HALO:CONV_HALO + seq, :] = xbc_ref[0]
    cw = cw_ref[...]
    cb = cb_ref[...]
    win = CHUNK + 2 * CONV_HALO

    def conv_body(c, carry):
        r0 = pl.multiple_of(c * CHUNK, CHUNK)
        blk = xpad[pl.ds(r0, win), :].astype(_f32)
        acc = blk * cw[CONV_W // 2:CONV_W // 2 + 1, :]
        for k in range(CONV_W):
            sh = CONV_W // 2 - k
            if sh == 0:
                continue
            acc = acc + pltpu.roll(blk, sh % win, 0) * cw[k:k + 1, :]
        v = acc[CONV_HALO:CONV_HALO + CHUNK, :] + cb
        xact[pl.ds(r0, CHUNK), :] = _silu(v).astype(_bf16)
        return carry

    lax.fori_loop(0, n_chunks, conv_body, 0)

    raw = dt_ref[0] + dtb_ref[...]
    dts[...] = jnp.maximum(raw, 0.0) + jnp.log1p(jnp.exp(-jnp.abs(raw)))

    a_row = a_ref[...]
    dsk = dsk_ref[...]
    row_i = lax.broadcasted_iota(jnp.int32, (CHUNK, CHUNK), 0)
    col_i = lax.broadcasted_iota(jnp.int32, (CHUNK, CHUNK), 1)

    def chunk_step(c, reverse, state):
        r0 = pl.multiple_of(c * CHUNK, CHUNK)
        dtc = dts[pl.ds(r0, CHUNK), :]
        cs = dtc * a_row
        k = 1
        while k < CHUNK:
            if reverse:
                cs = cs + jnp.where(row_i < CHUNK - k, pltpu.roll(cs, CHUNK - k, 0), 0.0)
            else:
                cs = cs + jnp.where(row_i >= k, pltpu.roll(cs, k, 0), 0.0)
            k *= 2
        cs_t = cs.T
        dt_t = dtc.T
        end_col = cs_t[:, 0:1] if reverse else cs_t[:, CHUNK - 1:CHUNK]
        w_t = jnp.exp(end_col - cs_t) * dt_t
        end_row = cs[0:1, :] if reverse else cs[CHUNK - 1:CHUNK, :]
        dec_row = jnp.exp(end_row)
        tri = (row_i <= col_i) if reverse else (row_i >= col_i)
        lane0 = SSD_HEADS if reverse else 0
        first_head = col_i < SSD_HEADDIM
        rows = pl.ds(r0, CHUNK)
        for g in range(SSD_GROUPS):
            b_g = xact[rows, SSD_W + g * SSD_STATE:SSD_W + (g + 1) * SSD_STATE]
            c_g = xact[rows, SSD_W + (SSD_GROUPS + g) * SSD_STATE:SSD_W + (SSD_GROUPS + g + 1) * SSD_STATE]
            cbm = lax.dot_general(c_g, b_g, (((1,), (1,)), ((), ())),
                                  preferred_element_type=_f32)
            b_t = b_g.astype(_f32).T
            for pp in range(HEADS_PER_GROUP // 2):
                pair = g * (HEADS_PER_GROUP // 2) + pp
                cols = slice(pair * LANES, (pair + 1) * LANES)
                xs_p = xact[rows, cols]
                st = state[pair]
                off = jnp.dot(c_g, st.astype(_bf16), preferred_element_type=_f32)
                halves = []
                for hh in range(2):
                    ln = lane0 + 2 * pair + hh
                    colb = jnp.broadcast_to(cs[:, ln:ln + 1], (CHUNK, CHUNK))
                    m_h = (cbm * jnp.exp(jnp.where(tri, colb - cs_t[ln:ln + 1, :], NEG_BIG))
                           * dt_t[ln:ln + 1, :])
                    y_h = (jnp.dot(m_h.astype(_bf16), xs_p, preferred_element_type=_f32)
                           + jnp.exp(colb) * off)
                    upd = jnp.dot((b_t * w_t[ln:ln + 1, :]).astype(_bf16), xs_p,
                                  preferred_element_type=_f32)
                    dec = jnp.broadcast_to(dec_row[:, ln:ln + 1], (CHUNK, CHUNK))
                    halves.append((y_h, upd, dec))
                y_p = jnp.where(first_head, halves[0][0], halves[1][0])
                upd_p = jnp.where(first_head, halves[0][1], halves[1][1])
                dec_p = jnp.where(first_head, halves[0][2], halves[1][2])
                state[pair] = st * dec_p + upd_p
                if reverse:
                    y_ref[0, rows, cols] = (yacc[rows, cols] + y_p).astype(_bf16)
                else:
                    yacc[rows, cols] = y_p + xs_p.astype(_f32) * dsk[:, cols]

    state_f[...] = jnp.zeros_like(state_f)
    state_b[...] = jnp.zeros_like(state_b)

    def fwd_body(c, carry):
        chunk_step(c, False, state_f)
        return carry

    lax.fori_loop(0, n_chunks, fwd_body, 0)

    def bwd_body(i, carry):
        chunk_step(n_chunks - 1 - i, True, state_b)
        return carry

    lax.fori_loop(0, n_chunks, bwd_body, 0)


def _ssd(proj3, dt3, conv_w, conv_b, dt_bias, a_row, d_skip_row):
    b, seq, _ = proj3.shape
    kern = functools.partial(_ssd_kernel, seq=seq)
    assert COL_XBC % XBC_W == 0
    return pl.pallas_call(
        kern,
        grid=(b,),
        in_specs=[
            pl.BlockSpec((1, seq, XBC_W), lambda i: (i, 0, COL_XBC // XBC_W)),
            pl.BlockSpec((1, seq, DT_PAD), lambda i: (i, 0, 0)),
            pl.BlockSpec((CONV_W, XBC_W), lambda i: (0, 0)),
            pl.BlockSpec((1, XBC_W), lambda i: (0, 0)),
            pl.BlockSpec((1, DT_PAD), lambda i: (0, 0)),
            pl.BlockSpec((1, DT_PAD), lambda i: (0, 0)),
            pl.BlockSpec((1, SSD_W), lambda i: (0, 0)),
        ],
        out_specs=pl.BlockSpec((1, seq, SSD_W), lambda i: (i, 0, 0)),
        out_shape=jax.ShapeDtypeStruct((b, seq, SSD_W), _bf16),
        scratch_shapes=[
            pltpu.VMEM((seq + 2 * CONV_HALO, XBC_W), _bf16),
            pltpu.VMEM((seq, XBC_W), _bf16),
            pltpu.VMEM((seq, DT_PAD), _f32),
            pltpu.VMEM((seq, SSD_W), _f32),
            pltpu.VMEM((SSD_HEADS // 2, SSD_STATE, 2 * SSD_HEADDIM), _f32),
            pltpu.VMEM((SSD_HEADS // 2, SSD_STATE, 2 * SSD_HEADDIM), _f32),
        ],
        compiler_params=pltpu.CompilerParams(
            dimension_semantics=("arbitrary",),
            vmem_limit_bytes=VMEM_LIMIT),
        name="ssd",
    )(proj3, dt3, conv_w, conv_b, dt_bias, a_row, d_skip_row)


ATT_QB = 256
ATT_UNROLL = 8


def _attn_kernel(q_ref, k_ref, v_ref, lam_ref, g_ref, o_ref, v1_scr, *, seq):
    hw = 2 * DIFF_HEAD_DIM
    lv = lam_ref[...]
    lam = (jnp.exp(jnp.sum(lv[0:1] * lv[1:2], axis=-1, keepdims=True))
           - jnp.exp(jnp.sum(lv[2:3] * lv[3:4], axis=-1, keepdims=True)) + LAM_INIT)
    k = k_ref[0]
    g = g_ref[...]
    v1_scr[:, 0:hw] = v_ref[0]
    v1_scr[:, hw:2 * hw] = (lax.broadcasted_iota(jnp.int32, (seq, hw), 1) == 0).astype(_bf16)
    first = lax.broadcasted_iota(jnp.int32, (1, hw), 1) < DIFF_HEAD_DIM
    nt = (((1,), (1,)), ((), ()))

    def chain(r0):
        q = q_ref[0, pl.ds(r0, ATT_QB), :]
        zero = jnp.zeros_like(q)
        ovs = []
        for qc in (jnp.where(first, q, zero), jnp.where(first, zero, q)):
            s = lax.dot_general(qc, k, nt, preferred_element_type=_f32)
            e = jnp.exp2(s - jnp.max(s, axis=-1, keepdims=True)).astype(_bf16)
            ovs.append(jnp.dot(e, v1_scr[...], preferred_element_type=_f32))
        o1, l1 = ovs[0][:, 0:hw], ovs[0][:, hw:hw + 1]
        o2, l2 = ovs[1][:, 0:hw], ovs[1][:, hw:hw + 1]
        o = o1 / l1 - (lam / l2) * o2
        o = _rms(o, g) * (1.0 - LAM_INIT)
        o_ref[0, pl.ds(r0, ATT_QB), :] = o.astype(_bf16)

    step = ATT_QB * ATT_UNROLL

    def body(i, carry):
        for u in range(ATT_UNROLL):
            chain(pl.multiple_of(i * step + u * ATT_QB, ATT_QB))
        return carry

    lax.fori_loop(0, seq // step, body, 0)


def _attn(proj3, lam_vecs, subln_g):
    b, seq, _ = proj3.shape
    hw = 2 * DIFF_HEAD_DIM
    kern = functools.partial(_attn_kernel, seq=seq)
    return pl.pallas_call(
        kern,
        grid=(b, DIFF_HEADS),
        in_specs=[
            pl.BlockSpec((1, seq, hw), lambda i, h: (i, 0, COL_Q // hw + h)),
            pl.BlockSpec((1, seq, hw), lambda i, h: (i, 0, COL_K // hw + h)),
            pl.BlockSpec((1, seq, hw), lambda i, h: (i, 0, COL_V // hw + h)),
            pl.BlockSpec((4, DIFF_HEAD_DIM), lambda i, h: (0, 0)),
            pl.BlockSpec((1, hw), lambda i, h: (0, 0)),
        ],
        out_specs=pl.BlockSpec((1, seq, hw), lambda i, h: (i, 0, h)),
        out_shape=jax.ShapeDtypeStruct((b, seq, ATTN_W), _bf16),
        scratch_shapes=[pltpu.VMEM((seq, 2 * hw), _bf16)],
        compiler_params=pltpu.CompilerParams(
            dimension_semantics=("arbitrary", "arbitrary"),
            vmem_limit_bytes=VMEM_LIMIT),
        name="diffattn",
    )(proj3, proj3, proj3, lam_vecs, subln_g)


OUT_TM = 256
ROUTE_ROWS = 8


def _outproj_kernel(y_ref, z_ref, att_ref, x_ref, w_ref, gs_ref, gf_ref, wr_ref, br_ref,
                    h_ref, hn_ref, route_ref, cnt_ref, gate_ref, cnt_scr):
    @pl.when(pl.program_id(0) == 0)
    def _():
        cnt_scr[...] = jnp.zeros_like(cnt_scr)

    y = y_ref[...].astype(_f32)
    z = z_ref[...].astype(_f32)
    s = _rms(y * _silu(z), gs_ref[...]).astype(_bf16)
    acc = jnp.dot(s, w_ref[0:SSD_W, :], preferred_element_type=_f32)
    acc = acc + jnp.dot(att_ref[...], w_ref[SSD_W:SSD_W + ATTN_W, :], preferred_element_type=_f32)
    h1 = x_ref[...] + acc
    h_ref[...] = h1
    hn = _rms(h1, gf_ref[...])
    half = D_MODEL // 2
    _store_row_tiles(hn_ref, _pack_bf16_pair(hn[:, 0:half], hn[:, half:D_MODEL]), hn.shape[0])
    hn_hi = hn.astype(_bf16)
    hn_lo = (hn - hn_hi.astype(_f32)).astype(_bf16)
    a = jnp.dot(hn_hi, wr_ref[...], preferred_element_type=_f32)
    bb = jnp.dot(hn_lo, wr_ref[:, 0:LANES], preferred_element_type=_f32)
    logits = (a[:, 0:LANES] + a[:, LANES:2 * LANES] + bb + br_ref[...]).T
    tm = logits.shape[1]
    iota = lax.broadcasted_iota(jnp.int32, (MOE_GROUPS, tm), 0)

    def first_argmax(val, vmax):
        return jnp.min(jnp.where(val == vmax, iota, MOE_GROUPS), axis=0, keepdims=True)

    gl = logits[0:MOE_GROUPS]
    gmax = jnp.max(gl, axis=0, keepdims=True)
    p_g = 1.0 / jnp.sum(jnp.exp(gl - gmax), axis=0, keepdims=True)
    g_sel = first_argmax(gl, gmax)
    el = jnp.zeros((EXPERTS_PER_GROUP, tm), _f32)
    for gi in range(MOE_GROUPS):
        lo = MOE_GROUPS + gi * EXPERTS_PER_GROUP
        el = jnp.where(g_sel == gi, logits[lo:lo + EXPERTS_PER_GROUP], el)
    ee = jnp.exp(el - jnp.max(el, axis=0, keepdims=True))
    pe = ee / jnp.sum(ee, axis=0, keepdims=True)
    p1 = jnp.max(pe, axis=0, keepdims=True)
    i1 = first_argmax(pe, p1)
    pe2 = jnp.where(iota == i1, -1.0, pe)
    p2 = jnp.max(pe2, axis=0, keepdims=True)
    i2 = first_argmax(pe2, p2)
    den = p1 + p2
    e1 = g_sel * EXPERTS_PER_GROUP + i1
    e2 = g_sel * EXPERTS_PER_GROUP + i2
    iota_e = lax.broadcasted_iota(jnp.int32, (N_EXPERTS, tm), 0)
    oh1 = (iota_e == e1).astype(_f32)
    oh2 = (iota_e == e2).astype(_f32)
    both = oh1 + oh2
    earlier = (lax.broadcasted_iota(jnp.int32, (tm, tm), 0)
               < lax.broadcasted_iota(jnp.int32, (tm, tm), 1)).astype(_bf16)
    before = cnt_scr[:, 0:1] + jnp.dot(both.astype(_bf16), earlier,
                                       preferred_element_type=_f32)
    r1 = jnp.sum(oh1 * before, axis=0, keepdims=True).astype(jnp.int32)
    r2 = jnp.sum(oh2 * before, axis=0, keepdims=True).astype(jnp.int32)
    cnt_scr[...] = cnt_scr[...] + jnp.sum(both, axis=1, keepdims=True)
    cnt_ref[...] = cnt_scr[...].astype(jnp.int32)
    route_ref[...] = jnp.where(iota == 0, e1, jnp.where(iota == 1, e2,
                               jnp.where(iota == 2, r1, jnp.where(iota == 3, r2, 0))))
    g8 = jnp.where(iota == 0, p_g * (p1 / den), jnp.where(iota == 1, p_g * (p2 / den), 0.0))
    gate_ref[...] = jnp.concatenate(
        [g8, jnp.zeros((LANES - ROUTE_ROWS, tm), _f32)], axis=0).T


def _outproj(y2, proj2, att2, x2, w_out, g_ssd, g_ffn, wr_t, br_col):
    t = x2.shape[0]
    row = lambda i: (i, 0)
    fix = lambda i: (0, 0)
    return pl.pallas_call(
        _outproj_kernel,
        grid=(t // OUT_TM,),
        in_specs=[
            pl.BlockSpec((OUT_TM, SSD_W), row),
            pl.BlockSpec((OUT_TM, SSD_W), row),
            pl.BlockSpec((OUT_TM, ATTN_W), row),
            pl.BlockSpec((OUT_TM, D_MODEL), row),
            pl.BlockSpec((SSD_W + ATTN_W, D_MODEL), fix),
            pl.BlockSpec((1, SSD_W), fix),
            pl.BlockSpec((1, D_MODEL), fix),
            pl.BlockSpec((D_MODEL, 2 * LANES), fix),
            pl.BlockSpec((1, LANES), fix),
        ],
        out_specs=[
            pl.BlockSpec((OUT_TM, D_MODEL), row),
            pl.BlockSpec((OUT_TM * ROW_TILE, LANES), row),
            pl.BlockSpec((ROUTE_ROWS, OUT_TM), lambda i: (0, i)),
            pl.BlockSpec((N_EXPERTS, LANES), fix),
            pl.BlockSpec((OUT_TM, LANES), row),
        ],
        out_shape=[
            jax.ShapeDtypeStruct((t, D_MODEL), _f32),
            jax.ShapeDtypeStruct((t * ROW_TILE, LANES), jnp.uint32),
            jax.ShapeDtypeStruct((ROUTE_ROWS, t), jnp.int32),
            jax.ShapeDtypeStruct((N_EXPERTS, LANES), jnp.int32),
            jax.ShapeDtypeStruct((t, LANES), _f32),
        ],
        scratch_shapes=[pltpu.VMEM((N_EXPERTS, LANES), _f32)],
        compiler_params=pltpu.CompilerParams(
            dimension_semantics=("arbitrary",),
            vmem_limit_bytes=VMEM_LIMIT),
        name="outproj_router",
    )(y2, proj2, att2, x2, w_out, g_ssd, g_ffn, wr_t, br_col)


def _n_rows(n_tok):
    n_assign = n_tok * TOP_K
    return (n_assign + N_EXPERTS * (ROW_BLOCK - 1) + ROW_BLOCK - 1) // ROW_BLOCK * ROW_BLOCK


def _prefix_sum(v, axis):
    n = v.shape[axis]
    idx = lax.broadcasted_iota(jnp.int32, v.shape, axis)
    k = 1
    while k < n:
        v = v + jnp.where(idx >= k, pltpu.roll(v, k, axis), 0)
        k *= 2
    return v


def _plan_kernel(route_ref, cnt_ref, pos_ref, blk_ref, meta_ref, *, n_blocks_pad):
    n_tok = route_ref.shape[1]
    cnt_col = jnp.concatenate(
        [cnt_ref[...], jnp.zeros((LANES - N_EXPERTS, LANES), jnp.int32)], axis=0)
    pad_up = lambda c: (c + (ROW_BLOCK - 1)) & (-ROW_BLOCK)
    ends_col = _prefix_sum(pad_up(cnt_col), 0)
    starts_col = (ends_col - pad_up(cnt_col)).astype(_f32)
    cnt_row = cnt_col.astype(_f32).T.astype(jnp.int32)
    ends_row = _prefix_sum(pad_up(cnt_row), 1)
    starts_row = ends_row - pad_up(cnt_row)
    n_used = ends_row[0:1, LANES - 1:LANES] >> (ROW_BLOCK.bit_length() - 1)

    ch = 1024
    iota_e = lax.broadcasted_iota(jnp.int32, (LANES, ch), 0)
    for c0 in range(0, n_tok, ch):
        rt = route_ref[:, c0:c0 + ch]
        s1 = jnp.sum(jnp.where(iota_e == rt[0:1], starts_col[:, 0:1], 0.0), axis=0, keepdims=True)
        s2 = jnp.sum(jnp.where(iota_e == rt[1:2], starts_col[:, 0:1], 0.0), axis=0, keepdims=True)
        p1 = s1.astype(jnp.int32) + rt[2:3]
        p2 = s2.astype(jnp.int32) + rt[3:4]
        sub = lax.broadcasted_iota(jnp.int32, (ROUTE_ROWS, ch), 0)
        pos_ref[:, c0:c0 + ch] = jnp.where(sub == 0, p1, jnp.where(sub == 1, p2, 0))

    blk_start = lax.broadcasted_iota(jnp.int32, (LANES, n_blocks_pad), 1) * ROW_BLOCK
    e_idx = lax.broadcasted_iota(jnp.int32, (LANES, n_blocks_pad), 0)
    real = e_idx < N_EXPERTS
    be = jnp.sum(jnp.where(jnp.logical_and(real, ends_col[:, 0:1] <= blk_start), 1.0, 0.0),
                 axis=0, keepdims=True).astype(jnp.int32)
    be = jnp.minimum(be, N_EXPERTS - 1)
    last = jnp.max(jnp.where(jnp.logical_and(real, cnt_col[:, 0:1] > 0), e_idx.astype(_f32), 0.0),
                   axis=0, keepdims=True).astype(jnp.int32)
    blk_i = lax.broadcasted_iota(jnp.int32, (1, n_blocks_pad), 1)
    be = jnp.where(blk_i < n_used, be, last)
    sub = lax.broadcasted_iota(jnp.int32, (ROUTE_ROWS, n_blocks_pad), 0)
    blk_ref[...] = jnp.where(sub == 0, be, jnp.where(sub == 1, n_used, 0))
    sub = lax.broadcasted_iota(jnp.int32, (ROUTE_ROWS, LANES), 0)
    meta_ref[...] = jnp.where(sub == 0, cnt_row[0:1], jnp.where(sub == 1, starts_row[0:1], 0))


def _plan(route, cnt):
    n_tok = route.shape[1]
    n_blocks_pad = -(-(_n_rows(n_tok) // ROW_BLOCK) // LANES) * LANES
    return pl.pallas_call(
        functools.partial(_plan_kernel, n_blocks_pad=n_blocks_pad),
        out_shape=[
            jax.ShapeDtypeStruct((ROUTE_ROWS, n_tok), jnp.int32),
            jax.ShapeDtypeStruct((ROUTE_ROWS, n_blocks_pad), jnp.int32),
            jax.ShapeDtypeStruct((ROUTE_ROWS, LANES), jnp.int32),
        ],
        compiler_params=pltpu.CompilerParams(vmem_limit_bytes=VMEM_LIMIT),
        name="route_plan",
    )(route, cnt)


DISP_TM = 256


def _dispatch_kernel(pos_ref, cnt_ref, start_ref, nused_ref, h_ref, x_hbm, stage, zrow, sem, zsem,
                     *, n_tok, n_blocks):
    i = pl.program_id(0)
    n_steps = pl.num_programs(0)
    slot = i % 2
    base = i * DISP_TM
    tile = lambda r: pl.ds(pl.multiple_of(r * ROW_TILE, ROW_TILE), ROW_TILE)

    def wait_slot(sl):
        for _ in range(TOP_K):
            pltpu.make_async_copy(stage.at[sl], x_hbm.at[pl.ds(0, DISP_TM * ROW_TILE), :],
                                  sem.at[sl]).wait()

    @pl.when(i >= 2)
    def _():
        wait_slot(slot)

    stage[slot] = h_ref[...]

    def body(r, carry):
        for kk in range(TOP_K):
            row = pos_ref[kk * n_tok + base + r]
            pltpu.make_async_copy(stage.at[slot, tile(r), :], x_hbm.at[tile(row), :],
                                  sem.at[slot]).start(priority=kk % 2)
        return carry

    lax.fori_loop(0, DISP_TM, body, 0, unroll=8)

    @pl.when(i == n_steps - 1)
    def _():
        zrow[...] = jnp.zeros_like(zrow)

        def per_expert(e, n_fill):
            cnt = cnt_ref[e]
            start = start_ref[e]
            end = (cnt + (ROW_BLOCK - 1)) & (-ROW_BLOCK)

            def fill(r, carry):
                pltpu.make_async_copy(zrow.at[tile(0), :], x_hbm.at[tile(start + r), :], zsem).start()
                return carry

            lax.fori_loop(cnt, end, fill, 0)
            return n_fill + (end - cnt)

        n_fill = lax.fori_loop(0, N_EXPERTS, per_expert, 0)
        blk_rows = ROW_BLOCK * ROW_TILE

        def fill_block(blk, carry):
            dst = pl.ds(pl.multiple_of(blk * blk_rows, blk_rows), blk_rows)
            pltpu.make_async_copy(zrow, x_hbm.at[dst, :], zsem).start()
            return carry

        lax.fori_loop(nused_ref[0], n_blocks, fill_block, 0)

        def wait_row(r, carry):
            pltpu.make_async_copy(zrow.at[tile(0), :], x_hbm.at[tile(0), :], zsem).wait()
            return carry

        lax.fori_loop(0, n_fill, wait_row, 0)

        def wait_block(blk, carry):
            pltpu.make_async_copy(zrow, x_hbm.at[pl.ds(0, blk_rows), :], zsem).wait()
            return carry

        lax.fori_loop(nused_ref[0], n_blocks, wait_block, 0)

        @pl.when(n_steps >= 2)
        def _():
            wait_slot(1 - slot)

        wait_slot(slot)


def _dispatch_rows(pos, counts, starts, n_used, hn_packed):
    t = hn_packed.shape[0] // ROW_TILE
    n_rows = _n_rows(t)
    blk = (DISP_TM * ROW_TILE, LANES)
    grid_spec = pltpu.PrefetchScalarGridSpec(
        num_scalar_prefetch=4,
        grid=(t // DISP_TM,),
        in_specs=[pl.BlockSpec(blk, lambda i, *_: (i, 0))],
        out_specs=pl.BlockSpec(memory_space=pl.ANY),
        scratch_shapes=[
            pltpu.VMEM((2,) + blk, hn_packed.dtype),
            pltpu.VMEM((ROW_BLOCK * ROW_TILE, LANES), hn_packed.dtype),
            pltpu.SemaphoreType.DMA((2,)),
            pltpu.SemaphoreType.DMA(()),
        ],
    )
    return pl.pallas_call(
        functools.partial(_dispatch_kernel, n_tok=t, n_blocks=n_rows // ROW_BLOCK),
        grid_spec=grid_spec,
        out_shape=jax.ShapeDtypeStruct((n_rows * ROW_TILE, LANES), hn_packed.dtype),
        compiler_params=pltpu.CompilerParams(
            dimension_semantics=("arbitrary",),
            vmem_limit_bytes=VMEM_LIMIT),
        name="dispatch_rows",
    )(pos, counts, starts, n_used, hn_packed)


MOE_W_SPLIT = 4
MOE_W_BUFS = 3


def _moe_kernel(be_ref, nused_ref, x_ref, wg_hbm, wu_hbm, wd_hbm,
                out_ref, wg_f, wu_f, wd_f, wg_s, wu_s, wd_s, wsem, nexp, *, n_blocks):
    i = pl.program_id(0)
    nused = nused_ref[0]
    half = D_MODEL // 2

    def weight_copies(e, sl):
        cps = []
        for src, dst in ((wg_hbm, wg_f), (wu_hbm, wu_f), (wd_hbm, wd_f)):
            rows = src.shape[1] // MOE_W_SPLIT
            for c in range(MOE_W_SPLIT):
                rs = pl.ds(c * rows, rows)
                cps.append(pltpu.make_async_copy(src.at[e, rs, :], dst.at[sl, rs, :], wsem.at[sl]))
        return cps

    def start_all(cps):
        for n, cp in enumerate(cps):
            cp.start(priority=n % 2)

    def block_expert(b):
        return be_ref[jnp.minimum(b, n_blocks - 1)]

    def next_change(b0):
        e0 = block_expert(b0)
        return lax.while_loop(
            lambda b: jnp.logical_and(b < nused, block_expert(b) == e0), lambda b: b + 1, b0 + 1)

    @pl.when(i == 0)
    def _():
        nexp[0] = 0
        start_all(weight_copies(be_ref[0], 0))
        nxt = next_change(0)

        @pl.when(nxt < nused)
        def _():
            start_all(weight_copies(block_expert(nxt), 1))

    @pl.when(i < nused)
    def _():
        e = be_ref[i]
        new_expert = jnp.logical_or(i == 0, e != be_ref[jnp.maximum(i - 1, 0)])

        @pl.when(new_expert)
        def _():
            sl = nexp[0] % MOE_W_BUFS
            ahead = next_change(jnp.minimum(next_change(i), nused - 1))

            @pl.when(jnp.logical_and(next_change(i) < nused, ahead < nused))
            def _():
                start_all(weight_copies(block_expert(ahead), (nexp[0] + 2) % MOE_W_BUFS))

            for cp in weight_copies(e, sl):
                cp.wait()
            wg_s[...] = wg_f[sl].astype(_bf16)
            wu_s[...] = wu_f[sl].astype(_bf16)
            wd_s[...] = wd_f[sl].astype(_bf16)
            nexp[0] = nexp[0] + 1

        x_lo, x_hi = _unpack_bf16_pair(_load_row_tiles(x_ref, ROW_BLOCK))
        x_lo, x_hi = x_lo.astype(_bf16), x_hi.astype(_bf16)
        gg = (jnp.dot(x_lo, wg_s[0:half, :], preferred_element_type=_f32)
              + jnp.dot(x_hi, wg_s[half:D_MODEL, :], preferred_element_type=_f32))
        uu = (jnp.dot(x_lo, wu_s[0:half, :], preferred_element_type=_f32)
              + jnp.dot(x_hi, wu_s[half:D_MODEL, :], preferred_element_type=_f32))
        a = (_silu(gg) * uu).astype(_bf16)
        y = jnp.dot(a, wd_s[...], preferred_element_type=_f32)
        _store_row_tiles(out_ref, _pack_bf16_pair(y[:, 0:half], y[:, half:D_MODEL]), ROW_BLOCK)

    @pl.when(i >= nused)
    def _():
        out_ref[...] = jnp.zeros_like(out_ref)


def _moe(block_expert, n_used, x_rows, w_gate, w_up, w_down):
    n_blocks = x_rows.shape[0] // (ROW_BLOCK * ROW_TILE)
    blk = (ROW_BLOCK * ROW_TILE, LANES)
    omap = lambda i, be, nu: (i, 0)
    grid_spec = pltpu.PrefetchScalarGridSpec(
        num_scalar_prefetch=2,
        grid=(n_blocks,),
        in_specs=[
            pl.BlockSpec(blk, omap),
            pl.BlockSpec(memory_space=pl.ANY),
            pl.BlockSpec(memory_space=pl.ANY),
            pl.BlockSpec(memory_space=pl.ANY),
        ],
        out_specs=pl.BlockSpec(blk, omap),
        scratch_shapes=[
            pltpu.VMEM((MOE_W_BUFS, D_MODEL, EXPERT_FF), _f32),
            pltpu.VMEM((MOE_W_BUFS, D_MODEL, EXPERT_FF), _f32),
            pltpu.VMEM((MOE_W_BUFS, EXPERT_FF, D_MODEL), _f32),
            pltpu.VMEM((D_MODEL, EXPERT_FF), _bf16),
            pltpu.VMEM((D_MODEL, EXPERT_FF), _bf16),
            pltpu.VMEM((EXPERT_FF, D_MODEL), _bf16),
            pltpu.SemaphoreType.DMA((MOE_W_BUFS,)),
            pltpu.SMEM((1,), jnp.int32),
        ],
    )
    return pl.pallas_call(
        functools.partial(_moe_kernel, n_blocks=n_blocks),
        grid_spec=grid_spec,
        out_shape=jax.ShapeDtypeStruct(x_rows.shape, x_rows.dtype),
        compiler_params=pltpu.CompilerParams(
            dimension_semantics=("arbitrary",),
            vmem_limit_bytes=VMEM_LIMIT),
        name="moe_experts",
    )(block_expert, n_used, x_rows, w_gate, w_up, w_down)


TAIL_TM = 256


def _tail_kernel(pos_ref, h_ref, gate_ref, y_hbm, p_ref, wpp_ref, gp_ref, wpg_ref, bpg_ref,
                 gfin_ref, out_ref, ybuf, sem, *, n_tok):
    i = pl.program_id(0)
    n_steps = pl.num_programs(0)
    slot = i % 2
    tile = lambda r: pl.ds(pl.multiple_of(r * ROW_TILE, ROW_TILE), ROW_TILE)

    def start_row(base, r, sl):
        for kk in range(TOP_K):
            row = pos_ref[kk * n_tok + base + r]
            pltpu.make_async_copy(y_hbm.at[tile(row), :], ybuf.at[sl, kk, tile(r), :],
                                  sem.at[sl]).start(priority=kk % 2)

    def wait_slot(sl):
        for kk in range(TOP_K):
            pltpu.make_async_copy(y_hbm.at[pl.ds(0, TAIL_TM * ROW_TILE), :], ybuf.at[sl, kk],
                                  sem.at[sl]).wait()

    @pl.when(i == 0)
    def _():
        def body(r, carry):
            start_row(0, r, 0)
            return carry

        lax.fori_loop(0, TAIL_TM, body, 0, unroll=8)

    wait_slot(slot)

    nxt_base = jnp.minimum(i + 1, n_steps - 1) * TAIL_TM
    for r in range(TAIL_TM):
        start_row(nxt_base, r, 1 - slot)

    gates = gate_ref[...]
    g1, g2 = gates[:, 0:1], gates[:, 1:2]
    y1_lo, y1_hi = _unpack_bf16_pair(_load_row_tiles(ybuf.at[slot, 0], TAIL_TM))
    y2_lo, y2_hi = _unpack_bf16_pair(_load_row_tiles(ybuf.at[slot, 1], TAIL_TM))
    moe = jnp.concatenate([g1 * y1_lo + g2 * y2_lo, g1 * y1_hi + g2 * y2_hi], axis=1)
    h2 = h_ref[...] + moe
    ple = _rms(jnp.dot(p_ref[...].astype(_bf16), wpp_ref[...], preferred_element_type=_f32),
               gp_ref[...])
    lg = jnp.dot(h2.astype(_bf16), wpg_ref[...], preferred_element_type=_f32) + bpg_ref[...]
    h3 = h2 + (1.0 / (1.0 + jnp.exp(-lg))) * ple
    out_ref[...] = _rms(h3, gfin_ref[...])

    @pl.when(i == n_steps - 1)
    def _():
        wait_slot(1 - slot)


def _tail(pos, h1, gates, y_rows, p2, w_pp, g_ple, w_pg, b_pg, g_fin):
    t = h1.shape[0]
    row = lambda i, ps: (i, 0)
    fix = lambda i, ps: (0, 0)
    grid_spec = pltpu.PrefetchScalarGridSpec(
        num_scalar_prefetch=1,
        grid=(t // TAIL_TM,),
        in_specs=[
            pl.BlockSpec((TAIL_TM, D_MODEL), row),
            pl.BlockSpec((TAIL_TM, LANES), row),
            pl.BlockSpec(memory_space=pl.ANY),
            pl.BlockSpec((TAIL_TM, PLE_DIM), row),
            pl.BlockSpec((PLE_DIM, D_MODEL), fix),
            pl.BlockSpec((1, D_MODEL), fix),
            pl.BlockSpec((D_MODEL, D_MODEL), fix),
            pl.BlockSpec((1, D_MODEL), fix),
            pl.BlockSpec((1, D_MODEL), fix),
        ],
        out_specs=pl.BlockSpec((TAIL_TM, D_MODEL), row),
        scratch_shapes=[
            pltpu.VMEM((2, TOP_K, TAIL_TM * ROW_TILE, LANES), y_rows.dtype),
            pltpu.SemaphoreType.DMA((2,)),
        ],
    )
    return pl.pallas_call(
        functools.partial(_tail_kernel, n_tok=t),
        grid_spec=grid_spec,
        out_shape=jax.ShapeDtypeStruct((t, D_MODEL), _f32),
        compiler_params=pltpu.CompilerParams(
            dimension_semantics=("arbitrary",),
            vmem_limit_bytes=VMEM_LIMIT),
        name="tail",
    )(pos, h1, gates, y_rows, p2, w_pp, g_ple, w_pg, b_pg, g_fin)


def _inv_freq_row():
    inv_freq = ROPE_THETA ** (-jnp.arange(0, ROT_DIM, 2, dtype=_f32) / ROT_DIM)
    comp = jnp.concatenate([inv_freq, inv_freq, jnp.zeros((DIFF_HEAD_DIM - ROT_DIM,), _f32)])
    return jnp.concatenate([comp, comp]).reshape(1, LANES)


def kernel(x, p, positions, norm_mix_g, w_in, conv_w, conv_b, dt_bias_f, dt_bias_b, a_log_f, a_log_b, d_skip, ssd_norm_g, lam_q1, lam_k1, lam_q2, lam_k2, subln_g, w_out, norm_ffn_g, w_route_group, b_route_group, w_route_expert, b_route_expert, w_exp_gate, w_exp_up, w_exp_down, w_ple_proj, ple_norm_g, w_ple_gate, b_ple_gate, final_norm_g):
    b, seq, d = x.shape
    t = b * seq
    x2 = x.reshape(t, d)
    row = lambda v: v.reshape(1, -1).astype(_f32)

    w_in_t = jnp.swapaxes(w_in[0], 0, 1)
    pos_col = positions.astype(_f32).reshape(t, 1)
    pad_dt = lambda v: jnp.pad(v, (0, DT_PAD - 2 * SSD_HEADS)).reshape(1, DT_PAD)
    dt_bias = pad_dt(jnp.concatenate([dt_bias_f[0], dt_bias_b[0]]))
    a_row = pad_dt(jnp.concatenate([-jnp.exp(a_log_f[0]), -jnp.exp(a_log_b[0])]))
    d_skip_row = jnp.repeat(d_skip[0], SSD_HEADDIM).reshape(1, SSD_W)
    lam_vecs = jnp.stack([lam_q1[0], lam_k1[0], lam_q2[0], lam_k2[0]])
    n_route = MOE_GROUPS + N_EXPERTS
    w_route = jnp.pad(jnp.concatenate([w_route_group[0], w_route_expert[0]], axis=1),
                      ((0, 0), (0, LANES - n_route)))
    w_route_hi = w_route.astype(_bf16)
    w_route_lo = (w_route - w_route_hi.astype(_f32)).astype(_bf16)
    w_route2 = jnp.concatenate([w_route_hi, w_route_lo], axis=1)
    b_route = jnp.pad(jnp.concatenate([b_route_group[0], b_route_expert[0]]),
                      (0, LANES - n_route)).reshape(1, LANES)

    proj, dt = _inproj(x2, row(norm_mix_g[0]), pos_col, _inv_freq_row(), w_in_t)
    proj3 = proj.reshape(b, seq, MAIN_W)
    y_ssd = _ssd(proj3, dt.reshape(b, seq, DT_PAD), conv_w[0], row(conv_b[0]),
                 dt_bias, a_row, d_skip_row)
    att = _attn(proj3, lam_vecs, row(subln_g[0]))
    h1, hn_packed, route, cnt, gates = _outproj(
        y_ssd.reshape(t, SSD_W), proj, att.reshape(t, ATTN_W), x2, w_out[0].astype(_bf16),
        row(ssd_norm_g[0]), row(norm_ffn_g[0]), w_route2, b_route)
    pos8, blk8, meta8 = _plan(route, cnt)
    pos = pos8[:TOP_K].reshape(-1)
    block_expert = blk8[0, :_n_rows(t) // ROW_BLOCK]
    n_used = blk8[1, :1]
    x_rows = _dispatch_rows(pos, meta8[0, :N_EXPERTS], meta8[1, :N_EXPERTS], n_used, hn_packed)
    y_rows = _moe(block_expert, n_used, x_rows, w_exp_gate[0], w_exp_up[0], w_exp_down[0])
    out = _tail(pos, h1, gates, y_rows, p[0].reshape(t, PLE_DIM), w_ple_proj[0].astype(_bf16),
                row(ple_norm_g[0]), w_ple_gate[0].astype(_bf16), row(b_ple_gate[0]),
                row(final_norm_g))
    return out.reshape(b, seq, d)
```

```python
import functools

import jax
import jax.numpy as jnp
import numpy as np
from jax import lax
from jax.experimental import pallas as pl
from jax.experimental.pallas import tpu as pltpu

D_MODEL = 2048
PLE_DIM = 256
SSD_W = 1024
ATTN_W = 1024
SSD_HEADDIM = 64
SSD_HEADS = 16
SSD_GROUPS = 2
HEADS_PER_GROUP = SSD_HEADS // SSD_GROUPS
SSD_STATE = 128
CHUNK = 128
CONV_W = 5
XBC_W = SSD_W + 2 * SSD_GROUPS * SSD_STATE
DIFF_HEAD_DIM = 64
DIFF_HEADS = 8
ROT_DIM = 16
ROPE_THETA = 500000.0
MOE_GROUPS = 8
EXPERTS_PER_GROUP = 8
N_EXPERTS = 64
TOP_K = 2
EXPERT_FF = 512
ROW_BLOCK = 128
EPS = 1e-6
LAM_INIT = 0.2

LANES = 128
DT_PAD = LANES
MAIN_W = SSD_W + XBC_W + 3 * ATTN_W
COL_Z = 0
COL_Q = SSD_W
COL_K = COL_Q + ATTN_W
COL_XBC = COL_K + ATTN_W
COL_V = COL_XBC + XBC_W
VMEM_LIMIT = 56 * 1024 * 1024
NEG_BIG = -1e30
Q_SCALE = float(DIFF_HEAD_DIM ** -0.5 * np.log2(np.e))

_f32 = jnp.float32
_bf16 = jnp.bfloat16


def _silu(v):
    return v * (1.0 / (1.0 + jnp.exp(-v)))


def _rms(v, g):
    return v * lax.rsqrt(jnp.mean(v * v, axis=-1, keepdims=True) + EPS) * g


_HI16 = np.uint32(0xFFFF0000)


def _pack_bf16_pair(lo, hi):
    lo_w = lax.bitcast_convert_type(lo.astype(_bf16).astype(_f32), jnp.uint32) >> 16
    hi_w = lax.bitcast_convert_type(hi.astype(_bf16).astype(_f32), jnp.uint32) & _HI16
    return lo_w | hi_w


def _unpack_bf16_pair(w):
    lo = lax.bitcast_convert_type(w << 16, _f32)
    hi = lax.bitcast_convert_type(w & _HI16, _f32)
    return lo, hi


ROW_TILE = 8
assert D_MODEL // 2 == ROW_TILE * LANES


def _store_row_tiles(ref, words, n):
    for c in range(ROW_TILE):
        ref[pl.ds(c, n, stride=ROW_TILE), :] = words[:, c * LANES:(c + 1) * LANES]


def _load_row_tiles(ref, n):
    return jnp.concatenate(
        [ref[pl.ds(c, n, stride=ROW_TILE), :] for c in range(ROW_TILE)], axis=1)


IN_TM = 1024
IN_TN = 512


def _inproj_kernel(x_ref, g_ref, pos_ref, invf_ref, wt_ref, wdt_ref,
                   out_ref, dt_ref, rc_ref, rp_ref, rm_ref, n_scr):
    j = pl.program_id(1)
    nt = (((1,), (1,)), ((), ()))
    half = ROT_DIM // 2

    @pl.when(j == 0)
    def _():
        n = _rms(x_ref[...], g_ref[...])
        n_scr[...] = n.astype(_bf16)
        lane = lax.broadcasted_iota(jnp.int32, (1, LANES), 1)
        dt = lax.dot_general(n_scr[...], wdt_ref[...].astype(_bf16), nt, preferred_element_type=_f32)
        dt_ref[...] = jnp.where(lane < 2 * SSD_HEADS, dt, 0.0)
        ang = pos_ref[...] * invf_ref[...]
        cos, sin = jnp.cos(ang), jnp.sin(ang)
        l64 = lane & (DIFF_HEAD_DIM - 1)
        rc_ref[...] = jnp.where(l64 < ROT_DIM, cos, 1.0)
        rp_ref[...] = jnp.where(jnp.logical_and(l64 >= half, l64 < ROT_DIM), sin, 0.0)
        rm_ref[...] = jnp.where(l64 < half, -sin, 0.0)

    out_ref[...] = lax.dot_general(n_scr[...], wt_ref[...].astype(_bf16), nt,
                                   preferred_element_type=_f32).astype(_bf16)


def _rotate(t, rc, rp, rm):
    half = ROT_DIM // 2
    return t * rc + pltpu.roll(t, half, 1) * rp + pltpu.roll(t, LANES - half, 1) * rm


SRC_XBC = SSD_W
SRC_DT = SRC_XBC + XBC_W
SRC_Q = SRC_DT + 2 * SSD_HEADS
SRC_V = SRC_Q + 2 * ATTN_W
SRC_ALIGN = 32
assert all(s % SRC_ALIGN == 0 for s in (SRC_XBC, SRC_DT, SRC_Q, SRC_V, IN_TN))


def _src_row(j):
    jq, jx, jv = COL_Q // IN_TN, COL_XBC // IN_TN, COL_V // IN_TN
    row = jnp.where(j < jq, j * IN_TN,
                    jnp.where(j < jx, SRC_Q + (j - jq) * IN_TN,
                              jnp.where(j < jv, SRC_XBC + (j - jx) * IN_TN,
                                        SRC_V + (j - jv) * IN_TN)))
    return pl.multiple_of(row, SRC_ALIGN)


def _inproj(x2, g, pos_col, invf_row, w_in_t):
    t = x2.shape[0]
    grid = (t // IN_TM, MAIN_W // IN_TN)
    return pl.pallas_call(
        _inproj_kernel,
        grid=grid,
        in_specs=[
            pl.BlockSpec((IN_TM, D_MODEL), lambda i, j: (i, 0)),
            pl.BlockSpec((1, D_MODEL), lambda i, j: (0, 0)),
            pl.BlockSpec((IN_TM, 1), lambda i, j: (i, 0)),
            pl.BlockSpec((1, LANES), lambda i, j: (0, 0)),
            pl.BlockSpec((pl.Element(IN_TN), pl.Element(D_MODEL)), lambda i, j: (_src_row(j), 0)),
            pl.BlockSpec((pl.Element(DT_PAD), pl.Element(D_MODEL)), lambda i, j: (SRC_DT, 0)),
        ],
        out_specs=[
            pl.BlockSpec((IN_TM, IN_TN), lambda i, j: (i, j)),
            pl.BlockSpec((IN_TM, DT_PAD), lambda i, j: (i, 0)),
        ] + [pl.BlockSpec((IN_TM, LANES), lambda i, j: (i, 0))] * 3,
        out_shape=[
            jax.ShapeDtypeStruct((t, MAIN_W), _bf16),
            jax.ShapeDtypeStruct((t, DT_PAD), _f32),
        ] + [jax.ShapeDtypeStruct((t, LANES), _f32)] * 3,
        scratch_shapes=[pltpu.VMEM((IN_TM, D_MODEL), _bf16)],
        compiler_params=pltpu.CompilerParams(
            dimension_semantics=("arbitrary", "arbitrary"),
            vmem_limit_bytes=VMEM_LIMIT),
        name="inproj",
    )(x2, g, pos_col, invf_row, w_in_t, w_in_t)


CONV_HALO = 16


def _ssd_kernel(xbc_ref, dt_ref, cw_ref, cb_ref, dtb_ref, a_ref, dsk_ref,
                y_ref, xpad, xact, dts, yacc, state_f, state_b, seq):
    n_chunks = seq // CHUNK

    zeros_halo = jnp.zeros((CONV_HALO, XBC_W), _bf16)
    xpad[0:CONV_HALO, :] = zeros_halo
    xpad[CONV_HALO + seq:CONV_HALO + seq + CONV_HALO, :] = zeros_halo
    xpad[CONV_HALO:CONV_HALO + seq, :] = xbc_ref[0]
    cw = cw_ref[...]
    cb = cb_ref[...]
    win = CHUNK + 2 * CONV_HALO

    def conv_body(c, carry):
        r0 = pl.multiple_of(c * CHUNK, CHUNK)
        blk = xpad[pl.ds(r0, win), :].astype(_f32)
        acc = blk * cw[CONV_W // 2:CONV_W // 2 + 1, :]
        for k in range(CONV_W):
            sh = CONV_W // 2 - k
            if sh == 0:
                continue
            acc = acc + pltpu.roll(blk, sh % win, 0) * cw[k:k + 1, :]
        v = acc[CONV_HALO:CONV_HALO + CHUNK, :] + cb
        xact[pl.ds(r0, CHUNK), :] = _silu(v).astype(_bf16)
        return carry

    lax.fori_loop(0, n_chunks, conv_body, 0)

    raw = dt_ref[0] + dtb_ref[...]
    dts[...] = jnp.maximum(raw, 0.0) + jnp.log1p(jnp.exp(-jnp.abs(raw)))

    a_row = a_ref[...]
    dsk = dsk_ref[...]
    row_i = lax.broadcasted_iota(jnp.int32, (CHUNK, CHUNK), 0)
    col_i = lax.broadcasted_iota(jnp.int32, (CHUNK, CHUNK), 1)

    def chunk_step(c, reverse, state):
        r0 = pl.multiple_of(c * CHUNK, CHUNK)
        dtc = dts[pl.ds(r0, CHUNK), :]
        cs = dtc * a_row
        k = 1
        while k < CHUNK:
            if reverse:
                cs = cs + jnp.where(row_i < CHUNK - k, pltpu.roll(cs, CHUNK - k, 0), 0.0)
            else:
                cs = cs + jnp.where(row_i >= k, pltpu.roll(cs, k, 0), 0.0)
            k *= 2
        cs_t = cs.T
        dt_t = dtc.T
        end_col = cs_t[:, 0:1] if reverse else cs_t[:, CHUNK - 1:CHUNK]
        w_t = jnp.exp(end_col - cs_t) * dt_t
        end_row = cs[0:1, :] if reverse else cs[CHUNK - 1:CHUNK, :]
        dec_row = jnp.exp(end_row)
        tri = (row_i <= col_i) if reverse else (row_i >= col_i)
        lane0 = SSD_HEADS if reverse else 0
        first_head = col_i < SSD_HEADDIM
        rows = pl.ds(r0, CHUNK)
        for g in range(SSD_GROUPS):
            b_g = xact[rows, SSD_W + g * SSD_STATE:SSD_W + (g + 1) * SSD_STATE]
            c_g = xact[rows, SSD_W + (SSD_GROUPS + g) * SSD_STATE:SSD_W + (SSD_GROUPS + g + 1) * SSD_STATE]
            cbm = lax.dot_general(c_g, b_g, (((1,), (1,)), ((), ())),
                                  preferred_element_type=_f32)
            b_t = b_g.astype(_f32).T
            for pp in range(HEADS_PER_GROUP // 2):
                pair = g * (HEADS_PER_GROUP // 2) + pp
                cols = slice(pair * LANES, (pair + 1) * LANES)
                xs_p = xact[rows, cols]
                st = state[pair]
                off = jnp.dot(c_g, st.astype(_bf16), preferred_element_type=_f32)
                halves = []
                for hh in range(2):
                    ln = lane0 + 2 * pair + hh
                    colb = jnp.broadcast_to(cs[:, ln:ln + 1], (CHUNK, CHUNK))
                    m_h = (cbm * jnp.exp(jnp.where(tri, colb - cs_t[ln:ln + 1, :], NEG_BIG))
                           * dt_t[ln:ln + 1, :])
                    y_h = (jnp.dot(m_h.astype(_bf16), xs_p, preferred_element_type=_f32)
                           + jnp.exp(colb) * off)
                    upd = jnp.dot((b_t * w_t[ln:ln + 1, :]).astype(_bf16), xs_p,
                                  preferred_element_type=_f32)
                    dec = jnp.broadcast_to(dec_row[:, ln:ln + 1], (CHUNK, CHUNK))
                    halves.append((y_h, upd, dec))
                y_p = jnp.where(first_head, halves[0][0], halves[1][0])
                upd_p = jnp.where(first_head, halves[0][1], halves[1][1])
                dec_p = jnp.where(first_head, halves[0][2], halves[1][2])
                state[pair] = st * dec_p + upd_p
                if reverse:
                    y_ref[0, rows, cols] = (yacc[rows, cols] + y_p).astype(_bf16)
                else:
                    yacc[rows, cols] = y_p + xs_p.astype(_f32) * dsk[:, cols]

    state_f[...] = jnp.zeros_like(state_f)
    state_b[...] = jnp.zeros_like(state_b)

    def fwd_body(c, carry):
        chunk_step(c, False, state_f)
        return carry

    lax.fori_loop(0, n_chunks, fwd_body, 0)

    def bwd_body(i, carry):
        chunk_step(n_chunks - 1 - i, True, state_b)
        return carry

    lax.fori_loop(0, n_chunks, bwd_body, 0)


def _ssd(proj3, dt3, conv_w, conv_b, dt_bias, a_row, d_skip_row):
    b, seq, _ = proj3.shape
    kern = functools.partial(_ssd_kernel, seq=seq)
    assert COL_XBC % XBC_W == 0
    return pl.pallas_call(
        kern,
        grid=(b,),
        in_specs=[
            pl.BlockSpec((1, seq, XBC_W), lambda i: (i, 0, COL_XBC // XBC_W)),
            pl.BlockSpec((1, seq, DT_PAD), lambda i: (i, 0, 0)),
            pl.BlockSpec((CONV_W, XBC_W), lambda i: (0, 0)),
            pl.BlockSpec((1, XBC_W), lambda i: (0, 0)),
            pl.BlockSpec((1, DT_PAD), lambda i: (0, 0)),
            pl.BlockSpec((1, DT_PAD), lambda i: (0, 0)),
            pl.BlockSpec((1, SSD_W), lambda i: (0, 0)),
        ],
        out_specs=pl.BlockSpec((1, seq, SSD_W), lambda i: (i, 0, 0)),
        out_shape=jax.ShapeDtypeStruct((b, seq, SSD_W), _bf16),
        scratch_shapes=[
            pltpu.VMEM((seq + 2 * CONV_HALO, XBC_W), _bf16),
            pltpu.VMEM((seq, XBC_W), _bf16),
            pltpu.VMEM((seq, DT_PAD), _f32),
            pltpu.VMEM((seq, SSD_W), _f32),
            pltpu.VMEM((SSD_HEADS // 2, SSD_STATE, 2 * SSD_HEADDIM), _f32),
            pltpu.VMEM((SSD_HEADS // 2, SSD_STATE, 2 * SSD_HEADDIM), _f32),
        ],
        compiler_params=pltpu.CompilerParams(
            dimension_semantics=("arbitrary",),
            vmem_limit_bytes=VMEM_LIMIT),
        name="ssd",
    )(proj3, dt3, conv_w, conv_b, dt_bias, a_row, d_skip_row)


ATT_QB = 256
ATT_UNROLL = 8


def _attn_kernel(q_ref, k_ref, v_ref, rc_ref, rp_ref, rm_ref, lam_ref, g_ref, o_ref,
                 q_scr, k_scr, v1_scr, *, seq):
    hw = 2 * DIFF_HEAD_DIM
    lv = lam_ref[...]
    lam = (jnp.exp(jnp.sum(lv[0:1] * lv[1:2], axis=-1, keepdims=True))
           - jnp.exp(jnp.sum(lv[2:3] * lv[3:4], axis=-1, keepdims=True)) + LAM_INIT)

    def rot_body(c, carry):
        rows = pl.ds(pl.multiple_of(c * ATT_QB, ATT_QB), ATT_QB)
        rc, rp, rm = rc_ref[0, rows, :], rp_ref[0, rows, :], rm_ref[0, rows, :]
        q_scr[rows, :] = (_rotate(q_ref[0, rows, :].astype(_f32), rc, rp, rm) * Q_SCALE).astype(_bf16)
        k_scr[rows, :] = _rotate(k_ref[0, rows, :].astype(_f32), rc, rp, rm).astype(_bf16)
        return carry

    lax.fori_loop(0, seq // ATT_QB, rot_body, 0)
    k = k_scr[...]
    g = g_ref[...]
    v1_scr[:, 0:hw] = v_ref[0]
    v1_scr[:, hw:2 * hw] = (lax.broadcasted_iota(jnp.int32, (seq, hw), 1) == 0).astype(_bf16)
    first = lax.broadcasted_iota(jnp.int32, (1, hw), 1) < DIFF_HEAD_DIM
    nt = (((1,), (1,)), ((), ()))

    def chain(r0):
        q = q_scr[pl.ds(r0, ATT_QB), :]
        zero = jnp.zeros_like(q)
        ovs = []
        for qc in (jnp.where(first, q, zero), jnp.where(first, zero, q)):
            s = lax.dot_general(qc, k, nt, preferred_element_type=_f32)
            e = jnp.exp2(s - jnp.max(s, axis=-1, keepdims=True)).astype(_bf16)
            ovs.append(jnp.dot(e, v1_scr[...], preferred_element_type=_f32))
        o1, l1 = ovs[0][:, 0:hw], ovs[0][:, hw:hw + 1]
        o2, l2 = ovs[1][:, 0:hw], ovs[1][:, hw:hw + 1]
        o = o1 / l1 - (lam / l2) * o2
        o = _rms(o, g) * (1.0 - LAM_INIT)
        o_ref[0, pl.ds(r0, ATT_QB), :] = o.astype(_bf16)

    step = ATT_QB * ATT_UNROLL

    def body(i, carry):
        for u in range(ATT_UNROLL):
            chain(pl.multiple_of(i * step + u * ATT_QB, ATT_QB))
        return carry

    lax.fori_loop(0, seq // step, body, 0)


def _attn(proj3, rot_tables, lam_vecs, subln_g):
    b, seq, _ = proj3.shape
    hw = 2 * DIFF_HEAD_DIM
    kern = functools.partial(_attn_kernel, seq=seq)
    return pl.pallas_call(
        kern,
        grid=(b, DIFF_HEADS),
        in_specs=[
            pl.BlockSpec((1, seq, hw), lambda i, h: (i, 0, COL_Q // hw + h)),
            pl.BlockSpec((1, seq, hw), lambda i, h: (i, 0, COL_K // hw + h)),
            pl.BlockSpec((1, seq, hw), lambda i, h: (i, 0, COL_V // hw + h)),
        ] + [pl.BlockSpec((1, seq, LANES), lambda i, h: (i, 0, 0))] * 3 + [
            pl.BlockSpec((4, DIFF_HEAD_DIM), lambda i, h: (0, 0)),
            pl.BlockSpec((1, hw), lambda i, h: (0, 0)),
        ],
        out_specs=pl.BlockSpec((1, seq, hw), lambda i, h: (i, 0, h)),
        out_shape=jax.ShapeDtypeStruct((b, seq, ATTN_W), _bf16),
        scratch_shapes=[pltpu.VMEM((seq, hw), _bf16), pltpu.VMEM((seq, hw), _bf16),
                        pltpu.VMEM((seq, 2 * hw), _bf16)],
        compiler_params=pltpu.CompilerParams(
            dimension_semantics=("arbitrary", "arbitrary"),
            vmem_limit_bytes=VMEM_LIMIT),
        name="diffattn",
    )(proj3, proj3, proj3, *rot_tables, lam_vecs, subln_g)


OUT_TM = 256
ROUTE_ROWS = 8


def _outproj_kernel(y_ref, z_ref, att_ref, x_ref, w_ref, gs_ref, gf_ref, wr_ref, br_ref,
                    h_ref, hn_ref, route_ref, cnt_ref, gate_ref, cnt_scr):
    @pl.when(pl.program_id(0) == 0)
    def _():
        cnt_scr[...] = jnp.zeros_like(cnt_scr)

    y = y_ref[...].astype(_f32)
    z = z_ref[...].astype(_f32)
    s = _rms(y * _silu(z), gs_ref[...]).astype(_bf16)
    acc = jnp.dot(s, w_ref[0:SSD_W, :], preferred_element_type=_f32)
    acc = acc + jnp.dot(att_ref[...], w_ref[SSD_W:SSD_W + ATTN_W, :], preferred_element_type=_f32)
    h1 = x_ref[...] + acc
    h_ref[...] = h1
    hn = _rms(h1, gf_ref[...])
    half = D_MODEL // 2
    _store_row_tiles(hn_ref, _pack_bf16_pair(hn[:, 0:half], hn[:, half:D_MODEL]), hn.shape[0])
    hn_hi = hn.astype(_bf16)
    hn_lo = (hn - hn_hi.astype(_f32)).astype(_bf16)
    a = jnp.dot(hn_hi, wr_ref[...], preferred_element_type=_f32)
    bb = jnp.dot(hn_lo, wr_ref[:, 0:LANES], preferred_element_type=_f32)
    logits = (a[:, 0:LANES] + a[:, LANES:2 * LANES] + bb + br_ref[...]).T
    tm = logits.shape[1]
    iota = lax.broadcasted_iota(jnp.int32, (MOE_GROUPS, tm), 0)

    def first_argmax(val, vmax):
        return jnp.min(jnp.where(val == vmax, iota, MOE_GROUPS), axis=0, keepdims=True)

    gl = logits[0:MOE_GROUPS]
    gmax = jnp.max(gl, axis=0, keepdims=True)
    p_g = 1.0 / jnp.sum(jnp.exp(gl - gmax), axis=0, keepdims=True)
    g_sel = first_argmax(gl, gmax)
    el = jnp.zeros((EXPERTS_PER_GROUP, tm), _f32)
    for gi in range(MOE_GROUPS):
        lo = MOE_GROUPS + gi * EXPERTS_PER_GROUP
        el = jnp.where(g_sel == gi, logits[lo:lo + EXPERTS_PER_GROUP], el)
    ee = jnp.exp(el - jnp.max(el, axis=0, keepdims=True))
    pe = ee / jnp.sum(ee, axis=0, keepdims=True)
    p1 = jnp.max(pe, axis=0, keepdims=True)
    i1 = first_argmax(pe, p1)
    pe2 = jnp.where(iota == i1, -1.0, pe)
    p2 = jnp.max(pe2, axis=0, keepdims=True)
    i2 = first_argmax(pe2, p2)
    den = p1 + p2
    e1 = g_sel * EXPERTS_PER_GROUP + i1
    e2 = g_sel * EXPERTS_PER_GROUP + i2
    iota_e = lax.broadcasted_iota(jnp.int32, (N_EXPERTS, tm), 0)
    oh1 = (iota_e == e1).astype(_f32)
    oh2 = (iota_e == e2).astype(_f32)
    both = oh1 + oh2
    earlier = (lax.broadcasted_iota(jnp.int32, (tm, tm), 0)
               < lax.broadcasted_iota(jnp.int32, (tm, tm), 1)).astype(_bf16)
    before = cnt_scr[:, 0:1] + jnp.dot(both.astype(_bf16), earlier,
                                       preferred_element_type=_f32)
    r1 = jnp.sum(oh1 * before, axis=0, keepdims=True).astype(jnp.int32)
    r2 = jnp.sum(oh2 * before, axis=0, keepdims=True).astype(jnp.int32)
    cnt_scr[...] = cnt_scr[...] + jnp.sum(both, axis=1, keepdims=True)
    cnt_ref[...] = cnt_scr[...].astype(jnp.int32)
    route_ref[...] = jnp.where(iota == 0, e1, jnp.where(iota == 1, e2,
                               jnp.where(iota == 2, r1, jnp.where(iota == 3, r2, 0))))
    g8 = jnp.where(iota == 0, p_g * (p1 / den), jnp.where(iota == 1, p_g * (p2 / den), 0.0))
    gate_ref[...] = jnp.concatenate(
        [g8, jnp.zeros((LANES - ROUTE_ROWS, tm), _f32)], axis=0).T


def _outproj(y2, proj2, att2, x2, w_out, g_ssd, g_ffn, wr_t, br_col):
    t = x2.shape[0]
    row = lambda i: (i, 0)
    fix = lambda i: (0, 0)
    return pl.pallas_call(
        _outproj_kernel,
        grid=(t // OUT_TM,),
        in_specs=[
            pl.BlockSpec((OUT_TM, SSD_W), row),
            pl.BlockSpec((OUT_TM, SSD_W), row),
            pl.BlockSpec((OUT_TM, ATTN_W), row),
            pl.BlockSpec((OUT_TM, D_MODEL), row),
            pl.BlockSpec((SSD_W + ATTN_W, D_MODEL), fix),
            pl.BlockSpec((1, SSD_W), fix),
            pl.BlockSpec((1, D_MODEL), fix),
            pl.BlockSpec((D_MODEL, 2 * LANES), fix),
            pl.BlockSpec((1, LANES), fix),
        ],
        out_specs=[
            pl.BlockSpec((OUT_TM, D_MODEL), row),
            pl.BlockSpec((OUT_TM * ROW_TILE, LANES), row),
            pl.BlockSpec((ROUTE_ROWS, OUT_TM), lambda i: (0, i)),
            pl.BlockSpec((N_EXPERTS, LANES), fix),
            pl.BlockSpec((OUT_TM, LANES), row),
        ],
        out_shape=[
            jax.ShapeDtypeStruct((t, D_MODEL), _f32),
            jax.ShapeDtypeStruct((t * ROW_TILE, LANES), jnp.uint32),
            jax.ShapeDtypeStruct((ROUTE_ROWS, t), jnp.int32),
            jax.ShapeDtypeStruct((N_EXPERTS, LANES), jnp.int32),
            jax.ShapeDtypeStruct((t, LANES), _f32),
        ],
        scratch_shapes=[pltpu.VMEM((N_EXPERTS, LANES), _f32)],
        compiler_params=pltpu.CompilerParams(
            dimension_semantics=("arbitrary",),
            vmem_limit_bytes=VMEM_LIMIT),
        name="outproj_router",
    )(y2, proj2, att2, x2, w_out, g_ssd, g_ffn, wr_t, br_col)


def _n_rows(n_tok):
    n_assign = n_tok * TOP_K
    return (n_assign + N_EXPERTS * (ROW_BLOCK - 1) + ROW_BLOCK - 1) // ROW_BLOCK * ROW_BLOCK


def _prefix_sum(v, axis):
    n = v.shape[axis]
    idx = lax.broadcasted_iota(jnp.int32, v.shape, axis)
    k = 1
    while k < n:
        v = v + jnp.where(idx >= k, pltpu.roll(v, k, axis), 0)
        k *= 2
    return v


def _plan_kernel(route_ref, cnt_ref, pos_ref, blk_ref, meta_ref, *, n_blocks_pad):
    n_tok = route_ref.shape[1]
    cnt_col = jnp.concatenate(
        [cnt_ref[...], jnp.zeros((LANES - N_EXPERTS, LANES), jnp.int32)], axis=0)
    pad_up = lambda c: (c + (ROW_BLOCK - 1)) & (-ROW_BLOCK)
    ends_col = _prefix_sum(pad_up(cnt_col), 0)
    starts_col = (ends_col - pad_up(cnt_col)).astype(_f32)
    cnt_row = cnt_col.astype(_f32).T.astype(jnp.int32)
    ends_row = _prefix_sum(pad_up(cnt_row), 1)
    starts_row = ends_row - pad_up(cnt_row)
    n_used = ends_row[0:1, LANES - 1:LANES] >> (ROW_BLOCK.bit_length() - 1)

    ch = 1024
    iota_e = lax.broadcasted_iota(jnp.int32, (LANES, ch), 0)
    for c0 in range(0, n_tok, ch):
        rt = route_ref[:, c0:c0 + ch]
        s1 = jnp.sum(jnp.where(iota_e == rt[0:1], starts_col[:, 0:1], 0.0), axis=0, keepdims=True)
        s2 = jnp.sum(jnp.where(iota_e == rt[1:2], starts_col[:, 0:1], 0.0), axis=0, keepdims=True)
        p1 = s1.astype(jnp.int32) + rt[2:3]
        p2 = s2.astype(jnp.int32) + rt[3:4]
        sub = lax.broadcasted_iota(jnp.int32, (ROUTE_ROWS, ch), 0)
        pos_ref[:, c0:c0 + ch] = jnp.where(sub == 0, p1, jnp.where(sub == 1, p2, 0))

    blk_start = lax.broadcasted_iota(jnp.int32, (LANES, n_blocks_pad), 1) * ROW_BLOCK
    e_idx = lax.broadcasted_iota(jnp.int32, (LANES, n_blocks_pad), 0)
    real = e_idx < N_EXPERTS
    be = jnp.sum(jnp.where(jnp.logical_and(real, ends_col[:, 0:1] <= blk_start), 1.0, 0.0),
                 axis=0, keepdims=True).astype(jnp.int32)
    be = jnp.minimum(be, N_EXPERTS - 1)
    last = jnp.max(jnp.where(jnp.logical_and(real, cnt_col[:, 0:1] > 0), e_idx.astype(_f32), 0.0),
                   axis=0, keepdims=True).astype(jnp.int32)
    blk_i = lax.broadcasted_iota(jnp.int32, (1, n_blocks_pad), 1)
    be = jnp.where(blk_i < n_used, be, last)
    sub = lax.broadcasted_iota(jnp.int32, (ROUTE_ROWS, n_blocks_pad), 0)
    blk_ref[...] = jnp.where(sub == 0, be, jnp.where(sub == 1, n_used, 0))
    sub = lax.broadcasted_iota(jnp.int32, (ROUTE_ROWS, LANES), 0)
    meta_ref[...] = jnp.where(sub == 0, cnt_row[0:1], jnp.where(sub == 1, starts_row[0:1], 0))


def _plan(route, cnt):
    n_tok = route.shape[1]
    n_blocks_pad = -(-(_n_rows(n_tok) // ROW_BLOCK) // LANES) * LANES
    return pl.pallas_call(
        functools.partial(_plan_kernel, n_blocks_pad=n_blocks_pad),
        out_shape=[
            jax.ShapeDtypeStruct((ROUTE_ROWS, n_tok), jnp.int32),
            jax.ShapeDtypeStruct((ROUTE_ROWS, n_blocks_pad), jnp.int32),
            jax.ShapeDtypeStruct((ROUTE_ROWS, LANES), jnp.int32),
        ],
        compiler_params=pltpu.CompilerParams(vmem_limit_bytes=VMEM_LIMIT),
        name="route_plan",
    )(route, cnt)


DISP_TM = 256


def _dispatch_kernel(pos_ref, cnt_ref, start_ref, nused_ref, h_ref, x_hbm, stage, zrow, sem, zsem,
                     *, n_tok, n_blocks):
    i = pl.program_id(0)
    n_steps = pl.num_programs(0)
    slot = i % 2
    base = i * DISP_TM
    tile = lambda r: pl.ds(pl.multiple_of(r * ROW_TILE, ROW_TILE), ROW_TILE)

    def wait_slot(sl):
        for _ in range(TOP_K):
            pltpu.make_async_copy(stage.at[sl], x_hbm.at[pl.ds(0, DISP_TM * ROW_TILE), :],
                                  sem.at[sl]).wait()

    @pl.when(i >= 2)
    def _():
        wait_slot(slot)

    stage[slot] = h_ref[...]

    def body(r, carry):
        for kk in range(TOP_K):
            row = pos_ref[kk * n_tok + base + r]
            pltpu.make_async_copy(stage.at[slot, tile(r), :], x_hbm.at[tile(row), :],
                                  sem.at[slot]).start(priority=kk % 2)
        return carry

    lax.fori_loop(0, DISP_TM, body, 0, unroll=8)

    @pl.when(i == n_steps - 1)
    def _():
        zrow[...] = jnp.zeros_like(zrow)

        pieces = [1 << b for b in reversed(range(ROW_BLOCK.bit_length() - 1))]

        def pad_copy(size, row0):
            src = zrow.at[pl.ds(0, size * ROW_TILE), :]
            dst = x_hbm.at[pl.ds(pl.multiple_of(row0 * ROW_TILE, ROW_TILE), size * ROW_TILE), :]
            return pltpu.make_async_copy(src, dst, zsem)

        def per_expert(e, carry, wait):
            cnt = cnt_ref[e]
            n_pad = ((cnt + (ROW_BLOCK - 1)) & (-ROW_BLOCK)) - cnt
            row0 = start_ref[e] + cnt
            for size in pieces:
                has = (n_pad & size) != 0

                @pl.when(has)
                def _():
                    cp = pad_copy(size, row0)
                    cp.wait() if wait else cp.start()

                row0 = row0 + jnp.where(has, size, 0)
            return carry

        lax.fori_loop(0, N_EXPERTS, functools.partial(per_expert, wait=False), 0)
        blk_rows = ROW_BLOCK * ROW_TILE

        def fill_block(blk, carry):
            dst = pl.ds(pl.multiple_of(blk * blk_rows, blk_rows), blk_rows)
            pltpu.make_async_copy(zrow, x_hbm.at[dst, :], zsem).start()
            return carry

        lax.fori_loop(nused_ref[0], n_blocks, fill_block, 0)

        lax.fori_loop(0, N_EXPERTS, functools.partial(per_expert, wait=True), 0)

        def wait_block(blk, carry):
            pltpu.make_async_copy(zrow, x_hbm.at[pl.ds(0, blk_rows), :], zsem).wait()
            return carry

        lax.fori_loop(nused_ref[0], n_blocks, wait_block, 0)

        @pl.when(n_steps >= 2)
        def _():
            wait_slot(1 - slot)

        wait_slot(slot)


def _dispatch_rows(pos, counts, starts, n_used, hn_packed):
    t = hn_packed.shape[0] // ROW_TILE
    n_rows = _n_rows(t)
    blk = (DISP_TM * ROW_TILE, LANES)
    grid_spec = pltpu.PrefetchScalarGridSpec(
        num_scalar_prefetch=4,
        grid=(t // DISP_TM,),
        in_specs=[pl.BlockSpec(blk, lambda i, *_: (i, 0))],
        out_specs=pl.BlockSpec(memory_space=pl.ANY),
        scratch_shapes=[
            pltpu.VMEM((2,) + blk, hn_packed.dtype),
            pltpu.VMEM((ROW_BLOCK * ROW_TILE, LANES), hn_packed.dtype),
            pltpu.SemaphoreType.DMA((2,)),
            pltpu.SemaphoreType.DMA(()),
        ],
    )
    return pl.pallas_call(
        functools.partial(_dispatch_kernel, n_tok=t, n_blocks=n_rows // ROW_BLOCK),
        grid_spec=grid_spec,
        out_shape=jax.ShapeDtypeStruct((n_rows * ROW_TILE, LANES), hn_packed.dtype),
        compiler_params=pltpu.CompilerParams(
            dimension_semantics=("arbitrary",),
            vmem_limit_bytes=VMEM_LIMIT),
        name="dispatch_rows",
    )(pos, counts, starts, n_used, hn_packed)


MOE_W_SPLIT = 4
MOE_W_BUFS = 3


def _moe_kernel(be_ref, nused_ref, x_ref, wg_hbm, wu_hbm, wd_hbm,
                out_ref, wg_f, wu_f, wd_f, wg_s, wu_s, wd_s, wsem, nexp, *, n_blocks):
    i = pl.program_id(0)
    nused = nused_ref[0]
    half = D_MODEL // 2

    def weight_copies(e, sl):
        cps = []
        for src, dst in ((wg_hbm, wg_f), (wu_hbm, wu_f), (wd_hbm, wd_f)):
            rows = src.shape[1] // MOE_W_SPLIT
            for c in range(MOE_W_SPLIT):
                rs = pl.ds(c * rows, rows)
                cps.append(pltpu.make_async_copy(src.at[e, rs, :], dst.at[sl, rs, :], wsem.at[sl]))
        return cps

    def start_all(cps):
        for n, cp in enumerate(cps):
            cp.start(priority=n % 2)

    def block_expert(b):
        return be_ref[jnp.minimum(b, n_blocks - 1)]

    def next_change(b0):
        e0 = block_expert(b0)
        return lax.while_loop(
            lambda b: jnp.logical_and(b < nused, block_expert(b) == e0), lambda b: b + 1, b0 + 1)

    @pl.when(i == 0)
    def _():
        nexp[0] = 0
        start_all(weight_copies(be_ref[0], 0))
        nxt = next_change(0)

        @pl.when(nxt < nused)
        def _():
            start_all(weight_copies(block_expert(nxt), 1))

    @pl.when(i < nused)
    def _():
        e = be_ref[i]
        new_expert = jnp.logical_or(i == 0, e != be_ref[jnp.maximum(i - 1, 0)])

        @pl.when(new_expert)
        def _():
            sl = nexp[0] % MOE_W_BUFS
            ahead = next_change(jnp.minimum(next_change(i), nused - 1))

            @pl.when(jnp.logical_and(next_change(i) < nused, ahead < nused))
            def _():
                start_all(weight_copies(block_expert(ahead), (nexp[0] + 2) % MOE_W_BUFS))

            for cp in weight_copies(e, sl):
                cp.wait()
            wg_s[...] = wg_f[sl].astype(_bf16)
            wu_s[...] = wu_f[sl].astype(_bf16)
            wd_s[...] = wd_f[sl].astype(_bf16)
            nexp[0] = nexp[0] + 1

        x_lo, x_hi = _unpack_bf16_pair(_load_row_tiles(x_ref, ROW_BLOCK))
        x_lo, x_hi = x_lo.astype(_bf16), x_hi.astype(_bf16)
        gg = (jnp.dot(x_lo, wg_s[0:half, :], preferred_element_type=_f32)
              + jnp.dot(x_hi, wg_s[half:D_MODEL, :], preferred_element_type=_f32))
        uu = (jnp.dot(x_lo, wu_s[0:half, :], preferred_element_type=_f32)
              + jnp.dot(x_hi, wu_s[half:D_MODEL, :], preferred_element_type=_f32))
        a = (_silu(gg) * uu).astype(_bf16)
        y = jnp.dot(a, wd_s[...], preferred_element_type=_f32)
        _store_row_tiles(out_ref, _pack_bf16_pair(y[:, 0:half], y[:, half:D_MODEL]), ROW_BLOCK)

    @pl.when(i >= nused)
    def _():
        out_ref[...] = jnp.zeros_like(out_ref)


def _moe(block_expert, n_used, x_rows, w_gate, w_up, w_down):
    n_blocks = x_rows.shape[0] // (ROW_BLOCK * ROW_TILE)
    blk = (ROW_BLOCK * ROW_TILE, LANES)
    omap = lambda i, be, nu: (i, 0)
    grid_spec = pltpu.PrefetchScalarGridSpec(
        num_scalar_prefetch=2,
        grid=(n_blocks,),
        in_specs=[
            pl.BlockSpec(blk, omap),
            pl.BlockSpec(memory_space=pl.ANY),
            pl.BlockSpec(memory_space=pl.ANY),
            pl.BlockSpec(memory_space=pl.ANY),
        ],
        out_specs=pl.BlockSpec(blk, omap),
        scratch_shapes=[
            pltpu.VMEM((MOE_W_BUFS, D_MODEL, EXPERT_FF), _f32),
            pltpu.VMEM((MOE_W_BUFS, D_MODEL, EXPERT_FF), _f32),
            pltpu.VMEM((MOE_W_BUFS, EXPERT_FF, D_MODEL), _f32),
            pltpu.VMEM((D_MODEL, EXPERT_FF), _bf16),
            pltpu.VMEM((D_MODEL, EXPERT_FF), _bf16),
            pltpu.VMEM((EXPERT_FF, D_MODEL), _bf16),
            pltpu.SemaphoreType.DMA((MOE_W_BUFS,)),
            pltpu.SMEM((1,), jnp.int32),
        ],
    )
    return pl.pallas_call(
        functools.partial(_moe_kernel, n_blocks=n_blocks),
        grid_spec=grid_spec,
        out_shape=jax.ShapeDtypeStruct(x_rows.shape, x_rows.dtype),
        compiler_params=pltpu.CompilerParams(
            dimension_semantics=("arbitrary",),
            vmem_limit_bytes=VMEM_LIMIT),
        name="moe_experts",
    )(block_expert, n_used, x_rows, w_gate, w_up, w_down)


TAIL_TM = 256


def _tail_kernel(pos_ref, h_ref, gate_ref, y_hbm, p_ref, wpp_ref, gp_ref, wpg_ref, bpg_ref,
                 gfin_ref, out_ref, ybuf, sem, *, n_tok):
    i = pl.program_id(0)
    n_steps = pl.num_programs(0)
    slot = i % 2
    tile = lambda r: pl.ds(pl.multiple_of(r * ROW_TILE, ROW_TILE), ROW_TILE)

    def start_row(base, r, sl):
        for kk in range(TOP_K):
            row = pos_ref[kk * n_tok + base + r]
            pltpu.make_async_copy(y_hbm.at[tile(row), :], ybuf.at[sl, kk, tile(r), :],
                                  sem.at[sl]).start(priority=kk % 2)

    def wait_slot(sl):
        for kk in range(TOP_K):
            pltpu.make_async_copy(y_hbm.at[pl.ds(0, TAIL_TM * ROW_TILE), :], ybuf.at[sl, kk],
                                  sem.at[sl]).wait()

    @pl.when(i == 0)
    def _():
        def body(r, carry):
            start_row(0, r, 0)
            return carry

        lax.fori_loop(0, TAIL_TM, body, 0, unroll=8)

    wait_slot(slot)

    nxt_base = jnp.minimum(i + 1, n_steps - 1) * TAIL_TM
    for r in range(TAIL_TM):
        start_row(nxt_base, r, 1 - slot)

    gates = gate_ref[...]
    g1, g2 = gates[:, 0:1], gates[:, 1:2]
    y1_lo, y1_hi = _unpack_bf16_pair(_load_row_tiles(ybuf.at[slot, 0], TAIL_TM))
    y2_lo, y2_hi = _unpack_bf16_pair(_load_row_tiles(ybuf.at[slot, 1], TAIL_TM))
    moe = jnp.concatenate([g1 * y1_lo + g2 * y2_lo, g1 * y1_hi + g2 * y2_hi], axis=1)
    h2 = h_ref[...] + moe
    ple = _rms(jnp.dot(p_ref[...].astype(_bf16), wpp_ref[...], preferred_element_type=_f32),
               gp_ref[...])
    lg = jnp.dot(h2.astype(_bf16), wpg_ref[...], preferred_element_type=_f32) + bpg_ref[...]
    h3 = h2 + (1.0 / (1.0 + jnp.exp(-lg))) * ple
    out_ref[...] = _rms(h3, gfin_ref[...])

    @pl.when(i == n_steps - 1)
    def _():
        wait_slot(1 - slot)


def _tail(pos, h1, gates, y_rows, p2, w_pp, g_ple, w_pg, b_pg, g_fin):
    t = h1.shape[0]
    row = lambda i, ps: (i, 0)
    fix = lambda i, ps: (0, 0)
    grid_spec = pltpu.PrefetchScalarGridSpec(
        num_scalar_prefetch=1,
        grid=(t // TAIL_TM,),
        in_specs=[
            pl.BlockSpec((TAIL_TM, D_MODEL), row),
            pl.BlockSpec((TAIL_TM, LANES), row),
            pl.BlockSpec(memory_space=pl.ANY),
            pl.BlockSpec((TAIL_TM, PLE_DIM), row),
            pl.BlockSpec((PLE_DIM, D_MODEL), fix),
            pl.BlockSpec((1, D_MODEL), fix),
            pl.BlockSpec((D_MODEL, D_MODEL), fix),
            pl.BlockSpec((1, D_MODEL), fix),
            pl.BlockSpec((1, D_MODEL), fix),
        ],
        out_specs=pl.BlockSpec((TAIL_TM, D_MODEL), row),
        scratch_shapes=[
            pltpu.VMEM((2, TOP_K, TAIL_TM * ROW_TILE, LANES), y_rows.dtype),
            pltpu.SemaphoreType.DMA((2,)),
        ],
    )
    return pl.pallas_call(
        functools.partial(_tail_kernel, n_tok=t),
        grid_spec=grid_spec,
        out_shape=jax.ShapeDtypeStruct((t, D_MODEL), _f32),
        compiler_params=pltpu.CompilerParams(
            dimension_semantics=("arbitrary",),
            vmem_limit_bytes=VMEM_LIMIT),
        name="tail",
    )(pos, h1, gates, y_rows, p2, w_pp, g_ple, w_pg, b_pg, g_fin)


def _inv_freq_row():
    inv_freq = ROPE_THETA ** (-jnp.arange(0, ROT_DIM, 2, dtype=_f32) / ROT_DIM)
    comp = jnp.concatenate([inv_freq, inv_freq, jnp.zeros((DIFF_HEAD_DIM - ROT_DIM,), _f32)])
    return jnp.concatenate([comp, comp]).reshape(1, LANES)


def kernel(x, p, positions, norm_mix_g, w_in, conv_w, conv_b, dt_bias_f, dt_bias_b, a_log_f, a_log_b, d_skip, ssd_norm_g, lam_q1, lam_k1, lam_q2, lam_k2, subln_g, w_out, norm_ffn_g, w_route_group, b_route_group, w_route_expert, b_route_expert, w_exp_gate, w_exp_up, w_exp_down, w_ple_proj, ple_norm_g, w_ple_gate, b_ple_gate, final_norm_g):
    b, seq, d = x.shape
    t = b * seq
    x2 = x.reshape(t, d)
    row = lambda v: v.reshape(1, -1).astype(_f32)

    w_in_t = jnp.swapaxes(w_in[0], 0, 1)
    pos_col = positions.astype(_f32).reshape(t, 1)
    pad_dt = lambda v: jnp.pad(v, (0, DT_PAD - 2 * SSD_HEADS)).reshape(1, DT_PAD)
    dt_bias = pad_dt(jnp.concatenate([dt_bias_f[0], dt_bias_b[0]]))
    a_row = pad_dt(jnp.concatenate([-jnp.exp(a_log_f[0]), -jnp.exp(a_log_b[0])]))
    d_skip_row = jnp.repeat(d_skip[0], SSD_HEADDIM).reshape(1, SSD_W)
    lam_vecs = jnp.stack([lam_q1[0], lam_k1[0], lam_q2[0], lam_k2[0]])
    n_route = MOE_GROUPS + N_EXPERTS
    w_route = jnp.pad(jnp.concatenate([w_route_group[0], w_route_expert[0]], axis=1),
                      ((0, 0), (0, LANES - n_route)))
    w_route_hi = w_route.astype(_bf16)
    w_route_lo = (w_route - w_route_hi.astype(_f32)).astype(_bf16)
    w_route2 = jnp.concatenate([w_route_hi, w_route_lo], axis=1)
    b_route = jnp.pad(jnp.concatenate([b_route_group[0], b_route_expert[0]]),
                      (0, LANES - n_route)).reshape(1, LANES)

    proj, dt, *rot = _inproj(x2, row(norm_mix_g[0]), pos_col, _inv_freq_row(), w_in_t)
    proj3 = proj.reshape(b, seq, MAIN_W)
    y_ssd = _ssd(proj3, dt.reshape(b, seq, DT_PAD), conv_w[0], row(conv_b[0]),
                 dt_bias, a_row, d_skip_row)
    att = _attn(proj3, [r.reshape(b, seq, LANES) for r in rot], lam_vecs, row(subln_g[0]))
    h1, hn_packed, route, cnt, gates = _outproj(
        y_ssd.reshape(t, SSD_W), proj, att.reshape(t, ATTN_W), x2, w_out[0].astype(_bf16),
        row(ssd_norm_g[0]), row(norm_ffn_g[0]), w_route2, b_route)
    pos8, blk8, meta8 = _plan(route, cnt)
    pos = pos8[:TOP_K].reshape(-1)
    block_expert = blk8[0, :_n_rows(t) // ROW_BLOCK]
    n_used = blk8[1, :1]
    x_rows = _dispatch_rows(pos, meta8[0, :N_EXPERTS], meta8[1, :N_EXPERTS], n_used, hn_packed)
    y_rows = _moe(block_expert, n_used, x_rows, w_exp_gate[0], w_exp_up[0], w_exp_down[0])
    out = _tail(pos, h1, gates, y_rows, p[0].reshape(t, PLE_DIM), w_ple_proj[0].astype(_bf16),
                row(ple_norm_g[0]), w_ple_gate[0].astype(_bf16), row(b_ple_gate[0]),
                row(final_norm_g))
    return out.reshape(b, seq, d)
```

```python
import functools

import jax
import jax.numpy as jnp
import numpy as np
from jax import lax
from jax.experimental import pallas as pl
from jax.experimental.pallas import tpu as pltpu

D_MODEL = 2048
PLE_DIM = 256
SSD_W = 1024
ATTN_W = 1024
SSD_HEADDIM = 64
SSD_HEADS = 16
SSD_GROUPS = 2
HEADS_PER_GROUP = SSD_HEADS // SSD_GROUPS
SSD_STATE = 128
CHUNK = 128
CONV_W = 5
XBC_W = SSD_W + 2 * SSD_GROUPS * SSD_STATE
DIFF_HEAD_DIM = 64
DIFF_HEADS = 8
ROT_DIM = 16
ROPE_THETA = 500000.0
MOE_GROUPS = 8
EXPERTS_PER_GROUP = 8
N_EXPERTS = 64
TOP_K = 2
EXPERT_FF = 512
ROW_BLOCK = 128
EPS = 1e-6
LAM_INIT = 0.2

LANES = 128
DT_PAD = LANES
MAIN_W = SSD_W + XBC_W + 3 * ATTN_W
COL_Z = 0
COL_Q = SSD_W
COL_K = COL_Q + ATTN_W
COL_XBC = COL_K + ATTN_W
COL_V = COL_XBC + XBC_W
VMEM_LIMIT = 56 * 1024 * 1024
NEG_BIG = -1e30
Q_SCALE = float(DIFF_HEAD_DIM ** -0.5 * np.log2(np.e))

_f32 = jnp.float32
_bf16 = jnp.bfloat16


def _silu(v):
    return v * (1.0 / (1.0 + jnp.exp(-v)))


def _rms(v, g):
    return v * lax.rsqrt(jnp.mean(v * v, axis=-1, keepdims=True) + EPS) * g


_HI16 = np.uint32(0xFFFF0000)


def _pack_bf16_pair(lo, hi):
    lo_w = lax.bitcast_convert_type(lo.astype(_bf16).astype(_f32), jnp.uint32) >> 16
    hi_w = lax.bitcast_convert_type(hi.astype(_bf16).astype(_f32), jnp.uint32) & _HI16
    return lo_w | hi_w


def _unpack_bf16_pair(w):
    lo = lax.bitcast_convert_type(w << 16, _f32)
    hi = lax.bitcast_convert_type(w & _HI16, _f32)
    return lo, hi


ROW_TILE = 8
assert D_MODEL // 2 == ROW_TILE * LANES


def _store_row_tiles(ref, words, n):
    for c in range(ROW_TILE):
        ref[pl.ds(c, n, stride=ROW_TILE), :] = words[:, c * LANES:(c + 1) * LANES]


def _load_row_tiles(ref, n, row0=0):
    return jnp.concatenate(
        [ref[pl.ds(row0 * ROW_TILE + c, n, stride=ROW_TILE), :] for c in range(ROW_TILE)], axis=1)


IN_TM = 1024
IN_TN = 512


def _inproj_kernel(x_ref, g_ref, pos_ref, invf_ref, wt_ref, wdt_ref,
                   out_ref, dt_ref, rc_ref, rp_ref, rm_ref, n_scr):
    j = pl.program_id(1)
    nt = (((1,), (1,)), ((), ()))
    half = ROT_DIM // 2

    @pl.when(j == 0)
    def _():
        n = _rms(x_ref[...], g_ref[...])
        n_scr[...] = n.astype(_bf16)
        lane = lax.broadcasted_iota(jnp.int32, (1, LANES), 1)
        dt = lax.dot_general(n_scr[...], wdt_ref[...].astype(_bf16), nt, preferred_element_type=_f32)
        dt_ref[...] = jnp.where(lane < 2 * SSD_HEADS, dt, 0.0)
        ang = pos_ref[...] * invf_ref[...]
        cos, sin = jnp.cos(ang), jnp.sin(ang)
        l64 = lane & (DIFF_HEAD_DIM - 1)
        rc_ref[...] = jnp.where(l64 < ROT_DIM, cos, 1.0)
        rp_ref[...] = jnp.where(jnp.logical_and(l64 >= half, l64 < ROT_DIM), sin, 0.0)
        rm_ref[...] = jnp.where(l64 < half, -sin, 0.0)

    out_ref[...] = lax.dot_general(n_scr[...], wt_ref[...].astype(_bf16), nt,
                                   preferred_element_type=_f32).astype(_bf16)


def _rotate(t, rc, rp, rm):
    half = ROT_DIM // 2
    return t * rc + pltpu.roll(t, half, 1) * rp + pltpu.roll(t, LANES - half, 1) * rm


SRC_XBC = SSD_W
SRC_DT = SRC_XBC + XBC_W
SRC_Q = SRC_DT + 2 * SSD_HEADS
SRC_V = SRC_Q + 2 * ATTN_W
SRC_ALIGN = 32
assert all(s % SRC_ALIGN == 0 for s in (SRC_XBC, SRC_DT, SRC_Q, SRC_V, IN_TN))


def _src_row(j):
    jq, jx, jv = COL_Q // IN_TN, COL_XBC // IN_TN, COL_V // IN_TN
    row = jnp.where(j < jq, j * IN_TN,
                    jnp.where(j < jx, SRC_Q + (j - jq) * IN_TN,
                              jnp.where(j < jv, SRC_XBC + (j - jx) * IN_TN,
                                        SRC_V + (j - jv) * IN_TN)))
    return pl.multiple_of(row, SRC_ALIGN)


def _inproj(x2, g, pos_col, invf_row, w_in_t):
    t = x2.shape[0]
    grid = (t // IN_TM, MAIN_W // IN_TN)
    return pl.pallas_call(
        _inproj_kernel,
        grid=grid,
        in_specs=[
            pl.BlockSpec((IN_TM, D_MODEL), lambda i, j: (i, 0)),
            pl.BlockSpec((1, D_MODEL), lambda i, j: (0, 0)),
            pl.BlockSpec((IN_TM, 1), lambda i, j: (i, 0)),
            pl.BlockSpec((1, LANES), lambda i, j: (0, 0)),
            pl.BlockSpec((pl.Element(IN_TN), pl.Element(D_MODEL)), lambda i, j: (_src_row(j), 0)),
            pl.BlockSpec((pl.Element(DT_PAD), pl.Element(D_MODEL)), lambda i, j: (SRC_DT, 0)),
        ],
        out_specs=[
            pl.BlockSpec((IN_TM, IN_TN), lambda i, j: (i, j)),
            pl.BlockSpec((IN_TM, DT_PAD), lambda i, j: (i, 0)),
        ] + [pl.BlockSpec((IN_TM, LANES), lambda i, j: (i, 0))] * 3,
        out_shape=[
            jax.ShapeDtypeStruct((t, MAIN_W), _bf16),
            jax.ShapeDtypeStruct((t, DT_PAD), _f32),
        ] + [jax.ShapeDtypeStruct((t, LANES), _f32)] * 3,
        scratch_shapes=[pltpu.VMEM((IN_TM, D_MODEL), _bf16)],
        compiler_params=pltpu.CompilerParams(
            dimension_semantics=("arbitrary", "arbitrary"),
            vmem_limit_bytes=VMEM_LIMIT),
        name="inproj",
    )(x2, g, pos_col, invf_row, w_in_t, w_in_t)


CONV_HALO = 16


def _ssd_kernel(xbc_ref, dt_ref, cw_ref, cb_ref, dtb_ref, a_ref, dsk_ref,
                y_ref, xpad, xact, dts, yacc, state_f, state_b, seq):
    n_chunks = seq // CHUNK

    zeros_halo = jnp.zeros((CONV_HALO, XBC_W), _bf16)
    xpad[0:CONV_HALO, :] = zeros_halo
    xpad[CONV_HALO + seq:CONV_HALO + seq + CONV_HALO, :] = zeros_halo
    xpad[CONV_HALO:CONV_HALO + seq, :] = xbc_ref[0]
    cw = cw_ref[...]
    cb = cb_ref[...]
    win = CHUNK + 2 * CONV_HALO

    def conv_body(c, carry):
        r0 = pl.multiple_of(c * CHUNK, CHUNK)
        blk = xpad[pl.ds(r0, win), :].astype(_f32)
        acc = blk * cw[CONV_W // 2:CONV_W // 2 + 1, :]
        for k in range(CONV_W):
            sh = CONV_W // 2 - k
            if sh == 0:
                continue
            acc = acc + pltpu.roll(blk, sh % win, 0) * cw[k:k + 1, :]
        v = acc[CONV_HALO:CONV_HALO + CHUNK, :] + cb
        xact[pl.ds(r0, CHUNK), :] = _silu(v).astype(_bf16)
        return carry

    lax.fori_loop(0, n_chunks, conv_body, 0)

    raw = dt_ref[0] + dtb_ref[...]
    dts[...] = jnp.maximum(raw, 0.0) + jnp.log1p(jnp.exp(-jnp.abs(raw)))

    a_row = a_ref[...]
    dsk = dsk_ref[...]
    row_i = lax.broadcasted_iota(jnp.int32, (CHUNK, CHUNK), 0)
    col_i = lax.broadcasted_iota(jnp.int32, (CHUNK, CHUNK), 1)

    def chunk_step(c, reverse, state):
        r0 = pl.multiple_of(c * CHUNK, CHUNK)
        dtc = dts[pl.ds(r0, CHUNK), :]
        cs = dtc * a_row
        k = 1
        while k < CHUNK:
            if reverse:
                cs = cs + jnp.where(row_i < CHUNK - k, pltpu.roll(cs, CHUNK - k, 0), 0.0)
            else:
                cs = cs + jnp.where(row_i >= k, pltpu.roll(cs, k, 0), 0.0)
            k *= 2
        cs_t = cs.T
        dt_t = dtc.T
        end_col = cs_t[:, 0:1] if reverse else cs_t[:, CHUNK - 1:CHUNK]
        w_t = jnp.exp(end_col - cs_t) * dt_t
        end_row = cs[0:1, :] if reverse else cs[CHUNK - 1:CHUNK, :]
        dec_row = jnp.exp(end_row)
        tri = (row_i <= col_i) if reverse else (row_i >= col_i)
        lane0 = SSD_HEADS if reverse else 0
        first_head = col_i < SSD_HEADDIM
        rows = pl.ds(r0, CHUNK)
        for g in range(SSD_GROUPS):
            b_g = xact[rows, SSD_W + g * SSD_STATE:SSD_W + (g + 1) * SSD_STATE]
            c_g = xact[rows, SSD_W + (SSD_GROUPS + g) * SSD_STATE:SSD_W + (SSD_GROUPS + g + 1) * SSD_STATE]
            cbm = lax.dot_general(c_g, b_g, (((1,), (1,)), ((), ())),
                                  preferred_element_type=_f32)
            b_t = b_g.astype(_f32).T
            for pp in range(HEADS_PER_GROUP // 2):
                pair = g * (HEADS_PER_GROUP // 2) + pp
                cols = slice(pair * LANES, (pair + 1) * LANES)
                xs_p = xact[rows, cols]
                st = state[pair]
                off = jnp.dot(c_g, st.astype(_bf16), preferred_element_type=_f32)
                halves = []
                for hh in range(2):
                    ln = lane0 + 2 * pair + hh
                    colb = jnp.broadcast_to(cs[:, ln:ln + 1], (CHUNK, CHUNK))
                    m_h = (cbm * jnp.exp(jnp.where(tri, colb - cs_t[ln:ln + 1, :], NEG_BIG))
                           * dt_t[ln:ln + 1, :])
                    y_h = (jnp.dot(m_h.astype(_bf16), xs_p, preferred_element_type=_f32)
                           + jnp.exp(colb) * off)
                    upd = jnp.dot((b_t * w_t[ln:ln + 1, :]).astype(_bf16), xs_p,
                                  preferred_element_type=_f32)
                    dec = jnp.broadcast_to(dec_row[:, ln:ln + 1], (CHUNK, CHUNK))
                    halves.append((y_h, upd, dec))
                y_p = jnp.where(first_head, halves[0][0], halves[1][0])
                upd_p = jnp.where(first_head, halves[0][1], halves[1][1])
                dec_p = jnp.where(first_head, halves[0][2], halves[1][2])
                state[pair] = st * dec_p + upd_p
                if reverse:
                    y_ref[0, rows, cols] = (yacc[rows, cols] + y_p).astype(_bf16)
                else:
                    yacc[rows, cols] = y_p + xs_p.astype(_f32) * dsk[:, cols]

    state_f[...] = jnp.zeros_like(state_f)
    state_b[...] = jnp.zeros_like(state_b)

    def fwd_body(c, carry):
        chunk_step(c, False, state_f)
        return carry

    lax.fori_loop(0, n_chunks, fwd_body, 0)

    def bwd_body(i, carry):
        chunk_step(n_chunks - 1 - i, True, state_b)
        return carry

    lax.fori_loop(0, n_chunks, bwd_body, 0)


def _ssd(proj3, dt3, conv_w, conv_b, dt_bias, a_row, d_skip_row):
    b, seq, _ = proj3.shape
    kern = functools.partial(_ssd_kernel, seq=seq)
    assert COL_XBC % XBC_W == 0
    return pl.pallas_call(
        kern,
        grid=(b,),
        in_specs=[
            pl.BlockSpec((1, seq, XBC_W), lambda i: (i, 0, COL_XBC // XBC_W)),
            pl.BlockSpec((1, seq, DT_PAD), lambda i: (i, 0, 0)),
            pl.BlockSpec((CONV_W, XBC_W), lambda i: (0, 0)),
            pl.BlockSpec((1, XBC_W), lambda i: (0, 0)),
            pl.BlockSpec((1, DT_PAD), lambda i: (0, 0)),
            pl.BlockSpec((1, DT_PAD), lambda i: (0, 0)),
            pl.BlockSpec((1, SSD_W), lambda i: (0, 0)),
        ],
        out_specs=pl.BlockSpec((1, seq, SSD_W), lambda i: (i, 0, 0)),
        out_shape=jax.ShapeDtypeStruct((b, seq, SSD_W), _bf16),
        scratch_shapes=[
            pltpu.VMEM((seq + 2 * CONV_HALO, XBC_W), _bf16),
            pltpu.VMEM((seq, XBC_W), _bf16),
            pltpu.VMEM((seq, DT_PAD), _f32),
            pltpu.VMEM((seq, SSD_W), _f32),
            pltpu.VMEM((SSD_HEADS // 2, SSD_STATE, 2 * SSD_HEADDIM), _f32),
            pltpu.VMEM((SSD_HEADS // 2, SSD_STATE, 2 * SSD_HEADDIM), _f32),
        ],
        compiler_params=pltpu.CompilerParams(
            dimension_semantics=("arbitrary",),
            vmem_limit_bytes=VMEM_LIMIT),
        name="ssd",
    )(proj3, dt3, conv_w, conv_b, dt_bias, a_row, d_skip_row)


ATT_QB = 256
ATT_UNROLL = 8


def _attn_kernel(q_ref, k_ref, v_ref, rc_ref, rp_ref, rm_ref, lam_ref, g_ref, o_ref,
                 k_scr, v1_scr, *, seq):
    hw = 2 * DIFF_HEAD_DIM
    lv = lam_ref[...]
    lam = (jnp.exp(jnp.sum(lv[0:1] * lv[1:2], axis=-1, keepdims=True))
           - jnp.exp(jnp.sum(lv[2:3] * lv[3:4], axis=-1, keepdims=True)) + LAM_INIT)

    def rotated(ref, rows):
        return _rotate(ref[0, rows, :].astype(_f32), rc_ref[0, rows, :], rp_ref[0, rows, :],
                       rm_ref[0, rows, :])

    def rot_body(c, carry):
        rows = pl.ds(pl.multiple_of(c * ATT_QB, ATT_QB), ATT_QB)
        k_scr[rows, :] = rotated(k_ref, rows).astype(_bf16)
        return carry

    lax.fori_loop(0, seq // ATT_QB, rot_body, 0)
    k = k_scr[...]
    g = g_ref[...]
    v1_scr[:, 0:hw] = v_ref[0]
    v1_scr[:, hw:2 * hw] = (lax.broadcasted_iota(jnp.int32, (seq, hw), 1) == 0).astype(_bf16)
    first = lax.broadcasted_iota(jnp.int32, (1, hw), 1) < DIFF_HEAD_DIM
    nt = (((1,), (1,)), ((), ()))

    def chain(r0):
        q = (rotated(q_ref, pl.ds(r0, ATT_QB)) * Q_SCALE).astype(_bf16)
        zero = jnp.zeros_like(q)
        ovs = []
        for qc in (jnp.where(first, q, zero), jnp.where(first, zero, q)):
            s = lax.dot_general(qc, k, nt, preferred_element_type=_f32)
            e = jnp.exp2(s - jnp.max(s, axis=-1, keepdims=True)).astype(_bf16)
            ovs.append(jnp.dot(e, v1_scr[...], preferred_element_type=_f32))
        o1, l1 = ovs[0][:, 0:hw], ovs[0][:, hw:hw + 1]
        o2, l2 = ovs[1][:, 0:hw], ovs[1][:, hw:hw + 1]
        o = o1 / l1 - (lam / l2) * o2
        o = _rms(o, g) * (1.0 - LAM_INIT)
        o_ref[0, pl.ds(r0, ATT_QB), :] = o.astype(_bf16)

    step = ATT_QB * ATT_UNROLL

    def body(i, carry):
        for u in range(ATT_UNROLL):
            chain(pl.multiple_of(i * step + u * ATT_QB, ATT_QB))
        return carry

    lax.fori_loop(0, seq // step, body, 0)


def _attn(proj3, rot_tables, lam_vecs, subln_g):
    b, seq, _ = proj3.shape
    hw = 2 * DIFF_HEAD_DIM
    kern = functools.partial(_attn_kernel, seq=seq)
    return pl.pallas_call(
        kern,
        grid=(b, DIFF_HEADS),
        in_specs=[
            pl.BlockSpec((1, seq, hw), lambda i, h: (i, 0, COL_Q // hw + h)),
            pl.BlockSpec((1, seq, hw), lambda i, h: (i, 0, COL_K // hw + h)),
            pl.BlockSpec((1, seq, hw), lambda i, h: (i, 0, COL_V // hw + h)),
        ] + [pl.BlockSpec((1, seq, LANES), lambda i, h: (i, 0, 0))] * 3 + [
            pl.BlockSpec((4, DIFF_HEAD_DIM), lambda i, h: (0, 0)),
            pl.BlockSpec((1, hw), lambda i, h: (0, 0)),
        ],
        out_specs=pl.BlockSpec((1, seq, hw), lambda i, h: (i, 0, h)),
        out_shape=jax.ShapeDtypeStruct((b, seq, ATTN_W), _bf16),
        scratch_shapes=[pltpu.VMEM((seq, hw), _bf16), pltpu.VMEM((seq, 2 * hw), _bf16)],
        compiler_params=pltpu.CompilerParams(
            dimension_semantics=("arbitrary", "arbitrary"),
            vmem_limit_bytes=VMEM_LIMIT),
        name="diffattn",
    )(proj3, proj3, proj3, *rot_tables, lam_vecs, subln_g)


OUT_TM = 256
ROUTE_ROWS = 8


def _outproj_kernel(y_ref, z_ref, att_ref, x_ref, w_ref, gs_ref, gf_ref, wr_ref, br_ref,
                    h_ref, hn_ref, route_ref, cnt_ref, gate_ref, cnt_scr):
    @pl.when(pl.program_id(0) == 0)
    def _():
        cnt_scr[...] = jnp.zeros_like(cnt_scr)

    y = y_ref[...].astype(_f32)
    z = z_ref[...].astype(_f32)
    s = _rms(y * _silu(z), gs_ref[...]).astype(_bf16)
    acc = jnp.dot(s, w_ref[0:SSD_W, :], preferred_element_type=_f32)
    acc = acc + jnp.dot(att_ref[...], w_ref[SSD_W:SSD_W + ATTN_W, :], preferred_element_type=_f32)
    h1 = x_ref[...] + acc
    h_ref[...] = h1
    hn = _rms(h1, gf_ref[...])
    half = D_MODEL // 2
    _store_row_tiles(hn_ref, _pack_bf16_pair(hn[:, 0:half], hn[:, half:D_MODEL]), hn.shape[0])
    hn_hi = hn.astype(_bf16)
    hn_lo = (hn - hn_hi.astype(_f32)).astype(_bf16)
    a = jnp.dot(hn_hi, wr_ref[...], preferred_element_type=_f32)
    bb = jnp.dot(hn_lo, wr_ref[:, 0:LANES], preferred_element_type=_f32)
    logits = (a[:, 0:LANES] + a[:, LANES:2 * LANES] + bb + br_ref[...]).T
    tm = logits.shape[1]
    iota = lax.broadcasted_iota(jnp.int32, (MOE_GROUPS, tm), 0)

    def first_argmax(val, vmax):
        return jnp.min(jnp.where(val == vmax, iota, MOE_GROUPS), axis=0, keepdims=True)

    gl = logits[0:MOE_GROUPS]
    gmax = jnp.max(gl, axis=0, keepdims=True)
    p_g = 1.0 / jnp.sum(jnp.exp(gl - gmax), axis=0, keepdims=True)
    g_sel = first_argmax(gl, gmax)
    el = jnp.zeros((EXPERTS_PER_GROUP, tm), _f32)
    for gi in range(MOE_GROUPS):
        lo = MOE_GROUPS + gi * EXPERTS_PER_GROUP
        el = jnp.where(g_sel == gi, logits[lo:lo + EXPERTS_PER_GROUP], el)
    ee = jnp.exp(el - jnp.max(el, axis=0, keepdims=True))
    pe = ee / jnp.sum(ee, axis=0, keepdims=True)
    p1 = jnp.max(pe, axis=0, keepdims=True)
    i1 = first_argmax(pe, p1)
    pe2 = jnp.where(iota == i1, -1.0, pe)
    p2 = jnp.max(pe2, axis=0, keepdims=True)
    i2 = first_argmax(pe2, p2)
    den = p1 + p2
    e1 = g_sel * EXPERTS_PER_GROUP + i1
    e2 = g_sel * EXPERTS_PER_GROUP + i2
    iota_e = lax.broadcasted_iota(jnp.int32, (N_EXPERTS, tm), 0)
    oh1 = (iota_e == e1).astype(_f32)
    oh2 = (iota_e == e2).astype(_f32)
    both = oh1 + oh2
    earlier = (lax.broadcasted_iota(jnp.int32, (tm, tm), 0)
               < lax.broadcasted_iota(jnp.int32, (tm, tm), 1)).astype(_bf16)
    before = cnt_scr[:, 0:1] + jnp.dot(both.astype(_bf16), earlier,
                                       preferred_element_type=_f32)
    r1 = jnp.sum(oh1 * before, axis=0, keepdims=True).astype(jnp.int32)
    r2 = jnp.sum(oh2 * before, axis=0, keepdims=True).astype(jnp.int32)
    cnt_scr[...] = cnt_scr[...] + jnp.sum(both, axis=1, keepdims=True)
    cnt_ref[...] = cnt_scr[...].astype(jnp.int32)
    route_ref[...] = jnp.where(iota == 0, e1, jnp.where(iota == 1, e2,
                               jnp.where(iota == 2, r1, jnp.where(iota == 3, r2, 0))))
    g8 = jnp.where(iota == 0, p_g * (p1 / den), jnp.where(iota == 1, p_g * (p2 / den), 0.0))
    gate_ref[...] = jnp.concatenate(
        [g8, jnp.zeros((LANES - ROUTE_ROWS, tm), _f32)], axis=0).T


def _outproj(y2, proj2, att2, x2, w_out, g_ssd, g_ffn, wr_t, br_col):
    t = x2.shape[0]
    row = lambda i: (i, 0)
    fix = lambda i: (0, 0)
    return pl.pallas_call(
        _outproj_kernel,
        grid=(t // OUT_TM,),
        in_specs=[
            pl.BlockSpec((OUT_TM, SSD_W), row),
            pl.BlockSpec((OUT_TM, SSD_W), row),
            pl.BlockSpec((OUT_TM, ATTN_W), row),
            pl.BlockSpec((OUT_TM, D_MODEL), row),
            pl.BlockSpec((SSD_W + ATTN_W, D_MODEL), fix),
            pl.BlockSpec((1, SSD_W), fix),
            pl.BlockSpec((1, D_MODEL), fix),
            pl.BlockSpec((D_MODEL, 2 * LANES), fix),
            pl.BlockSpec((1, LANES), fix),
        ],
        out_specs=[
            pl.BlockSpec((OUT_TM, D_MODEL), row),
            pl.BlockSpec((OUT_TM * ROW_TILE, LANES), row),
            pl.BlockSpec((ROUTE_ROWS, OUT_TM), lambda i: (0, i)),
            pl.BlockSpec((N_EXPERTS, LANES), fix),
            pl.BlockSpec((OUT_TM, LANES), row),
        ],
        out_shape=[
            jax.ShapeDtypeStruct((t, D_MODEL), _f32),
            jax.ShapeDtypeStruct((t * ROW_TILE, LANES), jnp.uint32),
            jax.ShapeDtypeStruct((ROUTE_ROWS, t), jnp.int32),
            jax.ShapeDtypeStruct((N_EXPERTS, LANES), jnp.int32),
            jax.ShapeDtypeStruct((t, LANES), _f32),
        ],
        scratch_shapes=[pltpu.VMEM((N_EXPERTS, LANES), _f32)],
        compiler_params=pltpu.CompilerParams(
            dimension_semantics=("arbitrary",),
            vmem_limit_bytes=VMEM_LIMIT),
        name="outproj_router",
    )(y2, proj2, att2, x2, w_out, g_ssd, g_ffn, wr_t, br_col)


def _n_rows(n_tok):
    n_assign = n_tok * TOP_K
    return (n_assign + N_EXPERTS * (ROW_BLOCK - 1) + ROW_BLOCK - 1) // ROW_BLOCK * ROW_BLOCK


def _prefix_sum(v, axis):
    n = v.shape[axis]
    idx = lax.broadcasted_iota(jnp.int32, v.shape, axis)
    k = 1
    while k < n:
        v = v + jnp.where(idx >= k, pltpu.roll(v, k, axis), 0)
        k *= 2
    return v


def _plan_kernel(route_ref, cnt_ref, pos_ref, blk_ref, meta_ref, *, n_blocks_pad):
    n_tok = route_ref.shape[1]
    cnt_col = jnp.concatenate(
        [cnt_ref[...], jnp.zeros((LANES - N_EXPERTS, LANES), jnp.int32)], axis=0)
    pad_up = lambda c: (c + (ROW_BLOCK - 1)) & (-ROW_BLOCK)
    ends_col = _prefix_sum(pad_up(cnt_col), 0)
    starts_col = (ends_col - pad_up(cnt_col)).astype(_f32)
    cnt_row = cnt_col.astype(_f32).T.astype(jnp.int32)
    ends_row = _prefix_sum(pad_up(cnt_row), 1)
    starts_row = ends_row - pad_up(cnt_row)
    n_used = ends_row[0:1, LANES - 1:LANES] >> (ROW_BLOCK.bit_length() - 1)

    ch = 1024
    iota_e = lax.broadcasted_iota(jnp.int32, (LANES, ch), 0)
    for c0 in range(0, n_tok, ch):
        rt = route_ref[:, c0:c0 + ch]
        s1 = jnp.sum(jnp.where(iota_e == rt[0:1], starts_col[:, 0:1], 0.0), axis=0, keepdims=True)
        s2 = jnp.sum(jnp.where(iota_e == rt[1:2], starts_col[:, 0:1], 0.0), axis=0, keepdims=True)
        p1 = s1.astype(jnp.int32) + rt[2:3]
        p2 = s2.astype(jnp.int32) + rt[3:4]
        sub = lax.broadcasted_iota(jnp.int32, (ROUTE_ROWS, ch), 0)
        pos_ref[:, c0:c0 + ch] = jnp.where(sub == 0, p1, jnp.where(sub == 1, p2, 0))

    blk_start = lax.broadcasted_iota(jnp.int32, (LANES, n_blocks_pad), 1) * ROW_BLOCK
    e_idx = lax.broadcasted_iota(jnp.int32, (LANES, n_blocks_pad), 0)
    real = e_idx < N_EXPERTS
    be = jnp.sum(jnp.where(jnp.logical_and(real, ends_col[:, 0:1] <= blk_start), 1.0, 0.0),
                 axis=0, keepdims=True).astype(jnp.int32)
    be = jnp.minimum(be, N_EXPERTS - 1)
    last = jnp.max(jnp.where(jnp.logical_and(real, cnt_col[:, 0:1] > 0), e_idx.astype(_f32), 0.0),
                   axis=0, keepdims=True).astype(jnp.int32)
    blk_i = lax.broadcasted_iota(jnp.int32, (1, n_blocks_pad), 1)
    be = jnp.where(blk_i < n_used, be, last)
    sub = lax.broadcasted_iota(jnp.int32, (ROUTE_ROWS, n_blocks_pad), 0)
    blk_ref[...] = jnp.where(sub == 0, be, jnp.where(sub == 1, n_used, 0))
    sub = lax.broadcasted_iota(jnp.int32, (ROUTE_ROWS, LANES), 0)
    meta_ref[...] = jnp.where(sub == 0, cnt_row[0:1], jnp.where(sub == 1, starts_row[0:1], 0))


def _plan(route, cnt):
    n_tok = route.shape[1]
    n_blocks_pad = -(-(_n_rows(n_tok) // ROW_BLOCK) // LANES) * LANES
    return pl.pallas_call(
        functools.partial(_plan_kernel, n_blocks_pad=n_blocks_pad),
        out_shape=[
            jax.ShapeDtypeStruct((ROUTE_ROWS, n_tok), jnp.int32),
            jax.ShapeDtypeStruct((ROUTE_ROWS, n_blocks_pad), jnp.int32),
            jax.ShapeDtypeStruct((ROUTE_ROWS, LANES), jnp.int32),
        ],
        compiler_params=pltpu.CompilerParams(vmem_limit_bytes=VMEM_LIMIT),
        name="route_plan",
    )(route, cnt)


DISP_TM = 256


def _dispatch_kernel(pos_ref, cnt_ref, start_ref, nused_ref, h_ref, x_hbm, stage, zrow, sem, zsem,
                     *, n_tok, n_blocks):
    i = pl.program_id(0)
    n_steps = pl.num_programs(0)
    slot = i % 2
    base = i * DISP_TM
    tile = lambda r: pl.ds(pl.multiple_of(r * ROW_TILE, ROW_TILE), ROW_TILE)

    def wait_slot(sl):
        for _ in range(TOP_K):
            pltpu.make_async_copy(stage.at[sl], x_hbm.at[pl.ds(0, DISP_TM * ROW_TILE), :],
                                  sem.at[sl]).wait()

    @pl.when(i >= 2)
    def _():
        wait_slot(slot)

    stage[slot] = h_ref[...]

    def body(r, carry):
        for kk in range(TOP_K):
            row = pos_ref[kk * n_tok + base + r]
            pltpu.make_async_copy(stage.at[slot, tile(r), :], x_hbm.at[tile(row), :],
                                  sem.at[slot]).start(priority=kk % 2)
        return carry

    lax.fori_loop(0, DISP_TM, body, 0, unroll=8)

    @pl.when(i == n_steps - 1)
    def _():
        zrow[...] = jnp.zeros_like(zrow)

        pieces = [1 << b for b in reversed(range(ROW_BLOCK.bit_length() - 1))]

        def pad_copy(size, row0):
            src = zrow.at[pl.ds(0, size * ROW_TILE), :]
            dst = x_hbm.at[pl.ds(pl.multiple_of(row0 * ROW_TILE, ROW_TILE), size * ROW_TILE), :]
            return pltpu.make_async_copy(src, dst, zsem)

        def per_expert(e, carry, wait):
            cnt = cnt_ref[e]
            n_pad = ((cnt + (ROW_BLOCK - 1)) & (-ROW_BLOCK)) - cnt
            row0 = start_ref[e] + cnt
            for size in pieces:
                has = (n_pad & size) != 0

                @pl.when(has)
                def _():
                    cp = pad_copy(size, row0)
                    cp.wait() if wait else cp.start()

                row0 = row0 + jnp.where(has, size, 0)
            return carry

        lax.fori_loop(0, N_EXPERTS, functools.partial(per_expert, wait=False), 0)
        blk_rows = ROW_BLOCK * ROW_TILE

        def fill_block(blk, carry):
            dst = pl.ds(pl.multiple_of(blk * blk_rows, blk_rows), blk_rows)
            pltpu.make_async_copy(zrow, x_hbm.at[dst, :], zsem).start()
            return carry

        lax.fori_loop(nused_ref[0], n_blocks, fill_block, 0)

        lax.fori_loop(0, N_EXPERTS, functools.partial(per_expert, wait=True), 0)

        def wait_block(blk, carry):
            pltpu.make_async_copy(zrow, x_hbm.at[pl.ds(0, blk_rows), :], zsem).wait()
            return carry

        lax.fori_loop(nused_ref[0], n_blocks, wait_block, 0)

        @pl.when(n_steps >= 2)
        def _():
            wait_slot(1 - slot)

        wait_slot(slot)


def _dispatch_rows(pos, counts, starts, n_used, hn_packed):
    t = hn_packed.shape[0] // ROW_TILE
    n_rows = _n_rows(t)
    blk = (DISP_TM * ROW_TILE, LANES)
    grid_spec = pltpu.PrefetchScalarGridSpec(
        num_scalar_prefetch=4,
        grid=(t // DISP_TM,),
        in_specs=[pl.BlockSpec(blk, lambda i, *_: (i, 0))],
        out_specs=pl.BlockSpec(memory_space=pl.ANY),
        scratch_shapes=[
            pltpu.VMEM((2,) + blk, hn_packed.dtype),
            pltpu.VMEM((ROW_BLOCK * ROW_TILE, LANES), hn_packed.dtype),
            pltpu.SemaphoreType.DMA((2,)),
            pltpu.SemaphoreType.DMA(()),
        ],
    )
    return pl.pallas_call(
        functools.partial(_dispatch_kernel, n_tok=t, n_blocks=n_rows // ROW_BLOCK),
        grid_spec=grid_spec,
        out_shape=jax.ShapeDtypeStruct((n_rows * ROW_TILE, LANES), hn_packed.dtype),
        compiler_params=pltpu.CompilerParams(
            dimension_semantics=("arbitrary",),
            vmem_limit_bytes=VMEM_LIMIT),
        name="dispatch_rows",
    )(pos, counts, starts, n_used, hn_packed)


MOE_W_SPLIT = 4
MOE_W_BUFS = 3


def _moe_kernel(be_ref, nused_ref, x_ref, wg_hbm, wu_hbm, wd_hbm,
                out_ref, wg_f, wu_f, wd_f, wg_s, wu_s, wd_s, wsem, nexp, *, n_blocks):
    i = pl.program_id(0)
    nused = nused_ref[0]
    half = D_MODEL // 2

    def weight_copies(e, sl):
        cps = []
        for src, dst in ((wg_hbm, wg_f), (wu_hbm, wu_f), (wd_hbm, wd_f)):
            rows = src.shape[1] // MOE_W_SPLIT
            for c in range(MOE_W_SPLIT):
                rs = pl.ds(c * rows, rows)
                cps.append(pltpu.make_async_copy(src.at[e, rs, :], dst.at[sl, rs, :], wsem.at[sl]))
        return cps

    def start_all(cps):
        for n, cp in enumerate(cps):
            cp.start(priority=n % 2)

    def block_expert(b):
        return be_ref[jnp.minimum(b, n_blocks - 1)]

    def next_change(b0):
        e0 = block_expert(b0)
        return lax.while_loop(
            lambda b: jnp.logical_and(b < nused, block_expert(b) == e0), lambda b: b + 1, b0 + 1)

    @pl.when(i == 0)
    def _():
        nexp[0] = 0
        start_all(weight_copies(be_ref[0], 0))
        nxt = next_change(0)

        @pl.when(nxt < nused)
        def _():
            start_all(weight_copies(block_expert(nxt), 1))

    @pl.when(i < nused)
    def _():
        e = be_ref[i]
        new_expert = jnp.logical_or(i == 0, e != be_ref[jnp.maximum(i - 1, 0)])

        @pl.when(new_expert)
        def _():
            sl = nexp[0] % MOE_W_BUFS
            ahead = next_change(jnp.minimum(next_change(i), nused - 1))

            @pl.when(jnp.logical_and(next_change(i) < nused, ahead < nused))
            def _():
                start_all(weight_copies(block_expert(ahead), (nexp[0] + 2) % MOE_W_BUFS))

            for cp in weight_copies(e, sl):
                cp.wait()
            wg_s[...] = wg_f[sl].astype(_bf16)
            wu_s[...] = wu_f[sl].astype(_bf16)
            wd_s[...] = wd_f[sl].astype(_bf16)
            nexp[0] = nexp[0] + 1

        x_lo, x_hi = _unpack_bf16_pair(_load_row_tiles(x_ref, ROW_BLOCK))
        x_lo, x_hi = x_lo.astype(_bf16), x_hi.astype(_bf16)
        gg = (jnp.dot(x_lo, wg_s[0:half, :], preferred_element_type=_f32)
              + jnp.dot(x_hi, wg_s[half:D_MODEL, :], preferred_element_type=_f32))
        uu = (jnp.dot(x_lo, wu_s[0:half, :], preferred_element_type=_f32)
              + jnp.dot(x_hi, wu_s[half:D_MODEL, :], preferred_element_type=_f32))
        a = (_silu(gg) * uu).astype(_bf16)
        y = jnp.dot(a, wd_s[...], preferred_element_type=_f32)
        _store_row_tiles(out_ref, _pack_bf16_pair(y[:, 0:half], y[:, half:D_MODEL]), ROW_BLOCK)

    @pl.when(i >= nused)
    def _():
        out_ref[...] = jnp.zeros_like(out_ref)


def _moe(block_expert, n_used, x_rows, w_gate, w_up, w_down):
    n_blocks = x_rows.shape[0] // (ROW_BLOCK * ROW_TILE)
    blk = (ROW_BLOCK * ROW_TILE, LANES)
    omap = lambda i, be, nu: (i, 0)
    grid_spec = pltpu.PrefetchScalarGridSpec(
        num_scalar_prefetch=2,
        grid=(n_blocks,),
        in_specs=[
            pl.BlockSpec(blk, omap),
            pl.BlockSpec(memory_space=pl.ANY),
            pl.BlockSpec(memory_space=pl.ANY),
            pl.BlockSpec(memory_space=pl.ANY),
        ],
        out_specs=pl.BlockSpec(blk, omap),
        scratch_shapes=[
            pltpu.VMEM((MOE_W_BUFS, D_MODEL, EXPERT_FF), _f32),
            pltpu.VMEM((MOE_W_BUFS, D_MODEL, EXPERT_FF), _f32),
            pltpu.VMEM((MOE_W_BUFS, EXPERT_FF, D_MODEL), _f32),
            pltpu.VMEM((D_MODEL, EXPERT_FF), _bf16),
            pltpu.VMEM((D_MODEL, EXPERT_FF), _bf16),
            pltpu.VMEM((EXPERT_FF, D_MODEL), _bf16),
            pltpu.SemaphoreType.DMA((MOE_W_BUFS,)),
            pltpu.SMEM((1,), jnp.int32),
        ],
    )
    return pl.pallas_call(
        functools.partial(_moe_kernel, n_blocks=n_blocks),
        grid_spec=grid_spec,
        out_shape=jax.ShapeDtypeStruct(x_rows.shape, x_rows.dtype),
        compiler_params=pltpu.CompilerParams(
            dimension_semantics=("arbitrary",),
            vmem_limit_bytes=VMEM_LIMIT),
        name="moe_experts",
    )(block_expert, n_used, x_rows, w_gate, w_up, w_down)


TAIL_TM = 256
TAIL_SUB = 256


def _tail_kernel(pos_ref, h_ref, gate_ref, y_hbm, p_ref, wpp_ref, gp_ref, wpg_ref, bpg_ref,
                 gfin_ref, out_ref, gbuf, cbuf, sem, *, n_tok):
    i = pl.program_id(0)
    n_steps = pl.num_programs(0)
    tile = lambda r: pl.ds(pl.multiple_of(r * ROW_TILE, ROW_TILE), ROW_TILE)

    def start_row(base, r):
        for kk in range(TOP_K):
            row = pos_ref[kk * n_tok + base + r]
            pltpu.make_async_copy(y_hbm.at[tile(row), :], gbuf.at[kk, tile(r), :],
                                  sem).start(priority=kk % 2)

    def wait_rows():
        for kk in range(TOP_K):
            pltpu.make_async_copy(y_hbm.at[pl.ds(0, TAIL_TM * ROW_TILE), :], gbuf.at[kk], sem).wait()

    @pl.when(i == 0)
    def _():
        def body(r, carry):
            start_row(0, r)
            return carry

        lax.fori_loop(0, TAIL_TM, body, 0, unroll=8)

    wait_rows()
    cbuf[...] = gbuf[...]

    nxt_base = jnp.minimum(i + 1, n_steps - 1) * TAIL_TM
    for r in range(TAIL_TM):
        start_row(nxt_base, r)

    for s0 in range(0, TAIL_TM, TAIL_SUB):
        rs = slice(s0, s0 + TAIL_SUB)
        gates = gate_ref[rs, :]
        g1, g2 = gates[:, 0:1], gates[:, 1:2]
        y1_lo, y1_hi = _unpack_bf16_pair(_load_row_tiles(cbuf.at[0], TAIL_SUB, s0))
        y2_lo, y2_hi = _unpack_bf16_pair(_load_row_tiles(cbuf.at[1], TAIL_SUB, s0))
        moe = jnp.concatenate([g1 * y1_lo + g2 * y2_lo, g1 * y1_hi + g2 * y2_hi], axis=1)
        h2 = h_ref[rs, :] + moe
        ple = _rms(jnp.dot(p_ref[rs, :].astype(_bf16), wpp_ref[...], preferred_element_type=_f32),
                   gp_ref[...])
        lg = jnp.dot(h2.astype(_bf16), wpg_ref[...], preferred_element_type=_f32) + bpg_ref[...]
        h3 = h2 + (1.0 / (1.0 + jnp.exp(-lg))) * ple
        out_ref[rs, :] = _rms(h3, gfin_ref[...])

    @pl.when(i == n_steps - 1)
    def _():
        wait_rows()


def _tail(pos, h1, gates, y_rows, p2, w_pp, g_ple, w_pg, b_pg, g_fin):
    t = h1.shape[0]
    row = lambda i, ps: (i, 0)
    fix = lambda i, ps: (0, 0)
    grid_spec = pltpu.PrefetchScalarGridSpec(
        num_scalar_prefetch=1,
        grid=(t // TAIL_TM,),
        in_specs=[
            pl.BlockSpec((TAIL_TM, D_MODEL), row),
            pl.BlockSpec((TAIL_TM, LANES), row),
            pl.BlockSpec(memory_space=pl.ANY),
            pl.BlockSpec((TAIL_TM, PLE_DIM), row),
            pl.BlockSpec((PLE_DIM, D_MODEL), fix),
            pl.BlockSpec((1, D_MODEL), fix),
            pl.BlockSpec((D_MODEL, D_MODEL), fix),
            pl.BlockSpec((1, D_MODEL), fix),
            pl.BlockSpec((1, D_MODEL), fix),
        ],
        out_specs=pl.BlockSpec((TAIL_TM, D_MODEL), row),
        scratch_shapes=[
            pltpu.VMEM((TOP_K, TAIL_TM * ROW_TILE, LANES), y_rows.dtype),
            pltpu.VMEM((TOP_K, TAIL_TM * ROW_TILE, LANES), y_rows.dtype),
            pltpu.SemaphoreType.DMA(()),
        ],
    )
    return pl.pallas_call(
        functools.partial(_tail_kernel, n_tok=t),
        grid_spec=grid_spec,
        out_shape=jax.ShapeDtypeStruct((t, D_MODEL), _f32),
        compiler_params=pltpu.CompilerParams(
            dimension_semantics=("arbitrary",),
            vmem_limit_bytes=VMEM_LIMIT),
        name="tail",
    )(pos, h1, gates, y_rows, p2, w_pp, g_ple, w_pg, b_pg, g_fin)


def _inv_freq_row():
    inv_freq = ROPE_THETA ** (-jnp.arange(0, ROT_DIM, 2, dtype=_f32) / ROT_DIM)
    comp = jnp.concatenate([inv_freq, inv_freq, jnp.zeros((DIFF_HEAD_DIM - ROT_DIM,), _f32)])
    return jnp.concatenate([comp, comp]).reshape(1, LANES)


def kernel(x, p, positions, norm_mix_g, w_in, conv_w, conv_b, dt_bias_f, dt_bias_b, a_log_f, a_log_b, d_skip, ssd_norm_g, lam_q1, lam_k1, lam_q2, lam_k2, subln_g, w_out, norm_ffn_g, w_route_group, b_route_group, w_route_expert, b_route_expert, w_exp_gate, w_exp_up, w_exp_down, w_ple_proj, ple_norm_g, w_ple_gate, b_ple_gate, final_norm_g):
    b, seq, d = x.shape
    t = b * seq
    x2 = x.reshape(t, d)
    row = lambda v: v.reshape(1, -1).astype(_f32)

    w_in_t = jnp.swapaxes(w_in[0], 0, 1)
    pos_col = positions.astype(_f32).reshape(t, 1)
    pad_dt = lambda v: jnp.pad(v, (0, DT_PAD - 2 * SSD_HEADS)).reshape(1, DT_PAD)
    dt_bias = pad_dt(jnp.concatenate([dt_bias_f[0], dt_bias_b[0]]))
    a_row = pad_dt(jnp.concatenate([-jnp.exp(a_log_f[0]), -jnp.exp(a_log_b[0])]))
    d_skip_row = jnp.repeat(d_skip[0], SSD_HEADDIM).reshape(1, SSD_W)
    lam_vecs = jnp.stack([lam_q1[0], lam_k1[0], lam_q2[0], lam_k2[0]])
    n_route = MOE_GROUPS + N_EXPERTS
    w_route = jnp.pad(jnp.concatenate([w_route_group[0], w_route_expert[0]], axis=1),
                      ((0, 0), (0, LANES - n_route)))
    w_route_hi = w_route.astype(_bf16)
    w_route_lo = (w_route - w_route_hi.astype(_f32)).astype(_bf16)
    w_route2 = jnp.concatenate([w_route_hi, w_route_lo], axis=1)
    b_route = jnp.pad(jnp.concatenate([b_route_group[0], b_route_expert[0]]),
                      (0, LANES - n_route)).reshape(1, LANES)

    proj, dt, *rot = _inproj(x2, row(norm_mix_g[0]), pos_col, _inv_freq_row(), w_in_t)
    proj3 = proj.reshape(b, seq, MAIN_W)
    y_ssd = _ssd(proj3, dt.reshape(b, seq, DT_PAD), conv_w[0], row(conv_b[0]),
                 dt_bias, a_row, d_skip_row)
    att = _attn(proj3, [r.reshape(b, seq, LANES) for r in rot], lam_vecs, row(subln_g[0]))
    h1, hn_packed, route, cnt, gates = _outproj(
        y_ssd.reshape(t, SSD_W), proj, att.reshape(t, ATTN_W), x2, w_out[0].astype(_bf16),
        row(ssd_norm_g[0]), row(norm_ffn_g[0]), w_route2, b_route)
    pos8, blk8, meta8 = _plan(route, cnt)
    pos = pos8[:TOP_K].reshape(-1)
    block_expert = blk8[0, :_n_rows(t) // ROW_BLOCK]
    n_used = blk8[1, :1]
    x_rows = _dispatch_rows(pos, meta8[0, :N_EXPERTS], meta8[1, :N_EXPERTS], n_used, hn_packed)
    y_rows = _moe(block_expert, n_used, x_rows, w_exp_gate[0], w_exp_up[0], w_exp_down[0])
    out = _tail(pos, h1, gates, y_rows, p[0].reshape(t, PLE_DIM), w_ple_proj[0].astype(_bf16),
                row(ple_norm_g[0]), w_ple_gate[0].astype(_bf16), row(b_ple_gate[0]),
                row(final_norm_g))
    return out.reshape(b, seq, d)
```

```python
import functools

import jax
import jax.numpy as jnp
import numpy as np
from jax import lax
from jax.experimental import pallas as pl
from jax.experimental.pallas import tpu as pltpu

D_MODEL = 2048
PLE_DIM = 256
SSD_W = 1024
ATTN_W = 1024
SSD_HEADDIM = 64
SSD_HEADS = 16
SSD_GROUPS = 2
HEADS_PER_GROUP = SSD_HEADS // SSD_GROUPS
SSD_STATE = 128
CHUNK = 128
CONV_W = 5
XBC_W = SSD_W + 2 * SSD_GROUPS * SSD_STATE
DIFF_HEAD_DIM = 64
DIFF_HEADS = 8
ROT_DIM = 16
ROPE_THETA = 500000.0
MOE_GROUPS = 8
EXPERTS_PER_GROUP = 8
N_EXPERTS = 64
TOP_K = 2
EXPERT_FF = 512
ROW_BLOCK = 128
EPS = 1e-6
LAM_INIT = 0.2

LANES = 128
DT_PAD = LANES
MAIN_W = SSD_W + XBC_W + 3 * ATTN_W
COL_Z = 0
COL_Q = SSD_W
COL_K = COL_Q + ATTN_W
COL_XBC = COL_K + ATTN_W
COL_V = COL_XBC + XBC_W
VMEM_LIMIT = 56 * 1024 * 1024
NEG_BIG = -1e30
Q_SCALE = float(DIFF_HEAD_DIM ** -0.5 * np.log2(np.e))

_f32 = jnp.float32
_bf16 = jnp.bfloat16


def _silu(v):
    return v * (1.0 / (1.0 + jnp.exp(-v)))


def _rms(v, g):
    return v * lax.rsqrt(jnp.mean(v * v, axis=-1, keepdims=True) + EPS) * g


_HI16 = np.uint32(0xFFFF0000)


def _pack_bf16_pair(lo, hi):
    lo_w = lax.bitcast_convert_type(lo.astype(_bf16).astype(_f32), jnp.uint32) >> 16
    hi_w = lax.bitcast_convert_type(hi.astype(_bf16).astype(_f32), jnp.uint32) & _HI16
    return lo_w | hi_w


def _unpack_bf16_pair(w):
    lo = lax.bitcast_convert_type(w << 16, _f32)
    hi = lax.bitcast_convert_type(w & _HI16, _f32)
    return lo, hi


ROW_TILE = 8
assert D_MODEL // 2 == ROW_TILE * LANES


def _store_row_tiles(ref, words, n, row0=0):
    for c in range(ROW_TILE):
        ref[pl.ds(row0 * ROW_TILE + c, n, stride=ROW_TILE), :] = words[:, c * LANES:(c + 1) * LANES]


def _load_row_tiles(ref, n, row0=0):
    return jnp.concatenate(
        [ref[pl.ds(row0 * ROW_TILE + c, n, stride=ROW_TILE), :] for c in range(ROW_TILE)], axis=1)


IN_TM = 1024
IN_TN = 512


def _inproj_kernel(x_ref, g_ref, pos_ref, invf_ref, wt_ref, wdt_ref,
                   out_ref, dt_ref, rc_ref, rp_ref, rm_ref, n_scr):
    j = pl.program_id(1)
    nt = (((1,), (1,)), ((), ()))
    half = ROT_DIM // 2

    @pl.when(j == 0)
    def _():
        n = _rms(x_ref[...], g_ref[...])
        n_scr[...] = n.astype(_bf16)
        lane = lax.broadcasted_iota(jnp.int32, (1, LANES), 1)
        dt = lax.dot_general(n_scr[...], wdt_ref[...].astype(_bf16), nt, preferred_element_type=_f32)
        dt_ref[...] = jnp.where(lane < 2 * SSD_HEADS, dt, 0.0)
        ang = pos_ref[...] * invf_ref[...]
        cos, sin = jnp.cos(ang), jnp.sin(ang)
        l64 = lane & (DIFF_HEAD_DIM - 1)
        rc_ref[...] = jnp.where(l64 < ROT_DIM, cos, 1.0)
        rp_ref[...] = jnp.where(jnp.logical_and(l64 >= half, l64 < ROT_DIM), sin, 0.0)
        rm_ref[...] = jnp.where(l64 < half, -sin, 0.0)

    out_ref[...] = lax.dot_general(n_scr[...], wt_ref[...].astype(_bf16), nt,
                                   preferred_element_type=_f32).astype(_bf16)


def _rotate(t, rc, rp, rm):
    half = ROT_DIM // 2
    return t * rc + pltpu.roll(t, half, 1) * rp + pltpu.roll(t, LANES - half, 1) * rm


SRC_XBC = SSD_W
SRC_DT = SRC_XBC + XBC_W
SRC_Q = SRC_DT + 2 * SSD_HEADS
SRC_V = SRC_Q + 2 * ATTN_W
SRC_ALIGN = 32
assert all(s % SRC_ALIGN == 0 for s in (SRC_XBC, SRC_DT, SRC_Q, SRC_V, IN_TN))


def _src_row(j):
    jq, jx, jv = COL_Q // IN_TN, COL_XBC // IN_TN, COL_V // IN_TN
    row = jnp.where(j < jq, j * IN_TN,
                    jnp.where(j < jx, SRC_Q + (j - jq) * IN_TN,
                              jnp.where(j < jv, SRC_XBC + (j - jx) * IN_TN,
                                        SRC_V + (j - jv) * IN_TN)))
    return pl.multiple_of(row, SRC_ALIGN)


def _inproj(x2, g, pos_col, invf_row, w_in_t):
    t = x2.shape[0]
    grid = (t // IN_TM, MAIN_W // IN_TN)
    return pl.pallas_call(
        _inproj_kernel,
        grid=grid,
        in_specs=[
            pl.BlockSpec((IN_TM, D_MODEL), lambda i, j: (i, 0)),
            pl.BlockSpec((1, D_MODEL), lambda i, j: (0, 0)),
            pl.BlockSpec((IN_TM, 1), lambda i, j: (i, 0)),
            pl.BlockSpec((1, LANES), lambda i, j: (0, 0)),
            pl.BlockSpec((pl.Element(IN_TN), pl.Element(D_MODEL)), lambda i, j: (_src_row(j), 0)),
            pl.BlockSpec((pl.Element(DT_PAD), pl.Element(D_MODEL)), lambda i, j: (SRC_DT, 0)),
        ],
        out_specs=[
            pl.BlockSpec((IN_TM, IN_TN), lambda i, j: (i, j)),
            pl.BlockSpec((IN_TM, DT_PAD), lambda i, j: (i, 0)),
        ] + [pl.BlockSpec((IN_TM, LANES), lambda i, j: (i, 0))] * 3,
        out_shape=[
            jax.ShapeDtypeStruct((t, MAIN_W), _bf16),
            jax.ShapeDtypeStruct((t, DT_PAD), _f32),
        ] + [jax.ShapeDtypeStruct((t, LANES), _f32)] * 3,
        scratch_shapes=[pltpu.VMEM((IN_TM, D_MODEL), _bf16)],
        compiler_params=pltpu.CompilerParams(
            dimension_semantics=("arbitrary", "arbitrary"),
            vmem_limit_bytes=VMEM_LIMIT),
        name="inproj",
    )(x2, g, pos_col, invf_row, w_in_t, w_in_t)


CONV_HALO = 16


def _ssd_kernel(xbc_ref, dt_ref, cw_ref, cb_ref, dtb_ref, a_ref, dsk_ref,
                y_ref, xpad, xact, dts, yacc, state_f, state_b, seq):
    n_chunks = seq // CHUNK

    zeros_halo = jnp.zeros((CONV_HALO, XBC_W), _bf16)
    xpad[0:CONV_HALO, :] = zeros_halo
    xpad[CONV_HALO + seq:CONV_HALO + seq + CONV_HALO, :] = zeros_halo
    xpad[CONV_HALO:CONV_HALO + seq, :] = xbc_ref[0]
    cw = cw_ref[...]
    cb = cb_ref[...]
    win = CHUNK + 2 * CONV_HALO

    def conv_body(c, carry):
        r0 = pl.multiple_of(c * CHUNK, CHUNK)
        blk = xpad[pl.ds(r0, win), :].astype(_f32)
        acc = blk * cw[CONV_W // 2:CONV_W // 2 + 1, :]
        for k in range(CONV_W):
            sh = CONV_W // 2 - k
            if sh == 0:
                continue
            acc = acc + pltpu.roll(blk, sh % win, 0) * cw[k:k + 1, :]
        v = acc[CONV_HALO:CONV_HALO + CHUNK, :] + cb
        xact[pl.ds(r0, CHUNK), :] = _silu(v).astype(_bf16)
        return carry

    lax.fori_loop(0, n_chunks, conv_body, 0)

    raw = dt_ref[0] + dtb_ref[...]
    dts[...] = jnp.maximum(raw, 0.0) + jnp.log1p(jnp.exp(-jnp.abs(raw)))

    a_row = a_ref[...]
    dsk = dsk_ref[...]
    row_i = lax.broadcasted_iota(jnp.int32, (CHUNK, CHUNK), 0)
    col_i = lax.broadcasted_iota(jnp.int32, (CHUNK, CHUNK), 1)

    def chunk_step(c, reverse, state):
        r0 = pl.multiple_of(c * CHUNK, CHUNK)
        dtc = dts[pl.ds(r0, CHUNK), :]
        cs = dtc * a_row
        k = 1
        while k < CHUNK:
            if reverse:
                cs = cs + jnp.where(row_i < CHUNK - k, pltpu.roll(cs, CHUNK - k, 0), 0.0)
            else:
                cs = cs + jnp.where(row_i >= k, pltpu.roll(cs, k, 0), 0.0)
            k *= 2
        cs_t = cs.T
        dt_t = dtc.T
        end_col = cs_t[:, 0:1] if reverse else cs_t[:, CHUNK - 1:CHUNK]
        w_t = jnp.exp(end_col - cs_t) * dt_t
        end_row = cs[0:1, :] if reverse else cs[CHUNK - 1:CHUNK, :]
        dec_row = jnp.exp(end_row)
        tri = (row_i <= col_i) if reverse else (row_i >= col_i)
        lane0 = SSD_HEADS if reverse else 0
        first_head = col_i < SSD_HEADDIM
        rows = pl.ds(r0, CHUNK)
        for g in range(SSD_GROUPS):
            b_g = xact[rows, SSD_W + g * SSD_STATE:SSD_W + (g + 1) * SSD_STATE]
            c_g = xact[rows, SSD_W + (SSD_GROUPS + g) * SSD_STATE:SSD_W + (SSD_GROUPS + g + 1) * SSD_STATE]
            cbm = lax.dot_general(c_g, b_g, (((1,), (1,)), ((), ())),
                                  preferred_element_type=_f32)
            b_t = b_g.astype(_f32).T
            for pp in range(HEADS_PER_GROUP // 2):
                pair = g * (HEADS_PER_GROUP // 2) + pp
                cols = slice(pair * LANES, (pair + 1) * LANES)
                xs_p = xact[rows, cols]
                st = state[pair]
                off = jnp.dot(c_g, st.astype(_bf16), preferred_element_type=_f32)
                halves = []
                for hh in range(2):
                    ln = lane0 + 2 * pair + hh
                    colb = jnp.broadcast_to(cs[:, ln:ln + 1], (CHUNK, CHUNK))
                    m_h = (cbm * jnp.exp(jnp.where(tri, colb - cs_t[ln:ln + 1, :], NEG_BIG))
                           * dt_t[ln:ln + 1, :])
                    y_h = (jnp.dot(m_h.astype(_bf16), xs_p, preferred_element_type=_f32)
                           + jnp.exp(colb) * off)
                    upd = jnp.dot((b_t * w_t[ln:ln + 1, :]).astype(_bf16), xs_p,
                                  preferred_element_type=_f32)
                    dec = jnp.broadcast_to(dec_row[:, ln:ln + 1], (CHUNK, CHUNK))
                    halves.append((y_h, upd, dec))
                y_p = jnp.where(first_head, halves[0][0], halves[1][0])
                upd_p = jnp.where(first_head, halves[0][1], halves[1][1])
                dec_p = jnp.where(first_head, halves[0][2], halves[1][2])
                state[pair] = st * dec_p + upd_p
                if reverse:
                    y_ref[0, rows, cols] = (yacc[rows, cols] + y_p).astype(_bf16)
                else:
                    yacc[rows, cols] = y_p + xs_p.astype(_f32) * dsk[:, cols]

    state_f[...] = jnp.zeros_like(state_f)
    state_b[...] = jnp.zeros_like(state_b)

    def fwd_body(c, carry):
        chunk_step(c, False, state_f)
        return carry

    lax.fori_loop(0, n_chunks, fwd_body, 0)

    def bwd_body(i, carry):
        chunk_step(n_chunks - 1 - i, True, state_b)
        return carry

    lax.fori_loop(0, n_chunks, bwd_body, 0)


def _ssd(proj3, dt3, conv_w, conv_b, dt_bias, a_row, d_skip_row):
    b, seq, _ = proj3.shape
    kern = functools.partial(_ssd_kernel, seq=seq)
    assert COL_XBC % XBC_W == 0
    return pl.pallas_call(
        kern,
        grid=(b,),
        in_specs=[
            pl.BlockSpec((1, seq, XBC_W), lambda i: (i, 0, COL_XBC // XBC_W)),
            pl.BlockSpec((1, seq, DT_PAD), lambda i: (i, 0, 0)),
            pl.BlockSpec((CONV_W, XBC_W), lambda i: (0, 0)),
            pl.BlockSpec((1, XBC_W), lambda i: (0, 0)),
            pl.BlockSpec((1, DT_PAD), lambda i: (0, 0)),
            pl.BlockSpec((1, DT_PAD), lambda i: (0, 0)),
            pl.BlockSpec((1, SSD_W), lambda i: (0, 0)),
        ],
        out_specs=pl.BlockSpec((1, seq, SSD_W), lambda i: (i, 0, 0)),
        out_shape=jax.ShapeDtypeStruct((b, seq, SSD_W), _bf16),
        scratch_shapes=[
            pltpu.VMEM((seq + 2 * CONV_HALO, XBC_W), _bf16),
            pltpu.VMEM((seq, XBC_W), _bf16),
            pltpu.VMEM((seq, DT_PAD), _f32),
            pltpu.VMEM((seq, SSD_W), _f32),
            pltpu.VMEM((SSD_HEADS // 2, SSD_STATE, 2 * SSD_HEADDIM), _f32),
            pltpu.VMEM((SSD_HEADS // 2, SSD_STATE, 2 * SSD_HEADDIM), _f32),
        ],
        compiler_params=pltpu.CompilerParams(
            dimension_semantics=("arbitrary",),
            vmem_limit_bytes=VMEM_LIMIT),
        name="ssd",
    )(proj3, dt3, conv_w, conv_b, dt_bias, a_row, d_skip_row)


ATT_QB = 256
ATT_UNROLL = 8


def _attn_kernel(q_ref, k_ref, v_ref, rc_ref, rp_ref, rm_ref, lam_ref, g_ref, o_ref,
                 k_scr, v1_scr, *, seq):
    hw = 2 * DIFF_HEAD_DIM
    lv = lam_ref[...]
    lam = (jnp.exp(jnp.sum(lv[0:1] * lv[1:2], axis=-1, keepdims=True))
           - jnp.exp(jnp.sum(lv[2:3] * lv[3:4], axis=-1, keepdims=True)) + LAM_INIT)

    def rotated(ref, rows):
        return _rotate(ref[0, rows, :].astype(_f32), rc_ref[0, rows, :], rp_ref[0, rows, :],
                       rm_ref[0, rows, :])

    def rot_body(c, carry):
        rows = pl.ds(pl.multiple_of(c * ATT_QB, ATT_QB), ATT_QB)
        k_scr[rows, :] = rotated(k_ref, rows).astype(_bf16)
        return carry

    lax.fori_loop(0, seq // ATT_QB, rot_body, 0)
    k = k_scr[...]
    g = g_ref[...]
    v1_scr[:, 0:hw] = v_ref[0]
    v1_scr[:, hw:2 * hw] = (lax.broadcasted_iota(jnp.int32, (seq, hw), 1) == 0).astype(_bf16)
    first = lax.broadcasted_iota(jnp.int32, (1, hw), 1) < DIFF_HEAD_DIM
    nt = (((1,), (1,)), ((), ()))

    def chain(r0):
        q = (rotated(q_ref, pl.ds(r0, ATT_QB)) * Q_SCALE).astype(_bf16)
        zero = jnp.zeros_like(q)
        ovs = []
        for qc in (jnp.where(first, q, zero), jnp.where(first, zero, q)):
            s = lax.dot_general(qc, k, nt, preferred_element_type=_f32)
            e = jnp.exp2(s - jnp.max(s, axis=-1, keepdims=True)).astype(_bf16)
            ovs.append(jnp.dot(e, v1_scr[...], preferred_element_type=_f32))
        o1, l1 = ovs[0][:, 0:hw], ovs[0][:, hw:hw + 1]
        o2, l2 = ovs[1][:, 0:hw], ovs[1][:, hw:hw + 1]
        o = o1 / l1 - (lam / l2) * o2
        o = _rms(o, g) * (1.0 - LAM_INIT)
        o_ref[0, pl.ds(r0, ATT_QB), :] = o.astype(_bf16)

    step = ATT_QB * ATT_UNROLL

    def body(i, carry):
        for u in range(ATT_UNROLL):
            chain(pl.multiple_of(i * step + u * ATT_QB, ATT_QB))
        return carry

    lax.fori_loop(0, seq // step, body, 0)


def _attn(proj3, rot_tables, lam_vecs, subln_g):
    b, seq, _ = proj3.shape
    hw = 2 * DIFF_HEAD_DIM
    kern = functools.partial(_attn_kernel, seq=seq)
    return pl.pallas_call(
        kern,
        grid=(b, DIFF_HEADS),
        in_specs=[
            pl.BlockSpec((1, seq, hw), lambda i, h: (i, 0, COL_Q // hw + h)),
            pl.BlockSpec((1, seq, hw), lambda i, h: (i, 0, COL_K // hw + h)),
            pl.BlockSpec((1, seq, hw), lambda i, h: (i, 0, COL_V // hw + h)),
        ] + [pl.BlockSpec((1, seq, LANES), lambda i, h: (i, 0, 0))] * 3 + [
            pl.BlockSpec((4, DIFF_HEAD_DIM), lambda i, h: (0, 0)),
            pl.BlockSpec((1, hw), lambda i, h: (0, 0)),
        ],
        out_specs=pl.BlockSpec((1, seq, hw), lambda i, h: (i, 0, h)),
        out_shape=jax.ShapeDtypeStruct((b, seq, ATTN_W), _bf16),
        scratch_shapes=[pltpu.VMEM((seq, hw), _bf16), pltpu.VMEM((seq, 2 * hw), _bf16)],
        compiler_params=pltpu.CompilerParams(
            dimension_semantics=("arbitrary", "arbitrary"),
            vmem_limit_bytes=VMEM_LIMIT),
        name="diffattn",
    )(proj3, proj3, proj3, *rot_tables, lam_vecs, subln_g)


OUT_TM = 256
ROUTE_ROWS = 8


def _outproj_kernel(y_ref, z_ref, att_ref, x_ref, w_ref, gs_ref, gf_ref, wr_ref, br_ref,
                    h_ref, hn_ref, route_ref, cnt_ref, gate_ref, cnt_scr):
    @pl.when(pl.program_id(0) == 0)
    def _():
        cnt_scr[...] = jnp.zeros_like(cnt_scr)

    y = y_ref[...].astype(_f32)
    z = z_ref[...].astype(_f32)
    s = _rms(y * _silu(z), gs_ref[...]).astype(_bf16)
    acc = jnp.dot(s, w_ref[0:SSD_W, :], preferred_element_type=_f32)
    acc = acc + jnp.dot(att_ref[...], w_ref[SSD_W:SSD_W + ATTN_W, :], preferred_element_type=_f32)
    h1 = x_ref[...] + acc
    h_ref[...] = h1
    hn = _rms(h1, gf_ref[...])
    half = D_MODEL // 2
    _store_row_tiles(hn_ref, _pack_bf16_pair(hn[:, 0:half], hn[:, half:D_MODEL]), hn.shape[0])
    hn_hi = hn.astype(_bf16)
    hn_lo = (hn - hn_hi.astype(_f32)).astype(_bf16)
    a = jnp.dot(hn_hi, wr_ref[...], preferred_element_type=_f32)
    bb = jnp.dot(hn_lo, wr_ref[:, 0:LANES], preferred_element_type=_f32)
    logits = (a[:, 0:LANES] + a[:, LANES:2 * LANES] + bb + br_ref[...]).T
    tm = logits.shape[1]
    iota = lax.broadcasted_iota(jnp.int32, (MOE_GROUPS, tm), 0)

    def first_argmax(val, vmax):
        return jnp.min(jnp.where(val == vmax, iota, MOE_GROUPS), axis=0, keepdims=True)

    gl = logits[0:MOE_GROUPS]
    gmax = jnp.max(gl, axis=0, keepdims=True)
    p_g = 1.0 / jnp.sum(jnp.exp(gl - gmax), axis=0, keepdims=True)
    g_sel = first_argmax(gl, gmax)
    el = jnp.zeros((EXPERTS_PER_GROUP, tm), _f32)
    for gi in range(MOE_GROUPS):
        lo = MOE_GROUPS + gi * EXPERTS_PER_GROUP
        el = jnp.where(g_sel == gi, logits[lo:lo + EXPERTS_PER_GROUP], el)
    ee = jnp.exp(el - jnp.max(el, axis=0, keepdims=True))
    pe = ee / jnp.sum(ee, axis=0, keepdims=True)
    p1 = jnp.max(pe, axis=0, keepdims=True)
    i1 = first_argmax(pe, p1)
    pe2 = jnp.where(iota == i1, -1.0, pe)
    p2 = jnp.max(pe2, axis=0, keepdims=True)
    i2 = first_argmax(pe2, p2)
    den = p1 + p2
    e1 = g_sel * EXPERTS_PER_GROUP + i1
    e2 = g_sel * EXPERTS_PER_GROUP + i2
    iota_e = lax.broadcasted_iota(jnp.int32, (N_EXPERTS, tm), 0)
    oh1 = (iota_e == e1).astype(_f32)
    oh2 = (iota_e == e2).astype(_f32)
    both = oh1 + oh2
    earlier = (lax.broadcasted_iota(jnp.int32, (tm, tm), 0)
               < lax.broadcasted_iota(jnp.int32, (tm, tm), 1)).astype(_bf16)
    before = cnt_scr[:, 0:1] + jnp.dot(both.astype(_bf16), earlier,
                                       preferred_element_type=_f32)
    r1 = jnp.sum(oh1 * before, axis=0, keepdims=True).astype(jnp.int32)
    r2 = jnp.sum(oh2 * before, axis=0, keepdims=True).astype(jnp.int32)
    cnt_scr[...] = cnt_scr[...] + jnp.sum(both, axis=1, keepdims=True)
    cnt_ref[...] = cnt_scr[...].astype(jnp.int32)
    route_ref[...] = jnp.where(iota == 0, e1, jnp.where(iota == 1, e2,
                               jnp.where(iota == 2, r1, jnp.where(iota == 3, r2, 0))))
    g8 = jnp.where(iota == 0, p_g * (p1 / den), jnp.where(iota == 1, p_g * (p2 / den), 0.0))
    gate_ref[...] = jnp.concatenate(
        [g8, jnp.zeros((LANES - ROUTE_ROWS, tm), _f32)], axis=0).T


def _outproj(y2, proj2, att2, x2, w_out, g_ssd, g_ffn, wr_t, br_col):
    t = x2.shape[0]
    row = lambda i: (i, 0)
    fix = lambda i: (0, 0)
    return pl.pallas_call(
        _outproj_kernel,
        grid=(t // OUT_TM,),
        in_specs=[
            pl.BlockSpec((OUT_TM, SSD_W), row),
            pl.BlockSpec((OUT_TM, SSD_W), row),
            pl.BlockSpec((OUT_TM, ATTN_W), row),
            pl.BlockSpec((OUT_TM, D_MODEL), row),
            pl.BlockSpec((SSD_W + ATTN_W, D_MODEL), fix),
            pl.BlockSpec((1, SSD_W), fix),
            pl.BlockSpec((1, D_MODEL), fix),
            pl.BlockSpec((D_MODEL, 2 * LANES), fix),
            pl.BlockSpec((1, LANES), fix),
        ],
        out_specs=[
            pl.BlockSpec((OUT_TM, D_MODEL), row),
            pl.BlockSpec((OUT_TM * ROW_TILE, LANES), row),
            pl.BlockSpec((ROUTE_ROWS, OUT_TM), lambda i: (0, i)),
            pl.BlockSpec((N_EXPERTS, LANES), fix),
            pl.BlockSpec((OUT_TM, LANES), row),
        ],
        out_shape=[
            jax.ShapeDtypeStruct((t, D_MODEL), _f32),
            jax.ShapeDtypeStruct((t * ROW_TILE, LANES), jnp.uint32),
            jax.ShapeDtypeStruct((ROUTE_ROWS, t), jnp.int32),
            jax.ShapeDtypeStruct((N_EXPERTS, LANES), jnp.int32),
            jax.ShapeDtypeStruct((t, LANES), _f32),
        ],
        scratch_shapes=[pltpu.VMEM((N_EXPERTS, LANES), _f32)],
        compiler_params=pltpu.CompilerParams(
            dimension_semantics=("arbitrary",),
            vmem_limit_bytes=VMEM_LIMIT),
        name="outproj_router",
    )(y2, proj2, att2, x2, w_out, g_ssd, g_ffn, wr_t, br_col)


def _n_rows(n_tok):
    n_assign = n_tok * TOP_K
    return (n_assign + N_EXPERTS * (ROW_BLOCK - 1) + ROW_BLOCK - 1) // ROW_BLOCK * ROW_BLOCK


def _prefix_sum(v, axis):
    n = v.shape[axis]
    idx = lax.broadcasted_iota(jnp.int32, v.shape, axis)
    k = 1
    while k < n:
        v = v + jnp.where(idx >= k, pltpu.roll(v, k, axis), 0)
        k *= 2
    return v


def _plan_kernel(route_ref, cnt_ref, pos_ref, blk_ref, meta_ref, *, n_blocks_pad):
    n_tok = route_ref.shape[1]
    cnt_col = jnp.concatenate(
        [cnt_ref[...], jnp.zeros((LANES - N_EXPERTS, LANES), jnp.int32)], axis=0)
    pad_up = lambda c: (c + (ROW_BLOCK - 1)) & (-ROW_BLOCK)
    ends_col = _prefix_sum(pad_up(cnt_col), 0)
    starts_col = (ends_col - pad_up(cnt_col)).astype(_f32)
    cnt_row = cnt_col.astype(_f32).T.astype(jnp.int32)
    ends_row = _prefix_sum(pad_up(cnt_row), 1)
    starts_row = ends_row - pad_up(cnt_row)
    n_used = ends_row[0:1, LANES - 1:LANES] >> (ROW_BLOCK.bit_length() - 1)

    ch = 1024
    iota_e = lax.broadcasted_iota(jnp.int32, (LANES, ch), 0)
    for c0 in range(0, n_tok, ch):
        rt = route_ref[:, c0:c0 + ch]
        s1 = jnp.sum(jnp.where(iota_e == rt[0:1], starts_col[:, 0:1], 0.0), axis=0, keepdims=True)
        s2 = jnp.sum(jnp.where(iota_e == rt[1:2], starts_col[:, 0:1], 0.0), axis=0, keepdims=True)
        p1 = s1.astype(jnp.int32) + rt[2:3]
        p2 = s2.astype(jnp.int32) + rt[3:4]
        sub = lax.broadcasted_iota(jnp.int32, (ROUTE_ROWS, ch), 0)
        pos_ref[:, c0:c0 + ch] = jnp.where(sub == 0, p1, jnp.where(sub == 1, p2, 0))

    blk_start = lax.broadcasted_iota(jnp.int32, (LANES, n_blocks_pad), 1) * ROW_BLOCK
    e_idx = lax.broadcasted_iota(jnp.int32, (LANES, n_blocks_pad), 0)
    real = e_idx < N_EXPERTS
    be = jnp.sum(jnp.where(jnp.logical_and(real, ends_col[:, 0:1] <= blk_start), 1.0, 0.0),
                 axis=0, keepdims=True).astype(jnp.int32)
    be = jnp.minimum(be, N_EXPERTS - 1)
    last = jnp.max(jnp.where(jnp.logical_and(real, cnt_col[:, 0:1] > 0), e_idx.astype(_f32), 0.0),
                   axis=0, keepdims=True).astype(jnp.int32)
    blk_i = lax.broadcasted_iota(jnp.int32, (1, n_blocks_pad), 1)
    be = jnp.where(blk_i < n_used, be, last)
    sub = lax.broadcasted_iota(jnp.int32, (ROUTE_ROWS, n_blocks_pad), 0)
    blk_ref[...] = jnp.where(sub == 0, be, jnp.where(sub == 1, n_used, 0))
    sub = lax.broadcasted_iota(jnp.int32, (ROUTE_ROWS, LANES), 0)
    meta_ref[...] = jnp.where(sub == 0, cnt_row[0:1], jnp.where(sub == 1, starts_row[0:1], 0))


def _plan(route, cnt):
    n_tok = route.shape[1]
    n_blocks_pad = -(-(_n_rows(n_tok) // ROW_BLOCK) // LANES) * LANES
    return pl.pallas_call(
        functools.partial(_plan_kernel, n_blocks_pad=n_blocks_pad),
        out_shape=[
            jax.ShapeDtypeStruct((ROUTE_ROWS, n_tok), jnp.int32),
            jax.ShapeDtypeStruct((ROUTE_ROWS, n_blocks_pad), jnp.int32),
            jax.ShapeDtypeStruct((ROUTE_ROWS, LANES), jnp.int32),
        ],
        compiler_params=pltpu.CompilerParams(vmem_limit_bytes=VMEM_LIMIT),
        name="route_plan",
    )(route, cnt)


DISP_TM = 256


def _dispatch_kernel(pos_ref, cnt_ref, start_ref, nused_ref, h_ref, x_hbm, stage, zrow, sem, zsem,
                     *, n_tok, n_blocks):
    i = pl.program_id(0)
    n_steps = pl.num_programs(0)
    slot = i % 2
    base = i * DISP_TM
    tile = lambda r: pl.ds(pl.multiple_of(r * ROW_TILE, ROW_TILE), ROW_TILE)

    def wait_slot(sl):
        for _ in range(TOP_K):
            pltpu.make_async_copy(stage.at[sl], x_hbm.at[pl.ds(0, DISP_TM * ROW_TILE), :],
                                  sem.at[sl]).wait()

    @pl.when(i >= 2)
    def _():
        wait_slot(slot)

    stage[slot] = h_ref[...]

    def body(r, carry):
        for kk in range(TOP_K):
            row = pos_ref[kk * n_tok + base + r]
            pltpu.make_async_copy(stage.at[slot, tile(r), :], x_hbm.at[tile(row), :],
                                  sem.at[slot]).start(priority=kk % 2)
        return carry

    lax.fori_loop(0, DISP_TM, body, 0, unroll=8)

    @pl.when(i == n_steps - 1)
    def _():
        zrow[...] = jnp.zeros_like(zrow)

        pieces = [1 << b for b in reversed(range(ROW_BLOCK.bit_length() - 1))]

        def pad_copy(size, row0):
            src = zrow.at[pl.ds(0, size * ROW_TILE), :]
            dst = x_hbm.at[pl.ds(pl.multiple_of(row0 * ROW_TILE, ROW_TILE), size * ROW_TILE), :]
            return pltpu.make_async_copy(src, dst, zsem)

        def per_expert(e, carry, wait):
            cnt = cnt_ref[e]
            n_pad = ((cnt + (ROW_BLOCK - 1)) & (-ROW_BLOCK)) - cnt
            row0 = start_ref[e] + cnt
            for size in pieces:
                has = (n_pad & size) != 0

                @pl.when(has)
                def _():
                    cp = pad_copy(size, row0)
                    cp.wait() if wait else cp.start()

                row0 = row0 + jnp.where(has, size, 0)
            return carry

        lax.fori_loop(0, N_EXPERTS, functools.partial(per_expert, wait=False), 0)
        blk_rows = ROW_BLOCK * ROW_TILE

        def fill_block(blk, carry):
            dst = pl.ds(pl.multiple_of(blk * blk_rows, blk_rows), blk_rows)
            pltpu.make_async_copy(zrow, x_hbm.at[dst, :], zsem).start()
            return carry

        lax.fori_loop(nused_ref[0], n_blocks, fill_block, 0)

        lax.fori_loop(0, N_EXPERTS, functools.partial(per_expert, wait=True), 0)

        def wait_block(blk, carry):
            pltpu.make_async_copy(zrow, x_hbm.at[pl.ds(0, blk_rows), :], zsem).wait()
            return carry

        lax.fori_loop(nused_ref[0], n_blocks, wait_block, 0)

        @pl.when(n_steps >= 2)
        def _():
            wait_slot(1 - slot)

        wait_slot(slot)


def _dispatch_rows(pos, counts, starts, n_used, hn_packed):
    t = hn_packed.shape[0] // ROW_TILE
    n_rows = _n_rows(t)
    blk = (DISP_TM * ROW_TILE, LANES)
    grid_spec = pltpu.PrefetchScalarGridSpec(
        num_scalar_prefetch=4,
        grid=(t // DISP_TM,),
        in_specs=[pl.BlockSpec(blk, lambda i, *_: (i, 0))],
        out_specs=pl.BlockSpec(memory_space=pl.ANY),
        scratch_shapes=[
            pltpu.VMEM((2,) + blk, hn_packed.dtype),
            pltpu.VMEM((ROW_BLOCK * ROW_TILE, LANES), hn_packed.dtype),
            pltpu.SemaphoreType.DMA((2,)),
            pltpu.SemaphoreType.DMA(()),
        ],
    )
    return pl.pallas_call(
        functools.partial(_dispatch_kernel, n_tok=t, n_blocks=n_rows // ROW_BLOCK),
        grid_spec=grid_spec,
        out_shape=jax.ShapeDtypeStruct((n_rows * ROW_TILE, LANES), hn_packed.dtype),
        compiler_params=pltpu.CompilerParams(
            dimension_semantics=("arbitrary",),
            vmem_limit_bytes=VMEM_LIMIT),
        name="dispatch_rows",
    )(pos, counts, starts, n_used, hn_packed)


MOE_W_SPLIT = 4
MOE_W_BUFS = 3
MOE_GROUP = 4


def _moe_kernel(be_ref, nused_ref, x_ref, wg_hbm, wu_hbm, wd_hbm,
                out_ref, wg_f, wu_f, wd_f, wg_s, wu_s, wd_s, wsem, nexp, *, n_blocks):
    i = pl.program_id(0)
    nused = nused_ref[0]
    half = D_MODEL // 2

    def weight_copies(e, sl):
        cps = []
        for src, dst in ((wg_hbm, wg_f), (wu_hbm, wu_f), (wd_hbm, wd_f)):
            rows = src.shape[1] // MOE_W_SPLIT
            for c in range(MOE_W_SPLIT):
                rs = pl.ds(c * rows, rows)
                cps.append(pltpu.make_async_copy(src.at[e, rs, :], dst.at[sl, rs, :], wsem.at[sl]))
        return cps

    def start_all(cps):
        for n, cp in enumerate(cps):
            cp.start(priority=n % 2)

    def block_expert(b):
        return be_ref[jnp.minimum(b, n_blocks - 1)]

    def next_change(b0):
        e0 = block_expert(b0)
        return lax.while_loop(
            lambda b: jnp.logical_and(b < nused, block_expert(b) == e0), lambda b: b + 1, b0 + 1)

    @pl.when(i == 0)
    def _():
        nexp[0] = 0
        start_all(weight_copies(be_ref[0], 0))
        nxt = next_change(0)

        @pl.when(nxt < nused)
        def _():
            start_all(weight_copies(block_expert(nxt), 1))

    def row_block(b, sub):
        sub_rows = pl.ds(sub * ROW_BLOCK * ROW_TILE, ROW_BLOCK * ROW_TILE)

        @pl.when(b < nused)
        def _():
            e = be_ref[b]
            new_expert = jnp.logical_or(b == 0, e != be_ref[jnp.maximum(b - 1, 0)])

            @pl.when(new_expert)
            def _():
                sl = nexp[0] % MOE_W_BUFS
                ahead = next_change(jnp.minimum(next_change(b), nused - 1))

                @pl.when(jnp.logical_and(next_change(b) < nused, ahead < nused))
                def _():
                    start_all(weight_copies(block_expert(ahead), (nexp[0] + 2) % MOE_W_BUFS))

                for cp in weight_copies(e, sl):
                    cp.wait()
                wg_s[...] = wg_f[sl].astype(_bf16)
                wu_s[...] = wu_f[sl].astype(_bf16)
                wd_s[...] = wd_f[sl].astype(_bf16)
                nexp[0] = nexp[0] + 1

            x_lo, x_hi = _unpack_bf16_pair(_load_row_tiles(x_ref, ROW_BLOCK, sub * ROW_BLOCK))
            x_lo, x_hi = x_lo.astype(_bf16), x_hi.astype(_bf16)
            gg = (jnp.dot(x_lo, wg_s[0:half, :], preferred_element_type=_f32)
                  + jnp.dot(x_hi, wg_s[half:D_MODEL, :], preferred_element_type=_f32))
            uu = (jnp.dot(x_lo, wu_s[0:half, :], preferred_element_type=_f32)
                  + jnp.dot(x_hi, wu_s[half:D_MODEL, :], preferred_element_type=_f32))
            a = (_silu(gg) * uu).astype(_bf16)
            y = jnp.dot(a, wd_s[...], preferred_element_type=_f32)
            _store_row_tiles(out_ref, _pack_bf16_pair(y[:, 0:half], y[:, half:D_MODEL]),
                             ROW_BLOCK, sub * ROW_BLOCK)

        @pl.when(b >= nused)
        def _():
            out_ref[sub_rows, :] = jnp.zeros((ROW_BLOCK * ROW_TILE, LANES), out_ref.dtype)

    for sub in range(MOE_GROUP):
        row_block(i * MOE_GROUP + sub, sub)


def _moe(block_expert, n_used, x_rows, w_gate, w_up, w_down):
    n_blocks = x_rows.shape[0] // (ROW_BLOCK * ROW_TILE)
    assert n_blocks % MOE_GROUP == 0
    blk = (MOE_GROUP * ROW_BLOCK * ROW_TILE, LANES)
    omap = lambda i, be, nu: (i, 0)
    grid_spec = pltpu.PrefetchScalarGridSpec(
        num_scalar_prefetch=2,
        grid=(n_blocks // MOE_GROUP,),
        in_specs=[
            pl.BlockSpec(blk, omap),
            pl.BlockSpec(memory_space=pl.ANY),
            pl.BlockSpec(memory_space=pl.ANY),
            pl.BlockSpec(memory_space=pl.ANY),
        ],
        out_specs=pl.BlockSpec(blk, omap),
        scratch_shapes=[
            pltpu.VMEM((MOE_W_BUFS, D_MODEL, EXPERT_FF), _f32),
            pltpu.VMEM((MOE_W_BUFS, D_MODEL, EXPERT_FF), _f32),
            pltpu.VMEM((MOE_W_BUFS, EXPERT_FF, D_MODEL), _f32),
            pltpu.VMEM((D_MODEL, EXPERT_FF), _bf16),
            pltpu.VMEM((D_MODEL, EXPERT_FF), _bf16),
            pltpu.VMEM((EXPERT_FF, D_MODEL), _bf16),
            pltpu.SemaphoreType.DMA((MOE_W_BUFS,)),
            pltpu.SMEM((1,), jnp.int32),
        ],
    )
    return pl.pallas_call(
        functools.partial(_moe_kernel, n_blocks=n_blocks),
        grid_spec=grid_spec,
        out_shape=jax.ShapeDtypeStruct(x_rows.shape, x_rows.dtype),
        compiler_params=pltpu.CompilerParams(
            dimension_semantics=("arbitrary",),
            vmem_limit_bytes=VMEM_LIMIT),
        name="moe_experts",
    )(block_expert, n_used, x_rows, w_gate, w_up, w_down)


TAIL_TM = 256
TAIL_SUB = 256


def _tail_kernel(pos_ref, h_ref, gate_ref, y_hbm, p_ref, wpp_ref, gp_ref, wpg_ref, bpg_ref,
                 gfin_ref, out_ref, gbuf, cbuf, sem, *, n_tok):
    i = pl.program_id(0)
    n_steps = pl.num_programs(0)
    tile = lambda r: pl.ds(pl.multiple_of(r * ROW_TILE, ROW_TILE), ROW_TILE)

    def start_row(base, r):
        for kk in range(TOP_K):
            row = pos_ref[kk * n_tok + base + r]
            pltpu.make_async_copy(y_hbm.at[tile(row), :], gbuf.at[kk, tile(r), :],
                                  sem).start(priority=kk % 2)

    def wait_rows():
        for kk in range(TOP_K):
            pltpu.make_async_copy(y_hbm.at[pl.ds(0, TAIL_TM * ROW_TILE), :], gbuf.at[kk], sem).wait()

    @pl.when(i == 0)
    def _():
        def body(r, carry):
            start_row(0, r)
            return carry

        lax.fori_loop(0, TAIL_TM, body, 0, unroll=8)

    wait_rows()
    cbuf[...] = gbuf[...]

    nxt_base = jnp.minimum(i + 1, n_steps - 1) * TAIL_TM
    for r in range(TAIL_TM):
        start_row(nxt_base, r)

    for s0 in range(0, TAIL_TM, TAIL_SUB):
        rs = slice(s0, s0 + TAIL_SUB)
        gates = gate_ref[rs, :]
        g1, g2 = gates[:, 0:1], gates[:, 1:2]
        y1_lo, y1_hi = _unpack_bf16_pair(_load_row_tiles(cbuf.at[0], TAIL_SUB, s0))
        y2_lo, y2_hi = _unpack_bf16_pair(_load_row_tiles(cbuf.at[1], TAIL_SUB, s0))
        moe = jnp.concatenate([g1 * y1_lo + g2 * y2_lo, g1 * y1_hi + g2 * y2_hi], axis=1)
        h2 = h_ref[rs, :] + moe
        ple = _rms(jnp.dot(p_ref[rs, :].astype(_bf16), wpp_ref[...], preferred_element_type=_f32),
                   gp_ref[...])
        lg = jnp.dot(h2.astype(_bf16), wpg_ref[...], preferred_element_type=_f32) + bpg_ref[...]
        h3 = h2 + (1.0 / (1.0 + jnp.exp(-lg))) * ple
        out_ref[rs, :] = _rms(h3, gfin_ref[...])

    @pl.when(i == n_steps - 1)
    def _():
        wait_rows()


def _tail(pos, h1, gates, y_rows, p2, w_pp, g_ple, w_pg, b_pg, g_fin):
    t = h1.shape[0]
    row = lambda i, ps: (i, 0)
    fix = lambda i, ps: (0, 0)
    grid_spec = pltpu.PrefetchScalarGridSpec(
        num_scalar_prefetch=1,
        grid=(t // TAIL_TM,),
        in_specs=[
            pl.BlockSpec((TAIL_TM, D_MODEL), row),
            pl.BlockSpec((TAIL_TM, LANES), row),
            pl.BlockSpec(memory_space=pl.ANY),
            pl.BlockSpec((TAIL_TM, PLE_DIM), row),
            pl.BlockSpec((PLE_DIM, D_MODEL), fix),
            pl.BlockSpec((1, D_MODEL), fix),
            pl.BlockSpec((D_MODEL, D_MODEL), fix),
            pl.BlockSpec((1, D_MODEL), fix),
            pl.BlockSpec((1, D_MODEL), fix),
        ],
        out_specs=pl.BlockSpec((TAIL_TM, D_MODEL), row),
        scratch_shapes=[
            pltpu.VMEM((TOP_K, TAIL_TM * ROW_TILE, LANES), y_rows.dtype),
            pltpu.VMEM((TOP_K, TAIL_TM * ROW_TILE, LANES), y_rows.dtype),
            pltpu.SemaphoreType.DMA(()),
        ],
    )
    return pl.pallas_call(
        functools.partial(_tail_kernel, n_tok=t),
        grid_spec=grid_spec,
        out_shape=jax.ShapeDtypeStruct((t, D_MODEL), _f32),
        compiler_params=pltpu.CompilerParams(
            dimension_semantics=("arbitrary",),
            vmem_limit_bytes=VMEM_LIMIT),
        name="tail",
    )(pos, h1, gates, y_rows, p2, w_pp, g_ple, w_pg, b_pg, g_fin)


def _inv_freq_row():
    inv_freq = ROPE_THETA ** (-jnp.arange(0, ROT_DIM, 2, dtype=_f32) / ROT_DIM)
    comp = jnp.concatenate([inv_freq, inv_freq, jnp.zeros((DIFF_HEAD_DIM - ROT_DIM,), _f32)])
    return jnp.concatenate([comp, comp]).reshape(1, LANES)


def kernel(x, p, positions, norm_mix_g, w_in, conv_w, conv_b, dt_bias_f, dt_bias_b, a_log_f, a_log_b, d_skip, ssd_norm_g, lam_q1, lam_k1, lam_q2, lam_k2, subln_g, w_out, norm_ffn_g, w_route_group, b_route_group, w_route_expert, b_route_expert, w_exp_gate, w_exp_up, w_exp_down, w_ple_proj, ple_norm_g, w_ple_gate, b_ple_gate, final_norm_g):
    b, seq, d = x.shape
    t = b * seq
    x2 = x.reshape(t, d)
    row = lambda v: v.reshape(1, -1).astype(_f32)

    w_in_t = jnp.swapaxes(w_in[0], 0, 1)
    pos_col = positions.astype(_f32).reshape(t, 1)
    pad_dt = lambda v: jnp.pad(v, (0, DT_PAD - 2 * SSD_HEADS)).reshape(1, DT_PAD)
    dt_bias = pad_dt(jnp.concatenate([dt_bias_f[0], dt_bias_b[0]]))
    a_row = pad_dt(jnp.concatenate([-jnp.exp(a_log_f[0]), -jnp.exp(a_log_b[0])]))
    d_skip_row = jnp.repeat(d_skip[0], SSD_HEADDIM).reshape(1, SSD_W)
    lam_vecs = jnp.stack([lam_q1[0], lam_k1[0], lam_q2[0], lam_k2[0]])
    n_route = MOE_GROUPS + N_EXPERTS
    w_route = jnp.pad(jnp.concatenate([w_route_group[0], w_route_expert[0]], axis=1),
                      ((0, 0), (0, LANES - n_route)))
    w_route_hi = w_route.astype(_bf16)
    w_route_lo = (w_route - w_route_hi.astype(_f32)).astype(_bf16)
    w_route2 = jnp.concatenate([w_route_hi, w_route_lo], axis=1)
    b_route = jnp.pad(jnp.concatenate([b_route_group[0], b_route_expert[0]]),
                      (0, LANES - n_route)).reshape(1, LANES)

    proj, dt, *rot = _inproj(x2, row(norm_mix_g[0]), pos_col, _inv_freq_row(), w_in_t)
    proj3 = proj.reshape(b, seq, MAIN_W)
    y_ssd = _ssd(proj3, dt.reshape(b, seq, DT_PAD), conv_w[0], row(conv_b[0]),
                 dt_bias, a_row, d_skip_row)
    att = _attn(proj3, [r.reshape(b, seq, LANES) for r in rot], lam_vecs, row(subln_g[0]))
    h1, hn_packed, route, cnt, gates = _outproj(
        y_ssd.reshape(t, SSD_W), proj, att.reshape(t, ATTN_W), x2, w_out[0].astype(_bf16),
        row(ssd_norm_g[0]), row(norm_ffn_g[0]), w_route2, b_route)
    pos8, blk8, meta8 = _plan(route, cnt)
    pos = pos8[:TOP_K].reshape(-1)
    block_expert = blk8[0, :_n_rows(t) // ROW_BLOCK]
    n_used = blk8[1, :1]
    x_rows = _dispatch_rows(pos, meta8[0, :N_EXPERTS], meta8[1, :N_EXPERTS], n_used, hn_packed)
    y_rows = _moe(block_expert, n_used, x_rows, w_exp_gate[0], w_exp_up[0], w_exp_down[0])
    out = _tail(pos, h1, gates, y_rows, p[0].reshape(t, PLE_DIM), w_ple_proj[0].astype(_bf16),
                row(ple_norm_g[0]), w_ple_gate[0].astype(_bf16), row(b_ple_gate[0]),
                row(final_norm_g))
    return out.reshape(b, seq, d)
```

```python
import functools

import jax
import jax.numpy as jnp
import numpy as np
from jax import lax
from jax.experimental import pallas as pl
from jax.experimental.pallas import tpu as pltpu

D_MODEL = 2048
PLE_DIM = 256
SSD_W = 1024
ATTN_W = 1024
SSD_HEADDIM = 64
SSD_HEADS = 16
SSD_GROUPS = 2
HEADS_PER_GROUP = SSD_HEADS // SSD_GROUPS
SSD_STATE = 128
CHUNK = 128
CONV_W = 5
XBC_W = SSD_W + 2 * SSD_GROUPS * SSD_STATE
DIFF_HEAD_DIM = 64
DIFF_HEADS = 8
ROT_DIM = 16
ROPE_THETA = 500000.0
MOE_GROUPS = 8
EXPERTS_PER_GROUP = 8
N_EXPERTS = 64
TOP_K = 2
EXPERT_FF = 512
ROW_BLOCK = 128
EPS = 1e-6
LAM_INIT = 0.2

LANES = 128
DT_PAD = LANES
MAIN_W = SSD_W + XBC_W + 3 * ATTN_W
COL_Z = 0
COL_Q = SSD_W
COL_K = COL_Q + ATTN_W
COL_XBC = COL_K + ATTN_W
COL_V = COL_XBC + XBC_W
VMEM_LIMIT = 56 * 1024 * 1024
NEG_BIG = -1e30
Q_SCALE = float(DIFF_HEAD_DIM ** -0.5 * np.log2(np.e))

_f32 = jnp.float32
_bf16 = jnp.bfloat16


def _silu(v):
    return v * (1.0 / (1.0 + jnp.exp(-v)))


def _rms(v, g):
    return v * lax.rsqrt(jnp.mean(v * v, axis=-1, keepdims=True) + EPS) * g


_HI16 = np.uint32(0xFFFF0000)


def _pack_bf16_pair(lo, hi):
    lo_w = lax.bitcast_convert_type(lo.astype(_bf16).astype(_f32), jnp.uint32) >> 16
    hi_w = lax.bitcast_convert_type(hi.astype(_bf16).astype(_f32), jnp.uint32) & _HI16
    return lo_w | hi_w


def _unpack_bf16_pair(w):
    lo = lax.bitcast_convert_type(w << 16, _f32)
    hi = lax.bitcast_convert_type(w & _HI16, _f32)
    return lo, hi


ROW_TILE = 8
assert D_MODEL // 2 == ROW_TILE * LANES


def _store_row_tiles(ref, words, n, row0=0):
    for c in range(ROW_TILE):
        ref[pl.ds(row0 * ROW_TILE + c, n, stride=ROW_TILE), :] = words[:, c * LANES:(c + 1) * LANES]


def _load_row_tiles(ref, n, row0=0):
    return jnp.concatenate(
        [ref[pl.ds(row0 * ROW_TILE + c, n, stride=ROW_TILE), :] for c in range(ROW_TILE)], axis=1)


IN_TM = 1024
IN_TN = 512


def _inproj_kernel(x_ref, g_ref, pos_ref, invf_ref, wt_ref, wdt_ref,
                   out_ref, dt_ref, rc_ref, rp_ref, rm_ref, n_scr):
    j = pl.program_id(1)
    nt = (((1,), (1,)), ((), ()))
    half = ROT_DIM // 2

    @pl.when(j == 0)
    def _():
        n = _rms(x_ref[...], g_ref[...])
        n_scr[...] = n.astype(_bf16)
        lane = lax.broadcasted_iota(jnp.int32, (1, LANES), 1)
        dt = lax.dot_general(n_scr[...], wdt_ref[...].astype(_bf16), nt, preferred_element_type=_f32)
        dt_ref[...] = jnp.where(lane < 2 * SSD_HEADS, dt, 0.0)
        ang = pos_ref[...] * invf_ref[...]
        cos, sin = jnp.cos(ang), jnp.sin(ang)
        l64 = lane & (DIFF_HEAD_DIM - 1)
        rc_ref[...] = jnp.where(l64 < ROT_DIM, cos, 1.0)
        rp_ref[...] = jnp.where(jnp.logical_and(l64 >= half, l64 < ROT_DIM), sin, 0.0)
        rm_ref[...] = jnp.where(l64 < half, -sin, 0.0)

    out_ref[...] = lax.dot_general(n_scr[...], wt_ref[...].astype(_bf16), nt,
                                   preferred_element_type=_f32).astype(_bf16)


def _rotate(t, rc, rp, rm):
    half = ROT_DIM // 2
    return t * rc + pltpu.roll(t, half, 1) * rp + pltpu.roll(t, LANES - half, 1) * rm


SRC_XBC = SSD_W
SRC_DT = SRC_XBC + XBC_W
SRC_Q = SRC_DT + 2 * SSD_HEADS
SRC_V = SRC_Q + 2 * ATTN_W
SRC_ALIGN = 32
assert all(s % SRC_ALIGN == 0 for s in (SRC_XBC, SRC_DT, SRC_Q, SRC_V, IN_TN))


def _src_row(j):
    jq, jx, jv = COL_Q // IN_TN, COL_XBC // IN_TN, COL_V // IN_TN
    row = jnp.where(j < jq, j * IN_TN,
                    jnp.where(j < jx, SRC_Q + (j - jq) * IN_TN,
                              jnp.where(j < jv, SRC_XBC + (j - jx) * IN_TN,
                                        SRC_V + (j - jv) * IN_TN)))
    return pl.multiple_of(row, SRC_ALIGN)


def _inproj(x2, g, pos_col, invf_row, w_in_t):
    t = x2.shape[0]
    grid = (t // IN_TM, MAIN_W // IN_TN)
    return pl.pallas_call(
        _inproj_kernel,
        grid=grid,
        in_specs=[
            pl.BlockSpec((IN_TM, D_MODEL), lambda i, j: (i, 0)),
            pl.BlockSpec((1, D_MODEL), lambda i, j: (0, 0)),
            pl.BlockSpec((IN_TM, 1), lambda i, j: (i, 0)),
            pl.BlockSpec((1, LANES), lambda i, j: (0, 0)),
            pl.BlockSpec((pl.Element(IN_TN), pl.Element(D_MODEL)), lambda i, j: (_src_row(j), 0)),
            pl.BlockSpec((pl.Element(DT_PAD), pl.Element(D_MODEL)), lambda i, j: (SRC_DT, 0)),
        ],
        out_specs=[
            pl.BlockSpec((IN_TM, IN_TN), lambda i, j: (i, j)),
            pl.BlockSpec((IN_TM, DT_PAD), lambda i, j: (i, 0)),
        ] + [pl.BlockSpec((IN_TM, LANES), lambda i, j: (i, 0))] * 3,
        out_shape=[
            jax.ShapeDtypeStruct((t, MAIN_W), _bf16),
            jax.ShapeDtypeStruct((t, DT_PAD), _f32),
        ] + [jax.ShapeDtypeStruct((t, LANES), _f32)] * 3,
        scratch_shapes=[pltpu.VMEM((IN_TM, D_MODEL), _bf16)],
        compiler_params=pltpu.CompilerParams(
            dimension_semantics=("arbitrary", "arbitrary"),
            vmem_limit_bytes=VMEM_LIMIT),
        name="inproj",
    )(x2, g, pos_col, invf_row, w_in_t, w_in_t)


CONV_HALO = 16


def _ssd_kernel(xbc_ref, dt_ref, cw_ref, cb_ref, dtb_ref, a_ref, dsk_ref,
                y_ref, xpad, xact, dts, yacc, state_f, state_b, seq):
    n_chunks = seq // CHUNK

    zeros_halo = jnp.zeros((CONV_HALO, XBC_W), _bf16)
    xpad[0:CONV_HALO, :] = zeros_halo
    xpad[CONV_HALO + seq:CONV_HALO + seq + CONV_HALO, :] = zeros_halo
    xpad[CONV_HALO:CONV_HALO + seq, :] = xbc_ref[0]
    cw = cw_ref[...]
    cb = cb_ref[...]
    win = CHUNK + 2 * CONV_HALO

    def conv_body(c, carry):
        r0 = pl.multiple_of(c * CHUNK, CHUNK)
        blk = xpad[pl.ds(r0, win), :].astype(_f32)
        acc = blk * cw[CONV_W // 2:CONV_W // 2 + 1, :]
        for k in range(CONV_W):
            sh = CONV_W // 2 - k
            if sh == 0:
                continue
            acc = acc + pltpu.roll(blk, sh % win, 0) * cw[k:k + 1, :]
        v = acc[CONV_HALO:CONV_HALO + CHUNK, :] + cb
        xact[pl.ds(r0, CHUNK), :] = _silu(v).astype(_bf16)
        return carry

    lax.fori_loop(0, n_chunks, conv_body, 0)

    raw = dt_ref[0] + dtb_ref[...]
    dts[...] = jnp.maximum(raw, 0.0) + jnp.log1p(jnp.exp(-jnp.abs(raw)))

    a_row = a_ref[...]
    dsk = dsk_ref[...]
    row_i = lax.broadcasted_iota(jnp.int32, (CHUNK, CHUNK), 0)
    col_i = lax.broadcasted_iota(jnp.int32, (CHUNK, CHUNK), 1)

    def chunk_step(c, reverse, state):
        r0 = pl.multiple_of(c * CHUNK, CHUNK)
        dtc = dts[pl.ds(r0, CHUNK), :]
        cs = dtc * a_row
        k = 1
        while k < CHUNK:
            if reverse:
                cs = cs + jnp.where(row_i < CHUNK - k, pltpu.roll(cs, CHUNK - k, 0), 0.0)
            else:
                cs = cs + jnp.where(row_i >= k, pltpu.roll(cs, k, 0), 0.0)
            k *= 2
        cs_t = cs.T
        dt_t = dtc.T
        end_col = cs_t[:, 0:1] if reverse else cs_t[:, CHUNK - 1:CHUNK]
        w_t = jnp.exp(end_col - cs_t) * dt_t
        end_row = cs[0:1, :] if reverse else cs[CHUNK - 1:CHUNK, :]
        dec_row = jnp.exp(end_row)
        tri = (row_i <= col_i) if reverse else (row_i >= col_i)
        lane0 = SSD_HEADS if reverse else 0
        first_head = col_i < SSD_HEADDIM
        rows = pl.ds(r0, CHUNK)
        for g in range(SSD_GROUPS):
            b_g = xact[rows, SSD_W + g * SSD_STATE:SSD_W + (g + 1) * SSD_STATE]
            c_g = xact[rows, SSD_W + (SSD_GROUPS + g) * SSD_STATE:SSD_W + (SSD_GROUPS + g + 1) * SSD_STATE]
            cbm = lax.dot_general(c_g, b_g, (((1,), (1,)), ((), ())),
                                  preferred_element_type=_f32)
            b_t = b_g.astype(_f32).T
            for pp in range(HEADS_PER_GROUP // 2):
                pair = g * (HEADS_PER_GROUP // 2) + pp
                cols = slice(pair * LANES, (pair + 1) * LANES)
                xs_p = xact[rows, cols]
                st = state[pair]
                off = jnp.dot(c_g, st.astype(_bf16), preferred_element_type=_f32)
                halves = []
                for hh in range(2):
                    ln = lane0 + 2 * pair + hh
                    colb = jnp.broadcast_to(cs[:, ln:ln + 1], (CHUNK, CHUNK))
                    m_h = (cbm * jnp.exp(jnp.where(tri, colb - cs_t[ln:ln + 1, :], NEG_BIG))
                           * dt_t[ln:ln + 1, :])
                    y_h = (jnp.dot(m_h.astype(_bf16), xs_p, preferred_element_type=_f32)
                           + jnp.exp(colb) * off)
                    upd = jnp.dot((b_t * w_t[ln:ln + 1, :]).astype(_bf16), xs_p,
                                  preferred_element_type=_f32)
                    dec = jnp.broadcast_to(dec_row[:, ln:ln + 1], (CHUNK, CHUNK))
                    halves.append((y_h, upd, dec))
                y_p = jnp.where(first_head, halves[0][0], halves[1][0])
                upd_p = jnp.where(first_head, halves[0][1], halves[1][1])
                dec_p = jnp.where(first_head, halves[0][2], halves[1][2])
                state[pair] = st * dec_p + upd_p
                if reverse:
                    y_ref[0, rows, cols] = (yacc[rows, cols] + y_p).astype(_bf16)
                else:
                    yacc[rows, cols] = y_p + xs_p.astype(_f32) * dsk[:, cols]

    state_f[...] = jnp.zeros_like(state_f)
    state_b[...] = jnp.zeros_like(state_b)

    def fwd_body(c, carry):
        chunk_step(c, False, state_f)
        return carry

    lax.fori_loop(0, n_chunks, fwd_body, 0)

    def bwd_body(i, carry):
        chunk_step(n_chunks - 1 - i, True, state_b)
        return carry

    lax.fori_loop(0, n_chunks, bwd_body, 0)


def _ssd(proj3, dt3, conv_w, conv_b, dt_bias, a_row, d_skip_row):
    b, seq, _ = proj3.shape
    kern = functools.partial(_ssd_kernel, seq=seq)
    assert COL_XBC % XBC_W == 0
    return pl.pallas_call(
        kern,
        grid=(b,),
        in_specs=[
            pl.BlockSpec((1, seq, XBC_W), lambda i: (i, 0, COL_XBC // XBC_W)),
            pl.BlockSpec((1, seq, DT_PAD), lambda i: (i, 0, 0)),
            pl.BlockSpec((CONV_W, XBC_W), lambda i: (0, 0)),
            pl.BlockSpec((1, XBC_W), lambda i: (0, 0)),
            pl.BlockSpec((1, DT_PAD), lambda i: (0, 0)),
            pl.BlockSpec((1, DT_PAD), lambda i: (0, 0)),
            pl.BlockSpec((1, SSD_W), lambda i: (0, 0)),
        ],
        out_specs=pl.BlockSpec((1, seq, SSD_W), lambda i: (i, 0, 0)),
        out_shape=jax.ShapeDtypeStruct((b, seq, SSD_W), _bf16),
        scratch_shapes=[
            pltpu.VMEM((seq + 2 * CONV_HALO, XBC_W), _bf16),
            pltpu.VMEM((seq, XBC_W), _bf16),
            pltpu.VMEM((seq, DT_PAD), _f32),
            pltpu.VMEM((seq, SSD_W), _f32),
            pltpu.VMEM((SSD_HEADS // 2, SSD_STATE, 2 * SSD_HEADDIM), _f32),
            pltpu.VMEM((SSD_HEADS // 2, SSD_STATE, 2 * SSD_HEADDIM), _f32),
        ],
        compiler_params=pltpu.CompilerParams(
            dimension_semantics=("arbitrary",),
            vmem_limit_bytes=VMEM_LIMIT),
        name="ssd",
    )(proj3, dt3, conv_w, conv_b, dt_bias, a_row, d_skip_row)


ATT_QB = 256
ATT_HEADS = 2


def _attn_kernel(q_ref, k_ref, v_ref, rc_ref, rp_ref, rm_ref, lam_ref, g_ref, o_ref,
                 k_scr, v1_scr, *, seq):
    hw = 2 * DIFF_HEAD_DIM
    lv = lam_ref[...]
    lam = (jnp.exp(jnp.sum(lv[0:1] * lv[1:2], axis=-1, keepdims=True))
           - jnp.exp(jnp.sum(lv[2:3] * lv[3:4], axis=-1, keepdims=True)) + LAM_INIT)

    def rotated(ref, rows, cols):
        return _rotate(ref[0, rows, cols].astype(_f32), rc_ref[0, rows, :], rp_ref[0, rows, :],
                       rm_ref[0, rows, :])

    g = g_ref[...]
    first = lax.broadcasted_iota(jnp.int32, (1, hw), 1) < DIFF_HEAD_DIM
    ones_col = (lax.broadcasted_iota(jnp.int32, (seq, hw), 1) == 0).astype(_bf16)
    nt = (((1,), (1,)), ((), ()))

    def prepare(hd):
        cols = slice(hd * hw, (hd + 1) * hw)
        for c in range(seq // ATT_QB):
            rows = slice(c * ATT_QB, (c + 1) * ATT_QB)
            k_scr[hd, rows, :] = rotated(k_ref, rows, cols).astype(_bf16)
        v1_scr[hd, :, 0:hw] = v_ref[0, :, cols]
        v1_scr[hd, :, hw:2 * hw] = ones_col

    def chain(hd, r0):
        cols = slice(hd * hw, (hd + 1) * hw)
        rows = slice(r0, r0 + ATT_QB)
        q = (rotated(q_ref, rows, cols) * Q_SCALE).astype(_bf16)
        zero = jnp.zeros_like(q)
        ovs = []
        for qc in (jnp.where(first, q, zero), jnp.where(first, zero, q)):
            s = lax.dot_general(qc, k_scr[hd], nt, preferred_element_type=_f32)
            e = jnp.exp2(s - jnp.max(s, axis=-1, keepdims=True)).astype(_bf16)
            ovs.append(jnp.dot(e, v1_scr[hd], preferred_element_type=_f32))
        o1, l1 = ovs[0][:, 0:hw], ovs[0][:, hw:hw + 1]
        o2, l2 = ovs[1][:, 0:hw], ovs[1][:, hw:hw + 1]
        o = o1 / l1 - (lam / l2) * o2
        o = _rms(o, g) * (1.0 - LAM_INIT)
        o_ref[0, rows, cols] = o.astype(_bf16)

    for hd in range(ATT_HEADS):
        prepare(hd)
    for hd in range(ATT_HEADS):
        for c in range(seq // ATT_QB):
            chain(hd, c * ATT_QB)


def _attn(proj3, rot_tables, lam_vecs, subln_g):
    b, seq, _ = proj3.shape
    hw = 2 * DIFF_HEAD_DIM
    gw = ATT_HEADS * hw
    assert COL_Q % gw == 0 and COL_K % gw == 0 and COL_V % gw == 0
    kern = functools.partial(_attn_kernel, seq=seq)
    return pl.pallas_call(
        kern,
        grid=(b, DIFF_HEADS // ATT_HEADS),
        in_specs=[
            pl.BlockSpec((1, seq, gw), lambda i, h: (i, 0, COL_Q // gw + h)),
            pl.BlockSpec((1, seq, gw), lambda i, h: (i, 0, COL_K // gw + h)),
            pl.BlockSpec((1, seq, gw), lambda i, h: (i, 0, COL_V // gw + h)),
        ] + [pl.BlockSpec((1, seq, LANES), lambda i, h: (i, 0, 0))] * 3 + [
            pl.BlockSpec((4, DIFF_HEAD_DIM), lambda i, h: (0, 0)),
            pl.BlockSpec((1, hw), lambda i, h: (0, 0)),
        ],
        out_specs=pl.BlockSpec((1, seq, gw), lambda i, h: (i, 0, h)),
        out_shape=jax.ShapeDtypeStruct((b, seq, ATTN_W), _bf16),
        scratch_shapes=[pltpu.VMEM((ATT_HEADS, seq, hw), _bf16),
                        pltpu.VMEM((ATT_HEADS, seq, 2 * hw), _bf16)],
        compiler_params=pltpu.CompilerParams(
            dimension_semantics=("arbitrary", "arbitrary"),
            vmem_limit_bytes=VMEM_LIMIT),
        name="diffattn",
    )(proj3, proj3, proj3, *rot_tables, lam_vecs, subln_g)


OUT_TM = 512
ROUTE_ROWS = 8


def _outproj_kernel(y_ref, z_ref, att_ref, x_ref, w_ref, gs_ref, gf_ref, wr_ref, br_ref,
                    h_ref, hn_ref, route_ref, cnt_ref, gate_ref, cnt_scr):
    @pl.when(pl.program_id(0) == 0)
    def _():
        cnt_scr[...] = jnp.zeros_like(cnt_scr)

    y = y_ref[...].astype(_f32)
    z = z_ref[...].astype(_f32)
    s = _rms(y * _silu(z), gs_ref[...]).astype(_bf16)
    acc = jnp.dot(s, w_ref[0:SSD_W, :], preferred_element_type=_f32)
    acc = acc + jnp.dot(att_ref[...], w_ref[SSD_W:SSD_W + ATTN_W, :], preferred_element_type=_f32)
    h1 = x_ref[...] + acc
    h_ref[...] = h1
    hn = _rms(h1, gf_ref[...])
    half = D_MODEL // 2
    _store_row_tiles(hn_ref, _pack_bf16_pair(hn[:, 0:half], hn[:, half:D_MODEL]), hn.shape[0])
    hn_hi = hn.astype(_bf16)
    hn_lo = (hn - hn_hi.astype(_f32)).astype(_bf16)
    a = jnp.dot(hn_hi, wr_ref[...], preferred_element_type=_f32)
    bb = jnp.dot(hn_lo, wr_ref[:, 0:LANES], preferred_element_type=_f32)
    logits = (a[:, 0:LANES] + a[:, LANES:2 * LANES] + bb + br_ref[...]).T
    tm = logits.shape[1]
    iota = lax.broadcasted_iota(jnp.int32, (MOE_GROUPS, tm), 0)

    def first_argmax(val, vmax):
        return jnp.min(jnp.where(val == vmax, iota, MOE_GROUPS), axis=0, keepdims=True)

    gl = logits[0:MOE_GROUPS]
    gmax = jnp.max(gl, axis=0, keepdims=True)
    p_g = 1.0 / jnp.sum(jnp.exp(gl - gmax), axis=0, keepdims=True)
    g_sel = first_argmax(gl, gmax)
    el = jnp.zeros((EXPERTS_PER_GROUP, tm), _f32)
    for gi in range(MOE_GROUPS):
        lo = MOE_GROUPS + gi * EXPERTS_PER_GROUP
        el = jnp.where(g_sel == gi, logits[lo:lo + EXPERTS_PER_GROUP], el)
    ee = jnp.exp(el - jnp.max(el, axis=0, keepdims=True))
    pe = ee / jnp.sum(ee, axis=0, keepdims=True)
    p1 = jnp.max(pe, axis=0, keepdims=True)
    i1 = first_argmax(pe, p1)
    pe2 = jnp.where(iota == i1, -1.0, pe)
    p2 = jnp.max(pe2, axis=0, keepdims=True)
    i2 = first_argmax(pe2, p2)
    den = p1 + p2
    e1 = g_sel * EXPERTS_PER_GROUP + i1
    e2 = g_sel * EXPERTS_PER_GROUP + i2
    iota_e = lax.broadcasted_iota(jnp.int32, (N_EXPERTS, tm), 0)
    oh1 = (iota_e == e1).astype(_f32)
    oh2 = (iota_e == e2).astype(_f32)
    both = oh1 + oh2
    earlier = (lax.broadcasted_iota(jnp.int32, (tm, tm), 0)
               < lax.broadcasted_iota(jnp.int32, (tm, tm), 1)).astype(_bf16)
    before = cnt_scr[:, 0:1] + jnp.dot(both.astype(_bf16), earlier,
                                       preferred_element_type=_f32)
    r1 = jnp.sum(oh1 * before, axis=0, keepdims=True).astype(jnp.int32)
    r2 = jnp.sum(oh2 * before, axis=0, keepdims=True).astype(jnp.int32)
    cnt_scr[...] = cnt_scr[...] + jnp.sum(both, axis=1, keepdims=True)
    cnt_ref[...] = cnt_scr[...].astype(jnp.int32)
    route_ref[...] = jnp.where(iota == 0, e1, jnp.where(iota == 1, e2,
                               jnp.where(iota == 2, r1, jnp.where(iota == 3, r2, 0))))
    g8 = jnp.where(iota == 0, p_g * (p1 / den), jnp.where(iota == 1, p_g * (p2 / den), 0.0))
    gate_ref[...] = jnp.concatenate(
        [g8, jnp.zeros((LANES - ROUTE_ROWS, tm), _f32)], axis=0).T


def _outproj(y2, proj2, att2, x2, w_out, g_ssd, g_ffn, wr_t, br_col):
    t = x2.shape[0]
    row = lambda i: (i, 0)
    fix = lambda i: (0, 0)
    return pl.pallas_call(
        _outproj_kernel,
        grid=(t // OUT_TM,),
        in_specs=[
            pl.BlockSpec((OUT_TM, SSD_W), row),
            pl.BlockSpec((OUT_TM, SSD_W), row),
            pl.BlockSpec((OUT_TM, ATTN_W), row),
            pl.BlockSpec((OUT_TM, D_MODEL), row),
            pl.BlockSpec((SSD_W + ATTN_W, D_MODEL), fix),
            pl.BlockSpec((1, SSD_W), fix),
            pl.BlockSpec((1, D_MODEL), fix),
            pl.BlockSpec((D_MODEL, 2 * LANES), fix),
            pl.BlockSpec((1, LANES), fix),
        ],
        out_specs=[
            pl.BlockSpec((OUT_TM, D_MODEL), row),
            pl.BlockSpec((OUT_TM * ROW_TILE, LANES), row),
            pl.BlockSpec((ROUTE_ROWS, OUT_TM), lambda i: (0, i)),
            pl.BlockSpec((N_EXPERTS, LANES), fix),
            pl.BlockSpec((OUT_TM, LANES), row),
        ],
        out_shape=[
            jax.ShapeDtypeStruct((t, D_MODEL), _f32),
            jax.ShapeDtypeStruct((t * ROW_TILE, LANES), jnp.uint32),
            jax.ShapeDtypeStruct((ROUTE_ROWS, t), jnp.int32),
            jax.ShapeDtypeStruct((N_EXPERTS, LANES), jnp.int32),
            jax.ShapeDtypeStruct((t, LANES), _f32),
        ],
        scratch_shapes=[pltpu.VMEM((N_EXPERTS, LANES), _f32)],
        compiler_params=pltpu.CompilerParams(
            dimension_semantics=("arbitrary",),
            vmem_limit_bytes=VMEM_LIMIT),
        name="outproj_router",
    )(y2, proj2, att2, x2, w_out, g_ssd, g_ffn, wr_t, br_col)


def _n_rows(n_tok):
    n_assign = n_tok * TOP_K
    return (n_assign + N_EXPERTS * (ROW_BLOCK - 1) + ROW_BLOCK - 1) // ROW_BLOCK * ROW_BLOCK


def _prefix_sum(v, axis):
    n = v.shape[axis]
    idx = lax.broadcasted_iota(jnp.int32, v.shape, axis)
    k = 1
    while k < n:
        v = v + jnp.where(idx >= k, pltpu.roll(v, k, axis), 0)
        k *= 2
    return v


def _plan_kernel(route_ref, cnt_ref, pos_ref, blk_ref, meta_ref, *, n_blocks_pad):
    n_tok = route_ref.shape[1]
    cnt_col = jnp.concatenate(
        [cnt_ref[...], jnp.zeros((LANES - N_EXPERTS, LANES), jnp.int32)], axis=0)
    pad_up = lambda c: (c + (ROW_BLOCK - 1)) & (-ROW_BLOCK)
    ends_col = _prefix_sum(pad_up(cnt_col), 0)
    starts_col = (ends_col - pad_up(cnt_col)).astype(_f32)
    cnt_row = cnt_col.astype(_f32).T.astype(jnp.int32)
    ends_row = _prefix_sum(pad_up(cnt_row), 1)
    starts_row = ends_row - pad_up(cnt_row)
    n_used = ends_row[0:1, LANES - 1:LANES] >> (ROW_BLOCK.bit_length() - 1)

    ch = 1024
    iota_e = lax.broadcasted_iota(jnp.int32, (LANES, ch), 0)
    for c0 in range(0, n_tok, ch):
        rt = route_ref[:, c0:c0 + ch]
        s1 = jnp.sum(jnp.where(iota_e == rt[0:1], starts_col[:, 0:1], 0.0), axis=0, keepdims=True)
        s2 = jnp.sum(jnp.where(iota_e == rt[1:2], starts_col[:, 0:1], 0.0), axis=0, keepdims=True)
        p1 = s1.astype(jnp.int32) + rt[2:3]
        p2 = s2.astype(jnp.int32) + rt[3:4]
        sub = lax.broadcasted_iota(jnp.int32, (ROUTE_ROWS, ch), 0)
        pos_ref[:, c0:c0 + ch] = jnp.where(sub == 0, p1, jnp.where(sub == 1, p2, 0))

    blk_start = lax.broadcasted_iota(jnp.int32, (LANES, n_blocks_pad), 1) * ROW_BLOCK
    e_idx = lax.broadcasted_iota(jnp.int32, (LANES, n_blocks_pad), 0)
    real = e_idx < N_EXPERTS
    be = jnp.sum(jnp.where(jnp.logical_and(real, ends_col[:, 0:1] <= blk_start), 1.0, 0.0),
                 axis=0, keepdims=True).astype(jnp.int32)
    be = jnp.minimum(be, N_EXPERTS - 1)
    last = jnp.max(jnp.where(jnp.logical_and(real, cnt_col[:, 0:1] > 0), e_idx.astype(_f32), 0.0),
                   axis=0, keepdims=True).astype(jnp.int32)
    blk_i = lax.broadcasted_iota(jnp.int32, (1, n_blocks_pad), 1)
    be = jnp.where(blk_i < n_used, be, last)
    sub = lax.broadcasted_iota(jnp.int32, (ROUTE_ROWS, n_blocks_pad), 0)
    blk_ref[...] = jnp.where(sub == 0, be, jnp.where(sub == 1, n_used, 0))
    sub = lax.broadcasted_iota(jnp.int32, (ROUTE_ROWS, LANES), 0)
    meta_ref[...] = jnp.where(sub == 0, cnt_row[0:1], jnp.where(sub == 1, starts_row[0:1], 0))


def _plan(route, cnt):
    n_tok = route.shape[1]
    n_blocks_pad = -(-(_n_rows(n_tok) // ROW_BLOCK) // LANES) * LANES
    return pl.pallas_call(
        functools.partial(_plan_kernel, n_blocks_pad=n_blocks_pad),
        out_shape=[
            jax.ShapeDtypeStruct((ROUTE_ROWS, n_tok), jnp.int32),
            jax.ShapeDtypeStruct((ROUTE_ROWS, n_blocks_pad), jnp.int32),
            jax.ShapeDtypeStruct((ROUTE_ROWS, LANES), jnp.int32),
        ],
        compiler_params=pltpu.CompilerParams(vmem_limit_bytes=VMEM_LIMIT),
        name="route_plan",
    )(route, cnt)


DISP_TM = 512


def _dispatch_kernel(pos_ref, cnt_ref, start_ref, nused_ref, h_ref, x_hbm, stage, zrow, sem, zsem,
                     *, n_tok, n_blocks):
    i = pl.program_id(0)
    n_steps = pl.num_programs(0)
    slot = i % 2
    base = i * DISP_TM
    tile = lambda r: pl.ds(pl.multiple_of(r * ROW_TILE, ROW_TILE), ROW_TILE)

    def wait_slot(sl):
        for _ in range(TOP_K):
            pltpu.make_async_copy(stage.at[sl], x_hbm.at[pl.ds(0, DISP_TM * ROW_TILE), :],
                                  sem.at[sl]).wait()

    @pl.when(i >= 2)
    def _():
        wait_slot(slot)

    stage[slot] = h_ref[...]

    def body(r, carry):
        for kk in range(TOP_K):
            row = pos_ref[kk * n_tok + base + r]
            pltpu.make_async_copy(stage.at[slot, tile(r), :], x_hbm.at[tile(row), :],
                                  sem.at[slot]).start(priority=kk % 2)
        return carry

    lax.fori_loop(0, DISP_TM, body, 0, unroll=8)

    @pl.when(i == n_steps - 1)
    def _():
        zrow[...] = jnp.zeros_like(zrow)

        pieces = [1 << b for b in reversed(range(ROW_BLOCK.bit_length() - 1))]

        def pad_copy(size, row0):
            src = zrow.at[pl.ds(0, size * ROW_TILE), :]
            dst = x_hbm.at[pl.ds(pl.multiple_of(row0 * ROW_TILE, ROW_TILE), size * ROW_TILE), :]
            return pltpu.make_async_copy(src, dst, zsem)

        def per_expert(e, carry, wait):
            cnt = cnt_ref[e]
            n_pad = ((cnt + (ROW_BLOCK - 1)) & (-ROW_BLOCK)) - cnt
            row0 = start_ref[e] + cnt
            for size in pieces:
                has = (n_pad & size) != 0

                @pl.when(has)
                def _():
                    cp = pad_copy(size, row0)
                    cp.wait() if wait else cp.start()

                row0 = row0 + jnp.where(has, size, 0)
            return carry

        lax.fori_loop(0, N_EXPERTS, functools.partial(per_expert, wait=False), 0)
        blk_rows = ROW_BLOCK * ROW_TILE

        def fill_block(blk, carry):
            dst = pl.ds(pl.multiple_of(blk * blk_rows, blk_rows), blk_rows)
            pltpu.make_async_copy(zrow, x_hbm.at[dst, :], zsem).start()
            return carry

        lax.fori_loop(nused_ref[0], n_blocks, fill_block, 0)

        lax.fori_loop(0, N_EXPERTS, functools.partial(per_expert, wait=True), 0)

        def wait_block(blk, carry):
            pltpu.make_async_copy(zrow, x_hbm.at[pl.ds(0, blk_rows), :], zsem).wait()
            return carry

        lax.fori_loop(nused_ref[0], n_blocks, wait_block, 0)

        @pl.when(n_steps >= 2)
        def _():
            wait_slot(1 - slot)

        wait_slot(slot)


def _dispatch_rows(pos, counts, starts, n_used, hn_packed):
    t = hn_packed.shape[0] // ROW_TILE
    n_rows = _n_rows(t)
    blk = (DISP_TM * ROW_TILE, LANES)
    grid_spec = pltpu.PrefetchScalarGridSpec(
        num_scalar_prefetch=4,
        grid=(t // DISP_TM,),
        in_specs=[pl.BlockSpec(blk, lambda i, *_: (i, 0))],
        out_specs=pl.BlockSpec(memory_space=pl.ANY),
        scratch_shapes=[
            pltpu.VMEM((2,) + blk, hn_packed.dtype),
            pltpu.VMEM((ROW_BLOCK * ROW_TILE, LANES), hn_packed.dtype),
            pltpu.SemaphoreType.DMA((2,)),
            pltpu.SemaphoreType.DMA(()),
        ],
    )
    return pl.pallas_call(
        functools.partial(_dispatch_kernel, n_tok=t, n_blocks=n_rows // ROW_BLOCK),
        grid_spec=grid_spec,
        out_shape=jax.ShapeDtypeStruct((n_rows * ROW_TILE, LANES), hn_packed.dtype),
        compiler_params=pltpu.CompilerParams(
            dimension_semantics=("arbitrary",),
            vmem_limit_bytes=VMEM_LIMIT),
        name="dispatch_rows",
    )(pos, counts, starts, n_used, hn_packed)


MOE_W_SPLIT = 4
MOE_W_BUFS = 3
MOE_GROUP = 4


def _moe_kernel(be_ref, nused_ref, x_ref, wg_hbm, wu_hbm, wd_hbm,
                out_ref, wg_f, wu_f, wd_f, wg_s, wu_s, wd_s, wsem, nexp, *, n_blocks):
    i = pl.program_id(0)
    nused = nused_ref[0]
    half = D_MODEL // 2

    def weight_copies(e, sl):
        cps = []
        for src, dst in ((wg_hbm, wg_f), (wu_hbm, wu_f), (wd_hbm, wd_f)):
            rows = src.shape[1] // MOE_W_SPLIT
            for c in range(MOE_W_SPLIT):
                rs = pl.ds(c * rows, rows)
                cps.append(pltpu.make_async_copy(src.at[e, rs, :], dst.at[sl, rs, :], wsem.at[sl]))
        return cps

    def start_all(cps):
        for n, cp in enumerate(cps):
            cp.start(priority=n % 2)

    def block_expert(b):
        return be_ref[jnp.minimum(b, n_blocks - 1)]

    def next_change(b0):
        e0 = block_expert(b0)
        return lax.while_loop(
            lambda b: jnp.logical_and(b < nused, block_expert(b) == e0), lambda b: b + 1, b0 + 1)

    @pl.when(i == 0)
    def _():
        nexp[0] = 0
        start_all(weight_copies(be_ref[0], 0))
        nxt = next_change(0)

        @pl.when(nxt < nused)
        def _():
            start_all(weight_copies(block_expert(nxt), 1))

    def row_block(b, sub):
        sub_rows = pl.ds(sub * ROW_BLOCK * ROW_TILE, ROW_BLOCK * ROW_TILE)

        @pl.when(b < nused)
        def _():
            e = be_ref[b]
            new_expert = jnp.logical_or(b == 0, e != be_ref[jnp.maximum(b - 1, 0)])

            @pl.when(new_expert)
            def _():
                sl = nexp[0] % MOE_W_BUFS
                ahead = next_change(jnp.minimum(next_change(b), nused - 1))

                @pl.when(jnp.logical_and(next_change(b) < nused, ahead < nused))
                def _():
                    start_all(weight_copies(block_expert(ahead), (nexp[0] + 2) % MOE_W_BUFS))

                for cp in weight_copies(e, sl):
                    cp.wait()
                wg_s[...] = wg_f[sl].astype(_bf16)
                wu_s[...] = wu_f[sl].astype(_bf16)
                wd_s[...] = wd_f[sl].astype(_bf16)
                nexp[0] = nexp[0] + 1

            x_lo, x_hi = _unpack_bf16_pair(_load_row_tiles(x_ref, ROW_BLOCK, sub * ROW_BLOCK))
            x_lo, x_hi = x_lo.astype(_bf16), x_hi.astype(_bf16)
            gg = (jnp.dot(x_lo, wg_s[0:half, :], preferred_element_type=_f32)
                  + jnp.dot(x_hi, wg_s[half:D_MODEL, :], preferred_element_type=_f32))
            uu = (jnp.dot(x_lo, wu_s[0:half, :], preferred_element_type=_f32)
                  + jnp.dot(x_hi, wu_s[half:D_MODEL, :], preferred_element_type=_f32))
            a = (_silu(gg) * uu).astype(_bf16)
            y = jnp.dot(a, wd_s[...], preferred_element_type=_f32)
            _store_row_tiles(out_ref, _pack_bf16_pair(y[:, 0:half], y[:, half:D_MODEL]),
                             ROW_BLOCK, sub * ROW_BLOCK)

        @pl.when(b >= nused)
        def _():
            out_ref[sub_rows, :] = jnp.zeros((ROW_BLOCK * ROW_TILE, LANES), out_ref.dtype)

    for sub in range(MOE_GROUP):
        row_block(i * MOE_GROUP + sub, sub)


def _moe(block_expert, n_used, x_rows, w_gate, w_up, w_down):
    n_blocks = x_rows.shape[0] // (ROW_BLOCK * ROW_TILE)
    assert n_blocks % MOE_GROUP == 0
    blk = (MOE_GROUP * ROW_BLOCK * ROW_TILE, LANES)
    omap = lambda i, be, nu: (i, 0)
    grid_spec = pltpu.PrefetchScalarGridSpec(
        num_scalar_prefetch=2,
        grid=(n_blocks // MOE_GROUP,),
        in_specs=[
            pl.BlockSpec(blk, omap),
            pl.BlockSpec(memory_space=pl.ANY),
            pl.BlockSpec(memory_space=pl.ANY),
            pl.BlockSpec(memory_space=pl.ANY),
        ],
        out_specs=pl.BlockSpec(blk, omap),
        scratch_shapes=[
            pltpu.VMEM((MOE_W_BUFS, D_MODEL, EXPERT_FF), _f32),
            pltpu.VMEM((MOE_W_BUFS, D_MODEL, EXPERT_FF), _f32),
            pltpu.VMEM((MOE_W_BUFS, EXPERT_FF, D_MODEL), _f32),
            pltpu.VMEM((D_MODEL, EXPERT_FF), _bf16),
            pltpu.VMEM((D_MODEL, EXPERT_FF), _bf16),
            pltpu.VMEM((EXPERT_FF, D_MODEL), _bf16),
            pltpu.SemaphoreType.DMA((MOE_W_BUFS,)),
            pltpu.SMEM((1,), jnp.int32),
        ],
    )
    return pl.pallas_call(
        functools.partial(_moe_kernel, n_blocks=n_blocks),
        grid_spec=grid_spec,
        out_shape=jax.ShapeDtypeStruct(x_rows.shape, x_rows.dtype),
        compiler_params=pltpu.CompilerParams(
            dimension_semantics=("arbitrary",),
            vmem_limit_bytes=VMEM_LIMIT),
        name="moe_experts",
    )(block_expert, n_used, x_rows, w_gate, w_up, w_down)


TAIL_TM = 256
TAIL_SUB = 256


def _tail_kernel(pos_ref, h_ref, gate_ref, y_hbm, p_ref, wpp_ref, gp_ref, wpg_ref, bpg_ref,
                 gfin_ref, out_ref, gbuf, cbuf, sem, *, n_tok):
    i = pl.program_id(0)
    n_steps = pl.num_programs(0)
    tile = lambda r: pl.ds(pl.multiple_of(r * ROW_TILE, ROW_TILE), ROW_TILE)

    def start_row(base, r):
        for kk in range(TOP_K):
            row = pos_ref[kk * n_tok + base + r]
            pltpu.make_async_copy(y_hbm.at[tile(row), :], gbuf.at[kk, tile(r), :],
                                  sem).start(priority=kk % 2)

    def wait_rows():
        for kk in range(TOP_K):
            pltpu.make_async_copy(y_hbm.at[pl.ds(0, TAIL_TM * ROW_TILE), :], gbuf.at[kk], sem).wait()

    @pl.when(i == 0)
    def _():
        def body(r, carry):
            start_row(0, r)
            return carry

        lax.fori_loop(0, TAIL_TM, body, 0, unroll=8)

    wait_rows()
    cbuf[...] = gbuf[...]

    nxt_base = jnp.minimum(i + 1, n_steps - 1) * TAIL_TM
    for r in range(TAIL_TM):
        start_row(nxt_base, r)

    for s0 in range(0, TAIL_TM, TAIL_SUB):
        rs = slice(s0, s0 + TAIL_SUB)
        gates = gate_ref[rs, :]
        g1, g2 = gates[:, 0:1], gates[:, 1:2]
        y1_lo, y1_hi = _unpack_bf16_pair(_load_row_tiles(cbuf.at[0], TAIL_SUB, s0))
        y2_lo, y2_hi = _unpack_bf16_pair(_load_row_tiles(cbuf.at[1], TAIL_SUB, s0))
        moe = jnp.concatenate([g1 * y1_lo + g2 * y2_lo, g1 * y1_hi + g2 * y2_hi], axis=1)
        h2 = h_ref[rs, :] + moe
        ple = _rms(jnp.dot(p_ref[rs, :].astype(_bf16), wpp_ref[...], preferred_element_type=_f32),
                   gp_ref[...])
        lg = jnp.dot(h2.astype(_bf16), wpg_ref[...], preferred_element_type=_f32) + bpg_ref[...]
        h3 = h2 + (1.0 / (1.0 + jnp.exp(-lg))) * ple
        out_ref[rs, :] = _rms(h3, gfin_ref[...])

    @pl.when(i == n_steps - 1)
    def _():
        wait_rows()


def _tail(pos, h1, gates, y_rows, p2, w_pp, g_ple, w_pg, b_pg, g_fin):
    t = h1.shape[0]
    row = lambda i, ps: (i, 0)
    fix = lambda i, ps: (0, 0)
    grid_spec = pltpu.PrefetchScalarGridSpec(
        num_scalar_prefetch=1,
        grid=(t // TAIL_TM,),
        in_specs=[
            pl.BlockSpec((TAIL_TM, D_MODEL), row),
            pl.BlockSpec((TAIL_TM, LANES), row),
            pl.BlockSpec(memory_space=pl.ANY),
            pl.BlockSpec((TAIL_TM, PLE_DIM), row),
            pl.BlockSpec((PLE_DIM, D_MODEL), fix),
            pl.BlockSpec((1, D_MODEL), fix),
            pl.BlockSpec((D_MODEL, D_MODEL), fix),
            pl.BlockSpec((1, D_MODEL), fix),
            pl.BlockSpec((1, D_MODEL), fix),
        ],
        out_specs=pl.BlockSpec((TAIL_TM, D_MODEL), row),
        scratch_shapes=[
            pltpu.VMEM((TOP_K, TAIL_TM * ROW_TILE, LANES), y_rows.dtype),
            pltpu.VMEM((TOP_K, TAIL_TM * ROW_TILE, LANES), y_rows.dtype),
            pltpu.SemaphoreType.DMA(()),
        ],
    )
    return pl.pallas_call(
        functools.partial(_tail_kernel, n_tok=t),
        grid_spec=grid_spec,
        out_shape=jax.ShapeDtypeStruct((t, D_MODEL), _f32),
        compiler_params=pltpu.CompilerParams(
            dimension_semantics=("arbitrary",),
            vmem_limit_bytes=VMEM_LIMIT),
        name="tail",
    )(pos, h1, gates, y_rows, p2, w_pp, g_ple, w_pg, b_pg, g_fin)


def _inv_freq_row():
    inv_freq = ROPE_THETA ** (-jnp.arange(0, ROT_DIM, 2, dtype=_f32) / ROT_DIM)
    comp = jnp.concatenate([inv_freq, inv_freq, jnp.zeros((DIFF_HEAD_DIM - ROT_DIM,), _f32)])
    return jnp.concatenate([comp, comp]).reshape(1, LANES)


def kernel(x, p, positions, norm_mix_g, w_in, conv_w, conv_b, dt_bias_f, dt_bias_b, a_log_f, a_log_b, d_skip, ssd_norm_g, lam_q1, lam_k1, lam_q2, lam_k2, subln_g, w_out, norm_ffn_g, w_route_group, b_route_group, w_route_expert, b_route_expert, w_exp_gate, w_exp_up, w_exp_down, w_ple_proj, ple_norm_g, w_ple_gate, b_ple_gate, final_norm_g):
    b, seq, d = x.shape
    t = b * seq
    x2 = x.reshape(t, d)
    row = lambda v: v.reshape(1, -1).astype(_f32)

    w_in_t = jnp.swapaxes(w_in[0], 0, 1)
    pos_col = positions.astype(_f32).reshape(t, 1)
    pad_dt = lambda v: jnp.pad(v, (0, DT_PAD - 2 * SSD_HEADS)).reshape(1, DT_PAD)
    dt_bias = pad_dt(jnp.concatenate([dt_bias_f[0], dt_bias_b[0]]))
    a_row = pad_dt(jnp.concatenate([-jnp.exp(a_log_f[0]), -jnp.exp(a_log_b[0])]))
    d_skip_row = jnp.repeat(d_skip[0], SSD_HEADDIM).reshape(1, SSD_W)
    lam_vecs = jnp.stack([lam_q1[0], lam_k1[0], lam_q2[0], lam_k2[0]])
    n_route = MOE_GROUPS + N_EXPERTS
    w_route = jnp.pad(jnp.concatenate([w_route_group[0], w_route_expert[0]], axis=1),
                      ((0, 0), (0, LANES - n_route)))
    w_route_hi = w_route.astype(_bf16)
    w_route_lo = (w_route - w_route_hi.astype(_f32)).astype(_bf16)
    w_route2 = jnp.concatenate([w_route_hi, w_route_lo], axis=1)
    b_route = jnp.pad(jnp.concatenate([b_route_group[0], b_route_expert[0]]),
                      (0, LANES - n_route)).reshape(1, LANES)

    proj, dt, *rot = _inproj(x2, row(norm_mix_g[0]), pos_col, _inv_freq_row(), w_in_t)
    proj3 = proj.reshape(b, seq, MAIN_W)
    y_ssd = _ssd(proj3, dt.reshape(b, seq, DT_PAD), conv_w[0], row(conv_b[0]),
                 dt_bias, a_row, d_skip_row)
    att = _attn(proj3, [r.reshape(b, seq, LANES) for r in rot], lam_vecs, row(subln_g[0]))
    h1, hn_packed, route, cnt, gates = _outproj(
        y_ssd.reshape(t, SSD_W), proj, att.reshape(t, ATTN_W), x2, w_out[0].astype(_bf16),
        row(ssd_norm_g[0]), row(norm_ffn_g[0]), w_route2, b_route)
    pos8, blk8, meta8 = _plan(route, cnt)
    pos = pos8[:TOP_K].reshape(-1)
    block_expert = blk8[0, :_n_rows(t) // ROW_BLOCK]
    n_used = blk8[1, :1]
    x_rows = _dispatch_rows(pos, meta8[0, :N_EXPERTS], meta8[1, :N_EXPERTS], n_used, hn_packed)
    y_rows = _moe(block_expert, n_used, x_rows, w_exp_gate[0], w_exp_up[0], w_exp_down[0])
    out = _tail(pos, h1, gates, y_rows, p[0].reshape(t, PLE_DIM), w_ple_proj[0].astype(_bf16),
                row(ple_norm_g[0]), w_ple_gate[0].astype(_bf16), row(b_ple_gate[0]),
                row(final_norm_g))
    return out.reshape(b, seq, d)
```

```python
import functools

import jax
import jax.numpy as jnp
import numpy as np
from jax import lax
from jax.experimental import pallas as pl
from jax.experimental.pallas import tpu as pltpu

D_MODEL = 2048
PLE_DIM = 256
SSD_W = 1024
ATTN_W = 1024
SSD_HEADDIM = 64
SSD_HEADS = 16
SSD_GROUPS = 2
HEADS_PER_GROUP = SSD_HEADS // SSD_GROUPS
SSD_STATE = 128
CHUNK = 128
CONV_W = 5
XBC_W = SSD_W + 2 * SSD_GROUPS * SSD_STATE
DIFF_HEAD_DIM = 64
DIFF_HEADS = 8
ROT_DIM = 16
ROPE_THETA = 500000.0
MOE_GROUPS = 8
EXPERTS_PER_GROUP = 8
N_EXPERTS = 64
TOP_K = 2
EXPERT_FF = 512
ROW_BLOCK = 128
EPS = 1e-6
LAM_INIT = 0.2

LANES = 128
DT_PAD = LANES
MAIN_W = SSD_W + XBC_W + 3 * ATTN_W
COL_Z = 0
COL_Q = SSD_W
COL_K = COL_Q + ATTN_W
COL_XBC = COL_K + ATTN_W
COL_V = COL_XBC + XBC_W
VMEM_LIMIT = 56 * 1024 * 1024
NEG_BIG = -1e30
Q_SCALE = float(DIFF_HEAD_DIM ** -0.5 * np.log2(np.e))

_f32 = jnp.float32
_bf16 = jnp.bfloat16


def _silu(v):
    return v * (1.0 / (1.0 + jnp.exp(-v)))


def _rms(v, g):
    return v * lax.rsqrt(jnp.mean(v * v, axis=-1, keepdims=True) + EPS) * g


_HI16 = np.uint32(0xFFFF0000)


def _pack_bf16_pair(lo, hi):
    lo_w = lax.bitcast_convert_type(lo.astype(_bf16).astype(_f32), jnp.uint32) >> 16
    hi_w = lax.bitcast_convert_type(hi.astype(_bf16).astype(_f32), jnp.uint32) & _HI16
    return lo_w | hi_w


def _unpack_bf16_pair(w):
    lo = lax.bitcast_convert_type(w << 16, _f32)
    hi = lax.bitcast_convert_type(w & _HI16, _f32)
    return lo, hi


ROW_TILE = 8
assert D_MODEL // 2 == ROW_TILE * LANES


def _store_row_tiles(ref, words, n, row0=0):
    for c in range(ROW_TILE):
        ref[pl.ds(row0 * ROW_TILE + c, n, stride=ROW_TILE), :] = words[:, c * LANES:(c + 1) * LANES]


def _load_row_tiles(ref, n, row0=0):
    return jnp.concatenate(
        [ref[pl.ds(row0 * ROW_TILE + c, n, stride=ROW_TILE), :] for c in range(ROW_TILE)], axis=1)


IN_TM = 1024
IN_TN = 512


def _inproj_kernel(x_ref, g_ref, pos_ref, invf_ref, wt_ref, wdt_ref,
                   out_ref, dt_ref, rc_ref, rp_ref, rm_ref, n_scr):
    j = pl.program_id(1)
    nt = (((1,), (1,)), ((), ()))
    half = ROT_DIM // 2

    @pl.when(j == 0)
    def _():
        n = _rms(x_ref[...], g_ref[...])
        n_scr[...] = n.astype(_bf16)
        lane = lax.broadcasted_iota(jnp.int32, (1, LANES), 1)
        dt = lax.dot_general(n_scr[...], wdt_ref[...].astype(_bf16), nt, preferred_element_type=_f32)
        dt_ref[...] = jnp.where(lane < 2 * SSD_HEADS, dt, 0.0)
        ang = pos_ref[...] * invf_ref[...]
        cos, sin = jnp.cos(ang), jnp.sin(ang)
        l64 = lane & (DIFF_HEAD_DIM - 1)
        rc_ref[...] = jnp.where(l64 < ROT_DIM, cos, 1.0)
        rp_ref[...] = jnp.where(jnp.logical_and(l64 >= half, l64 < ROT_DIM), sin, 0.0)
        rm_ref[...] = jnp.where(l64 < half, -sin, 0.0)

    out_ref[...] = lax.dot_general(n_scr[...], wt_ref[...].astype(_bf16), nt,
                                   preferred_element_type=_f32).astype(_bf16)


def _rotate(t, rc, rp, rm):
    half = ROT_DIM // 2
    return t * rc + pltpu.roll(t, half, 1) * rp + pltpu.roll(t, LANES - half, 1) * rm


SRC_XBC = SSD_W
SRC_DT = SRC_XBC + XBC_W
SRC_Q = SRC_DT + 2 * SSD_HEADS
SRC_V = SRC_Q + 2 * ATTN_W
SRC_ALIGN = 32
assert all(s % SRC_ALIGN == 0 for s in (SRC_XBC, SRC_DT, SRC_Q, SRC_V, IN_TN))


def _src_row(j):
    jq, jx, jv = COL_Q // IN_TN, COL_XBC // IN_TN, COL_V // IN_TN
    row = jnp.where(j < jq, j * IN_TN,
                    jnp.where(j < jx, SRC_Q + (j - jq) * IN_TN,
                              jnp.where(j < jv, SRC_XBC + (j - jx) * IN_TN,
                                        SRC_V + (j - jv) * IN_TN)))
    return pl.multiple_of(row, SRC_ALIGN)


def _inproj(x2, g, pos_col, invf_row, w_in_t):
    t = x2.shape[0]
    grid = (t // IN_TM, MAIN_W // IN_TN)
    return pl.pallas_call(
        _inproj_kernel,
        grid=grid,
        in_specs=[
            pl.BlockSpec((IN_TM, D_MODEL), lambda i, j: (i, 0)),
            pl.BlockSpec((1, D_MODEL), lambda i, j: (0, 0)),
            pl.BlockSpec((IN_TM, 1), lambda i, j: (i, 0)),
            pl.BlockSpec((1, LANES), lambda i, j: (0, 0)),
            pl.BlockSpec((pl.Element(IN_TN), pl.Element(D_MODEL)), lambda i, j: (_src_row(j), 0)),
            pl.BlockSpec((pl.Element(DT_PAD), pl.Element(D_MODEL)), lambda i, j: (SRC_DT, 0)),
        ],
        out_specs=[
            pl.BlockSpec((IN_TM, IN_TN), lambda i, j: (i, j)),
            pl.BlockSpec((IN_TM, DT_PAD), lambda i, j: (i, 0)),
        ] + [pl.BlockSpec((IN_TM, LANES), lambda i, j: (i, 0))] * 3,
        out_shape=[
            jax.ShapeDtypeStruct((t, MAIN_W), _bf16),
            jax.ShapeDtypeStruct((t, DT_PAD), _f32),
        ] + [jax.ShapeDtypeStruct((t, LANES), _f32)] * 3,
        scratch_shapes=[pltpu.VMEM((IN_TM, D_MODEL), _bf16)],
        compiler_params=pltpu.CompilerParams(
            dimension_semantics=("arbitrary", "arbitrary"),
            vmem_limit_bytes=VMEM_LIMIT),
        name="inproj",
    )(x2, g, pos_col, invf_row, w_in_t, w_in_t)


CONV_HALO = 16
CONV_ROWS = 256


def _ssd_kernel(xbc_ref, dt_ref, cw_ref, cb_ref, dtb_ref, a_ref, dsk_ref,
                y_ref, xpad, xact, dts, yacc, state_f, state_b, seq):
    n_chunks = seq // CHUNK

    zeros_halo = jnp.zeros((CONV_HALO, XBC_W), _bf16)
    xpad[0:CONV_HALO, :] = zeros_halo
    xpad[CONV_HALO + seq:CONV_HALO + seq + CONV_HALO, :] = zeros_halo
    xpad[CONV_HALO:CONV_HALO + seq, :] = xbc_ref[0]
    cw = cw_ref[...]
    cb = cb_ref[...]
    win = CONV_ROWS + 2 * CONV_HALO

    def conv_body(c, carry):
        r0 = pl.multiple_of(c * CONV_ROWS, CONV_ROWS)
        blk = xpad[pl.ds(r0, win), :].astype(_f32)
        acc = blk * cw[CONV_W // 2:CONV_W // 2 + 1, :]
        for k in range(CONV_W):
            sh = CONV_W // 2 - k
            if sh == 0:
                continue
            acc = acc + pltpu.roll(blk, sh % win, 0) * cw[k:k + 1, :]
        v = acc[CONV_HALO:CONV_HALO + CONV_ROWS, :] + cb
        xact[pl.ds(r0, CONV_ROWS), :] = _silu(v).astype(_bf16)
        return carry

    lax.fori_loop(0, seq // CONV_ROWS, conv_body, 0)

    raw = dt_ref[0] + dtb_ref[...]
    dts[...] = jnp.maximum(raw, 0.0) + jnp.log1p(jnp.exp(-jnp.abs(raw)))

    a_row = a_ref[...]
    dsk = dsk_ref[...]
    row_i = lax.broadcasted_iota(jnp.int32, (CHUNK, CHUNK), 0)
    col_i = lax.broadcasted_iota(jnp.int32, (CHUNK, CHUNK), 1)

    def chunk_step(c, reverse, state):
        r0 = pl.multiple_of(c * CHUNK, CHUNK)
        dtc = dts[pl.ds(r0, CHUNK), :]
        cs = dtc * a_row
        k = 1
        while k < CHUNK:
            if reverse:
                cs = cs + jnp.where(row_i < CHUNK - k, pltpu.roll(cs, CHUNK - k, 0), 0.0)
            else:
                cs = cs + jnp.where(row_i >= k, pltpu.roll(cs, k, 0), 0.0)
            k *= 2
        cs_t = cs.T
        dt_t = dtc.T
        end_col = cs_t[:, 0:1] if reverse else cs_t[:, CHUNK - 1:CHUNK]
        w_t = jnp.exp(end_col - cs_t) * dt_t
        end_row = cs[0:1, :] if reverse else cs[CHUNK - 1:CHUNK, :]
        dec_row = jnp.exp(end_row)
        tri = (row_i <= col_i) if reverse else (row_i >= col_i)
        lane0 = SSD_HEADS if reverse else 0
        first_head = col_i < SSD_HEADDIM
        rows = pl.ds(r0, CHUNK)
        for g in range(SSD_GROUPS):
            b_g = xact[rows, SSD_W + g * SSD_STATE:SSD_W + (g + 1) * SSD_STATE]
            c_g = xact[rows, SSD_W + (SSD_GROUPS + g) * SSD_STATE:SSD_W + (SSD_GROUPS + g + 1) * SSD_STATE]
            cbm = lax.dot_general(c_g, b_g, (((1,), (1,)), ((), ())),
                                  preferred_element_type=_f32)
            b_t = b_g.astype(_f32).T
            for pp in range(HEADS_PER_GROUP // 2):
                pair = g * (HEADS_PER_GROUP // 2) + pp
                cols = slice(pair * LANES, (pair + 1) * LANES)
                xs_p = xact[rows, cols]
                st = state[pair]
                off = jnp.dot(c_g, st.astype(_bf16), preferred_element_type=_f32)
                halves = []
                for hh in range(2):
                    ln = lane0 + 2 * pair + hh
                    colb = jnp.broadcast_to(cs[:, ln:ln + 1], (CHUNK, CHUNK))
                    m_h = (cbm * jnp.exp(jnp.where(tri, colb - cs_t[ln:ln + 1, :], NEG_BIG))
                           * dt_t[ln:ln + 1, :])
                    y_h = (jnp.dot(m_h.astype(_bf16), xs_p, preferred_element_type=_f32)
                           + jnp.exp(colb) * off)
                    upd = jnp.dot((b_t * w_t[ln:ln + 1, :]).astype(_bf16), xs_p,
                                  preferred_element_type=_f32)
                    dec = jnp.broadcast_to(dec_row[:, ln:ln + 1], (CHUNK, CHUNK))
                    halves.append((y_h, upd, dec))
                y_p = jnp.where(first_head, halves[0][0], halves[1][0])
                upd_p = jnp.where(first_head, halves[0][1], halves[1][1])
                dec_p = jnp.where(first_head, halves[0][2], halves[1][2])
                state[pair] = st * dec_p + upd_p
                if reverse:
                    y_ref[0, rows, cols] = (yacc[rows, cols] + y_p).astype(_bf16)
                else:
                    yacc[rows, cols] = y_p + xs_p.astype(_f32) * dsk[:, cols]

    state_f[...] = jnp.zeros_like(state_f)
    state_b[...] = jnp.zeros_like(state_b)

    def fwd_body(c, carry):
        chunk_step(c, False, state_f)
        return carry

    lax.fori_loop(0, n_chunks, fwd_body, 0)

    def bwd_body(i, carry):
        chunk_step(n_chunks - 1 - i, True, state_b)
        return carry

    lax.fori_loop(0, n_chunks, bwd_body, 0)


def _ssd(proj3, dt3, conv_w, conv_b, dt_bias, a_row, d_skip_row):
    b, seq, _ = proj3.shape
    kern = functools.partial(_ssd_kernel, seq=seq)
    assert COL_XBC % XBC_W == 0
    return pl.pallas_call(
        kern,
        grid=(b,),
        in_specs=[
            pl.BlockSpec((1, seq, XBC_W), lambda i: (i, 0, COL_XBC // XBC_W)),
            pl.BlockSpec((1, seq, DT_PAD), lambda i: (i, 0, 0)),
            pl.BlockSpec((CONV_W, XBC_W), lambda i: (0, 0)),
            pl.BlockSpec((1, XBC_W), lambda i: (0, 0)),
            pl.BlockSpec((1, DT_PAD), lambda i: (0, 0)),
            pl.BlockSpec((1, DT_PAD), lambda i: (0, 0)),
            pl.BlockSpec((1, SSD_W), lambda i: (0, 0)),
        ],
        out_specs=pl.BlockSpec((1, seq, SSD_W), lambda i: (i, 0, 0)),
        out_shape=jax.ShapeDtypeStruct((b, seq, SSD_W), _bf16),
        scratch_shapes=[
            pltpu.VMEM((seq + 2 * CONV_HALO, XBC_W), _bf16),
            pltpu.VMEM((seq, XBC_W), _bf16),
            pltpu.VMEM((seq, DT_PAD), _f32),
            pltpu.VMEM((seq, SSD_W), _f32),
            pltpu.VMEM((SSD_HEADS // 2, SSD_STATE, 2 * SSD_HEADDIM), _f32),
            pltpu.VMEM((SSD_HEADS // 2, SSD_STATE, 2 * SSD_HEADDIM), _f32),
        ],
        compiler_params=pltpu.CompilerParams(
            dimension_semantics=("arbitrary",),
            vmem_limit_bytes=VMEM_LIMIT),
        name="ssd",
    )(proj3, dt3, conv_w, conv_b, dt_bias, a_row, d_skip_row)


ATT_QB = 256
ATT_HEADS = 2


def _attn_kernel(q_ref, k_ref, v_ref, rc_ref, rp_ref, rm_ref, lam_ref, g_ref, o_ref,
                 k_scr, v1_scr, *, seq):
    hw = 2 * DIFF_HEAD_DIM
    lv = lam_ref[...]
    lam = (jnp.exp(jnp.sum(lv[0:1] * lv[1:2], axis=-1, keepdims=True))
           - jnp.exp(jnp.sum(lv[2:3] * lv[3:4], axis=-1, keepdims=True)) + LAM_INIT)

    def rotated(ref, rows, cols):
        return _rotate(ref[0, rows, cols].astype(_f32), rc_ref[0, rows, :], rp_ref[0, rows, :],
                       rm_ref[0, rows, :])

    g = g_ref[...]
    first = lax.broadcasted_iota(jnp.int32, (1, hw), 1) < DIFF_HEAD_DIM
    ones_col = (lax.broadcasted_iota(jnp.int32, (seq, hw), 1) == 0).astype(_bf16)
    nt = (((1,), (1,)), ((), ()))

    def prepare(hd):
        cols = slice(hd * hw, (hd + 1) * hw)
        for c in range(seq // ATT_QB):
            rows = slice(c * ATT_QB, (c + 1) * ATT_QB)
            k_scr[hd, rows, :] = rotated(k_ref, rows, cols).astype(_bf16)
        v1_scr[hd, :, 0:hw] = v_ref[0, :, cols]
        v1_scr[hd, :, hw:2 * hw] = ones_col

    def chain(hd, r0):
        cols = slice(hd * hw, (hd + 1) * hw)
        rows = slice(r0, r0 + ATT_QB)
        q = (rotated(q_ref, rows, cols) * Q_SCALE).astype(_bf16)
        zero = jnp.zeros_like(q)
        ovs = []
        for qc in (jnp.where(first, q, zero), jnp.where(first, zero, q)):
            s = lax.dot_general(qc, k_scr[hd], nt, preferred_element_type=_f32)
            e = jnp.exp2(s - jnp.max(s, axis=-1, keepdims=True)).astype(_bf16)
            ovs.append(jnp.dot(e, v1_scr[hd], preferred_element_type=_f32))
        o1, l1 = ovs[0][:, 0:hw], ovs[0][:, hw:hw + 1]
        o2, l2 = ovs[1][:, 0:hw], ovs[1][:, hw:hw + 1]
        o = o1 / l1 - (lam / l2) * o2
        o = _rms(o, g) * (1.0 - LAM_INIT)
        o_ref[0, rows, cols] = o.astype(_bf16)

    for hd in range(ATT_HEADS):
        prepare(hd)
    for hd in range(ATT_HEADS):
        for c in range(seq // ATT_QB):
            chain(hd, c * ATT_QB)


def _attn(proj3, rot_tables, lam_vecs, subln_g):
    b, seq, _ = proj3.shape
    hw = 2 * DIFF_HEAD_DIM
    gw = ATT_HEADS * hw
    assert COL_Q % gw == 0 and COL_K % gw == 0 and COL_V % gw == 0
    kern = functools.partial(_attn_kernel, seq=seq)
    return pl.pallas_call(
        kern,
        grid=(b, DIFF_HEADS // ATT_HEADS),
        in_specs=[
            pl.BlockSpec((1, seq, gw), lambda i, h: (i, 0, COL_Q // gw + h)),
            pl.BlockSpec((1, seq, gw), lambda i, h: (i, 0, COL_K // gw + h)),
            pl.BlockSpec((1, seq, gw), lambda i, h: (i, 0, COL_V // gw + h)),
        ] + [pl.BlockSpec((1, seq, LANES), lambda i, h: (i, 0, 0))] * 3 + [
            pl.BlockSpec((4, DIFF_HEAD_DIM), lambda i, h: (0, 0)),
            pl.BlockSpec((1, hw), lambda i, h: (0, 0)),
        ],
        out_specs=pl.BlockSpec((1, seq, gw), lambda i, h: (i, 0, h)),
        out_shape=jax.ShapeDtypeStruct((b, seq, ATTN_W), _bf16),
        scratch_shapes=[pltpu.VMEM((ATT_HEADS, seq, hw), _bf16),
                        pltpu.VMEM((ATT_HEADS, seq, 2 * hw), _bf16)],
        compiler_params=pltpu.CompilerParams(
            dimension_semantics=("arbitrary", "arbitrary"),
            vmem_limit_bytes=VMEM_LIMIT),
        name="diffattn",
    )(proj3, proj3, proj3, *rot_tables, lam_vecs, subln_g)


OUT_TM = 512
ROUTE_ROWS = 8


def _outproj_kernel(y_ref, z_ref, att_ref, x_ref, w_ref, gs_ref, gf_ref, wr_ref, br_ref,
                    h_ref, hn_ref, route_ref, cnt_ref, gate_ref, cnt_scr):
    @pl.when(pl.program_id(0) == 0)
    def _():
        cnt_scr[...] = jnp.zeros_like(cnt_scr)

    y = y_ref[...].astype(_f32)
    z = z_ref[...].astype(_f32)
    s = _rms(y * _silu(z), gs_ref[...]).astype(_bf16)
    acc = jnp.dot(s, w_ref[0:SSD_W, :], preferred_element_type=_f32)
    acc = acc + jnp.dot(att_ref[...], w_ref[SSD_W:SSD_W + ATTN_W, :], preferred_element_type=_f32)
    h1 = x_ref[...] + acc
    h_ref[...] = h1
    hn = _rms(h1, gf_ref[...])
    half = D_MODEL // 2
    _store_row_tiles(hn_ref, _pack_bf16_pair(hn[:, 0:half], hn[:, half:D_MODEL]), hn.shape[0])
    hn_hi = hn.astype(_bf16)
    hn_lo = (hn - hn_hi.astype(_f32)).astype(_bf16)
    a = jnp.dot(hn_hi, wr_ref[...], preferred_element_type=_f32)
    bb = jnp.dot(hn_lo, wr_ref[:, 0:LANES], preferred_element_type=_f32)
    logits = (a[:, 0:LANES] + a[:, LANES:2 * LANES] + bb + br_ref[...]).T
    tm = logits.shape[1]
    iota = lax.broadcasted_iota(jnp.int32, (MOE_GROUPS, tm), 0)

    def first_argmax(val, vmax):
        return jnp.min(jnp.where(val == vmax, iota, MOE_GROUPS), axis=0, keepdims=True)

    gl = logits[0:MOE_GROUPS]
    gmax = jnp.max(gl, axis=0, keepdims=True)
    p_g = 1.0 / jnp.sum(jnp.exp(gl - gmax), axis=0, keepdims=True)
    g_sel = first_argmax(gl, gmax)
    el = jnp.zeros((EXPERTS_PER_GROUP, tm), _f32)
    for gi in range(MOE_GROUPS):
        lo = MOE_GROUPS + gi * EXPERTS_PER_GROUP
        el = jnp.where(g_sel == gi, logits[lo:lo + EXPERTS_PER_GROUP], el)
    ee = jnp.exp(el - jnp.max(el, axis=0, keepdims=True))
    pe = ee / jnp.sum(ee, axis=0, keepdims=True)
    p1 = jnp.max(pe, axis=0, keepdims=True)
    i1 = first_argmax(pe, p1)
    pe2 = jnp.where(iota == i1, -1.0, pe)
    p2 = jnp.max(pe2, axis=0, keepdims=True)
    i2 = first_argmax(pe2, p2)
    den = p1 + p2
    e1 = g_sel * EXPERTS_PER_GROUP + i1
    e2 = g_sel * EXPERTS_PER_GROUP + i2
    iota_e = lax.broadcasted_iota(jnp.int32, (N_EXPERTS, tm), 0)
    oh1 = (iota_e == e1).astype(_f32)
    oh2 = (iota_e == e2).astype(_f32)
    both = oh1 + oh2
    earlier = (lax.broadcasted_iota(jnp.int32, (tm, tm), 0)
               < lax.broadcasted_iota(jnp.int32, (tm, tm), 1)).astype(_bf16)
    before = cnt_scr[:, 0:1] + jnp.dot(both.astype(_bf16), earlier,
                                       preferred_element_type=_f32)
    r1 = jnp.sum(oh1 * before, axis=0, keepdims=True).astype(jnp.int32)
    r2 = jnp.sum(oh2 * before, axis=0, keepdims=True).astype(jnp.int32)
    cnt_scr[...] = cnt_scr[...] + jnp.sum(both, axis=1, keepdims=True)
    cnt_ref[...] = cnt_scr[...].astype(jnp.int32)
    route_ref[...] = jnp.where(iota == 0, e1, jnp.where(iota == 1, e2,
                               jnp.where(iota == 2, r1, jnp.where(iota == 3, r2, 0))))
    g8 = jnp.where(iota == 0, p_g * (p1 / den), jnp.where(iota == 1, p_g * (p2 / den), 0.0))
    gate_ref[...] = jnp.concatenate(
        [g8, jnp.zeros((LANES - ROUTE_ROWS, tm), _f32)], axis=0).T


def _outproj(y2, proj2, att2, x2, w_out, g_ssd, g_ffn, wr_t, br_col):
    t = x2.shape[0]
    row = lambda i: (i, 0)
    fix = lambda i: (0, 0)
    return pl.pallas_call(
        _outproj_kernel,
        grid=(t // OUT_TM,),
        in_specs=[
            pl.BlockSpec((OUT_TM, SSD_W), row),
            pl.BlockSpec((OUT_TM, SSD_W), row),
            pl.BlockSpec((OUT_TM, ATTN_W), row),
            pl.BlockSpec((OUT_TM, D_MODEL), row),
            pl.BlockSpec((SSD_W + ATTN_W, D_MODEL), fix),
            pl.BlockSpec((1, SSD_W), fix),
            pl.BlockSpec((1, D_MODEL), fix),
            pl.BlockSpec((D_MODEL, 2 * LANES), fix),
            pl.BlockSpec((1, LANES), fix),
        ],
        out_specs=[
            pl.BlockSpec((OUT_TM, D_MODEL), row),
            pl.BlockSpec((OUT_TM * ROW_TILE, LANES), row),
            pl.BlockSpec((ROUTE_ROWS, OUT_TM), lambda i: (0, i)),
            pl.BlockSpec((N_EXPERTS, LANES), fix),
            pl.BlockSpec((OUT_TM, LANES), row),
        ],
        out_shape=[
            jax.ShapeDtypeStruct((t, D_MODEL), _f32),
            jax.ShapeDtypeStruct((t * ROW_TILE, LANES), jnp.uint32),
            jax.ShapeDtypeStruct((ROUTE_ROWS, t), jnp.int32),
            jax.ShapeDtypeStruct((N_EXPERTS, LANES), jnp.int32),
            jax.ShapeDtypeStruct((t, LANES), _f32),
        ],
        scratch_shapes=[pltpu.VMEM((N_EXPERTS, LANES), _f32)],
        compiler_params=pltpu.CompilerParams(
            dimension_semantics=("arbitrary",),
            vmem_limit_bytes=VMEM_LIMIT),
        name="outproj_router",
    )(y2, proj2, att2, x2, w_out, g_ssd, g_ffn, wr_t, br_col)


def _n_rows(n_tok):
    n_assign = n_tok * TOP_K
    return (n_assign + N_EXPERTS * (ROW_BLOCK - 1) + ROW_BLOCK - 1) // ROW_BLOCK * ROW_BLOCK


def _prefix_sum(v, axis):
    n = v.shape[axis]
    idx = lax.broadcasted_iota(jnp.int32, v.shape, axis)
    k = 1
    while k < n:
        v = v + jnp.where(idx >= k, pltpu.roll(v, k, axis), 0)
        k *= 2
    return v


def _plan_kernel(route_ref, cnt_ref, pos_ref, blk_ref, meta_ref, *, n_blocks_pad):
    n_tok = route_ref.shape[1]
    cnt_col = jnp.concatenate(
        [cnt_ref[...], jnp.zeros((LANES - N_EXPERTS, LANES), jnp.int32)], axis=0)
    pad_up = lambda c: (c + (ROW_BLOCK - 1)) & (-ROW_BLOCK)
    ends_col = _prefix_sum(pad_up(cnt_col), 0)
    starts_col = (ends_col - pad_up(cnt_col)).astype(_f32)
    cnt_row = cnt_col.astype(_f32).T.astype(jnp.int32)
    ends_row = _prefix_sum(pad_up(cnt_row), 1)
    starts_row = ends_row - pad_up(cnt_row)
    n_used = ends_row[0:1, LANES - 1:LANES] >> (ROW_BLOCK.bit_length() - 1)

    ch = 1024
    iota_e = lax.broadcasted_iota(jnp.int32, (LANES, ch), 0)
    for c0 in range(0, n_tok, ch):
        rt = route_ref[:, c0:c0 + ch]
        s1 = jnp.sum(jnp.where(iota_e == rt[0:1], starts_col[:, 0:1], 0.0), axis=0, keepdims=True)
        s2 = jnp.sum(jnp.where(iota_e == rt[1:2], starts_col[:, 0:1], 0.0), axis=0, keepdims=True)
        p1 = s1.astype(jnp.int32) + rt[2:3]
        p2 = s2.astype(jnp.int32) + rt[3:4]
        sub = lax.broadcasted_iota(jnp.int32, (ROUTE_ROWS, ch), 0)
        pos_ref[:, c0:c0 + ch] = jnp.where(sub == 0, p1, jnp.where(sub == 1, p2, 0))

    blk_start = lax.broadcasted_iota(jnp.int32, (LANES, n_blocks_pad), 1) * ROW_BLOCK
    e_idx = lax.broadcasted_iota(jnp.int32, (LANES, n_blocks_pad), 0)
    real = e_idx < N_EXPERTS
    be = jnp.sum(jnp.where(jnp.logical_and(real, ends_col[:, 0:1] <= blk_start), 1.0, 0.0),
                 axis=0, keepdims=True).astype(jnp.int32)
    be = jnp.minimum(be, N_EXPERTS - 1)
    last = jnp.max(jnp.where(jnp.logical_and(real, cnt_col[:, 0:1] > 0), e_idx.astype(_f32), 0.0),
                   axis=0, keepdims=True).astype(jnp.int32)
    blk_i = lax.broadcasted_iota(jnp.int32, (1, n_blocks_pad), 1)
    be = jnp.where(blk_i < n_used, be, last)
    sub = lax.broadcasted_iota(jnp.int32, (ROUTE_ROWS, n_blocks_pad), 0)
    blk_ref[...] = jnp.where(sub == 0, be, jnp.where(sub == 1, n_used, 0))
    sub = lax.broadcasted_iota(jnp.int32, (ROUTE_ROWS, LANES), 0)
    meta_ref[...] = jnp.where(sub == 0, cnt_row[0:1], jnp.where(sub == 1, starts_row[0:1], 0))


def _plan(route, cnt):
    n_tok = route.shape[1]
    n_blocks_pad = -(-(_n_rows(n_tok) // ROW_BLOCK) // LANES) * LANES
    return pl.pallas_call(
        functools.partial(_plan_kernel, n_blocks_pad=n_blocks_pad),
        out_shape=[
            jax.ShapeDtypeStruct((ROUTE_ROWS, n_tok), jnp.int32),
            jax.ShapeDtypeStruct((ROUTE_ROWS, n_blocks_pad), jnp.int32),
            jax.ShapeDtypeStruct((ROUTE_ROWS, LANES), jnp.int32),
        ],
        compiler_params=pltpu.CompilerParams(vmem_limit_bytes=VMEM_LIMIT),
        name="route_plan",
    )(route, cnt)


DISP_TM = 512


def _dispatch_kernel(pos_ref, cnt_ref, start_ref, nused_ref, h_ref, x_hbm, stage, zrow, sem, zsem,
                     *, n_tok, n_blocks):
    i = pl.program_id(0)
    n_steps = pl.num_programs(0)
    slot = i % 2
    base = i * DISP_TM
    tile = lambda r: pl.ds(pl.multiple_of(r * ROW_TILE, ROW_TILE), ROW_TILE)

    def wait_slot(sl):
        for _ in range(TOP_K):
            pltpu.make_async_copy(stage.at[sl], x_hbm.at[pl.ds(0, DISP_TM * ROW_TILE), :],
                                  sem.at[sl]).wait()

    @pl.when(i >= 2)
    def _():
        wait_slot(slot)

    stage[slot] = h_ref[...]

    def body(r, carry):
        for kk in range(TOP_K):
            row = pos_ref[kk * n_tok + base + r]
            pltpu.make_async_copy(stage.at[slot, tile(r), :], x_hbm.at[tile(row), :],
                                  sem.at[slot]).start(priority=kk % 2)
        return carry

    lax.fori_loop(0, DISP_TM, body, 0, unroll=8)

    @pl.when(i == n_steps - 1)
    def _():
        zrow[...] = jnp.zeros_like(zrow)

        pieces = [1 << b for b in reversed(range(ROW_BLOCK.bit_length() - 1))]

        def pad_copy(size, row0):
            src = zrow.at[pl.ds(0, size * ROW_TILE), :]
            dst = x_hbm.at[pl.ds(pl.multiple_of(row0 * ROW_TILE, ROW_TILE), size * ROW_TILE), :]
            return pltpu.make_async_copy(src, dst, zsem)

        def per_expert(e, carry, wait):
            cnt = cnt_ref[e]
            n_pad = ((cnt + (ROW_BLOCK - 1)) & (-ROW_BLOCK)) - cnt
            row0 = start_ref[e] + cnt
            for size in pieces:
                has = (n_pad & size) != 0

                @pl.when(has)
                def _():
                    cp = pad_copy(size, row0)
                    cp.wait() if wait else cp.start()

                row0 = row0 + jnp.where(has, size, 0)
            return carry

        lax.fori_loop(0, N_EXPERTS, functools.partial(per_expert, wait=False), 0)
        blk_rows = ROW_BLOCK * ROW_TILE

        def fill_block(blk, carry):
            dst = pl.ds(pl.multiple_of(blk * blk_rows, blk_rows), blk_rows)
            pltpu.make_async_copy(zrow, x_hbm.at[dst, :], zsem).start()
            return carry

        lax.fori_loop(nused_ref[0], n_blocks, fill_block, 0)

        lax.fori_loop(0, N_EXPERTS, functools.partial(per_expert, wait=True), 0)

        def wait_block(blk, carry):
            pltpu.make_async_copy(zrow, x_hbm.at[pl.ds(0, blk_rows), :], zsem).wait()
            return carry

        lax.fori_loop(nused_ref[0], n_blocks, wait_block, 0)

        @pl.when(n_steps >= 2)
        def _():
            wait_slot(1 - slot)

        wait_slot(slot)


def _dispatch_rows(pos, counts, starts, n_used, hn_packed):
    t = hn_packed.shape[0] // ROW_TILE
    n_rows = _n_rows(t)
    blk = (DISP_TM * ROW_TILE, LANES)
    grid_spec = pltpu.PrefetchScalarGridSpec(
        num_scalar_prefetch=4,
        grid=(t // DISP_TM,),
        in_specs=[pl.BlockSpec(blk, lambda i, *_: (i, 0))],
        out_specs=pl.BlockSpec(memory_space=pl.ANY),
        scratch_shapes=[
            pltpu.VMEM((2,) + blk, hn_packed.dtype),
            pltpu.VMEM((ROW_BLOCK * ROW_TILE, LANES), hn_packed.dtype),
            pltpu.SemaphoreType.DMA((2,)),
            pltpu.SemaphoreType.DMA(()),
        ],
    )
    return pl.pallas_call(
        functools.partial(_dispatch_kernel, n_tok=t, n_blocks=n_rows // ROW_BLOCK),
        grid_spec=grid_spec,
        out_shape=jax.ShapeDtypeStruct((n_rows * ROW_TILE, LANES), hn_packed.dtype),
        compiler_params=pltpu.CompilerParams(
            dimension_semantics=("arbitrary",),
            vmem_limit_bytes=VMEM_LIMIT),
        name="dispatch_rows",
    )(pos, counts, starts, n_used, hn_packed)


MOE_W_SPLIT = 4
MOE_W_BUFS = 3
MOE_GROUP = 4


def _moe_kernel(be_ref, nused_ref, x_ref, wg_hbm, wu_hbm, wd_hbm,
                out_ref, wg_f, wu_f, wd_f, wg_s, wu_s, wd_s, wsem, nexp, *, n_blocks):
    i = pl.program_id(0)
    nused = nused_ref[0]
    half = D_MODEL // 2

    def weight_copies(e, sl):
        cps = []
        for src, dst in ((wg_hbm, wg_f), (wu_hbm, wu_f), (wd_hbm, wd_f)):
            rows = src.shape[1] // MOE_W_SPLIT
            for c in range(MOE_W_SPLIT):
                rs = pl.ds(c * rows, rows)
                cps.append(pltpu.make_async_copy(src.at[e, rs, :], dst.at[sl, rs, :], wsem.at[sl]))
        return cps

    def start_all(cps):
        for n, cp in enumerate(cps):
            cp.start(priority=n % 2)

    def block_expert(b):
        return be_ref[jnp.minimum(b, n_blocks - 1)]

    def next_change(b0):
        e0 = block_expert(b0)
        return lax.while_loop(
            lambda b: jnp.logical_and(b < nused, block_expert(b) == e0), lambda b: b + 1, b0 + 1)

    @pl.when(i == 0)
    def _():
        nexp[0] = 0
        start_all(weight_copies(be_ref[0], 0))
        nxt = next_change(0)

        @pl.when(nxt < nused)
        def _():
            start_all(weight_copies(block_expert(nxt), 1))

    def row_block(b, sub):
        sub_rows = pl.ds(sub * ROW_BLOCK * ROW_TILE, ROW_BLOCK * ROW_TILE)

        @pl.when(b < nused)
        def _():
            e = be_ref[b]
            new_expert = jnp.logical_or(b == 0, e != be_ref[jnp.maximum(b - 1, 0)])

            @pl.when(new_expert)
            def _():
                sl = nexp[0] % MOE_W_BUFS
                ahead = next_change(jnp.minimum(next_change(b), nused - 1))

                @pl.when(jnp.logical_and(next_change(b) < nused, ahead < nused))
                def _():
                    start_all(weight_copies(block_expert(ahead), (nexp[0] + 2) % MOE_W_BUFS))

                for cp in weight_copies(e, sl):
                    cp.wait()
                wg_s[...] = wg_f[sl].astype(_bf16)
                wu_s[...] = wu_f[sl].astype(_bf16)
                wd_s[...] = wd_f[sl].astype(_bf16)
                nexp[0] = nexp[0] + 1

            x_lo, x_hi = _unpack_bf16_pair(_load_row_tiles(x_ref, ROW_BLOCK, sub * ROW_BLOCK))
            x_lo, x_hi = x_lo.astype(_bf16), x_hi.astype(_bf16)
            gg = (jnp.dot(x_lo, wg_s[0:half, :], preferred_element_type=_f32)
                  + jnp.dot(x_hi, wg_s[half:D_MODEL, :], preferred_element_type=_f32))
            uu = (jnp.dot(x_lo, wu_s[0:half, :], preferred_element_type=_f32)
                  + jnp.dot(x_hi, wu_s[half:D_MODEL, :], preferred_element_type=_f32))
            a = (_silu(gg) * uu).astype(_bf16)
            y = jnp.dot(a, wd_s[...], preferred_element_type=_f32)
            _store_row_tiles(out_ref, _pack_bf16_pair(y[:, 0:half], y[:, half:D_MODEL]),
                             ROW_BLOCK, sub * ROW_BLOCK)

        @pl.when(b >= nused)
        def _():
            out_ref[sub_rows, :] = jnp.zeros((ROW_BLOCK * ROW_TILE, LANES), out_ref.dtype)

    for sub in range(MOE_GROUP):
        row_block(i * MOE_GROUP + sub, sub)


def _moe(block_expert, n_used, x_rows, w_gate, w_up, w_down):
    n_blocks = x_rows.shape[0] // (ROW_BLOCK * ROW_TILE)
    assert n_blocks % MOE_GROUP == 0
    blk = (MOE_GROUP * ROW_BLOCK * ROW_TILE, LANES)
    omap = lambda i, be, nu: (i, 0)
    grid_spec = pltpu.PrefetchScalarGridSpec(
        num_scalar_prefetch=2,
        grid=(n_blocks // MOE_GROUP,),
        in_specs=[
            pl.BlockSpec(blk, omap),
            pl.BlockSpec(memory_space=pl.ANY),
            pl.BlockSpec(memory_space=pl.ANY),
            pl.BlockSpec(memory_space=pl.ANY),
        ],
        out_specs=pl.BlockSpec(blk, omap),
        scratch_shapes=[
            pltpu.VMEM((MOE_W_BUFS, D_MODEL, EXPERT_FF), _f32),
            pltpu.VMEM((MOE_W_BUFS, D_MODEL, EXPERT_FF), _f32),
            pltpu.VMEM((MOE_W_BUFS, EXPERT_FF, D_MODEL), _f32),
            pltpu.VMEM((D_MODEL, EXPERT_FF), _bf16),
            pltpu.VMEM((D_MODEL, EXPERT_FF), _bf16),
            pltpu.VMEM((EXPERT_FF, D_MODEL), _bf16),
            pltpu.SemaphoreType.DMA((MOE_W_BUFS,)),
            pltpu.SMEM((1,), jnp.int32),
        ],
    )
    return pl.pallas_call(
        functools.partial(_moe_kernel, n_blocks=n_blocks),
        grid_spec=grid_spec,
        out_shape=jax.ShapeDtypeStruct(x_rows.shape, x_rows.dtype),
        compiler_params=pltpu.CompilerParams(
            dimension_semantics=("arbitrary",),
            vmem_limit_bytes=VMEM_LIMIT),
        name="moe_experts",
    )(block_expert, n_used, x_rows, w_gate, w_up, w_down)


TAIL_TM = 512
TAIL_SUB = 256


def _tail_kernel(pos_ref, h_ref, gate_ref, y_hbm, p_ref, wpp_ref, gp_ref, wpg_ref, bpg_ref,
                 gfin_ref, out_ref, gbuf, cbuf, sem, *, n_tok):
    i = pl.program_id(0)
    n_steps = pl.num_programs(0)
    tile = lambda r: pl.ds(pl.multiple_of(r * ROW_TILE, ROW_TILE), ROW_TILE)

    def start_row(base, r):
        for kk in range(TOP_K):
            row = pos_ref[kk * n_tok + base + r]
            pltpu.make_async_copy(y_hbm.at[tile(row), :], gbuf.at[kk, tile(r), :],
                                  sem).start(priority=kk % 2)

    def wait_rows():
        for kk in range(TOP_K):
            pltpu.make_async_copy(y_hbm.at[pl.ds(0, TAIL_TM * ROW_TILE), :], gbuf.at[kk], sem).wait()

    @pl.when(i == 0)
    def _():
        def body(r, carry):
            start_row(0, r)
            return carry

        lax.fori_loop(0, TAIL_TM, body, 0, unroll=8)

    wait_rows()
    cbuf[...] = gbuf[...]

    nxt_base = jnp.minimum(i + 1, n_steps - 1) * TAIL_TM
    for r in range(TAIL_TM):
        start_row(nxt_base, r)

    for s0 in range(0, TAIL_TM, TAIL_SUB):
        rs = slice(s0, s0 + TAIL_SUB)
        gates = gate_ref[rs, :]
        g1, g2 = gates[:, 0:1], gates[:, 1:2]
        y1_lo, y1_hi = _unpack_bf16_pair(_load_row_tiles(cbuf.at[0], TAIL_SUB, s0))
        y2_lo, y2_hi = _unpack_bf16_pair(_load_row_tiles(cbuf.at[1], TAIL_SUB, s0))
        moe = jnp.concatenate([g1 * y1_lo + g2 * y2_lo, g1 * y1_hi + g2 * y2_hi], axis=1)
        h2 = h_ref[rs, :] + moe
        ple = _rms(jnp.dot(p_ref[rs, :].astype(_bf16), wpp_ref[...], preferred_element_type=_f32),
                   gp_ref[...])
        lg = jnp.dot(h2.astype(_bf16), wpg_ref[...], preferred_element_type=_f32) + bpg_ref[...]
        h3 = h2 + (1.0 / (1.0 + jnp.exp(-lg))) * ple
        out_ref[rs, :] = _rms(h3, gfin_ref[...])

    @pl.when(i == n_steps - 1)
    def _():
        wait_rows()


def _tail(pos, h1, gates, y_rows, p2, w_pp, g_ple, w_pg, b_pg, g_fin):
    t = h1.shape[0]
    row = lambda i, ps: (i, 0)
    fix = lambda i, ps: (0, 0)
    grid_spec = pltpu.PrefetchScalarGridSpec(
        num_scalar_prefetch=1,
        grid=(t // TAIL_TM,),
        in_specs=[
            pl.BlockSpec((TAIL_TM, D_MODEL), row),
            pl.BlockSpec((TAIL_TM, LANES), row),
            pl.BlockSpec(memory_space=pl.ANY),
            pl.BlockSpec((TAIL_TM, PLE_DIM), row),
            pl.BlockSpec((PLE_DIM, D_MODEL), fix),
            pl.BlockSpec((1, D_MODEL), fix),
            pl.BlockSpec((D_MODEL, D_MODEL), fix),
            pl.BlockSpec((1, D_MODEL), fix),
            pl.BlockSpec((1, D_MODEL), fix),
        ],
        out_specs=pl.BlockSpec((TAIL_TM, D_MODEL), row),
        scratch_shapes=[
            pltpu.VMEM((TOP_K, TAIL_TM * ROW_TILE, LANES), y_rows.dtype),
            pltpu.VMEM((TOP_K, TAIL_TM * ROW_TILE, LANES), y_rows.dtype),
            pltpu.SemaphoreType.DMA(()),
        ],
    )
    return pl.pallas_call(
        functools.partial(_tail_kernel, n_tok=t),
        grid_spec=grid_spec,
        out_shape=jax.ShapeDtypeStruct((t, D_MODEL), _f32),
        compiler_params=pltpu.CompilerParams(
            dimension_semantics=("arbitrary",),
            vmem_limit_bytes=VMEM_LIMIT),
        name="tail",
    )(pos, h1, gates, y_rows, p2, w_pp, g_ple, w_pg, b_pg, g_fin)


def _inv_freq_row():
    inv_freq = ROPE_THETA ** (-jnp.arange(0, ROT_DIM, 2, dtype=_f32) / ROT_DIM)
    comp = jnp.concatenate([inv_freq, inv_freq, jnp.zeros((DIFF_HEAD_DIM - ROT_DIM,), _f32)])
    return jnp.concatenate([comp, comp]).reshape(1, LANES)


def kernel(x, p, positions, norm_mix_g, w_in, conv_w, conv_b, dt_bias_f, dt_bias_b, a_log_f, a_log_b, d_skip, ssd_norm_g, lam_q1, lam_k1, lam_q2, lam_k2, subln_g, w_out, norm_ffn_g, w_route_group, b_route_group, w_route_expert, b_route_expert, w_exp_gate, w_exp_up, w_exp_down, w_ple_proj, ple_norm_g, w_ple_gate, b_ple_gate, final_norm_g):
    b, seq, d = x.shape
    t = b * seq
    x2 = x.reshape(t, d)
    row = lambda v: v.reshape(1, -1).astype(_f32)

    w_in_t = jnp.swapaxes(w_in[0], 0, 1)
    pos_col = positions.astype(_f32).reshape(t, 1)
    pad_dt = lambda v: jnp.pad(v, (0, DT_PAD - 2 * SSD_HEADS)).reshape(1, DT_PAD)
    dt_bias = pad_dt(jnp.concatenate([dt_bias_f[0], dt_bias_b[0]]))
    a_row = pad_dt(jnp.concatenate([-jnp.exp(a_log_f[0]), -jnp.exp(a_log_b[0])]))
    d_skip_row = jnp.repeat(d_skip[0], SSD_HEADDIM).reshape(1, SSD_W)
    lam_vecs = jnp.stack([lam_q1[0], lam_k1[0], lam_q2[0], lam_k2[0]])
    n_route = MOE_GROUPS + N_EXPERTS
    w_route = jnp.pad(jnp.concatenate([w_route_group[0], w_route_expert[0]], axis=1),
                      ((0, 0), (0, LANES - n_route)))
    w_route_hi = w_route.astype(_bf16)
    w_route_lo = (w_route - w_route_hi.astype(_f32)).astype(_bf16)
    w_route2 = jnp.concatenate([w_route_hi, w_route_lo], axis=1)
    b_route = jnp.pad(jnp.concatenate([b_route_group[0], b_route_expert[0]]),
                      (0, LANES - n_route)).reshape(1, LANES)

    proj, dt, *rot = _inproj(x2, row(norm_mix_g[0]), pos_col, _inv_freq_row(), w_in_t)
    proj3 = proj.reshape(b, seq, MAIN_W)
    y_ssd = _ssd(proj3, dt.reshape(b, seq, DT_PAD), conv_w[0], row(conv_b[0]),
                 dt_bias, a_row, d_skip_row)
    att = _attn(proj3, [r.reshape(b, seq, LANES) for r in rot], lam_vecs, row(subln_g[0]))
    h1, hn_packed, route, cnt, gates = _outproj(
        y_ssd.reshape(t, SSD_W), proj, att.reshape(t, ATTN_W), x2, w_out[0].astype(_bf16),
        row(ssd_norm_g[0]), row(norm_ffn_g[0]), w_route2, b_route)
    pos8, blk8, meta8 = _plan(route, cnt)
    pos = pos8[:TOP_K].reshape(-1)
    block_expert = blk8[0, :_n_rows(t) // ROW_BLOCK]
    n_used = blk8[1, :1]
    x_rows = _dispatch_rows(pos, meta8[0, :N_EXPERTS], meta8[1, :N_EXPERTS], n_used, hn_packed)
    y_rows = _moe(block_expert, n_used, x_rows, w_exp_gate[0], w_exp_up[0], w_exp_down[0])
    out = _tail(pos, h1, gates, y_rows, p[0].reshape(t, PLE_DIM), w_ple_proj[0].astype(_bf16),
                row(ple_norm_g[0]), w_ple_gate[0].astype(_bf16), row(b_ple_gate[0]),
                row(final_norm_g))
    return out.reshape(b, seq, d)
```

```python
import functools

import jax
import jax.numpy as jnp
import numpy as np
from jax import lax
from jax.experimental import pallas as pl
from jax.experimental.pallas import tpu as pltpu

D_MODEL = 2048
PLE_DIM = 256
SSD_W = 1024
ATTN_W = 1024
SSD_HEADDIM = 64
SSD_HEADS = 16
SSD_GROUPS = 2
HEADS_PER_GROUP = SSD_HEADS // SSD_GROUPS
SSD_STATE = 128
CHUNK = 128
CONV_W = 5
XBC_W = SSD_W + 2 * SSD_GROUPS * SSD_STATE
DIFF_HEAD_DIM = 64
DIFF_HEADS = 8
ROT_DIM = 16
ROPE_THETA = 500000.0
MOE_GROUPS = 8
EXPERTS_PER_GROUP = 8
N_EXPERTS = 64
TOP_K = 2
EXPERT_FF = 512
ROW_BLOCK = 128
EPS = 1e-6
LAM_INIT = 0.2

LANES = 128
DT_PAD = LANES
MAIN_W = SSD_W + XBC_W + 3 * ATTN_W
COL_Z = 0
COL_Q = SSD_W
COL_K = COL_Q + ATTN_W
COL_XBC = COL_K + ATTN_W
COL_V = COL_XBC + XBC_W
VMEM_LIMIT = 56 * 1024 * 1024
NEG_BIG = -1e30
Q_SCALE = float(DIFF_HEAD_DIM ** -0.5 * np.log2(np.e))

_f32 = jnp.float32
_bf16 = jnp.bfloat16


def _silu(v):
    return v * (1.0 / (1.0 + jnp.exp(-v)))


def _rms(v, g):
    return v * lax.rsqrt(jnp.mean(v * v, axis=-1, keepdims=True) + EPS) * g


_HI16 = np.uint32(0xFFFF0000)


def _pack_bf16_pair(lo, hi):
    lo_w = lax.bitcast_convert_type(lo.astype(_bf16).astype(_f32), jnp.uint32) >> 16
    hi_w = lax.bitcast_convert_type(hi.astype(_bf16).astype(_f32), jnp.uint32) & _HI16
    return lo_w | hi_w


def _unpack_bf16_pair(w):
    lo = lax.bitcast_convert_type(w << 16, _f32)
    hi = lax.bitcast_convert_type(w & _HI16, _f32)
    return lo, hi


ROW_TILE = 8
assert D_MODEL // 2 == ROW_TILE * LANES


def _store_row_tiles(ref, words, n, row0=0):
    for c in range(ROW_TILE):
        ref[pl.ds(row0 * ROW_TILE + c, n, stride=ROW_TILE), :] = words[:, c * LANES:(c + 1) * LANES]


def _load_row_tiles(ref, n, row0=0):
    return jnp.concatenate(
        [ref[pl.ds(row0 * ROW_TILE + c, n, stride=ROW_TILE), :] for c in range(ROW_TILE)], axis=1)


IN_TM = 1024
IN_TN = 512


def _inproj_kernel(x_ref, g_ref, pos_ref, invf_ref, wt_ref, wdt_ref,
                   out_ref, dt_ref, rc_ref, rp_ref, rm_ref, n_scr):
    j = pl.program_id(1)
    nt = (((1,), (1,)), ((), ()))
    half = ROT_DIM // 2

    @pl.when(j == 0)
    def _():
        n = _rms(x_ref[...], g_ref[...])
        n_scr[...] = n.astype(_bf16)
        lane = lax.broadcasted_iota(jnp.int32, (1, LANES), 1)
        dt = lax.dot_general(n_scr[...], wdt_ref[...].astype(_bf16), nt, preferred_element_type=_f32)
        dt_ref[...] = jnp.where(lane < 2 * SSD_HEADS, dt, 0.0)
        ang = pos_ref[...] * invf_ref[...]
        cos, sin = jnp.cos(ang), jnp.sin(ang)
        l64 = lane & (DIFF_HEAD_DIM - 1)
        rc_ref[...] = jnp.where(l64 < ROT_DIM, cos, 1.0)
        rp_ref[...] = jnp.where(jnp.logical_and(l64 >= half, l64 < ROT_DIM), sin, 0.0)
        rm_ref[...] = jnp.where(l64 < half, -sin, 0.0)

    out_ref[...] = lax.dot_general(n_scr[...], wt_ref[...].astype(_bf16), nt,
                                   preferred_element_type=_f32).astype(_bf16)


def _rotate(t, rc, rp, rm):
    half = ROT_DIM // 2
    return t * rc + pltpu.roll(t, half, 1) * rp + pltpu.roll(t, LANES - half, 1) * rm


SRC_XBC = SSD_W
SRC_DT = SRC_XBC + XBC_W
SRC_Q = SRC_DT + 2 * SSD_HEADS
SRC_V = SRC_Q + 2 * ATTN_W
SRC_ALIGN = 32
assert all(s % SRC_ALIGN == 0 for s in (SRC_XBC, SRC_DT, SRC_Q, SRC_V, IN_TN))


def _src_row(j):
    jq, jx, jv = COL_Q // IN_TN, COL_XBC // IN_TN, COL_V // IN_TN
    row = jnp.where(j < jq, j * IN_TN,
                    jnp.where(j < jx, SRC_Q + (j - jq) * IN_TN,
                              jnp.where(j < jv, SRC_XBC + (j - jx) * IN_TN,
                                        SRC_V + (j - jv) * IN_TN)))
    return pl.multiple_of(row, SRC_ALIGN)


def _inproj(x2, g, pos_col, invf_row, w_in_t):
    t = x2.shape[0]
    grid = (t // IN_TM, MAIN_W // IN_TN)
    return pl.pallas_call(
        _inproj_kernel,
        grid=grid,
        in_specs=[
            pl.BlockSpec((IN_TM, D_MODEL), lambda i, j: (i, 0)),
            pl.BlockSpec((1, D_MODEL), lambda i, j: (0, 0)),
            pl.BlockSpec((IN_TM, 1), lambda i, j: (i, 0)),
            pl.BlockSpec((1, LANES), lambda i, j: (0, 0)),
            pl.BlockSpec((pl.Element(IN_TN), pl.Element(D_MODEL)), lambda i, j: (_src_row(j), 0)),
            pl.BlockSpec((pl.Element(DT_PAD), pl.Element(D_MODEL)), lambda i, j: (SRC_DT, 0)),
        ],
        out_specs=[
            pl.BlockSpec((IN_TM, IN_TN), lambda i, j: (i, j)),
            pl.BlockSpec((IN_TM, DT_PAD), lambda i, j: (i, 0)),
        ] + [pl.BlockSpec((IN_TM, LANES), lambda i, j: (i, 0))] * 3,
        out_shape=[
            jax.ShapeDtypeStruct((t, MAIN_W), _bf16),
            jax.ShapeDtypeStruct((t, DT_PAD), _f32),
        ] + [jax.ShapeDtypeStruct((t, LANES), _f32)] * 3,
        scratch_shapes=[pltpu.VMEM((IN_TM, D_MODEL), _bf16)],
        compiler_params=pltpu.CompilerParams(
            dimension_semantics=("arbitrary", "arbitrary"),
            vmem_limit_bytes=VMEM_LIMIT),
        name="inproj",
    )(x2, g, pos_col, invf_row, w_in_t, w_in_t)


CONV_HALO = 16


def _ssd_kernel(xbc_ref, dt_ref, cw_ref, cb_ref, dtb_ref, a_ref, dsk_ref,
                y_ref, xpad, xact, dts, yacc, state_f, state_b, seq):
    n_chunks = seq // CHUNK

    zeros_halo = jnp.zeros((CONV_HALO, XBC_W), _bf16)
    xpad[0:CONV_HALO, :] = zeros_halo
    xpad[CONV_HALO + seq:CONV_HALO + seq + CONV_HALO, :] = zeros_halo
    xpad[CONV_HALO:CONV_HALO + seq, :] = xbc_ref[0]
    cw = cw_ref[...]
    cb = cb_ref[...]
    win = CHUNK + 2 * CONV_HALO

    def conv_body(c, carry):
        r0 = pl.multiple_of(c * CHUNK, CHUNK)
        blk = xpad[pl.ds(r0, win), :].astype(_f32)
        acc = blk * cw[CONV_W // 2:CONV_W // 2 + 1, :]
        for k in range(CONV_W):
            sh = CONV_W // 2 - k
            if sh == 0:
                continue
            acc = acc + pltpu.roll(blk, sh % win, 0) * cw[k:k + 1, :]
        v = acc[CONV_HALO:CONV_HALO + CHUNK, :] + cb
        xact[pl.ds(r0, CHUNK), :] = _silu(v).astype(_bf16)
        return carry

    lax.fori_loop(0, n_chunks, conv_body, 0)

    raw = dt_ref[0] + dtb_ref[...]
    dts[...] = jnp.maximum(raw, 0.0) + jnp.log1p(jnp.exp(-jnp.abs(raw)))

    a_row = a_ref[...]
    dsk = dsk_ref[...]
    row_i = lax.broadcasted_iota(jnp.int32, (CHUNK, CHUNK), 0)
    col_i = lax.broadcasted_iota(jnp.int32, (CHUNK, CHUNK), 1)

    def chunk_step(c, reverse, state):
        r0 = pl.multiple_of(c * CHUNK, CHUNK)
        dtc = dts[pl.ds(r0, CHUNK), :]
        cs = dtc * a_row
        k = 1
        while k < CHUNK:
            if reverse:
                cs = cs + jnp.where(row_i < CHUNK - k, pltpu.roll(cs, CHUNK - k, 0), 0.0)
            else:
                cs = cs + jnp.where(row_i >= k, pltpu.roll(cs, k, 0), 0.0)
            k *= 2
        cs_t = cs.T
        dt_t = dtc.T
        end_col = cs_t[:, 0:1] if reverse else cs_t[:, CHUNK - 1:CHUNK]
        w_t = jnp.exp(end_col - cs_t) * dt_t
        end_row = cs[0:1, :] if reverse else cs[CHUNK - 1:CHUNK, :]
        dec_row = jnp.exp(end_row)
        tri = (row_i <= col_i) if reverse else (row_i >= col_i)
        lane0 = SSD_HEADS if reverse else 0
        first_head = col_i < SSD_HEADDIM
        rows = pl.ds(r0, CHUNK)
        for g in range(SSD_GROUPS):
            b_g = xact[rows, SSD_W + g * SSD_STATE:SSD_W + (g + 1) * SSD_STATE]
            c_g = xact[rows, SSD_W + (SSD_GROUPS + g) * SSD_STATE:SSD_W + (SSD_GROUPS + g + 1) * SSD_STATE]
            cbm = lax.dot_general(c_g, b_g, (((1,), (1,)), ((), ())),
                                  preferred_element_type=_f32)
            b_t = b_g.astype(_f32).T
            for pp in range(HEADS_PER_GROUP // 2):
                pair = g * (HEADS_PER_GROUP // 2) + pp
                cols = slice(pair * LANES, (pair + 1) * LANES)
                xs_p = xact[rows, cols]
                st = state[pair]
                off = jnp.dot(c_g, st.astype(_bf16), preferred_element_type=_f32)
                halves = []
                for hh in range(2):
                    ln = lane0 + 2 * pair + hh
                    colb = jnp.broadcast_to(cs[:, ln:ln + 1], (CHUNK, CHUNK))
                    m_h = (cbm * jnp.exp(jnp.where(tri, colb - cs_t[ln:ln + 1, :], NEG_BIG))
                           * dt_t[ln:ln + 1, :])
                    y_h = (jnp.dot(m_h.astype(_bf16), xs_p, preferred_element_type=_f32)
                           + jnp.exp(colb) * off)
                    upd = jnp.dot((b_t * w_t[ln:ln + 1, :]).astype(_bf16), xs_p,
                                  preferred_element_type=_f32)
                    dec = jnp.broadcast_to(dec_row[:, ln:ln + 1], (CHUNK, CHUNK))
                    halves.append((y_h, upd, dec))
                y_p = jnp.where(first_head, halves[0][0], halves[1][0])
                upd_p = jnp.where(first_head, halves[0][1], halves[1][1])
                dec_p = jnp.where(first_head, halves[0][2], halves[1][2])
                state[pair] = st * dec_p + upd_p
                if reverse:
                    y_ref[0, rows, cols] = (yacc[rows, cols] + y_p).astype(_bf16)
                else:
                    yacc[rows, cols] = y_p + xs_p.astype(_f32) * dsk[:, cols]

    state_f[...] = jnp.zeros_like(state_f)
    state_b[...] = jnp.zeros_like(state_b)

    def fwd_body(c, carry):
        chunk_step(c, False, state_f)
        return carry

    lax.fori_loop(0, n_chunks, fwd_body, 0)

    def bwd_body(i, carry):
        chunk_step(n_chunks - 1 - i, True, state_b)
        return carry

    lax.fori_loop(0, n_chunks, bwd_body, 0)


def _ssd(proj3, dt3, conv_w, conv_b, dt_bias, a_row, d_skip_row):
    b, seq, _ = proj3.shape
    kern = functools.partial(_ssd_kernel, seq=seq)
    assert COL_XBC % XBC_W == 0
    return pl.pallas_call(
        kern,
        grid=(b,),
        in_specs=[
            pl.BlockSpec((1, seq, XBC_W), lambda i: (i, 0, COL_XBC // XBC_W)),
            pl.BlockSpec((1, seq, DT_PAD), lambda i: (i, 0, 0)),
            pl.BlockSpec((CONV_W, XBC_W), lambda i: (0, 0)),
            pl.BlockSpec((1, XBC_W), lambda i: (0, 0)),
            pl.BlockSpec((1, DT_PAD), lambda i: (0, 0)),
            pl.BlockSpec((1, DT_PAD), lambda i: (0, 0)),
            pl.BlockSpec((1, SSD_W), lambda i: (0, 0)),
        ],
        out_specs=pl.BlockSpec((1, seq, SSD_W), lambda i: (i, 0, 0)),
        out_shape=jax.ShapeDtypeStruct((b, seq, SSD_W), _bf16),
        scratch_shapes=[
            pltpu.VMEM((seq + 2 * CONV_HALO, XBC_W), _bf16),
            pltpu.VMEM((seq, XBC_W), _bf16),
            pltpu.VMEM((seq, DT_PAD), _f32),
            pltpu.VMEM((seq, SSD_W), _f32),
            pltpu.VMEM((SSD_HEADS // 2, SSD_STATE, 2 * SSD_HEADDIM), _f32),
            pltpu.VMEM((SSD_HEADS // 2, SSD_STATE, 2 * SSD_HEADDIM), _f32),
        ],
        compiler_params=pltpu.CompilerParams(
            dimension_semantics=("arbitrary",),
            vmem_limit_bytes=VMEM_LIMIT),
        name="ssd",
    )(proj3, dt3, conv_w, conv_b, dt_bias, a_row, d_skip_row)


ATT_QB = 256
ATT_HEADS = 2


def _attn_kernel(q_ref, k_ref, v_ref, rc_ref, rp_ref, rm_ref, lam_ref, g_ref, o_ref,
                 k_scr, v1_scr, *, seq):
    hw = 2 * DIFF_HEAD_DIM
    lv = lam_ref[...]
    lam = (jnp.exp(jnp.sum(lv[0:1] * lv[1:2], axis=-1, keepdims=True))
           - jnp.exp(jnp.sum(lv[2:3] * lv[3:4], axis=-1, keepdims=True)) + LAM_INIT)

    def rotated(ref, rows, cols):
        return _rotate(ref[0, rows, cols].astype(_f32), rc_ref[0, rows, :], rp_ref[0, rows, :],
                       rm_ref[0, rows, :])

    g = g_ref[...]
    first = lax.broadcasted_iota(jnp.int32, (1, hw), 1) < DIFF_HEAD_DIM
    ones_col = (lax.broadcasted_iota(jnp.int32, (seq, hw), 1) == 0).astype(_bf16)
    nt = (((1,), (1,)), ((), ()))

    def prepare(hd):
        cols = slice(hd * hw, (hd + 1) * hw)
        for c in range(seq // ATT_QB):
            rows = slice(c * ATT_QB, (c + 1) * ATT_QB)
            k_scr[hd, rows, :] = rotated(k_ref, rows, cols).astype(_bf16)
        v1_scr[hd, :, 0:hw] = v_ref[0, :, cols]
        v1_scr[hd, :, hw:2 * hw] = ones_col

    def chain(hd, r0):
        cols = slice(hd * hw, (hd + 1) * hw)
        rows = slice(r0, r0 + ATT_QB)
        q = (rotated(q_ref, rows, cols) * Q_SCALE).astype(_bf16)
        zero = jnp.zeros_like(q)
        ovs = []
        for qc in (jnp.where(first, q, zero), jnp.where(first, zero, q)):
            s = lax.dot_general(qc, k_scr[hd], nt, preferred_element_type=_f32)
            e = jnp.exp2(s - jnp.max(s, axis=-1, keepdims=True)).astype(_bf16)
            ovs.append(jnp.dot(e, v1_scr[hd], preferred_element_type=_f32))
        o1, l1 = ovs[0][:, 0:hw], ovs[0][:, hw:hw + 1]
        o2, l2 = ovs[1][:, 0:hw], ovs[1][:, hw:hw + 1]
        o = o1 / l1 - (lam / l2) * o2
        o = _rms(o, g) * (1.0 - LAM_INIT)
        o_ref[0, rows, cols] = o.astype(_bf16)

    for hd in range(ATT_HEADS):
        prepare(hd)
    for hd in range(ATT_HEADS):
        for c in range(seq // ATT_QB):
            chain(hd, c * ATT_QB)


def _attn(proj3, rot_tables, lam_vecs, subln_g):
    b, seq, _ = proj3.shape
    hw = 2 * DIFF_HEAD_DIM
    gw = ATT_HEADS * hw
    assert COL_Q % gw == 0 and COL_K % gw == 0 and COL_V % gw == 0
    kern = functools.partial(_attn_kernel, seq=seq)
    return pl.pallas_call(
        kern,
        grid=(b, DIFF_HEADS // ATT_HEADS),
        in_specs=[
            pl.BlockSpec((1, seq, gw), lambda i, h: (i, 0, COL_Q // gw + h)),
            pl.BlockSpec((1, seq, gw), lambda i, h: (i, 0, COL_K // gw + h)),
            pl.BlockSpec((1, seq, gw), lambda i, h: (i, 0, COL_V // gw + h)),
        ] + [pl.BlockSpec((1, seq, LANES), lambda i, h: (i, 0, 0))] * 3 + [
            pl.BlockSpec((4, DIFF_HEAD_DIM), lambda i, h: (0, 0)),
            pl.BlockSpec((1, hw), lambda i, h: (0, 0)),
        ],
        out_specs=pl.BlockSpec((1, seq, gw), lambda i, h: (i, 0, h)),
        out_shape=jax.ShapeDtypeStruct((b, seq, ATTN_W), _bf16),
        scratch_shapes=[pltpu.VMEM((ATT_HEADS, seq, hw), _bf16),
                        pltpu.VMEM((ATT_HEADS, seq, 2 * hw), _bf16)],
        compiler_params=pltpu.CompilerParams(
            dimension_semantics=("arbitrary", "arbitrary"),
            vmem_limit_bytes=VMEM_LIMIT),
        name="diffattn",
    )(proj3, proj3, proj3, *rot_tables, lam_vecs, subln_g)


OUT_TM = 512
ROUTE_ROWS = 8


def _outproj_kernel(y_ref, z_ref, att_ref, x_ref, w_ref, gs_ref, gf_ref, wr_ref, br_ref,
                    h_ref, hn_ref, route_ref, cnt_ref, gate_ref, cnt_scr):
    @pl.when(pl.program_id(0) == 0)
    def _():
        cnt_scr[...] = jnp.zeros_like(cnt_scr)

    y = y_ref[...].astype(_f32)
    z = z_ref[...].astype(_f32)
    s = _rms(y * _silu(z), gs_ref[...]).astype(_bf16)
    acc = jnp.dot(s, w_ref[0:SSD_W, :], preferred_element_type=_f32)
    acc = acc + jnp.dot(att_ref[...], w_ref[SSD_W:SSD_W + ATTN_W, :], preferred_element_type=_f32)
    h1 = x_ref[...] + acc
    h_ref[...] = h1
    hn = _rms(h1, gf_ref[...])
    half = D_MODEL // 2
    _store_row_tiles(hn_ref, _pack_bf16_pair(hn[:, 0:half], hn[:, half:D_MODEL]), hn.shape[0])
    hn_hi = hn.astype(_bf16)
    hn_lo = (hn - hn_hi.astype(_f32)).astype(_bf16)
    a = jnp.dot(hn_hi, wr_ref[...], preferred_element_type=_f32)
    bb = jnp.dot(hn_lo, wr_ref[:, 0:LANES], preferred_element_type=_f32)
    logits = (a[:, 0:LANES] + a[:, LANES:2 * LANES] + bb + br_ref[...]).T
    tm = logits.shape[1]
    iota = lax.broadcasted_iota(jnp.int32, (MOE_GROUPS, tm), 0)

    def first_argmax(val, vmax):
        return jnp.min(jnp.where(val == vmax, iota, MOE_GROUPS), axis=0, keepdims=True)

    gl = logits[0:MOE_GROUPS]
    gmax = jnp.max(gl, axis=0, keepdims=True)
    p_g = 1.0 / jnp.sum(jnp.exp(gl - gmax), axis=0, keepdims=True)
    g_sel = first_argmax(gl, gmax)
    el = jnp.zeros((EXPERTS_PER_GROUP, tm), _f32)
    for gi in range(MOE_GROUPS):
        lo = MOE_GROUPS + gi * EXPERTS_PER_GROUP
        el = jnp.where(g_sel == gi, logits[lo:lo + EXPERTS_PER_GROUP], el)
    ee = jnp.exp(el - jnp.max(el, axis=0, keepdims=True))
    pe = ee / jnp.sum(ee, axis=0, keepdims=True)
    p1 = jnp.max(pe, axis=0, keepdims=True)
    i1 = first_argmax(pe, p1)
    pe2 = jnp.where(iota == i1, -1.0, pe)
    p2 = jnp.max(pe2, axis=0, keepdims=True)
    i2 = first_argmax(pe2, p2)
    den = p1 + p2
    e1 = g_sel * EXPERTS_PER_GROUP + i1
    e2 = g_sel * EXPERTS_PER_GROUP + i2
    iota_e = lax.broadcasted_iota(jnp.int32, (N_EXPERTS, tm), 0)
    oh1 = (iota_e == e1).astype(_f32)
    oh2 = (iota_e == e2).astype(_f32)
    both = oh1 + oh2
    earlier = (lax.broadcasted_iota(jnp.int32, (tm, tm), 0)
               < lax.broadcasted_iota(jnp.int32, (tm, tm), 1)).astype(_bf16)
    before = cnt_scr[:, 0:1] + jnp.dot(both.astype(_bf16), earlier,
                                       preferred_element_type=_f32)
    r1 = jnp.sum(oh1 * before, axis=0, keepdims=True).astype(jnp.int32)
    r2 = jnp.sum(oh2 * before, axis=0, keepdims=True).astype(jnp.int32)
    cnt_scr[...] = cnt_scr[...] + jnp.sum(both, axis=1, keepdims=True)
    cnt_ref[...] = cnt_scr[...].astype(jnp.int32)
    route_ref[...] = jnp.where(iota == 0, e1, jnp.where(iota == 1, e2,
                               jnp.where(iota == 2, r1, jnp.where(iota == 3, r2, 0))))
    g8 = jnp.where(iota == 0, p_g * (p1 / den), jnp.where(iota == 1, p_g * (p2 / den), 0.0))
    gate_ref[...] = jnp.concatenate(
        [g8, jnp.zeros((LANES - ROUTE_ROWS, tm), _f32)], axis=0).T


def _outproj(y2, proj2, att2, x2, w_out, g_ssd, g_ffn, wr_t, br_col):
    t = x2.shape[0]
    row = lambda i: (i, 0)
    fix = lambda i: (0, 0)
    return pl.pallas_call(
        _outproj_kernel,
        grid=(t // OUT_TM,),
        in_specs=[
            pl.BlockSpec((OUT_TM, SSD_W), row),
            pl.BlockSpec((OUT_TM, SSD_W), row),
            pl.BlockSpec((OUT_TM, ATTN_W), row),
            pl.BlockSpec((OUT_TM, D_MODEL), row),
            pl.BlockSpec((SSD_W + ATTN_W, D_MODEL), fix),
            pl.BlockSpec((1, SSD_W), fix),
            pl.BlockSpec((1, D_MODEL), fix),
            pl.BlockSpec((D_MODEL, 2 * LANES), fix),
            pl.BlockSpec((1, LANES), fix),
        ],
        out_specs=[
            pl.BlockSpec((OUT_TM, D_MODEL), row),
            pl.BlockSpec((OUT_TM * ROW_TILE, LANES), row),
            pl.BlockSpec((ROUTE_ROWS, OUT_TM), lambda i: (0, i)),
            pl.BlockSpec((N_EXPERTS, LANES), fix),
            pl.BlockSpec((OUT_TM, LANES), row),
        ],
        out_shape=[
            jax.ShapeDtypeStruct((t, D_MODEL), _f32),
            jax.ShapeDtypeStruct((t * ROW_TILE, LANES), jnp.uint32),
            jax.ShapeDtypeStruct((ROUTE_ROWS, t), jnp.int32),
            jax.ShapeDtypeStruct((N_EXPERTS, LANES), jnp.int32),
            jax.ShapeDtypeStruct((t, LANES), _f32),
        ],
        scratch_shapes=[pltpu.VMEM((N_EXPERTS, LANES), _f32)],
        compiler_params=pltpu.CompilerParams(
            dimension_semantics=("arbitrary",),
            vmem_limit_bytes=VMEM_LIMIT),
        name="outproj_router",
    )(y2, proj2, att2, x2, w_out, g_ssd, g_ffn, wr_t, br_col)


def _n_rows(n_tok):
    n_assign = n_tok * TOP_K
    return (n_assign + N_EXPERTS * (ROW_BLOCK - 1) + ROW_BLOCK - 1) // ROW_BLOCK * ROW_BLOCK


def _prefix_sum(v, axis):
    n = v.shape[axis]
    idx = lax.broadcasted_iota(jnp.int32, v.shape, axis)
    k = 1
    while k < n:
        v = v + jnp.where(idx >= k, pltpu.roll(v, k, axis), 0)
        k *= 2
    return v


def _plan_kernel(route_ref, cnt_ref, pos_ref, blk_ref, meta_ref, *, n_blocks_pad):
    n_tok = route_ref.shape[1]
    cnt_col = jnp.concatenate(
        [cnt_ref[...], jnp.zeros((LANES - N_EXPERTS, LANES), jnp.int32)], axis=0)
    pad_up = lambda c: (c + (ROW_BLOCK - 1)) & (-ROW_BLOCK)
    ends_col = _prefix_sum(pad_up(cnt_col), 0)
    starts_col = (ends_col - pad_up(cnt_col)).astype(_f32)
    cnt_row = cnt_col.astype(_f32).T.astype(jnp.int32)
    ends_row = _prefix_sum(pad_up(cnt_row), 1)
    starts_row = ends_row - pad_up(cnt_row)
    n_used = ends_row[0:1, LANES - 1:LANES] >> (ROW_BLOCK.bit_length() - 1)

    ch = 1024
    iota_e = lax.broadcasted_iota(jnp.int32, (LANES, ch), 0)
    for c0 in range(0, n_tok, ch):
        rt = route_ref[:, c0:c0 + ch]
        s1 = jnp.sum(jnp.where(iota_e == rt[0:1], starts_col[:, 0:1], 0.0), axis=0, keepdims=True)
        s2 = jnp.sum(jnp.where(iota_e == rt[1:2], starts_col[:, 0:1], 0.0), axis=0, keepdims=True)
        p1 = s1.astype(jnp.int32) + rt[2:3]
        p2 = s2.astype(jnp.int32) + rt[3:4]
        sub = lax.broadcasted_iota(jnp.int32, (ROUTE_ROWS, ch), 0)
        pos_ref[:, c0:c0 + ch] = jnp.where(sub == 0, p1, jnp.where(sub == 1, p2, 0))

    blk_start = lax.broadcasted_iota(jnp.int32, (LANES, n_blocks_pad), 1) * ROW_BLOCK
    e_idx = lax.broadcasted_iota(jnp.int32, (LANES, n_blocks_pad), 0)
    real = e_idx < N_EXPERTS
    be = jnp.sum(jnp.where(jnp.logical_and(real, ends_col[:, 0:1] <= blk_start), 1.0, 0.0),
                 axis=0, keepdims=True).astype(jnp.int32)
    be = jnp.minimum(be, N_EXPERTS - 1)
    last = jnp.max(jnp.where(jnp.logical_and(real, cnt_col[:, 0:1] > 0), e_idx.astype(_f32), 0.0),
                   axis=0, keepdims=True).astype(jnp.int32)
    blk_i = lax.broadcasted_iota(jnp.int32, (1, n_blocks_pad), 1)
    be = jnp.where(blk_i < n_used, be, last)
    sub = lax.broadcasted_iota(jnp.int32, (ROUTE_ROWS, n_blocks_pad), 0)
    blk_ref[...] = jnp.where(sub == 0, be, jnp.where(sub == 1, n_used, 0))
    sub = lax.broadcasted_iota(jnp.int32, (ROUTE_ROWS, LANES), 0)
    meta_ref[...] = jnp.where(sub == 0, cnt_row[0:1], jnp.where(sub == 1, starts_row[0:1], 0))


def _plan(route, cnt):
    n_tok = route.shape[1]
    n_blocks_pad = -(-(_n_rows(n_tok) // ROW_BLOCK) // LANES) * LANES
    return pl.pallas_call(
        functools.partial(_plan_kernel, n_blocks_pad=n_blocks_pad),
        out_shape=[
            jax.ShapeDtypeStruct((ROUTE_ROWS, n_tok), jnp.int32),
            jax.ShapeDtypeStruct((ROUTE_ROWS, n_blocks_pad), jnp.int32),
            jax.ShapeDtypeStruct((ROUTE_ROWS, LANES), jnp.int32),
        ],
        compiler_params=pltpu.CompilerParams(vmem_limit_bytes=VMEM_LIMIT),
        name="route_plan",
    )(route, cnt)


DISP_TM = 1024


def _dispatch_kernel(pos_ref, cnt_ref, start_ref, nused_ref, h_ref, x_hbm, stage, zrow, sem, zsem,
                     *, n_tok, n_blocks):
    i = pl.program_id(0)
    n_steps = pl.num_programs(0)
    slot = i % 2
    base = i * DISP_TM
    tile = lambda r: pl.ds(pl.multiple_of(r * ROW_TILE, ROW_TILE), ROW_TILE)

    def wait_slot(sl):
        for _ in range(TOP_K):
            pltpu.make_async_copy(stage.at[sl], x_hbm.at[pl.ds(0, DISP_TM * ROW_TILE), :],
                                  sem.at[sl]).wait()

    @pl.when(i >= 2)
    def _():
        wait_slot(slot)

    stage[slot] = h_ref[...]

    def body(r, carry):
        for kk in range(TOP_K):
            row = pos_ref[kk * n_tok + base + r]
            pltpu.make_async_copy(stage.at[slot, tile(r), :], x_hbm.at[tile(row), :],
                                  sem.at[slot]).start(priority=kk % 2)
        return carry

    lax.fori_loop(0, DISP_TM, body, 0, unroll=8)

    @pl.when(i == n_steps - 1)
    def _():
        zrow[...] = jnp.zeros_like(zrow)

        pieces = [1 << b for b in reversed(range(ROW_BLOCK.bit_length() - 1))]

        def pad_copy(size, row0):
            src = zrow.at[pl.ds(0, size * ROW_TILE), :]
            dst = x_hbm.at[pl.ds(pl.multiple_of(row0 * ROW_TILE, ROW_TILE), size * ROW_TILE), :]
            return pltpu.make_async_copy(src, dst, zsem)

        def per_expert(e, carry, wait):
            cnt = cnt_ref[e]
            n_pad = ((cnt + (ROW_BLOCK - 1)) & (-ROW_BLOCK)) - cnt
            row0 = start_ref[e] + cnt
            for size in pieces:
                has = (n_pad & size) != 0

                @pl.when(has)
                def _():
                    cp = pad_copy(size, row0)
                    cp.wait() if wait else cp.start()

                row0 = row0 + jnp.where(has, size, 0)
            return carry

        lax.fori_loop(0, N_EXPERTS, functools.partial(per_expert, wait=False), 0)
        blk_rows = ROW_BLOCK * ROW_TILE

        def fill_block(blk, carry):
            dst = pl.ds(pl.multiple_of(blk * blk_rows, blk_rows), blk_rows)
            pltpu.make_async_copy(zrow, x_hbm.at[dst, :], zsem).start()
            return carry

        lax.fori_loop(nused_ref[0], n_blocks, fill_block, 0)

        lax.fori_loop(0, N_EXPERTS, functools.partial(per_expert, wait=True), 0)

        def wait_block(blk, carry):
            pltpu.make_async_copy(zrow, x_hbm.at[pl.ds(0, blk_rows), :], zsem).wait()
            return carry

        lax.fori_loop(nused_ref[0], n_blocks, wait_block, 0)

        @pl.when(n_steps >= 2)
        def _():
            wait_slot(1 - slot)

        wait_slot(slot)


def _dispatch_rows(pos, counts, starts, n_used, hn_packed):
    t = hn_packed.shape[0] // ROW_TILE
    n_rows = _n_rows(t)
    blk = (DISP_TM * ROW_TILE, LANES)
    grid_spec = pltpu.PrefetchScalarGridSpec(
        num_scalar_prefetch=4,
        grid=(t // DISP_TM,),
        in_specs=[pl.BlockSpec(blk, lambda i, *_: (i, 0))],
        out_specs=pl.BlockSpec(memory_space=pl.ANY),
        scratch_shapes=[
            pltpu.VMEM((2,) + blk, hn_packed.dtype),
            pltpu.VMEM((ROW_BLOCK * ROW_TILE, LANES), hn_packed.dtype),
            pltpu.SemaphoreType.DMA((2,)),
            pltpu.SemaphoreType.DMA(()),
        ],
    )
    return pl.pallas_call(
        functools.partial(_dispatch_kernel, n_tok=t, n_blocks=n_rows // ROW_BLOCK),
        grid_spec=grid_spec,
        out_shape=jax.ShapeDtypeStruct((n_rows * ROW_TILE, LANES), hn_packed.dtype),
        compiler_params=pltpu.CompilerParams(
            dimension_semantics=("arbitrary",),
            vmem_limit_bytes=VMEM_LIMIT),
        name="dispatch_rows",
    )(pos, counts, starts, n_used, hn_packed)


MOE_W_SPLIT = 2
MOE_W_BUFS = 3
MOE_GROUP = 4


def _moe_kernel(be_ref, nused_ref, x_ref, wg_hbm, wu_hbm, wd_hbm,
                out_ref, wg_f, wu_f, wd_f, wg_s, wu_s, wd_s, wsem, nexp, *, n_blocks):
    i = pl.program_id(0)
    nused = nused_ref[0]
    half = D_MODEL // 2

    def weight_copies(e, sl):
        cps = []
        for src, dst in ((wg_hbm, wg_f), (wu_hbm, wu_f), (wd_hbm, wd_f)):
            rows = src.shape[1] // MOE_W_SPLIT
            for c in range(MOE_W_SPLIT):
                rs = pl.ds(c * rows, rows)
                cps.append(pltpu.make_async_copy(src.at[e, rs, :], dst.at[sl, rs, :], wsem.at[sl]))
        return cps

    def start_all(cps):
        for n, cp in enumerate(cps):
            cp.start(priority=n % 2)

    def block_expert(b):
        return be_ref[jnp.minimum(b, n_blocks - 1)]

    def next_change(b0):
        e0 = block_expert(b0)
        return lax.while_loop(
            lambda b: jnp.logical_and(b < nused, block_expert(b) == e0), lambda b: b + 1, b0 + 1)

    @pl.when(i == 0)
    def _():
        nexp[0] = 0
        start_all(weight_copies(be_ref[0], 0))
        nxt = next_change(0)

        @pl.when(nxt < nused)
        def _():
            start_all(weight_copies(block_expert(nxt), 1))

    def row_block(b, sub):
        sub_rows = pl.ds(sub * ROW_BLOCK * ROW_TILE, ROW_BLOCK * ROW_TILE)

        @pl.when(b < nused)
        def _():
            e = be_ref[b]
            new_expert = jnp.logical_or(b == 0, e != be_ref[jnp.maximum(b - 1, 0)])

            @pl.when(new_expert)
            def _():
                sl = nexp[0] % MOE_W_BUFS
                ahead = next_change(jnp.minimum(next_change(b), nused - 1))

                @pl.when(jnp.logical_and(next_change(b) < nused, ahead < nused))
                def _():
                    start_all(weight_copies(block_expert(ahead), (nexp[0] + 2) % MOE_W_BUFS))

                for cp in weight_copies(e, sl):
                    cp.wait()
                wg_s[...] = wg_f[sl].astype(_bf16)
                wu_s[...] = wu_f[sl].astype(_bf16)
                wd_s[...] = wd_f[sl].astype(_bf16)
                nexp[0] = nexp[0] + 1

            x_lo, x_hi = _unpack_bf16_pair(_load_row_tiles(x_ref, ROW_BLOCK, sub * ROW_BLOCK))
            x_lo, x_hi = x_lo.astype(_bf16), x_hi.astype(_bf16)
            gg = (jnp.dot(x_lo, wg_s[0:half, :], preferred_element_type=_f32)
                  + jnp.dot(x_hi, wg_s[half:D_MODEL, :], preferred_element_type=_f32))
            uu = (jnp.dot(x_lo, wu_s[0:half, :], preferred_element_type=_f32)
                  + jnp.dot(x_hi, wu_s[half:D_MODEL, :], preferred_element_type=_f32))
            a = (_silu(gg) * uu).astype(_bf16)
            y = jnp.dot(a, wd_s[...], preferred_element_type=_f32)
            _store_row_tiles(out_ref, _pack_bf16_pair(y[:, 0:half], y[:, half:D_MODEL]),
                             ROW_BLOCK, sub * ROW_BLOCK)

        @pl.when(b >= nused)
        def _():
            out_ref[sub_rows, :] = jnp.zeros((ROW_BLOCK * ROW_TILE, LANES), out_ref.dtype)

    for sub in range(MOE_GROUP):
        row_block(i * MOE_GROUP + sub, sub)


def _moe(block_expert, n_used, x_rows, w_gate, w_up, w_down):
    n_blocks = x_rows.shape[0] // (ROW_BLOCK * ROW_TILE)
    assert n_blocks % MOE_GROUP == 0
    blk = (MOE_GROUP * ROW_BLOCK * ROW_TILE, LANES)
    omap = lambda i, be, nu: (i, 0)
    grid_spec = pltpu.PrefetchScalarGridSpec(
        num_scalar_prefetch=2,
        grid=(n_blocks // MOE_GROUP,),
        in_specs=[
            pl.BlockSpec(blk, omap),
            pl.BlockSpec(memory_space=pl.ANY),
            pl.BlockSpec(memory_space=pl.ANY),
            pl.BlockSpec(memory_space=pl.ANY),
        ],
        out_specs=pl.BlockSpec(blk, omap),
        scratch_shapes=[
            pltpu.VMEM((MOE_W_BUFS, D_MODEL, EXPERT_FF), _f32),
            pltpu.VMEM((MOE_W_BUFS, D_MODEL, EXPERT_FF), _f32),
            pltpu.VMEM((MOE_W_BUFS, EXPERT_FF, D_MODEL), _f32),
            pltpu.VMEM((D_MODEL, EXPERT_FF), _bf16),
            pltpu.VMEM((D_MODEL, EXPERT_FF), _bf16),
            pltpu.VMEM((EXPERT_FF, D_MODEL), _bf16),
            pltpu.SemaphoreType.DMA((MOE_W_BUFS,)),
            pltpu.SMEM((1,), jnp.int32),
        ],
    )
    return pl.pallas_call(
        functools.partial(_moe_kernel, n_blocks=n_blocks),
        grid_spec=grid_spec,
        out_shape=jax.ShapeDtypeStruct(x_rows.shape, x_rows.dtype),
        compiler_params=pltpu.CompilerParams(
            dimension_semantics=("arbitrary",),
            vmem_limit_bytes=VMEM_LIMIT),
        name="moe_experts",
    )(block_expert, n_used, x_rows, w_gate, w_up, w_down)


TAIL_TM = 256
TAIL_SUB = 256


def _tail_kernel(pos_ref, h_ref, gate_ref, y_hbm, p_ref, wpp_ref, gp_ref, wpg_ref, bpg_ref,
                 gfin_ref, out_ref, gbuf, cbuf, sem, *, n_tok):
    i = pl.program_id(0)
    n_steps = pl.num_programs(0)
    tile = lambda r: pl.ds(pl.multiple_of(r * ROW_TILE, ROW_TILE), ROW_TILE)

    def start_row(base, r):
        for kk in range(TOP_K):
            row = pos_ref[kk * n_tok + base + r]
            pltpu.make_async_copy(y_hbm.at[tile(row), :], gbuf.at[kk, tile(r), :],
                                  sem).start(priority=kk % 2)

    def wait_rows():
        for kk in range(TOP_K):
            pltpu.make_async_copy(y_hbm.at[pl.ds(0, TAIL_TM * ROW_TILE), :], gbuf.at[kk], sem).wait()

    @pl.when(i == 0)
    def _():
        def body(r, carry):
            start_row(0, r)
            return carry

        lax.fori_loop(0, TAIL_TM, body, 0, unroll=8)

    wait_rows()
    cbuf[...] = gbuf[...]

    nxt_base = jnp.minimum(i + 1, n_steps - 1) * TAIL_TM
    for r in range(TAIL_TM):
        start_row(nxt_base, r)

    for s0 in range(0, TAIL_TM, TAIL_SUB):
        rs = slice(s0, s0 + TAIL_SUB)
        gates = gate_ref[rs, :]
        g1, g2 = gates[:, 0:1], gates[:, 1:2]
        y1_lo, y1_hi = _unpack_bf16_pair(_load_row_tiles(cbuf.at[0], TAIL_SUB, s0))
        y2_lo, y2_hi = _unpack_bf16_pair(_load_row_tiles(cbuf.at[1], TAIL_SUB, s0))
        moe = jnp.concatenate([g1 * y1_lo + g2 * y2_lo, g1 * y1_hi + g2 * y2_hi], axis=1)
        h2 = h_ref[rs, :] + moe
        ple = _rms(jnp.dot(p_ref[rs, :].astype(_bf16), wpp_ref[...], preferred_element_type=_f32),
                   gp_ref[...])
        lg = jnp.dot(h2.astype(_bf16), wpg_ref[...], preferred_element_type=_f32) + bpg_ref[...]
        h3 = h2 + (1.0 / (1.0 + jnp.exp(-lg))) * ple
        out_ref[rs, :] = _rms(h3, gfin_ref[...])

    @pl.when(i == n_steps - 1)
    def _():
        wait_rows()


def _tail(pos, h1, gates, y_rows, p2, w_pp, g_ple, w_pg, b_pg, g_fin):
    t = h1.shape[0]
    row = lambda i, ps: (i, 0)
    fix = lambda i, ps: (0, 0)
    grid_spec = pltpu.PrefetchScalarGridSpec(
        num_scalar_prefetch=1,
        grid=(t // TAIL_TM,),
        in_specs=[
            pl.BlockSpec((TAIL_TM, D_MODEL), row),
            pl.BlockSpec((TAIL_TM, LANES), row),
            pl.BlockSpec(memory_space=pl.ANY),
            pl.BlockSpec((TAIL_TM, PLE_DIM), row),
            pl.BlockSpec((PLE_DIM, D_MODEL), fix),
            pl.BlockSpec((1, D_MODEL), fix),
            pl.BlockSpec((D_MODEL, D_MODEL), fix),
            pl.BlockSpec((1, D_MODEL), fix),
            pl.BlockSpec((1, D_MODEL), fix),
        ],
        out_specs=pl.BlockSpec((TAIL_TM, D_MODEL), row),
        scratch_shapes=[
            pltpu.VMEM((TOP_K, TAIL_TM * ROW_TILE, LANES), y_rows.dtype),
            pltpu.VMEM((TOP_K, TAIL_TM * ROW_TILE, LANES), y_rows.dtype),
            pltpu.SemaphoreType.DMA(()),
        ],
    )
    return pl.pallas_call(
        functools.partial(_tail_kernel, n_tok=t),
        grid_spec=grid_spec,
        out_shape=jax.ShapeDtypeStruct((t, D_MODEL), _f32),
        compiler_params=pltpu.CompilerParams(
            dimension_semantics=("arbitrary",),
            vmem_limit_bytes=VMEM_LIMIT),
        name="tail",
    )(pos, h1, gates, y_rows, p2, w_pp, g_ple, w_pg, b_pg, g_fin)


def _inv_freq_row():
    inv_freq = ROPE_THETA ** (-jnp.arange(0, ROT_DIM, 2, dtype=_f32) / ROT_DIM)
    comp = jnp.concatenate([inv_freq, inv_freq, jnp.zeros((DIFF_HEAD_DIM - ROT_DIM,), _f32)])
    return jnp.concatenate([comp, comp]).reshape(1, LANES)


def kernel(x, p, positions, norm_mix_g, w_in, conv_w, conv_b, dt_bias_f, dt_bias_b, a_log_f, a_log_b, d_skip, ssd_norm_g, lam_q1, lam_k1, lam_q2, lam_k2, subln_g, w_out, norm_ffn_g, w_route_group, b_route_group, w_route_expert, b_route_expert, w_exp_gate, w_exp_up, w_exp_down, w_ple_proj, ple_norm_g, w_ple_gate, b_ple_gate, final_norm_g):
    b, seq, d = x.shape
    t = b * seq
    x2 = x.reshape(t, d)
    row = lambda v: v.reshape(1, -1).astype(_f32)

    w_in_t = jnp.swapaxes(w_in[0], 0, 1)
    pos_col = positions.astype(_f32).reshape(t, 1)
    pad_dt = lambda v: jnp.pad(v, (0, DT_PAD - 2 * SSD_HEADS)).reshape(1, DT_PAD)
    dt_bias = pad_dt(jnp.concatenate([dt_bias_f[0], dt_bias_b[0]]))
    a_row = pad_dt(jnp.concatenate([-jnp.exp(a_log_f[0]), -jnp.exp(a_log_b[0])]))
    d_skip_row = jnp.repeat(d_skip[0], SSD_HEADDIM).reshape(1, SSD_W)
    lam_vecs = jnp.stack([lam_q1[0], lam_k1[0], lam_q2[0], lam_k2[0]])
    n_route = MOE_GROUPS + N_EXPERTS
    w_route = jnp.pad(jnp.concatenate([w_route_group[0], w_route_expert[0]], axis=1),
                      ((0, 0), (0, LANES - n_route)))
    w_route_hi = w_route.astype(_bf16)
    w_route_lo = (w_route - w_route_hi.astype(_f32)).astype(_bf16)
    w_route2 = jnp.concatenate([w_route_hi, w_route_lo], axis=1)
    b_route = jnp.pad(jnp.concatenate([b_route_group[0], b_route_expert[0]]),
                      (0, LANES - n_route)).reshape(1, LANES)

    proj, dt, *rot = _inproj(x2, row(norm_mix_g[0]), pos_col, _inv_freq_row(), w_in_t)
    proj3 = proj.reshape(b, seq, MAIN_W)
    y_ssd = _ssd(proj3, dt.reshape(b, seq, DT_PAD), conv_w[0], row(conv_b[0]),
                 dt_bias, a_row, d_skip_row)
    att = _attn(proj3, [r.reshape(b, seq, LANES) for r in rot], lam_vecs, row(subln_g[0]))
    h1, hn_packed, route, cnt, gates = _outproj(
        y_ssd.reshape(t, SSD_W), proj, att.reshape(t, ATTN_W), x2, w_out[0].astype(_bf16),
        row(ssd_norm_g[0]), row(norm_ffn_g[0]), w_route2, b_route)
    pos8, blk8, meta8 = _plan(route, cnt)
    pos = pos8[:TOP_K].reshape(-1)
    block_expert = blk8[0, :_n_rows(t) // ROW_BLOCK]
    n_used = blk8[1, :1]
    x_rows = _dispatch_rows(pos, meta8[0, :N_EXPERTS], meta8[1, :N_EXPERTS], n_used, hn_packed)
    y_rows = _moe(block_expert, n_used, x_rows, w_exp_gate[0], w_exp_up[0], w_exp_down[0])
    out = _tail(pos, h1, gates, y_rows, p[0].reshape(t, PLE_DIM), w_ple_proj[0].astype(_bf16),
                row(ple_norm_g[0]), w_ple_gate[0].astype(_bf16), row(b_ple_gate[0]),
                row(final_norm_g))
    return out.reshape(b, seq, d)
```

```python
import functools

import jax
import jax.numpy as jnp
import numpy as np
from jax import lax
from jax.experimental import pallas as pl
from jax.experimental.pallas import tpu as pltpu

D_MODEL = 2048
PLE_DIM = 256
SSD_W = 1024
ATTN_W = 1024
SSD_HEADDIM = 64
SSD_HEADS = 16
SSD_GROUPS = 2
HEADS_PER_GROUP = SSD_HEADS // SSD_GROUPS
SSD_STATE = 128
CHUNK = 128
CONV_W = 5
XBC_W = SSD_W + 2 * SSD_GROUPS * SSD_STATE
DIFF_HEAD_DIM = 64
DIFF_HEADS = 8
ROT_DIM = 16
ROPE_THETA = 500000.0
MOE_GROUPS = 8
EXPERTS_PER_GROUP = 8
N_EXPERTS = 64
TOP_K = 2
EXPERT_FF = 512
ROW_BLOCK = 128
EPS = 1e-6
LAM_INIT = 0.2

LANES = 128
DT_PAD = LANES
MAIN_W = SSD_W + XBC_W + 3 * ATTN_W
COL_Z = 0
COL_Q = SSD_W
COL_K = COL_Q + ATTN_W
COL_XBC = COL_K + ATTN_W
COL_V = COL_XBC + XBC_W
VMEM_LIMIT = 56 * 1024 * 1024
NEG_BIG = -1e30
Q_SCALE = float(DIFF_HEAD_DIM ** -0.5 * np.log2(np.e))

_f32 = jnp.float32
_bf16 = jnp.bfloat16


def _silu(v):
    return v * (1.0 / (1.0 + jnp.exp(-v)))


def _rms(v, g):
    return v * lax.rsqrt(jnp.mean(v * v, axis=-1, keepdims=True) + EPS) * g


_HI16 = np.uint32(0xFFFF0000)


def _pack_bf16_pair(lo, hi):
    lo_w = lax.bitcast_convert_type(lo.astype(_bf16).astype(_f32), jnp.uint32) >> 16
    hi_w = lax.bitcast_convert_type(hi.astype(_bf16).astype(_f32), jnp.uint32) & _HI16
    return lo_w | hi_w


def _unpack_bf16_pair(w):
    lo = lax.bitcast_convert_type(w << 16, _f32)
    hi = lax.bitcast_convert_type(w & _HI16, _f32)
    return lo, hi


ROW_TILE = 8
assert D_MODEL // 2 == ROW_TILE * LANES


def _store_row_tiles(ref, words, n, row0=0):
    for c in range(ROW_TILE):
        ref[pl.ds(row0 * ROW_TILE + c, n, stride=ROW_TILE), :] = words[:, c * LANES:(c + 1) * LANES]


def _load_row_tiles(ref, n, row0=0):
    return jnp.concatenate(
        [ref[pl.ds(row0 * ROW_TILE + c, n, stride=ROW_TILE), :] for c in range(ROW_TILE)], axis=1)


IN_TM = 1024
IN_TN = 512


def _inproj_kernel(x_ref, g_ref, pos_ref, invf_ref, wt_ref, wdt_ref,
                   out_ref, dt_ref, rc_ref, rp_ref, rm_ref, n_scr):
    j = pl.program_id(1)
    nt = (((1,), (1,)), ((), ()))
    half = ROT_DIM // 2

    @pl.when(j == 0)
    def _():
        n = _rms(x_ref[...], g_ref[...])
        n_scr[...] = n.astype(_bf16)
        lane = lax.broadcasted_iota(jnp.int32, (1, LANES), 1)
        dt = lax.dot_general(n_scr[...], wdt_ref[...].astype(_bf16), nt, preferred_element_type=_f32)
        dt_ref[...] = jnp.where(lane < 2 * SSD_HEADS, dt, 0.0)
        ang = pos_ref[...] * invf_ref[...]
        cos, sin = jnp.cos(ang), jnp.sin(ang)
        l64 = lane & (DIFF_HEAD_DIM - 1)
        rc_ref[...] = jnp.where(l64 < ROT_DIM, cos, 1.0)
        rp_ref[...] = jnp.where(jnp.logical_and(l64 >= half, l64 < ROT_DIM), sin, 0.0)
        rm_ref[...] = jnp.where(l64 < half, -sin, 0.0)

    out_ref[...] = lax.dot_general(n_scr[...], wt_ref[...].astype(_bf16), nt,
                                   preferred_element_type=_f32).astype(_bf16)


def _rotate(t, rc, rp, rm):
    half = ROT_DIM // 2
    return t * rc + pltpu.roll(t, half, 1) * rp + pltpu.roll(t, LANES - half, 1) * rm


SRC_XBC = SSD_W
SRC_DT = SRC_XBC + XBC_W
SRC_Q = SRC_DT + 2 * SSD_HEADS
SRC_V = SRC_Q + 2 * ATTN_W
SRC_ALIGN = 32
assert all(s % SRC_ALIGN == 0 for s in (SRC_XBC, SRC_DT, SRC_Q, SRC_V, IN_TN))


def _src_row(j):
    jq, jx, jv = COL_Q // IN_TN, COL_XBC // IN_TN, COL_V // IN_TN
    row = jnp.where(j < jq, j * IN_TN,
                    jnp.where(j < jx, SRC_Q + (j - jq) * IN_TN,
                              jnp.where(j < jv, SRC_XBC + (j - jx) * IN_TN,
                                        SRC_V + (j - jv) * IN_TN)))
    return pl.multiple_of(row, SRC_ALIGN)


def _inproj(x2, g, pos_col, invf_row, w_in_t):
    t = x2.shape[0]
    grid = (t // IN_TM, MAIN_W // IN_TN)
    return pl.pallas_call(
        _inproj_kernel,
        grid=grid,
        in_specs=[
            pl.BlockSpec((IN_TM, D_MODEL), lambda i, j: (i, 0)),
            pl.BlockSpec((1, D_MODEL), lambda i, j: (0, 0)),
            pl.BlockSpec((IN_TM, 1), lambda i, j: (i, 0)),
            pl.BlockSpec((1, LANES), lambda i, j: (0, 0)),
            pl.BlockSpec((pl.Element(IN_TN), pl.Element(D_MODEL)), lambda i, j: (_src_row(j), 0)),
            pl.BlockSpec((pl.Element(DT_PAD), pl.Element(D_MODEL)), lambda i, j: (SRC_DT, 0)),
        ],
        out_specs=[
            pl.BlockSpec((IN_TM, IN_TN), lambda i, j: (i, j)),
            pl.BlockSpec((IN_TM, DT_PAD), lambda i, j: (i, 0)),
        ] + [pl.BlockSpec((IN_TM, LANES), lambda i, j: (i, 0))] * 3,
        out_shape=[
            jax.ShapeDtypeStruct((t, MAIN_W), _bf16),
            jax.ShapeDtypeStruct((t, DT_PAD), _f32),
        ] + [jax.ShapeDtypeStruct((t, LANES), _f32)] * 3,
        scratch_shapes=[pltpu.VMEM((IN_TM, D_MODEL), _bf16)],
        compiler_params=pltpu.CompilerParams(
            dimension_semantics=("arbitrary", "arbitrary"),
            vmem_limit_bytes=VMEM_LIMIT),
        name="inproj",
    )(x2, g, pos_col, invf_row, w_in_t, w_in_t)


CONV_HALO = 16


def _ssd_kernel(xbc_ref, dt_ref, cw_ref, cb_ref, dtb_ref, a_ref, dsk_ref,
                y_ref, xpad, xact, dts, yacc, state_f, state_b, seq):
    n_chunks = seq // CHUNK

    zeros_halo = jnp.zeros((CONV_HALO, XBC_W), _bf16)
    xpad[0:CONV_HALO, :] = zeros_halo
    xpad[CONV_HALO + seq:CONV_HALO + seq + CONV_HALO, :] = zeros_halo
    xpad[CONV_HALO:CONV_HALO + seq, :] = xbc_ref[0]
    cw = cw_ref[...]
    cb = cb_ref[...]
    win = CHUNK + 2 * CONV_HALO

    def conv_body(c, carry):
        r0 = pl.multiple_of(c * CHUNK, CHUNK)
        blk = xpad[pl.ds(r0, win), :].astype(_f32)
        acc = blk * cw[CONV_W // 2:CONV_W // 2 + 1, :]
        for k in range(CONV_W):
            sh = CONV_W // 2 - k
            if sh == 0:
                continue
            acc = acc + pltpu.roll(blk, sh % win, 0) * cw[k:k + 1, :]
        v = acc[CONV_HALO:CONV_HALO + CHUNK, :] + cb
        xact[pl.ds(r0, CHUNK), :] = _silu(v).astype(_bf16)
        return carry

    lax.fori_loop(0, n_chunks, conv_body, 0)

    raw = dt_ref[0] + dtb_ref[...]
    dts[...] = jnp.maximum(raw, 0.0) + jnp.log1p(jnp.exp(-jnp.abs(raw)))

    a_row = a_ref[...]
    dsk = dsk_ref[...]
    row_i = lax.broadcasted_iota(jnp.int32, (CHUNK, CHUNK), 0)
    col_i = lax.broadcasted_iota(jnp.int32, (CHUNK, CHUNK), 1)

    def chunk_step(c, reverse, state):
        r0 = pl.multiple_of(c * CHUNK, CHUNK)
        dtc = dts[pl.ds(r0, CHUNK), :]
        cs = dtc * a_row
        k = 1
        while k < CHUNK:
            if reverse:
                cs = cs + jnp.where(row_i < CHUNK - k, pltpu.roll(cs, CHUNK - k, 0), 0.0)
            else:
                cs = cs + jnp.where(row_i >= k, pltpu.roll(cs, k, 0), 0.0)
            k *= 2
        cs_t = cs.T
        dt_t = dtc.T
        end_col = cs_t[:, 0:1] if reverse else cs_t[:, CHUNK - 1:CHUNK]
        w_t = jnp.exp(end_col - cs_t) * dt_t
        end_row = cs[0:1, :] if reverse else cs[CHUNK - 1:CHUNK, :]
        dec_row = jnp.exp(end_row)
        tri = (row_i <= col_i) if reverse else (row_i >= col_i)
        lane0 = SSD_HEADS if reverse else 0
        first_head = col_i < SSD_HEADDIM
        rows = pl.ds(r0, CHUNK)
        for g in range(SSD_GROUPS):
            b_g = xact[rows, SSD_W + g * SSD_STATE:SSD_W + (g + 1) * SSD_STATE]
            c_g = xact[rows, SSD_W + (SSD_GROUPS + g) * SSD_STATE:SSD_W + (SSD_GROUPS + g + 1) * SSD_STATE]
            cbm = lax.dot_general(c_g, b_g, (((1,), (1,)), ((), ())),
                                  preferred_element_type=_f32)
            b_t = b_g.astype(_f32).T
            for pp in range(HEADS_PER_GROUP // 2):
                pair = g * (HEADS_PER_GROUP // 2) + pp
                cols = slice(pair * LANES, (pair + 1) * LANES)
                xs_p = xact[rows, cols]
                st = state[pair]
                off = jnp.dot(c_g, st.astype(_bf16), preferred_element_type=_f32)
                halves = []
                for hh in range(2):
                    ln = lane0 + 2 * pair + hh
                    colb = jnp.broadcast_to(cs[:, ln:ln + 1], (CHUNK, CHUNK))
                    m_h = (cbm * jnp.exp(jnp.where(tri, colb - cs_t[ln:ln + 1, :], NEG_BIG))
                           * dt_t[ln:ln + 1, :])
                    y_h = (jnp.dot(m_h.astype(_bf16), xs_p, preferred_element_type=_f32)
                           + jnp.exp(colb) * off)
                    upd = jnp.dot((b_t * w_t[ln:ln + 1, :]).astype(_bf16), xs_p,
                                  preferred_element_type=_f32)
                    dec = jnp.broadcast_to(dec_row[:, ln:ln + 1], (CHUNK, CHUNK))
                    halves.append((y_h, upd, dec))
                y_p = jnp.where(first_head, halves[0][0], halves[1][0])
                upd_p = jnp.where(first_head, halves[0][1], halves[1][1])
                dec_p = jnp.where(first_head, halves[0][2], halves[1][2])
                state[pair] = st * dec_p + upd_p
                if reverse:
                    y_ref[0, rows, cols] = (yacc[rows, cols] + y_p).astype(_bf16)
                else:
                    yacc[rows, cols] = y_p + xs_p.astype(_f32) * dsk[:, cols]

    state_f[...] = jnp.zeros_like(state_f)
    state_b[...] = jnp.zeros_like(state_b)

    def fwd_body(c, carry):
        chunk_step(c, False, state_f)
        return carry

    lax.fori_loop(0, n_chunks, fwd_body, 0)

    def bwd_body(i, carry):
        chunk_step(n_chunks - 1 - i, True, state_b)
        return carry

    lax.fori_loop(0, n_chunks, bwd_body, 0)


def _ssd(proj3, dt3, conv_w, conv_b, dt_bias, a_row, d_skip_row):
    b, seq, _ = proj3.shape
    kern = functools.partial(_ssd_kernel, seq=seq)
    assert COL_XBC % XBC_W == 0
    return pl.pallas_call(
        kern,
        grid=(b,),
        in_specs=[
            pl.BlockSpec((1, seq, XBC_W), lambda i: (i, 0, COL_XBC // XBC_W)),
            pl.BlockSpec((1, seq, DT_PAD), lambda i: (i, 0, 0)),
            pl.BlockSpec((CONV_W, XBC_W), lambda i: (0, 0)),
            pl.BlockSpec((1, XBC_W), lambda i: (0, 0)),
            pl.BlockSpec((1, DT_PAD), lambda i: (0, 0)),
            pl.BlockSpec((1, DT_PAD), lambda i: (0, 0)),
            pl.BlockSpec((1, SSD_W), lambda i: (0, 0)),
        ],
        out_specs=pl.BlockSpec((1, seq, SSD_W), lambda i: (i, 0, 0)),
        out_shape=jax.ShapeDtypeStruct((b, seq, SSD_W), _bf16),
        scratch_shapes=[
            pltpu.VMEM((seq + 2 * CONV_HALO, XBC_W), _bf16),
            pltpu.VMEM((seq, XBC_W), _bf16),
            pltpu.VMEM((seq, DT_PAD), _f32),
            pltpu.VMEM((seq, SSD_W), _f32),
            pltpu.VMEM((SSD_HEADS // 2, SSD_STATE, 2 * SSD_HEADDIM), _f32),
            pltpu.VMEM((SSD_HEADS // 2, SSD_STATE, 2 * SSD_HEADDIM), _f32),
        ],
        compiler_params=pltpu.CompilerParams(
            dimension_semantics=("arbitrary",),
            vmem_limit_bytes=VMEM_LIMIT),
        name="ssd",
    )(proj3, dt3, conv_w, conv_b, dt_bias, a_row, d_skip_row)


ATT_QB = 256
ATT_HEADS = 2


def _attn_kernel(q_ref, k_ref, v_ref, rc_ref, rp_ref, rm_ref, lam_ref, g_ref, o_ref,
                 k_scr, v1_scr, *, seq):
    hw = 2 * DIFF_HEAD_DIM
    lv = lam_ref[...]
    lam = (jnp.exp(jnp.sum(lv[0:1] * lv[1:2], axis=-1, keepdims=True))
           - jnp.exp(jnp.sum(lv[2:3] * lv[3:4], axis=-1, keepdims=True)) + LAM_INIT)

    def rotated(ref, rows, cols):
        return _rotate(ref[0, rows, cols].astype(_f32), rc_ref[0, rows, :], rp_ref[0, rows, :],
                       rm_ref[0, rows, :])

    g = g_ref[...]
    first = lax.broadcasted_iota(jnp.int32, (1, hw), 1) < DIFF_HEAD_DIM
    ones_col = (lax.broadcasted_iota(jnp.int32, (seq, hw), 1) == 0).astype(_bf16)
    nt = (((1,), (1,)), ((), ()))

    def prepare(hd):
        cols = slice(hd * hw, (hd + 1) * hw)
        for c in range(seq // ATT_QB):
            rows = slice(c * ATT_QB, (c + 1) * ATT_QB)
            k_scr[hd, rows, :] = rotated(k_ref, rows, cols).astype(_bf16)
        v1_scr[hd, :, 0:hw] = v_ref[0, :, cols]
        v1_scr[hd, :, hw:2 * hw] = ones_col

    def chain(hd, r0):
        cols = slice(hd * hw, (hd + 1) * hw)
        rows = slice(r0, r0 + ATT_QB)
        q = (rotated(q_ref, rows, cols) * Q_SCALE).astype(_bf16)
        zero = jnp.zeros_like(q)
        ovs = []
        for qc in (jnp.where(first, q, zero), jnp.where(first, zero, q)):
            s = lax.dot_general(qc, k_scr[hd], nt, preferred_element_type=_f32)
            e = jnp.exp2(s - jnp.max(s, axis=-1, keepdims=True)).astype(_bf16)
            ovs.append(jnp.dot(e, v1_scr[hd], preferred_element_type=_f32))
        o1, l1 = ovs[0][:, 0:hw], ovs[0][:, hw:hw + 1]
        o2, l2 = ovs[1][:, 0:hw], ovs[1][:, hw:hw + 1]
        o = o1 / l1 - (lam / l2) * o2
        o = _rms(o, g) * (1.0 - LAM_INIT)
        o_ref[0, rows, cols] = o.astype(_bf16)

    for hd in range(ATT_HEADS):
        prepare(hd)
    for hd in range(ATT_HEADS):
        for c in range(seq // ATT_QB):
            chain(hd, c * ATT_QB)


def _attn(proj3, rot_tables, lam_vecs, subln_g):
    b, seq, _ = proj3.shape
    hw = 2 * DIFF_HEAD_DIM
    gw = ATT_HEADS * hw
    assert COL_Q % gw == 0 and COL_K % gw == 0 and COL_V % gw == 0
    kern = functools.partial(_attn_kernel, seq=seq)
    return pl.pallas_call(
        kern,
        grid=(b, DIFF_HEADS // ATT_HEADS),
        in_specs=[
            pl.BlockSpec((1, seq, gw), lambda i, h: (i, 0, COL_Q // gw + h)),
            pl.BlockSpec((1, seq, gw), lambda i, h: (i, 0, COL_K // gw + h)),
            pl.BlockSpec((1, seq, gw), lambda i, h: (i, 0, COL_V // gw + h)),
        ] + [pl.BlockSpec((1, seq, LANES), lambda i, h: (i, 0, 0))] * 3 + [
            pl.BlockSpec((4, DIFF_HEAD_DIM), lambda i, h: (0, 0)),
            pl.BlockSpec((1, hw), lambda i, h: (0, 0)),
        ],
        out_specs=pl.BlockSpec((1, seq, gw), lambda i, h: (i, 0, h)),
        out_shape=jax.ShapeDtypeStruct((b, seq, ATTN_W), _bf16),
        scratch_shapes=[pltpu.VMEM((ATT_HEADS, seq, hw), _bf16),
                        pltpu.VMEM((ATT_HEADS, seq, 2 * hw), _bf16)],
        compiler_params=pltpu.CompilerParams(
            dimension_semantics=("arbitrary", "arbitrary"),
            vmem_limit_bytes=VMEM_LIMIT),
        name="diffattn",
    )(proj3, proj3, proj3, *rot_tables, lam_vecs, subln_g)


OUT_TM = 512
ROUTE_ROWS = 8


def _outproj_kernel(y_ref, z_ref, att_ref, x_ref, w_ref, gs_ref, gf_ref, wr_ref, br_ref,
                    h_ref, hn_ref, route_ref, cnt_ref, gate_ref, cnt_scr):
    @pl.when(pl.program_id(0) == 0)
    def _():
        cnt_scr[...] = jnp.zeros_like(cnt_scr)

    y = y_ref[...].astype(_f32)
    z = z_ref[...].astype(_f32)
    s = _rms(y * _silu(z), gs_ref[...]).astype(_bf16)
    acc = jnp.dot(s, w_ref[0:SSD_W, :], preferred_element_type=_f32)
    acc = acc + jnp.dot(att_ref[...], w_ref[SSD_W:SSD_W + ATTN_W, :], preferred_element_type=_f32)
    h1 = x_ref[...] + acc
    h_ref[...] = h1
    hn = _rms(h1, gf_ref[...])
    half = D_MODEL // 2
    _store_row_tiles(hn_ref, _pack_bf16_pair(hn[:, 0:half], hn[:, half:D_MODEL]), hn.shape[0])
    hn_hi = hn.astype(_bf16)
    hn_lo = (hn - hn_hi.astype(_f32)).astype(_bf16)
    a = jnp.dot(hn_hi, wr_ref[...], preferred_element_type=_f32)
    bb = jnp.dot(hn_lo, wr_ref[:, 0:LANES], preferred_element_type=_f32)
    logits = (a[:, 0:LANES] + a[:, LANES:2 * LANES] + bb + br_ref[...]).T
    tm = logits.shape[1]
    iota = lax.broadcasted_iota(jnp.int32, (MOE_GROUPS, tm), 0)

    def first_argmax(val, vmax):
        return jnp.min(jnp.where(val == vmax, iota, MOE_GROUPS), axis=0, keepdims=True)

    gl = logits[0:MOE_GROUPS]
    gmax = jnp.max(gl, axis=0, keepdims=True)
    p_g = 1.0 / jnp.sum(jnp.exp(gl - gmax), axis=0, keepdims=True)
    g_sel = first_argmax(gl, gmax)
    el = jnp.zeros((EXPERTS_PER_GROUP, tm), _f32)
    for gi in range(MOE_GROUPS):
        lo = MOE_GROUPS + gi * EXPERTS_PER_GROUP
        el = jnp.where(g_sel == gi, logits[lo:lo + EXPERTS_PER_GROUP], el)
    ee = jnp.exp(el - jnp.max(el, axis=0, keepdims=True))
    pe = ee / jnp.sum(ee, axis=0, keepdims=True)
    p1 = jnp.max(pe, axis=0, keepdims=True)
    i1 = first_argmax(pe, p1)
    pe2 = jnp.where(iota == i1, -1.0, pe)
    p2 = jnp.max(pe2, axis=0, keepdims=True)
    i2 = first_argmax(pe2, p2)
    den = p1 + p2
    e1 = g_sel * EXPERTS_PER_GROUP + i1
    e2 = g_sel * EXPERTS_PER_GROUP + i2
    iota_e = lax.broadcasted_iota(jnp.int32, (N_EXPERTS, tm), 0)
    oh1 = (iota_e == e1).astype(_f32)
    oh2 = (iota_e == e2).astype(_f32)
    both = oh1 + oh2
    earlier = (lax.broadcasted_iota(jnp.int32, (tm, tm), 0)
               < lax.broadcasted_iota(jnp.int32, (tm, tm), 1)).astype(_bf16)
    before = cnt_scr[:, 0:1] + jnp.dot(both.astype(_bf16), earlier,
                                       preferred_element_type=_f32)
    r1 = jnp.sum(oh1 * before, axis=0, keepdims=True).astype(jnp.int32)
    r2 = jnp.sum(oh2 * before, axis=0, keepdims=True).astype(jnp.int32)
    cnt_scr[...] = cnt_scr[...] + jnp.sum(both, axis=1, keepdims=True)
    cnt_ref[...] = cnt_scr[...].astype(jnp.int32)
    route_ref[...] = jnp.where(iota == 0, e1, jnp.where(iota == 1, e2,
                               jnp.where(iota == 2, r1, jnp.where(iota == 3, r2, 0))))
    g8 = jnp.where(iota == 0, p_g * (p1 / den), jnp.where(iota == 1, p_g * (p2 / den), 0.0))
    gate_ref[...] = jnp.concatenate(
        [g8, jnp.zeros((LANES - ROUTE_ROWS, tm), _f32)], axis=0).T


def _outproj(y2, proj2, att2, x2, w_out, g_ssd, g_ffn, wr_t, br_col):
    t = x2.shape[0]
    row = lambda i: (i, 0)
    fix = lambda i: (0, 0)
    return pl.pallas_call(
        _outproj_kernel,
        grid=(t // OUT_TM,),
        in_specs=[
            pl.BlockSpec((OUT_TM, SSD_W), row),
            pl.BlockSpec((OUT_TM, SSD_W), row),
            pl.BlockSpec((OUT_TM, ATTN_W), row),
            pl.BlockSpec((OUT_TM, D_MODEL), row),
            pl.BlockSpec((SSD_W + ATTN_W, D_MODEL), fix),
            pl.BlockSpec((1, SSD_W), fix),
            pl.BlockSpec((1, D_MODEL), fix),
            pl.BlockSpec((D_MODEL, 2 * LANES), fix),
            pl.BlockSpec((1, LANES), fix),
        ],
        out_specs=[
            pl.BlockSpec((OUT_TM, D_MODEL), row),
            pl.BlockSpec((OUT_TM * ROW_TILE, LANES), row),
            pl.BlockSpec((ROUTE_ROWS, OUT_TM), lambda i: (0, i)),
            pl.BlockSpec((N_EXPERTS, LANES), fix),
            pl.BlockSpec((OUT_TM, LANES), row),
        ],
        out_shape=[
            jax.ShapeDtypeStruct((t, D_MODEL), _f32),
            jax.ShapeDtypeStruct((t * ROW_TILE, LANES), jnp.uint32),
            jax.ShapeDtypeStruct((ROUTE_ROWS, t), jnp.int32),
            jax.ShapeDtypeStruct((N_EXPERTS, LANES), jnp.int32),
            jax.ShapeDtypeStruct((t, LANES), _f32),
        ],
        scratch_shapes=[pltpu.VMEM((N_EXPERTS, LANES), _f32)],
        compiler_params=pltpu.CompilerParams(
            dimension_semantics=("arbitrary",),
            vmem_limit_bytes=VMEM_LIMIT),
        name="outproj_router",
    )(y2, proj2, att2, x2, w_out, g_ssd, g_ffn, wr_t, br_col)


def _n_rows(n_tok):
    n_assign = n_tok * TOP_K
    return (n_assign + N_EXPERTS * (ROW_BLOCK - 1) + ROW_BLOCK - 1) // ROW_BLOCK * ROW_BLOCK


def _prefix_sum(v, axis):
    n = v.shape[axis]
    idx = lax.broadcasted_iota(jnp.int32, v.shape, axis)
    k = 1
    while k < n:
        v = v + jnp.where(idx >= k, pltpu.roll(v, k, axis), 0)
        k *= 2
    return v


def _plan_kernel(route_ref, cnt_ref, pos_ref, blk_ref, meta_ref, *, n_blocks_pad):
    n_tok = route_ref.shape[1]
    cnt_col = jnp.concatenate(
        [cnt_ref[...], jnp.zeros((LANES - N_EXPERTS, LANES), jnp.int32)], axis=0)
    pad_up = lambda c: (c + (ROW_BLOCK - 1)) & (-ROW_BLOCK)
    ends_col = _prefix_sum(pad_up(cnt_col), 0)
    starts_col = (ends_col - pad_up(cnt_col)).astype(_f32)
    cnt_row = cnt_col.astype(_f32).T.astype(jnp.int32)
    ends_row = _prefix_sum(pad_up(cnt_row), 1)
    starts_row = ends_row - pad_up(cnt_row)
    n_used = ends_row[0:1, LANES - 1:LANES] >> (ROW_BLOCK.bit_length() - 1)

    ch = 1024
    iota_e = lax.broadcasted_iota(jnp.int32, (LANES, ch), 0)
    for c0 in range(0, n_tok, ch):
        rt = route_ref[:, c0:c0 + ch]
        s1 = jnp.sum(jnp.where(iota_e == rt[0:1], starts_col[:, 0:1], 0.0), axis=0, keepdims=True)
        s2 = jnp.sum(jnp.where(iota_e == rt[1:2], starts_col[:, 0:1], 0.0), axis=0, keepdims=True)
        p1 = s1.astype(jnp.int32) + rt[2:3]
        p2 = s2.astype(jnp.int32) + rt[3:4]
        sub = lax.broadcasted_iota(jnp.int32, (ROUTE_ROWS, ch), 0)
        pos_ref[:, c0:c0 + ch] = jnp.where(sub == 0, p1, jnp.where(sub == 1, p2, 0))

    blk_start = lax.broadcasted_iota(jnp.int32, (LANES, n_blocks_pad), 1) * ROW_BLOCK
    e_idx = lax.broadcasted_iota(jnp.int32, (LANES, n_blocks_pad), 0)
    real = e_idx < N_EXPERTS
    be = jnp.sum(jnp.where(jnp.logical_and(real, ends_col[:, 0:1] <= blk_start), 1.0, 0.0),
                 axis=0, keepdims=True).astype(jnp.int32)
    be = jnp.minimum(be, N_EXPERTS - 1)
    last = jnp.max(jnp.where(jnp.logical_and(real, cnt_col[:, 0:1] > 0), e_idx.astype(_f32), 0.0),
                   axis=0, keepdims=True).astype(jnp.int32)
    blk_i = lax.broadcasted_iota(jnp.int32, (1, n_blocks_pad), 1)
    be = jnp.where(blk_i < n_used, be, last)
    sub = lax.broadcasted_iota(jnp.int32, (ROUTE_ROWS, n_blocks_pad), 0)
    blk_ref[...] = jnp.where(sub == 0, be, jnp.where(sub == 1, n_used, 0))
    sub = lax.broadcasted_iota(jnp.int32, (ROUTE_ROWS, LANES), 0)
    meta_ref[...] = jnp.where(sub == 0, cnt_row[0:1], jnp.where(sub == 1, starts_row[0:1], 0))


def _plan(route, cnt):
    n_tok = route.shape[1]
    n_blocks_pad = -(-(_n_rows(n_tok) // ROW_BLOCK) // LANES) * LANES
    return pl.pallas_call(
        functools.partial(_plan_kernel, n_blocks_pad=n_blocks_pad),
        out_shape=[
            jax.ShapeDtypeStruct((ROUTE_ROWS, n_tok), jnp.int32),
            jax.ShapeDtypeStruct((ROUTE_ROWS, n_blocks_pad), jnp.int32),
            jax.ShapeDtypeStruct((ROUTE_ROWS, LANES), jnp.int32),
        ],
        compiler_params=pltpu.CompilerParams(vmem_limit_bytes=VMEM_LIMIT),
        name="route_plan",
    )(route, cnt)


DISP_TM = 512


def _dispatch_kernel(pos_ref, cnt_ref, start_ref, nused_ref, h_ref, x_hbm, stage, zrow, sem, zsem,
                     *, n_tok, n_blocks):
    i = pl.program_id(0)
    n_steps = pl.num_programs(0)
    slot = i % 2
    base = i * DISP_TM
    tile = lambda r: pl.ds(pl.multiple_of(r * ROW_TILE, ROW_TILE), ROW_TILE)

    def wait_slot(sl):
        for _ in range(TOP_K):
            pltpu.make_async_copy(stage.at[sl], x_hbm.at[pl.ds(0, DISP_TM * ROW_TILE), :],
                                  sem.at[sl]).wait()

    @pl.when(i >= 2)
    def _():
        wait_slot(slot)

    stage[slot] = h_ref[...]

    def body(r, carry):
        for kk in range(TOP_K):
            row = pos_ref[kk * n_tok + base + r]
            pltpu.make_async_copy(stage.at[slot, tile(r), :], x_hbm.at[tile(row), :],
                                  sem.at[slot]).start(priority=kk % 2)
        return carry

    lax.fori_loop(0, DISP_TM, body, 0, unroll=8)

    @pl.when(i == n_steps - 1)
    def _():
        zrow[...] = jnp.zeros_like(zrow)

        pieces = [1 << b for b in reversed(range(ROW_BLOCK.bit_length() - 1))]

        def pad_copy(size, row0):
            src = zrow.at[pl.ds(0, size * ROW_TILE), :]
            dst = x_hbm.at[pl.ds(pl.multiple_of(row0 * ROW_TILE, ROW_TILE), size * ROW_TILE), :]
            return pltpu.make_async_copy(src, dst, zsem)

        def per_expert(e, carry, wait):
            cnt = cnt_ref[e]
            n_pad = ((cnt + (ROW_BLOCK - 1)) & (-ROW_BLOCK)) - cnt
            row0 = start_ref[e] + cnt
            for size in pieces:
                has = (n_pad & size) != 0

                @pl.when(has)
                def _():
                    cp = pad_copy(size, row0)
                    cp.wait() if wait else cp.start()

                row0 = row0 + jnp.where(has, size, 0)
            return carry

        lax.fori_loop(0, N_EXPERTS, functools.partial(per_expert, wait=False), 0)
        blk_rows = ROW_BLOCK * ROW_TILE

        def fill_block(blk, carry):
            dst = pl.ds(pl.multiple_of(blk * blk_rows, blk_rows), blk_rows)
            pltpu.make_async_copy(zrow, x_hbm.at[dst, :], zsem).start()
            return carry

        lax.fori_loop(nused_ref[0], n_blocks, fill_block, 0)

        lax.fori_loop(0, N_EXPERTS, functools.partial(per_expert, wait=True), 0)

        def wait_block(blk, carry):
            pltpu.make_async_copy(zrow, x_hbm.at[pl.ds(0, blk_rows), :], zsem).wait()
            return carry

        lax.fori_loop(nused_ref[0], n_blocks, wait_block, 0)

        @pl.when(n_steps >= 2)
        def _():
            wait_slot(1 - slot)

        wait_slot(slot)


def _dispatch_rows(pos, counts, starts, n_used, hn_packed):
    t = hn_packed.shape[0] // ROW_TILE
    n_rows = _n_rows(t)
    blk = (DISP_TM * ROW_TILE, LANES)
    grid_spec = pltpu.PrefetchScalarGridSpec(
        num_scalar_prefetch=4,
        grid=(t // DISP_TM,),
        in_specs=[pl.BlockSpec(blk, lambda i, *_: (i, 0))],
        out_specs=pl.BlockSpec(memory_space=pl.ANY),
        scratch_shapes=[
            pltpu.VMEM((2,) + blk, hn_packed.dtype),
            pltpu.VMEM((ROW_BLOCK * ROW_TILE, LANES), hn_packed.dtype),
            pltpu.SemaphoreType.DMA((2,)),
            pltpu.SemaphoreType.DMA(()),
        ],
    )
    return pl.pallas_call(
        functools.partial(_dispatch_kernel, n_tok=t, n_blocks=n_rows // ROW_BLOCK),
        grid_spec=grid_spec,
        out_shape=jax.ShapeDtypeStruct((n_rows * ROW_TILE, LANES), hn_packed.dtype),
        compiler_params=pltpu.CompilerParams(
            dimension_semantics=("arbitrary",),
            vmem_limit_bytes=VMEM_LIMIT),
        name="dispatch_rows",
    )(pos, counts, starts, n_used, hn_packed)


MOE_W_SPLIT = 4
MOE_W_BUFS = 3
MOE_GROUP = 4


def _moe_kernel(be_ref, nused_ref, x_ref, wg_hbm, wu_hbm, wd_hbm,
                out_ref, wg_f, wu_f, wd_f, wg_s, wu_s, wd_s, wsem, nexp, *, n_blocks):
    i = pl.program_id(0)
    nused = nused_ref[0]
    half = D_MODEL // 2

    def weight_copies(e, sl):
        cps = []
        for src, dst in ((wg_hbm, wg_f), (wu_hbm, wu_f), (wd_hbm, wd_f)):
            rows = src.shape[1] // MOE_W_SPLIT
            for c in range(MOE_W_SPLIT):
                rs = pl.ds(c * rows, rows)
                cps.append(pltpu.make_async_copy(src.at[e, rs, :], dst.at[sl, rs, :], wsem.at[sl]))
        return cps

    def start_all(cps):
        for n, cp in enumerate(cps):
            cp.start(priority=n % 2)

    def block_expert(b):
        return be_ref[jnp.minimum(b, n_blocks - 1)]

    def next_change(b0):
        e0 = block_expert(b0)
        return lax.while_loop(
            lambda b: jnp.logical_and(b < nused, block_expert(b) == e0), lambda b: b + 1, b0 + 1)

    @pl.when(i == 0)
    def _():
        nexp[0] = 0
        start_all(weight_copies(be_ref[0], 0))
        nxt = next_change(0)

        @pl.when(nxt < nused)
        def _():
            start_all(weight_copies(block_expert(nxt), 1))

    def row_block(b, sub):
        sub_rows = pl.ds(sub * ROW_BLOCK * ROW_TILE, ROW_BLOCK * ROW_TILE)

        @pl.when(b < nused)
        def _():
            e = be_ref[b]
            new_expert = jnp.logical_or(b == 0, e != be_ref[jnp.maximum(b - 1, 0)])

            @pl.when(new_expert)
            def _():
                sl = nexp[0] % MOE_W_BUFS
                ahead = next_change(jnp.minimum(next_change(b), nused - 1))

                @pl.when(jnp.logical_and(next_change(b) < nused, ahead < nused))
                def _():
                    start_all(weight_copies(block_expert(ahead), (nexp[0] + 2) % MOE_W_BUFS))

                for cp in weight_copies(e, sl):
                    cp.wait()
                wg_s[...] = wg_f[sl].astype(_bf16)
                wu_s[...] = wu_f[sl].astype(_bf16)
                wd_s[...] = wd_f[sl].astype(_bf16)
                nexp[0] = nexp[0] + 1

            x_lo, x_hi = _unpack_bf16_pair(_load_row_tiles(x_ref, ROW_BLOCK, sub * ROW_BLOCK))
            x_lo, x_hi = x_lo.astype(_bf16), x_hi.astype(_bf16)
            gg = (jnp.dot(x_lo, wg_s[0:half, :], preferred_element_type=_f32)
                  + jnp.dot(x_hi, wg_s[half:D_MODEL, :], preferred_element_type=_f32))
            uu = (jnp.dot(x_lo, wu_s[0:half, :], preferred_element_type=_f32)
                  + jnp.dot(x_hi, wu_s[half:D_MODEL, :], preferred_element_type=_f32))
            a = (_silu(gg) * uu).astype(_bf16)
            y = jnp.dot(a, wd_s[...], preferred_element_type=_f32)
            _store_row_tiles(out_ref, _pack_bf16_pair(y[:, 0:half], y[:, half:D_MODEL]),
                             ROW_BLOCK, sub * ROW_BLOCK)

        @pl.when(b >= nused)
        def _():
            out_ref[sub_rows, :] = jnp.zeros((ROW_BLOCK * ROW_TILE, LANES), out_ref.dtype)

    for sub in range(MOE_GROUP):
        row_block(i * MOE_GROUP + sub, sub)


def _moe(block_expert, n_used, x_rows, w_gate, w_up, w_down):
    n_blocks = x_rows.shape[0] // (ROW_BLOCK * ROW_TILE)
    assert n_blocks % MOE_GROUP == 0
    blk = (MOE_GROUP * ROW_BLOCK * ROW_TILE, LANES)
    omap = lambda i, be, nu: (i, 0)
    grid_spec = pltpu.PrefetchScalarGridSpec(
        num_scalar_prefetch=2,
        grid=(n_blocks // MOE_GROUP,),
        in_specs=[
            pl.BlockSpec(blk, omap),
            pl.BlockSpec(memory_space=pl.ANY),
            pl.BlockSpec(memory_space=pl.ANY),
            pl.BlockSpec(memory_space=pl.ANY),
        ],
        out_specs=pl.BlockSpec(blk, omap),
        scratch_shapes=[
            pltpu.VMEM((MOE_W_BUFS, D_MODEL, EXPERT_FF), _f32),
            pltpu.VMEM((MOE_W_BUFS, D_MODEL, EXPERT_FF), _f32),
            pltpu.VMEM((MOE_W_BUFS, EXPERT_FF, D_MODEL), _f32),
            pltpu.VMEM((D_MODEL, EXPERT_FF), _bf16),
            pltpu.VMEM((D_MODEL, EXPERT_FF), _bf16),
            pltpu.VMEM((EXPERT_FF, D_MODEL), _bf16),
            pltpu.SemaphoreType.DMA((MOE_W_BUFS,)),
            pltpu.SMEM((1,), jnp.int32),
        ],
    )
    return pl.pallas_call(
        functools.partial(_moe_kernel, n_blocks=n_blocks),
        grid_spec=grid_spec,
        out_shape=jax.ShapeDtypeStruct(x_rows.shape, x_rows.dtype),
        compiler_params=pltpu.CompilerParams(
            dimension_semantics=("arbitrary",),
            vmem_limit_bytes=VMEM_LIMIT),
        name="moe_experts",
    )(block_expert, n_used, x_rows, w_gate, w_up, w_down)


TAIL_TM = 256


def _tail_kernel(pos_ref, h_ref, gate_ref, y_hbm, p_ref, wpp_ref, gp_ref, wpg_ref, bpg_ref,
                 gfin_ref, out_ref, gbuf, cbuf, sem, *, n_tok):
    i = pl.program_id(0)
    n_steps = pl.num_programs(0)
    tile = lambda r: pl.ds(pl.multiple_of(r * ROW_TILE, ROW_TILE), ROW_TILE)

    def start_row(base, r):
        for kk in range(TOP_K):
            row = pos_ref[kk * n_tok + base + r]
            pltpu.make_async_copy(y_hbm.at[tile(row), :], gbuf.at[kk, tile(r), :],
                                  sem).start(priority=kk % 2)

    def wait_rows():
        for kk in range(TOP_K):
            pltpu.make_async_copy(y_hbm.at[pl.ds(0, TAIL_TM * ROW_TILE), :], gbuf.at[kk], sem).wait()

    @pl.when(i == 0)
    def _():
        def body(r, carry):
            start_row(0, r)
            return carry

        lax.fori_loop(0, TAIL_TM, body, 0, unroll=8)

    wait_rows()
    cbuf[...] = gbuf[...]

    nxt_base = jnp.minimum(i + 1, n_steps - 1) * TAIL_TM
    for r in range(TAIL_TM):
        start_row(nxt_base, r)

    gates = gate_ref[...]
    g1, g2 = gates[:, 0:1], gates[:, 1:2]
    y1_lo, y1_hi = _unpack_bf16_pair(_load_row_tiles(cbuf.at[0], TAIL_TM))
    y2_lo, y2_hi = _unpack_bf16_pair(_load_row_tiles(cbuf.at[1], TAIL_TM))
    moe = jnp.concatenate([g1 * y1_lo + g2 * y2_lo, g1 * y1_hi + g2 * y2_hi], axis=1)
    h2 = h_ref[...] + moe
    ple = _rms(jnp.dot(p_ref[...].astype(_bf16), wpp_ref[...], preferred_element_type=_f32),
               gp_ref[...])
    lg = jnp.dot(h2.astype(_bf16), wpg_ref[...], preferred_element_type=_f32) + bpg_ref[...]
    h3 = h2 + (1.0 / (1.0 + jnp.exp(-lg))) * ple
    out_ref[...] = _rms(h3, gfin_ref[...])

    @pl.when(i == n_steps - 1)
    def _():
        wait_rows()


def _tail(pos, h1, gates, y_rows, p2, w_pp, g_ple, w_pg, b_pg, g_fin):
    t = h1.shape[0]
    row = lambda i, ps: (i, 0)
    fix = lambda i, ps: (0, 0)
    grid_spec = pltpu.PrefetchScalarGridSpec(
        num_scalar_prefetch=1,
        grid=(t // TAIL_TM,),
        in_specs=[
            pl.BlockSpec((TAIL_TM, D_MODEL), row),
            pl.BlockSpec((TAIL_TM, LANES), row),
            pl.BlockSpec(memory_space=pl.ANY),
            pl.BlockSpec((TAIL_TM, PLE_DIM), row),
            pl.BlockSpec((PLE_DIM, D_MODEL), fix),
            pl.BlockSpec((1, D_MODEL), fix),
            pl.BlockSpec((D_MODEL, D_MODEL), fix),
            pl.BlockSpec((1, D_MODEL), fix),
            pl.BlockSpec((1, D_MODEL), fix),
        ],
        out_specs=pl.BlockSpec((TAIL_TM, D_MODEL), row),
        scratch_shapes=[
            pltpu.VMEM((TOP_K, TAIL_TM * ROW_TILE, LANES), y_rows.dtype),
            pltpu.VMEM((TOP_K, TAIL_TM * ROW_TILE, LANES), y_rows.dtype),
            pltpu.SemaphoreType.DMA(()),
        ],
    )
    return pl.pallas_call(
        functools.partial(_tail_kernel, n_tok=t),
        grid_spec=grid_spec,
        out_shape=jax.ShapeDtypeStruct((t, D_MODEL), _f32),
        compiler_params=pltpu.CompilerParams(
            dimension_semantics=("arbitrary",),
            vmem_limit_bytes=VMEM_LIMIT),
        name="tail",
    )(pos, h1, gates, y_rows, p2, w_pp, g_ple, w_pg, b_pg, g_fin)


def _inv_freq_row():
    inv_freq = ROPE_THETA ** (-jnp.arange(0, ROT_DIM, 2, dtype=_f32) / ROT_DIM)
    comp = jnp.concatenate([inv_freq, inv_freq, jnp.zeros((DIFF_HEAD_DIM - ROT_DIM,), _f32)])
    return jnp.concatenate([comp, comp]).reshape(1, LANES)


def kernel(x, p, positions, norm_mix_g, w_in, conv_w, conv_b, dt_bias_f, dt_bias_b, a_log_f, a_log_b, d_skip, ssd_norm_g, lam_q1, lam_k1, lam_q2, lam_k2, subln_g, w_out, norm_ffn_g, w_route_group, b_route_group, w_route_expert, b_route_expert, w_exp_gate, w_exp_up, w_exp_down, w_ple_proj, ple_norm_g, w_ple_gate, b_ple_gate, final_norm_g):
    b, seq, d = x.shape
    t = b * seq
    x2 = x.reshape(t, d)
    row = lambda v: v.reshape(1, -1).astype(_f32)

    w_in_t = jnp.swapaxes(w_in[0], 0, 1)
    pos_col = positions.astype(_f32).reshape(t, 1)
    pad_dt = lambda v: jnp.pad(v, (0, DT_PAD - 2 * SSD_HEADS)).reshape(1, DT_PAD)
    dt_bias = pad_dt(jnp.concatenate([dt_bias_f[0], dt_bias_b[0]]))
    a_row = pad_dt(jnp.concatenate([-jnp.exp(a_log_f[0]), -jnp.exp(a_log_b[0])]))
    d_skip_row = jnp.repeat(d_skip[0], SSD_HEADDIM).reshape(1, SSD_W)
    lam_vecs = jnp.stack([lam_q1[0], lam_k1[0], lam_q2[0], lam_k2[0]])
    n_route = MOE_GROUPS + N_EXPERTS
    w_route = jnp.pad(jnp.concatenate([w_route_group[0], w_route_expert[0]], axis=1),
                      ((0, 0), (0, LANES - n_route)))
    w_route_hi = w_route.astype(_bf16)
    w_route_lo = (w_route - w_route_hi.astype(_f32)).astype(_bf16)
    w_route2 = jnp.concatenate([w_route_hi, w_route_lo], axis=1)
    b_route = jnp.pad(jnp.concatenate([b_route_group[0], b_route_expert[0]]),
                      (0, LANES - n_route)).reshape(1, LANES)

    proj, dt, *rot = _inproj(x2, row(norm_mix_g[0]), pos_col, _inv_freq_row(), w_in_t)
    proj3 = proj.reshape(b, seq, MAIN_W)
    y_ssd = _ssd(proj3, dt.reshape(b, seq, DT_PAD), conv_w[0], row(conv_b[0]),
                 dt_bias, a_row, d_skip_row)
    att = _attn(proj3, [r.reshape(b, seq, LANES) for r in rot], lam_vecs, row(subln_g[0]))
    h1, hn_packed, route, cnt, gates = _outproj(
        y_ssd.reshape(t, SSD_W), proj, att.reshape(t, ATTN_W), x2, w_out[0].astype(_bf16),
        row(ssd_norm_g[0]), row(norm_ffn_g[0]), w_route2, b_route)
    pos8, blk8, meta8 = _plan(route, cnt)
    pos = pos8[:TOP_K].reshape(-1)
    block_expert = blk8[0, :_n_rows(t) // ROW_BLOCK]
    n_used = blk8[1, :1]
    x_rows = _dispatch_rows(pos, meta8[0, :N_EXPERTS], meta8[1, :N_EXPERTS], n_used, hn_packed)
    y_rows = _moe(block_expert, n_used, x_rows, w_exp_gate[0], w_exp_up[0], w_exp_down[0])
    out = _tail(pos, h1, gates, y_rows, p[0].reshape(t, PLE_DIM), w_ple_proj[0].astype(_bf16),
                row(ple_norm_g[0]), w_ple_gate[0].astype(_bf16), row(b_ple_gate[0]),
                row(final_norm_g))
    return out.reshape(b, seq, d)
```
